```python
import math
import jax, jax.numpy as jnp
from jax import lax
import numpy as np

D_MODEL = 1024
BATCH = 8
SEQ = 2048
DEPTH = 2

CHUNK = 64
Q_BLOCK = 128
NORM_EPS = 1e-6
RWKV_HEADS = 8
RWKV_HEAD_DIM = 64
RWKV_WIDTH = RWKV_HEADS * RWKV_HEAD_DIM
DECAY_LORA = 64
ICLR_LORA = 64
GATE_LORA = 128
DECAY_SCALE = math.exp(-0.5)
LNX_EPS = 64e-5
MLA_HEADS = 8
Q_LORA = 384
KV_LORA = 256
QK_NOPE = 64
QK_ROPE = 32
V_HEAD = 64
ROPE_THETA = 10000.0
D_FF = 2816
N_EXPERTS = 8
TOP_K = 2
D_FF_EXPERT = 3584
N_DENSE = (DEPTH + 1) // 2
N_MOE = DEPTH // 2
SHIFT_SIZES = (RWKV_WIDTH, RWKV_WIDTH, RWKV_WIDTH, DECAY_LORA, ICLR_LORA, GATE_LORA)
REST_SIZES = (Q_LORA, KV_LORA, QK_ROPE, D_MODEL, D_MODEL)
SHIFT_WIDTH = sum(SHIFT_SIZES)
IN_WIDTH = SHIFT_WIDTH + sum(REST_SIZES)

kernel_name = 'hybrid_rwkv7_mla_moe_adaln_block'


def _split(x, sizes):
    out, o = [], 0
    for s in sizes:
        out.append(x[..., o:o + s])
        o += s
    return out


def rmsnorm(x, g):
    xf = x.astype(jnp.float32)
    y = xf * lax.rsqrt(jnp.mean(xf * xf, axis=-1, keepdims=True) + NORM_EPS)
    return (y * g.astype(jnp.float32)).astype(x.dtype)


def ada_modulation(c, w, b):
    mod = jax.nn.silu(c) @ w + b
    shift, scale, gate = jnp.split(mod, 3, axis=-1)
    return shift[:, None, :], scale[:, None, :], gate[:, None, :]


def token_shift(p):
    return jnp.pad(p, ((0, 0), (1, 0), (0, 0)))[:, :-1]


def apply_rope(x, cos, sin):
    x1, x2 = jnp.split(x, 2, axis=-1)
    return jnp.concatenate([x1 * cos - x2 * sin, x1 * sin + x2 * cos], axis=-1)


def _rwkv7_step(state, inp):
    r, w, k, v, a, b = inp
    sa = jnp.einsum('bhvk,bhk->bhv', state, a)
    state = state * w[:, :, None, :] + sa[..., None] * b[:, :, None, :] + v[..., None] * k[:, :, None, :]
    y = jnp.einsum('bhvk,bhk->bhv', state, r)
    return state, y


def rwkv7_branch(p_r, p_k, p_v, p_w, p_a, p_g, w0, w_up, a0, a_up, g_up, k_k, k_a, r_k, lnx_g, lnx_b, w_out):
    dt = p_r.dtype
    B, S, _ = p_r.shape
    H, N = RWKV_HEADS, RWKV_HEAD_DIM
    f = lambda t: t.astype(jnp.float32)
    decay = jnp.exp(-DECAY_SCALE * jax.nn.sigmoid(f(w0) + jnp.tanh(f(p_w)) @ f(w_up)))
    iclr = jax.nn.sigmoid(f(a0) + f(p_a) @ f(a_up))
    gate = jax.nn.sigmoid(f(p_g)) @ f(g_up)
    k = f(p_k)
    kk = (k * f(k_k)).reshape(B, S, H, N)
    kk = kk * lax.rsqrt(jnp.sum(kk * kk, axis=-1, keepdims=True) + 1e-12)
    k = k * (1.0 + (iclr - 1.0) * f(k_a))
    heads = lambda t: t.reshape(B, S, H, N)
    r, k, v, decay, iclr = heads(f(p_r)), heads(k), heads(f(p_v)), heads(decay), heads(iclr)
    a_vec = -kk
    b_vec = kk * iclr
    xs = tuple(jnp.swapaxes(t, 0, 1) for t in (r, decay, k, v, a_vec, b_vec))
    state0 = jnp.zeros((B, H, N, N), jnp.float32)
    _, y = lax.scan(_rwkv7_step, state0, xs)
    y = jnp.swapaxes(y, 0, 1)
    mu = jnp.mean(y, axis=-1, keepdims=True)
    var = jnp.mean(jnp.square(y - mu), axis=-1, keepdims=True)
    y = ((y - mu) * lax.rsqrt(var + LNX_EPS)).reshape(B, S, H * N) * f(lnx_g) + f(lnx_b)
    bonus = jnp.sum(r * k * f(r_k), axis=-1, keepdims=True) * v
    y = (y + bonus.reshape(B, S, H * N)) * gate
    return y.astype(dt) @ w_out


def mla_branch(q_a, kv_a, k_rope, cos, sin, q_norm, w_qb, kv_norm, w_kvb, w_out):
    B, S, _ = q_a.shape
    H = MLA_HEADS
    q = (rmsnorm(q_a, q_norm) @ w_qb).reshape(B, S, H, QK_NOPE + QK_ROPE)
    q_nope, q_pe = q[..., :QK_NOPE], q[..., QK_NOPE:]
    q_pe = apply_rope(q_pe, cos[:, :, None, :], sin[:, :, None, :])
    kv = (rmsnorm(kv_a, kv_norm) @ w_kvb).reshape(B, S, H, QK_NOPE + V_HEAD)
    k_nope, v = kv[..., :QK_NOPE], kv[..., QK_NOPE:]
    k_pe = apply_rope(k_rope, cos, sin)
    q = jnp.concatenate([q_nope, q_pe], axis=-1)
    k = jnp.concatenate([k_nope, jnp.broadcast_to(k_pe[:, :, None, :], (B, S, H, QK_ROPE))], axis=-1)
    scale = (QK_NOPE + QK_ROPE) ** -0.5
    outs = []
    for i in range(S // Q_BLOCK):
        q_lo = i * Q_BLOCK
        k_hi = q_lo + Q_BLOCK
        s = jnp.einsum('bqhd,bkhd->bhqk', q[:, q_lo:k_hi], k[:, :k_hi]).astype(jnp.float32) * scale
        q_chunk = (q_lo + jnp.arange(Q_BLOCK)) // CHUNK
        k_chunk = jnp.arange(k_hi) // CHUNK
        mask = k_chunk[None, :] <= q_chunk[:, None]
        s = jnp.where(mask, s, jnp.finfo(jnp.float32).min)
        p = jax.nn.softmax(s, axis=-1).astype(v.dtype)
        outs.append(jnp.einsum('bhqk,bkhd->bqhd', p, v[:, :k_hi]))
    o = jnp.concatenate(outs, axis=1).reshape(B, S, H * V_HEAD)
    return o @ w_out


def swiglu(h, w1, w3, w2):
    return (jax.nn.silu(h @ w1) * (h @ w3)) @ w2


def moe_swiglu(h, router_w, router_b, w1, w3, w2):
    logits = (h @ router_w).astype(jnp.float32) + router_b.astype(jnp.float32)
    top_v, top_i = lax.top_k(logits, TOP_K)
    probs = jax.nn.softmax(top_v, axis=-1)
    combine = jnp.sum(jax.nn.one_hot(top_i, N_EXPERTS, dtype=jnp.float32) * probs[..., None], axis=-2)
    out = jnp.zeros_like(h)
    for e in range(N_EXPERTS):
        out = out + combine[..., e:e + 1].astype(h.dtype) * swiglu(h, w1[e], w3[e], w2[e])
    return out


def setup_inputs(seed: int = 0) -> dict:
    key = jax.random.key(seed)
    keys = iter(jax.random.split(key, 48))
    def nrm(shape, s):
        return s * jax.random.normal(next(keys), shape, jnp.float32)
    L = DEPTH
    RW = RWKV_WIDTH
    x = nrm((BATCH, SEQ, D_MODEL), 1.0)
    c = nrm((BATCH, D_MODEL), 1.0)
    offs = jax.random.randint(next(keys), (BATCH, 1), 0, 4096, dtype=jnp.int32)
    positions = offs + jnp.arange(SEQ, dtype=jnp.int32)[None, :]
    return {
        'x': x,
        'c': c,
        'positions': positions,
        'ada_w': nrm((L, 2, D_MODEL, 3 * D_MODEL), 0.5 * D_MODEL ** -0.5),
        'ada_b': nrm((L, 2, 3 * D_MODEL), 0.02),
        'norm_mix': 1.0 + nrm((L, D_MODEL), 0.05),
        'norm_ffn': 1.0 + nrm((L, D_MODEL), 0.05),
        'norm_final': 1.0 + nrm((D_MODEL,), 0.05),
        'w_in': nrm((L, D_MODEL, IN_WIDTH), D_MODEL ** -0.5),
        'tshift_mu': jax.random.uniform(next(keys), (L, SHIFT_WIDTH), jnp.float32),
        'w0': nrm((L, RW), 0.5),
        'w_up': nrm((L, DECAY_LORA, RW), 0.5 * DECAY_LORA ** -0.5),
        'a0': nrm((L, RW), 0.1),
        'a_up': nrm((L, ICLR_LORA, RW), 0.5 * ICLR_LORA ** -0.5),
        'g_up': nrm((L, GATE_LORA, RW), GATE_LORA ** -0.5),
        'k_k': 0.85 + nrm((L, RW), 0.05),
        'k_a': 1.0 + nrm((L, RW), 0.05),
        'r_k': nrm((L, RWKV_HEADS, RWKV_HEAD_DIM), 0.1),
        'lnx_g': 1.0 + nrm((L, RW), 0.05),
        'lnx_b': nrm((L, RW), 0.02),
        'rwkv_out': nrm((L, RW, D_MODEL), RW ** -0.5),
        'q_norm': 1.0 + nrm((L, Q_LORA), 0.05),
        'w_qb': nrm((L, Q_LORA, MLA_HEADS * (QK_NOPE + QK_ROPE)), Q_LORA ** -0.5),
        'kv_norm': 1.0 + nrm((L, KV_LORA), 0.05),
        'w_kvb': nrm((L, KV_LORA, MLA_HEADS * (QK_NOPE + V_HEAD)), KV_LORA ** -0.5),
        'mla_out': nrm((L, MLA_HEADS * V_HEAD, D_MODEL), (MLA_HEADS * V_HEAD) ** -0.5),
        'w_o': nrm((L, D_MODEL, D_MODEL), D_MODEL ** -0.5),
        'ffn_w1': nrm((N_DENSE, D_MODEL, D_FF), D_MODEL ** -0.5),
        'ffn_w3': nrm((N_DENSE, D_MODEL, D_FF), D_MODEL ** -0.5),
        'ffn_w2': nrm((N_DENSE, D_FF, D_MODEL), D_FF ** -0.5),
        'router_w': nrm((N_MOE, D_MODEL, N_EXPERTS), D_MODEL ** -0.5),
        'router_b': nrm((N_MOE, N_EXPERTS), 0.01),
        'moe_w1': nrm((N_MOE, N_EXPERTS, D_MODEL, D_FF_EXPERT), D_MODEL ** -0.5),
        'moe_w3': nrm((N_MOE, N_EXPERTS, D_MODEL, D_FF_EXPERT), D_MODEL ** -0.5),
        'moe_w2': nrm((N_MOE, N_EXPERTS, D_FF_EXPERT, D_MODEL), D_FF_EXPERT ** -0.5),
    }


def reference(x, c, positions, ada_w, ada_b, norm_mix, norm_ffn, norm_final, w_in, tshift_mu,
              w0, w_up, a0, a_up, g_up, k_k, k_a, r_k, lnx_g, lnx_b, rwkv_out,
              q_norm, w_qb, kv_norm, w_kvb, mla_out, w_o,
              ffn_w1, ffn_w3, ffn_w2, router_w, router_b, moe_w1, moe_w3, moe_w2):
    inv_freq = ROPE_THETA ** (-jnp.arange(0, QK_ROPE, 2, dtype=jnp.float32) / QK_ROPE)
    ang = positions.astype(jnp.float32)[..., None] * inv_freq
    cos = jnp.cos(ang).astype(x.dtype)
    sin = jnp.sin(ang).astype(x.dtype)
    for l in range(DEPTH):
        shift, scale, gate = ada_modulation(c, ada_w[l, 0], ada_b[l, 0])
        h = rmsnorm(x, norm_mix[l]) * (1.0 + scale) + shift
        proj = h @ w_in[l]
        ps = proj[..., :SHIFT_WIDTH]
        ps = ps + (token_shift(ps) - ps) * tshift_mu[l]
        p_r, p_k, p_v, p_w, p_a, p_g = _split(ps, SHIFT_SIZES)
        q_a, kv_a, k_rope, g_a, g_b = _split(proj[..., SHIFT_WIDTH:], REST_SIZES)
        y_a = rwkv7_branch(p_r, p_k, p_v, p_w, p_a, p_g, w0[l], w_up[l], a0[l], a_up[l], g_up[l],
                           k_k[l], k_a[l], r_k[l], lnx_g[l], lnx_b[l], rwkv_out[l])
        y_b = mla_branch(q_a, kv_a, k_rope, cos, sin, q_norm[l], w_qb[l], kv_norm[l], w_kvb[l], mla_out[l])
        merged = jax.nn.sigmoid(g_a) * y_a + jax.nn.sigmoid(g_b) * y_b
        x = x + gate * (merged @ w_o[l])
        shift, scale, gate = ada_modulation(c, ada_w[l, 1], ada_b[l, 1])
        h = rmsnorm(x, norm_ffn[l]) * (1.0 + scale) + shift
        if l % 2 == 0:
            f = swiglu(h, ffn_w1[l // 2], ffn_w3[l // 2], ffn_w2[l // 2])
        else:
            f = moe_swiglu(h, router_w[l // 2], router_b[l // 2], moe_w1[l // 2], moe_w3[l // 2], moe_w2[l // 2])
        x = x + gate * f
    return rmsnorm(x, norm_final)
```

```python
import functools
import math

import jax
import jax.numpy as jnp
from jax import lax
from jax.experimental import pallas as pl
from jax.experimental.pallas import tpu as pltpu

F32 = jnp.float32
BF16 = jnp.bfloat16

D_MODEL = 1024
DEPTH = 2
CHUNK = 64
NORM_EPS = 1e-6
RWKV_HEADS = 8
RWKV_HEAD_DIM = 64
RW = RWKV_HEADS * RWKV_HEAD_DIM
DECAY_LORA = 64
ICLR_LORA = 64
GATE_LORA = 128
DECAY_SCALE = math.exp(-0.5)
LNX_EPS = 64e-5
MLA_HEADS = 8
Q_LORA = 384
KV_LORA = 256
QK_NOPE = 64
QK_ROPE = 32
V_HEAD = 64
ROPE_THETA = 10000.0
N_EXPERTS = 8

LANE = 128
HEAD_PAD = 128
PS_W = 3 * RW + 3 * LANE
MLA_W = Q_LORA + KV_LORA + 2 * LANE
GATE_W = 2 * D_MODEL
HALF = 256
VMEM_LIMIT = 56 * 1024 * 1024


def _sigmoid(x):
    return 1.0 / (1.0 + jnp.exp(-x))


def _dot(a, b):
    return jnp.dot(a, b, preferred_element_type=F32)


def _dot_nt(a, b):
    return lax.dot_general(a, b, (((1,), (1,)), ((), ())), preferred_element_type=F32)


def _split2(x):
    hi = x.astype(BF16)
    lo = (x - hi.astype(F32)).astype(BF16)
    return hi, lo


def _dot3(a, b):
    ah, al = _split2(a)
    bh, bl = _split2(b)
    return _dot(ah, bh) + _dot(al, bh) + _dot(ah, bl)


def _params(sem):
    return pltpu.CompilerParams(dimension_semantics=sem, vmem_limit_bytes=VMEM_LIMIT)


def _ada_kernel(c_ref, w_ref, b_ref, o_ref):
    c = c_ref[...]
    o_ref[...] = _dot3(c * _sigmoid(c), w_ref[...]) + b_ref[...]


def _ada_call(c, ada_w, ada_b):
    n = ada_w.shape[0]
    nb = c.shape[0]
    return pl.pallas_call(
        _ada_kernel,
        grid=(n, 3),
        in_specs=[
            pl.BlockSpec((nb, D_MODEL), lambda i, j: (0, 0)),
            pl.BlockSpec((None, D_MODEL, D_MODEL), lambda i, j: (i, 0, j)),
            pl.BlockSpec((None, 1, D_MODEL), lambda i, j: (i, 0, j)),
        ],
        out_specs=pl.BlockSpec((None, nb, D_MODEL), lambda i, j: (i, 0, j)),
        out_shape=jax.ShapeDtypeStruct((n, nb, 3 * D_MODEL), F32),
        compiler_params=_params(("arbitrary", "arbitrary")),
        name="ada",
    )(c, ada_w, ada_b)


def _norm_mod(x, g, scale, shift):
    var = jnp.mean(x * x, axis=-1, keepdims=True)
    return (x * lax.rsqrt(var + NORM_EPS) * g) * (1.0 + scale) + shift


def _inproj_kernel(x_ref, shift_ref, scale_ref, g_ref, w_ref, mu_ref,
                   ps_ref, mla_ref, gates_ref, carry_ref, *, tiles_per_seq, tm):
    i = pl.program_id(0)
    h = _norm_mod(x_ref[...], g_ref[...], scale_ref[...], shift_ref[...]).astype(BF16)

    @pl.when(i % tiles_per_seq == 0)
    def _():
        carry_ref[...] = jnp.zeros_like(carry_ref)

    p = _dot(h, w_ref[:, 0:PS_W])
    prev = pltpu.roll(p, 1, axis=0)
    row = lax.broadcasted_iota(jnp.int32, p.shape, 0)
    prev = jnp.where(row == 0, carry_ref[0:1, :], prev)
    carry_ref[0:1, :] = p[tm - 1:tm, :]
    ps_ref[...] = p + (prev - p) * mu_ref[...]
    mla_ref[...] = _dot(h, w_ref[:, PS_W:PS_W + MLA_W])
    gates_ref[...] = _sigmoid(_dot(h, w_ref[:, PS_W + MLA_W:])).astype(BF16)


def _inproj_call(x2, shift, scale, g, w_pad, mu_pad, seq, tm=256):
    t = x2.shape[0]
    tiles_per_seq = seq // tm
    bmap = lambda i: (i // tiles_per_seq, 0, 0)
    wtot = PS_W + MLA_W + GATE_W
    return pl.pallas_call(
        functools.partial(_inproj_kernel, tiles_per_seq=tiles_per_seq, tm=tm),
        grid=(t // tm,),
        in_specs=[
            pl.BlockSpec((tm, D_MODEL), lambda i: (i, 0)),
            pl.BlockSpec((None, 1, D_MODEL), bmap),
            pl.BlockSpec((None, 1, D_MODEL), bmap),
            pl.BlockSpec((1, D_MODEL), lambda i: (0, 0)),
            pl.BlockSpec((D_MODEL, wtot), lambda i: (0, 0)),
            pl.BlockSpec((1, PS_W), lambda i: (0, 0)),
        ],
        out_specs=[
            pl.BlockSpec((tm, PS_W), lambda i: (i, 0)),
            pl.BlockSpec((tm, MLA_W), lambda i: (i, 0)),
            pl.BlockSpec((tm, GATE_W), lambda i: (i, 0)),
        ],
        out_shape=[
            jax.ShapeDtypeStruct((t, PS_W), F32),
            jax.ShapeDtypeStruct((t, MLA_W), F32),
            jax.ShapeDtypeStruct((t, GATE_W), BF16),
        ],
        scratch_shapes=[pltpu.VMEM((8, PS_W), F32)],
        compiler_params=_params(("arbitrary",)),
        name="inproj",
    )(x2, shift, scale, g, w_pad, mu_pad)


def _blockdiag(xb, bdmask):
    return jnp.tile(xb, (HALF // CHUNK, 1)) * bdmask


def _bd_nn(a, b, bdmask):
    ah, al = _split2(a)
    bh, bl = _split2(b)
    r = _dot(jnp.concatenate([ah, al], axis=0), _blockdiag(bh, bdmask))
    n = a.shape[0]
    return r[:n] + r[n:] + _dot(ah, _blockdiag(bl, bdmask))


def _bd_nt(a, b, bdmask):
    ah, al = _split2(a)
    bh, bl = _split2(b)
    r = _dot_nt(jnp.concatenate([ah, al], axis=0), _blockdiag(bh, bdmask))
    n = a.shape[0]
    return r[:n] + r[n:] + _dot_nt(ah, _blockdiag(bl, bdmask))


def _bd_tn(a, b, bdmask_f):
    full = _dot3(a.T, b) * bdmask_f
    return jnp.sum(full.reshape(HALF // CHUNK, CHUNK, HALF), axis=0)


def _halves(fn, *xs):
    outs = [fn(*[x[:, h * HALF:(h + 1) * HALF] for x in xs]) for h in range(RW // HALF)]
    return jnp.concatenate(outs, axis=1)


def _rwkv_kernel(ps_ref, w0_ref, a0_ref, kk_ref, ka_ref, rk_ref, lg_ref, lb_ref,
                 wup_ref, aup_ref, gup_ref, out_ref, s_ref, *, nc):
    @pl.when(pl.program_id(1) == 0)
    def _():
        s_ref[...] = jnp.zeros_like(s_ref)

    C = CHUNK
    r_i = lax.broadcasted_iota(jnp.int32, (HALF, HALF), 0)
    c_i = lax.broadcasted_iota(jnp.int32, (HALF, HALF), 1)
    bdsame = (r_i // C) == (c_i // C)
    bdmask = bdsame.astype(BF16)
    bdmask_f = bdsame.astype(F32)
    t_i = lax.broadcasted_iota(jnp.int32, (C, HALF), 0)
    i_i = lax.broadcasted_iota(jnp.int32, (C, HALF), 1) % C
    strict = i_i < t_i
    incl = i_i <= t_i
    eye_l = (i_i == t_i).astype(F32)
    tr = lax.broadcasted_iota(jnp.int32, (C, C), 0)
    tc = lax.broadcasted_iota(jnp.int32, (C, C), 1)
    tril = (tc <= tr).astype(BF16)

    def headsum(x):
        def one(xh):
            hi, lo = _split2(xh)
            return _dot(hi, bdmask) + _dot(lo, bdmask)
        return _halves(one, x)

    def cumsum_rows(x):
        hi = x.astype(BF16)
        r1 = x - hi.astype(F32)
        mid = r1.astype(BF16)
        lo = (r1 - mid.astype(F32)).astype(BF16)
        return _dot(tril, hi) + _dot(tril, mid) + _dot(tril, lo)

    for c in range(nc):
        rows = slice(c * C, (c + 1) * C)
        p_r = ps_ref[rows, 0:RW]
        p_k = ps_ref[rows, RW:2 * RW]
        p_v = ps_ref[rows, 2 * RW:3 * RW]
        p_w = ps_ref[rows, 3 * RW:3 * RW + LANE]
        p_a = ps_ref[rows, 3 * RW + LANE:3 * RW + 2 * LANE]
        p_g = ps_ref[rows, 3 * RW + 2 * LANE:3 * RW + 3 * LANE]

        logw = -DECAY_SCALE * _sigmoid(w0_ref[...] + _dot3(jnp.tanh(p_w), wup_ref[...]))
        iclr = _sigmoid(a0_ref[...] + _dot3(p_a, aup_ref[...]))
        gate = _dot3(_sigmoid(p_g), gup_ref[...])
        kk = p_k * kk_ref[...]
        kk = kk * lax.rsqrt(headsum(kk * kk) + 1e-12)
        k2 = p_k * (1.0 + (iclr - 1.0) * ka_ref[...])
        a_v = -kk
        b_v = kk * iclr

        cum = cumsum_rows(logw)
        cum_c = cum[C - 1:C, :]
        e_neg = jnp.exp(-cum)
        e_rem = jnp.exp(cum_c - cum)
        a_t = a_v * jnp.exp(cum - logw)
        b_t = b_v * e_neg
        k_t = k2 * e_neg
        r_t = p_r * jnp.exp(cum)
        b_g = b_v * e_rem
        k_g = k2 * e_rem
        gam_c = jnp.exp(cum_c)

        def chunk_half(a_t, b_t, k_t, r_t, b_g, k_g, v, gam_c):
            ar = jnp.concatenate([a_t, r_t], axis=0)
            sb = _bd_nt(ar, b_t, bdmask)
            sk = _bd_nt(ar, k_t, bdmask)
            l_ab = jnp.where(strict, sb[:C], 0.0)
            q_b = jnp.where(incl, sb[C:], 0.0)
            m_ak = jnp.where(strict, sk[:C], 0.0)
            q_k = jnp.where(incl, sk[C:], 0.0)
            t_m = eye_l + l_ab
            pw = l_ab
            for _ in range(5):
                pw = _bd_nn(pw, pw, bdmask)
                t_m = t_m + _bd_nn(t_m, pw, bdmask)
            mq = _bd_nn(jnp.concatenate([m_ak, q_k], axis=0), v, bdmask)
            mv, qkv = mq[:C], mq[C:]
            a_h = _bd_nn(t_m, a_t, bdmask)
            u0 = _bd_nn(t_m, mv, bdmask)
            r_h = r_t + _bd_nn(q_b, a_h, bdmask)
            y0 = qkv + _bd_nn(q_b, u0, bdmask)
            g_l = _bd_tn(a_h, b_g, bdmask_f) + eye_l * gam_c
            h_l = _bd_tn(jnp.concatenate([u0, v], axis=0),
                         jnp.concatenate([b_g, k_g], axis=0), bdmask_f)
            return jnp.concatenate([r_h, y0, g_l, h_l], axis=0)

        pre = _halves(chunk_half, a_t, b_t, k_t, r_t, b_g, k_g, p_v, gam_c)
        r_h, y0, g_l, h_l = pre[0:C], pre[C:2 * C], pre[2 * C:3 * C], pre[3 * C:4 * C]

        s_prev = s_ref[...]
        y = _halves(lambda rh, s: _bd_nt(rh, s, bdmask), r_h, s_prev) + y0
        s_ref[...] = _halves(lambda s, g: _bd_nn(s, g, bdmask), s_prev, g_l) + h_l

        inv_n = 1.0 / RWKV_HEAD_DIM
        mu = headsum(y) * inv_n
        d = y - mu
        var = headsum(d * d) * inv_n
        yn = d * lax.rsqrt(var + LNX_EPS) * lg_ref[...] + lb_ref[...]
        bonus = headsum(p_r * k2 * rk_ref[...]) * p_v
        out_ref[rows, :] = ((yn + bonus) * gate).astype(BF16)


def _rwkv_call(ps, vecs, wup, aup, gup, batch, seq, nc=2):
    t = ps.shape[0]
    rows = nc * CHUNK
    steps = seq // rows
    vec_spec = pl.BlockSpec((1, RW), lambda b, j: (0, 0))
    lora_spec = pl.BlockSpec((LANE, RW), lambda b, j: (0, 0))
    return pl.pallas_call(
        functools.partial(_rwkv_kernel, nc=nc),
        grid=(batch, steps),
        in_specs=[pl.BlockSpec((rows, PS_W), lambda b, j: (b * steps + j, 0))]
                 + [vec_spec] * 7 + [lora_spec] * 3,
        out_specs=pl.BlockSpec((rows, RW), lambda b, j: (b * steps + j, 0)),
        out_shape=jax.ShapeDtypeStruct((t, RW), BF16),
        scratch_shapes=[pltpu.VMEM((CHUNK, RW), F32)],
        compiler_params=_params(("arbitrary", "arbitrary")),
        name="rwkv",
    )(ps, *vecs, wup, aup, gup)


def _mlaprep_kernel(mla_ref, cos_ref, sin_ref, qn_ref, kvn_ref, wq_ref, wqs_ref, wk_ref, wv_ref,
                    q_ref, k_ref, v_ref):
    def rms(x, g):
        return (x * lax.rsqrt(jnp.mean(x * x, axis=-1, keepdims=True) + NORM_EPS) * g).astype(BF16)

    cos = cos_ref[...]
    sin = sin_ref[...]
    lane = lax.broadcasted_iota(jnp.int32, cos.shape, 1)
    cos_q = jnp.tile(cos + (lane < QK_NOPE).astype(F32), (1, MLA_HEADS))
    sin_q = jnp.tile(sin, (1, MLA_HEADS))
    qn = rms(mla_ref[:, 0:Q_LORA], qn_ref[...])
    q = _dot(qn, wq_ref[...]) * cos_q + _dot(qn, wqs_ref[...]) * sin_q
    q_ref[...] = (q * ((QK_NOPE + QK_ROPE) ** -0.5)).astype(BF16)
    kvn = rms(mla_ref[:, Q_LORA:Q_LORA + KV_LORA], kvn_ref[...])
    o = Q_LORA + KV_LORA
    k_pe = mla_ref[:, o:o + LANE] * cos + mla_ref[:, o + LANE:o + 2 * LANE] * sin
    k_ref[...] = (_dot(kvn, wk_ref[...]) + jnp.tile(k_pe, (1, MLA_HEADS))).astype(BF16)
    v_ref[...] = _dot(kvn, wv_ref[...]).astype(BF16)


def _mlaprep_call(mla, cos, sin, qn, kvn, wq, wqs, wk, wv, tm=512):
    t = mla.shape[0]
    hw = MLA_HEADS * HEAD_PAD
    full = lambda shape: pl.BlockSpec(shape, lambda i: (0, 0))
    row = lambda w: pl.BlockSpec((tm, w), lambda i: (i, 0))
    return pl.pallas_call(
        _mlaprep_kernel,
        grid=(t // tm,),
        in_specs=[row(MLA_W), row(LANE), row(LANE), full((1, Q_LORA)), full((1, KV_LORA)),
                  full((Q_LORA, hw)), full((Q_LORA, hw)), full((KV_LORA, hw)), full((KV_LORA, hw))],
        out_specs=[row(hw)] * 3,
        out_shape=[jax.ShapeDtypeStruct((t, hw), BF16)] * 3,
        compiler_params=_params(("arbitrary",)),
        name="mlaprep",
    )(mla, cos, sin, qn, kvn, wq, wqs, wk, wv)


def _attn_kernel(q_ref, k_ref, v_ref, o_ref, *, seq, tq):
    r_i = lax.broadcasted_iota(jnp.int32, (tq, tq), 0) // CHUNK
    c_i = lax.broadcasted_iota(jnp.int32, (tq, tq), 1) // CHUNK
    mask = c_i <= r_i
    for qi in range(seq // tq):
        lo = qi * tq
        q = q_ref[lo:lo + tq, :]
        sd = jnp.where(mask, _dot_nt(q, k_ref[lo:lo + tq, :]), -1e30)
        m = jnp.max(sd, axis=-1, keepdims=True)
        if qi > 0:
            so = _dot_nt(q, k_ref[0:lo, :])
            m = jnp.maximum(m, jnp.max(so, axis=-1, keepdims=True))
        pd = jnp.exp(sd - m)
        l = jnp.sum(pd, axis=-1, keepdims=True)
        acc = _dot(pd.astype(BF16), v_ref[lo:lo + tq, :])
        if qi > 0:
            po = jnp.exp(so - m)
            l = l + jnp.sum(po, axis=-1, keepdims=True)
            acc = acc + _dot(po.astype(BF16), v_ref[0:lo, :])
        o_ref[lo:lo + tq, :] = (acc / l).astype(BF16)


def _attn_call(q, k, v, batch, seq, tq=256):
    spec = pl.BlockSpec((seq, HEAD_PAD), lambda b, h: (b, h))
    return pl.pallas_call(
        functools.partial(_attn_kernel, seq=seq, tq=tq),
        grid=(batch, MLA_HEADS),
        in_specs=[spec] * 3,
        out_specs=spec,
        out_shape=jax.ShapeDtypeStruct(q.shape, BF16),
        compiler_params=_params(("arbitrary", "arbitrary")),
        name="attn",
    )(q, k, v)


def _mixout_kernel(*refs, router):
    if router:
        (rw_ref, o_ref, gates_ref, x_ref, gate_ref, wr_ref, wm_ref, wo_ref,
         g2_ref, shift2_ref, scale2_ref, rwt_ref, rb_ref,
         xo_ref, h2_ref, ti_ref, tp_ref) = refs
    else:
        (rw_ref, o_ref, gates_ref, x_ref, gate_ref, wr_ref, wm_ref, wo_ref,
         g2_ref, shift2_ref, scale2_ref, xo_ref, h2_ref) = refs
    ya = _dot(rw_ref[...], wr_ref[...])
    yb = _dot(o_ref[...], wm_ref[...])
    merged = gates_ref[:, 0:D_MODEL].astype(F32) * ya + gates_ref[:, D_MODEL:].astype(F32) * yb
    xn = x_ref[...] + gate_ref[...] * _dot(merged.astype(BF16), wo_ref[...])
    xo_ref[...] = xn
    h2 = _norm_mod(xn, g2_ref[...], scale2_ref[...], shift2_ref[...])
    h2_ref[...] = h2.astype(BF16)
    if router:
        hh, hl = _split2(h2)
        wh, wl = _split2(rwt_ref[...])
        logits = _dot_nt(wh, hh) + _dot_nt(wl, hh) + _dot_nt(wh, hl) + rb_ref[...]
        e_i = lax.broadcasted_iota(jnp.int32, logits.shape, 0)
        m1 = jnp.max(logits, axis=0, keepdims=True)
        i1 = jnp.min(jnp.where(logits == m1, e_i, N_EXPERTS), axis=0, keepdims=True)
        rest = jnp.where(e_i == i1, -jnp.inf, logits)
        m2 = jnp.max(rest, axis=0, keepdims=True)
        i2 = jnp.min(jnp.where(rest == m2, e_i, N_EXPERTS), axis=0, keepdims=True)
        e2 = jnp.exp(m2 - m1)
        den = 1.0 + e2
        ti_ref[0:1, :] = i1
        ti_ref[1:2, :] = i2
        tp_ref[0:1, :] = 1.0 / den
        tp_ref[1:2, :] = e2 / den


def _mixout_call(rw, o, gates, x2, gate, wr, wm, wo, g2, shift2, scale2, seq, router=None, tm=512):
    t = x2.shape[0]
    tiles_per_seq = seq // tm
    bmap = lambda i: (i // tiles_per_seq, 0, 0)
    full = lambda shape: pl.BlockSpec(shape, lambda i: (0, 0))
    row = lambda w: pl.BlockSpec((tm, w), lambda i: (i, 0))
    mod = pl.BlockSpec((None, 1, D_MODEL), bmap)
    in_specs = [row(RW), row(MLA_HEADS * HEAD_PAD), row(GATE_W), row(D_MODEL), mod,
                full(wr.shape), full(wm.shape), full(wo.shape), full((1, D_MODEL)), mod, mod]
    args = [rw, o, gates, x2, gate, wr, wm, wo, g2, shift2, scale2]
    out_specs = [row(D_MODEL), row(D_MODEL)]
    out_shape = [jax.ShapeDtypeStruct((t, D_MODEL), F32), jax.ShapeDtypeStruct((t, D_MODEL), BF16)]
    if router is not None:
        in_specs += [full((N_EXPERTS, D_MODEL)), full((N_EXPERTS, 1))]
        args += list(router)
        out_specs += [pl.BlockSpec((2, tm), lambda i: (0, i))] * 2
        out_shape += [jax.ShapeDtypeStruct((2, t), jnp.int32), jax.ShapeDtypeStruct((2, t), F32)]
    return pl.pallas_call(
        functools.partial(_mixout_kernel, router=router is not None),
        grid=(t // tm,),
        in_specs=in_specs,
        out_specs=out_specs,
        out_shape=out_shape,
        compiler_params=_params(("arbitrary",)),
        name="mixout",
    )(*args)


def _ffn_kernel(h_ref, sc_ref, w1_ref, w3_ref, w2_ref, y_ref, acc_ref):
    g = pl.program_id(1)
    f = pl.program_id(2)

    @pl.when((g == 0) & (f == 0))
    def _():
        acc_ref[...] = jnp.zeros_like(acc_ref)

    h = h_ref[...]
    a1 = _dot(h, w1_ref[...])
    a3 = _dot(h, w3_ref[...])
    act = (a1 * _sigmoid(a1)) * a3 * sc_ref[...]
    acc_ref[...] += _dot(act.astype(BF16), w2_ref[...])

    @pl.when((g == pl.num_programs(1) - 1) & (f == pl.num_programs(2) - 1))
    def _():
        y_ref[...] = acc_ref[...]


def _ffn_call(h, scale, w1, w3, w2, tm=512, tf=None):
    r = h.shape[0]
    ne, _, ff = w1.shape
    return pl.pallas_call(
        _ffn_kernel,
        grid=(r // tm, ne, ff // tf),
        in_specs=[
            pl.BlockSpec((tm, D_MODEL), lambda i, g, f: (i, 0)),
            pl.BlockSpec((None, tm, 1), lambda i, g, f: (g, i, 0)),
            pl.BlockSpec((None, D_MODEL, tf), lambda i, g, f: (g, 0, f)),
            pl.BlockSpec((None, D_MODEL, tf), lambda i, g, f: (g, 0, f)),
            pl.BlockSpec((None, tf, D_MODEL), lambda i, g, f: (g, f, 0)),
        ],
        out_specs=pl.BlockSpec((tm, D_MODEL), lambda i, g, f: (i, 0)),
        out_shape=jax.ShapeDtypeStruct((r, D_MODEL), F32),
        scratch_shapes=[pltpu.VMEM((tm, D_MODEL), F32)],
        compiler_params=_params(("arbitrary", "arbitrary", "arbitrary")),
        name="ffn",
    )(h, scale, w1, w3, w2)


def _resid_kernel(x_ref, y_ref, gate_ref, gf_ref, o_ref, *, final):
    xn = x_ref[...] + gate_ref[...] * y_ref[...]
    if final:
        xn = xn * lax.rsqrt(jnp.mean(xn * xn, axis=-1, keepdims=True) + NORM_EPS) * gf_ref[...]
    o_ref[...] = xn


def _resid_call(x2, y, gate, gf, seq, final, tm=512):
    t = x2.shape[0]
    tiles_per_seq = seq // tm
    row = pl.BlockSpec((tm, D_MODEL), lambda i: (i, 0))
    return pl.pallas_call(
        functools.partial(_resid_kernel, final=final),
        grid=(t // tm,),
        in_specs=[row, row, pl.BlockSpec((None, 1, D_MODEL), lambda i: (i // tiles_per_seq, 0, 0)),
                  pl.BlockSpec((1, D_MODEL), lambda i: (0, 0))],
        out_specs=row,
        out_shape=jax.ShapeDtypeStruct((t, D_MODEL), F32),
        compiler_params=_params(("arbitrary",)),
        name="resid",
    )(x2, y, gate, gf)


def _pad_cols(w, n):
    return jnp.pad(w, ((0, 0), (0, n)))


def _layout_w_in(w):
    z64 = jnp.zeros((w.shape[0], 64), w.dtype)
    z32 = jnp.zeros((w.shape[0], 32), w.dtype)
    o = 3 * RW
    pw, pa, pg = w[:, o:o + 64], w[:, o + 64:o + 128], w[:, o + 128:o + 256]
    o = 3 * RW + 256
    qa, kva = w[:, o:o + Q_LORA], w[:, o + Q_LORA:o + Q_LORA + KV_LORA]
    o += Q_LORA + KV_LORA
    kr = w[:, o:o + QK_ROPE]
    kr_sw = jnp.concatenate([-kr[:, QK_ROPE // 2:], kr[:, :QK_ROPE // 2]], axis=1)
    gates = w[:, o + QK_ROPE:]
    return jnp.concatenate([w[:, :3 * RW], pw, z64, pa, z64, pg,
                            qa, kva, z64, kr, z32, z64, kr_sw, z32, gates], axis=1).astype(BF16)


def _layout_mu(mu):
    z = jnp.zeros((64,), mu.dtype)
    o = 3 * RW
    return jnp.concatenate([mu[:o + 64], z, mu[o + 64:o + 128], z, mu[o + 128:]])[None, :]


def _layout_wq(w_qb):
    hd = QK_NOPE + QK_ROPE
    w = w_qb.reshape(Q_LORA, MLA_HEADS, hd)
    half = QK_ROPE // 2
    x1, x2 = w[:, :, QK_NOPE:QK_NOPE + half], w[:, :, QK_NOPE + half:]
    zn = jnp.zeros((Q_LORA, MLA_HEADS, QK_NOPE), w.dtype)
    zp = jnp.zeros((Q_LORA, MLA_HEADS, HEAD_PAD - hd), w.dtype)
    wq = jnp.concatenate([w, zp], axis=2).reshape(Q_LORA, -1)
    wqs = jnp.concatenate([zn, -x2, x1, zp], axis=2).reshape(Q_LORA, -1)
    return wq.astype(BF16), wqs.astype(BF16)


def _layout_wkv(w_kvb):
    w = w_kvb.reshape(KV_LORA, MLA_HEADS, QK_NOPE + V_HEAD)
    z = jnp.zeros((KV_LORA, MLA_HEADS, HEAD_PAD - QK_NOPE), w.dtype)
    wk = jnp.concatenate([w[:, :, :QK_NOPE], z], axis=2).reshape(KV_LORA, -1)
    wv = jnp.concatenate([w[:, :, QK_NOPE:], z], axis=2).reshape(KV_LORA, -1)
    return wk.astype(BF16), wv.astype(BF16)


def _layout_mla_out(w):
    w = w.reshape(MLA_HEADS, V_HEAD, D_MODEL)
    z = jnp.zeros((MLA_HEADS, HEAD_PAD - V_HEAD, D_MODEL), w.dtype)
    return jnp.concatenate([w, z], axis=1).reshape(MLA_HEADS * HEAD_PAD, D_MODEL).astype(BF16)


def _rope_tables(positions):
    inv_freq = ROPE_THETA ** (-jnp.arange(0, QK_ROPE, 2, dtype=F32) / QK_ROPE)
    ang = positions.astype(F32).reshape(-1, 1) * inv_freq
    z = jnp.zeros((ang.shape[0], QK_NOPE), F32)
    zp = jnp.zeros((ang.shape[0], LANE - QK_NOPE - QK_ROPE), F32)
    cos, sin = jnp.cos(ang), jnp.sin(ang)
    return (jnp.concatenate([z, cos, cos, zp], axis=1), jnp.concatenate([z, sin, sin, zp], axis=1))


def kernel(x, c, positions, ada_w, ada_b, norm_mix, norm_ffn, norm_final, w_in, tshift_mu, w0, w_up, a0, a_up, g_up, k_k, k_a, r_k, lnx_g, lnx_b, rwkv_out, q_norm, w_qb, kv_norm, w_kvb, mla_out, w_o, ffn_w1, ffn_w3, ffn_w2, router_w, router_b, moe_w1, moe_w3, moe_w2):
    batch, seq, _ = x.shape
    t = batch * seq
    mods = _ada_call(c, ada_w.reshape(2 * DEPTH, D_MODEL, 3 * D_MODEL),
                     ada_b.reshape(2 * DEPTH, 1, 3 * D_MODEL))
    mods = mods.reshape(DEPTH, 2, batch, 3, 1, D_MODEL)
    cos, sin = _rope_tables(positions)
    x2 = x.reshape(t, D_MODEL)
    row = lambda v: v.reshape(1, -1)
    ones_col = jnp.ones((1, t, 1), F32)

    for l in range(DEPTH):
        shift, scale, gate = (mods[l, 0, :, j] for j in range(3))
        ps, mla, gates = _inproj_call(x2, shift, scale, row(norm_mix[l]), _layout_w_in(w_in[l]),
                                      _layout_mu(tshift_mu[l]), seq)
        vecs = [row(v[l]) for v in (w0, a0, k_k, k_a, r_k, lnx_g, lnx_b)]
        lora_pad = lambda w: jnp.pad(w, ((0, LANE - w.shape[0]), (0, 0)))
        rw = _rwkv_call(ps, vecs, lora_pad(w_up[l]), lora_pad(a_up[l]), g_up[l], batch, seq)
        wq, wqs = _layout_wq(w_qb[l])
        wk, wv = _layout_wkv(w_kvb[l])
        q, k, v = _mlaprep_call(mla, cos, sin, row(q_norm[l]), row(kv_norm[l]), wq, wqs, wk, wv)
        o = _attn_call(q, k, v, batch, seq)

        shift2, scale2, gate2 = (mods[l, 1, :, j] for j in range(3))
        moe = l % 2 == 1
        router = (router_w[l // 2].T, router_b[l // 2].reshape(N_EXPERTS, 1)) if moe else None
        outs = _mixout_call(rw, o, gates, x2, gate, rwkv_out[l].astype(BF16), _layout_mla_out(mla_out[l]),
                            w_o[l].astype(BF16), row(norm_ffn[l]), shift2, scale2, seq, router=router)
        if moe:
            x2, h2, top_i, top_p = outs
            e_ids = jnp.arange(N_EXPERTS, dtype=jnp.int32)[:, None]
            comb = (jnp.where(top_i[0:1] == e_ids, top_p[0:1], 0.0)
                    + jnp.where(top_i[1:2] == e_ids, top_p[1:2], 0.0))
            y = _ffn_call(h2, comb[:, :, None], moe_w1[l // 2].astype(BF16), moe_w3[l // 2].astype(BF16),
                          moe_w2[l // 2].astype(BF16), tf=1792)
        else:
            x2, h2 = outs
            y = _ffn_call(h2, ones_col, ffn_w1[l // 2:l // 2 + 1].astype(BF16),
                          ffn_w3[l // 2:l // 2 + 1].astype(BF16), ffn_w2[l // 2:l // 2 + 1].astype(BF16), tf=1408)
        x2 = _resid_call(x2, y, gate2, row(norm_final), seq, final=(l == DEPTH - 1))
    return x2.reshape(batch, seq, D_MODEL)
```

```python
import functools
import math

import jax
import jax.numpy as jnp
from jax import lax
from jax.experimental import pallas as pl
from jax.experimental.pallas import tpu as pltpu

F32 = jnp.float32
BF16 = jnp.bfloat16

D_MODEL = 1024
DEPTH = 2
CHUNK = 64
NORM_EPS = 1e-6
RWKV_HEADS = 8
RWKV_HEAD_DIM = 64
RW = RWKV_HEADS * RWKV_HEAD_DIM
DECAY_LORA = 64
ICLR_LORA = 64
GATE_LORA = 128
DECAY_SCALE = math.exp(-0.5)
LNX_EPS = 64e-5
MLA_HEADS = 8
Q_LORA = 384
KV_LORA = 256
QK_NOPE = 64
QK_ROPE = 32
V_HEAD = 64
ROPE_THETA = 10000.0
N_EXPERTS = 8

LANE = 128
HEAD_PAD = 128
PS_W = 3 * RW + 3 * LANE
MLA_W = Q_LORA + KV_LORA + 2 * LANE
GATE_W = 2 * D_MODEL
HALF = 256
VMEM_LIMIT = 56 * 1024 * 1024


def _sigmoid(x):
    return 1.0 / (1.0 + jnp.exp(-x))


def _dot(a, b):
    return jnp.dot(a, b, preferred_element_type=F32)


def _dot_nt(a, b):
    return lax.dot_general(a, b, (((1,), (1,)), ((), ())), preferred_element_type=F32)


def _split2(x):
    hi = x.astype(BF16)
    lo = (x - hi.astype(F32)).astype(BF16)
    return hi, lo


def _dot3(a, b):
    ah, al = _split2(a)
    bh, bl = _split2(b)
    return _dot(ah, bh) + _dot(al, bh) + _dot(ah, bl)


def _params(sem):
    return pltpu.CompilerParams(dimension_semantics=sem, vmem_limit_bytes=VMEM_LIMIT)


def _ada_kernel(c_ref, w_ref, b_ref, o_ref):
    c = c_ref[...]
    o_ref[...] = _dot3(c * _sigmoid(c), w_ref[...]) + b_ref[...]


def _ada_call(c, ada_w, ada_b):
    n = ada_w.shape[0]
    nb = c.shape[0]
    return pl.pallas_call(
        _ada_kernel,
        grid=(n, 3),
        in_specs=[
            pl.BlockSpec((nb, D_MODEL), lambda i, j: (0, 0)),
            pl.BlockSpec((None, D_MODEL, D_MODEL), lambda i, j: (i, 0, j)),
            pl.BlockSpec((None, 1, D_MODEL), lambda i, j: (i, 0, j)),
        ],
        out_specs=pl.BlockSpec((None, nb, D_MODEL), lambda i, j: (i, 0, j)),
        out_shape=jax.ShapeDtypeStruct((n, nb, 3 * D_MODEL), F32),
        compiler_params=_params(("arbitrary", "arbitrary")),
        name="ada",
    )(c, ada_w, ada_b)


def _norm_mod(x, g, scale, shift):
    var = jnp.mean(x * x, axis=-1, keepdims=True)
    return (x * lax.rsqrt(var + NORM_EPS) * g) * (1.0 + scale) + shift


def _inproj_kernel(x_ref, shift_ref, scale_ref, g_ref, w_ref, mu_ref,
                   ps_ref, mla_ref, gates_ref, carry_ref, *, tiles_per_seq, tm):
    i = pl.program_id(0)
    h = _norm_mod(x_ref[...], g_ref[...], scale_ref[...], shift_ref[...]).astype(BF16)

    @pl.when(i % tiles_per_seq == 0)
    def _():
        carry_ref[...] = jnp.zeros_like(carry_ref)

    p = _dot(h, w_ref[:, 0:PS_W])
    prev = pltpu.roll(p, 1, axis=0)
    row = lax.broadcasted_iota(jnp.int32, p.shape, 0)
    prev = jnp.where(row == 0, carry_ref[0:1, :], prev)
    carry_ref[0:1, :] = p[tm - 1:tm, :]
    ps_ref[...] = p + (prev - p) * mu_ref[...]
    mla_ref[...] = _dot(h, w_ref[:, PS_W:PS_W + MLA_W])
    gates_ref[...] = _sigmoid(_dot(h, w_ref[:, PS_W + MLA_W:])).astype(BF16)


def _inproj_call(x2, shift, scale, g, w_pad, mu_pad, seq, tm=256):
    t = x2.shape[0]
    tiles_per_seq = seq // tm
    bmap = lambda i: (i // tiles_per_seq, 0, 0)
    wtot = PS_W + MLA_W + GATE_W
    return pl.pallas_call(
        functools.partial(_inproj_kernel, tiles_per_seq=tiles_per_seq, tm=tm),
        grid=(t // tm,),
        in_specs=[
            pl.BlockSpec((tm, D_MODEL), lambda i: (i, 0)),
            pl.BlockSpec((None, 1, D_MODEL), bmap),
            pl.BlockSpec((None, 1, D_MODEL), bmap),
            pl.BlockSpec((1, D_MODEL), lambda i: (0, 0)),
            pl.BlockSpec((D_MODEL, wtot), lambda i: (0, 0)),
            pl.BlockSpec((1, PS_W), lambda i: (0, 0)),
        ],
        out_specs=[
            pl.BlockSpec((tm, PS_W), lambda i: (i, 0)),
            pl.BlockSpec((tm, MLA_W), lambda i: (i, 0)),
            pl.BlockSpec((tm, GATE_W), lambda i: (i, 0)),
        ],
        out_shape=[
            jax.ShapeDtypeStruct((t, PS_W), F32),
            jax.ShapeDtypeStruct((t, MLA_W), F32),
            jax.ShapeDtypeStruct((t, GATE_W), BF16),
        ],
        scratch_shapes=[pltpu.VMEM((8, PS_W), F32)],
        compiler_params=_params(("arbitrary",)),
        name="inproj",
    )(x2, shift, scale, g, w_pad, mu_pad)


def _blockdiag(xb, bdmask):
    return jnp.tile(xb, (HALF // CHUNK, 1)) * bdmask


def _bd_nn(a, b, bdmask):
    ah, al = _split2(a)
    bh, bl = _split2(b)
    r = _dot(jnp.concatenate([ah, al], axis=0), _blockdiag(bh, bdmask))
    n = a.shape[0]
    return r[:n] + r[n:] + _dot(ah, _blockdiag(bl, bdmask))


def _bd_nt(a, b, bdmask):
    ah, al = _split2(a)
    bh, bl = _split2(b)
    r = _dot_nt(jnp.concatenate([ah, al], axis=0), _blockdiag(bh, bdmask))
    n = a.shape[0]
    return r[:n] + r[n:] + _dot_nt(ah, _blockdiag(bl, bdmask))


def _bd_tn(a, b, bdmask_f):
    full = _dot3(a.T, b) * bdmask_f
    return jnp.sum(full.reshape(HALF // CHUNK, CHUNK, HALF), axis=0)


def _halves(fn, *xs):
    outs = [fn(*[x[:, h * HALF:(h + 1) * HALF] for x in xs]) for h in range(RW // HALF)]
    return jnp.concatenate(outs, axis=1)


def _rwkv_kernel(ps_ref, w0_ref, a0_ref, kk_ref, ka_ref, rk_ref, lg_ref, lb_ref,
                 wup_ref, aup_ref, gup_ref, out_ref, s_ref, *, nc):
    @pl.when(pl.program_id(1) == 0)
    def _():
        s_ref[...] = jnp.zeros_like(s_ref)

    C = CHUNK
    r_i = lax.broadcasted_iota(jnp.int32, (HALF, HALF), 0)
    c_i = lax.broadcasted_iota(jnp.int32, (HALF, HALF), 1)
    bdsame = (r_i // C) == (c_i // C)
    bdmask = bdsame.astype(BF16)
    bdmask_f = bdsame.astype(F32)
    t_i = lax.broadcasted_iota(jnp.int32, (C, HALF), 0)
    i_i = lax.broadcasted_iota(jnp.int32, (C, HALF), 1) % C
    strict = i_i < t_i
    incl = i_i <= t_i
    eye_l = (i_i == t_i).astype(F32)
    tr = lax.broadcasted_iota(jnp.int32, (C, C), 0)
    tc = lax.broadcasted_iota(jnp.int32, (C, C), 1)
    tril = (tc <= tr).astype(BF16)

    def headsum(x):
        def one(xh):
            hi, lo = _split2(xh)
            return _dot(hi, bdmask) + _dot(lo, bdmask)
        return _halves(one, x)

    def cumsum_rows(x):
        hi = x.astype(BF16)
        r1 = x - hi.astype(F32)
        mid = r1.astype(BF16)
        lo = (r1 - mid.astype(F32)).astype(BF16)
        return _dot(tril, hi) + _dot(tril, mid) + _dot(tril, lo)

    for c in range(nc):
        rows = slice(c * C, (c + 1) * C)
        p_r = ps_ref[rows, 0:RW]
        p_k = ps_ref[rows, RW:2 * RW]
        p_v = ps_ref[rows, 2 * RW:3 * RW]
        p_w = ps_ref[rows, 3 * RW:3 * RW + LANE]
        p_a = ps_ref[rows, 3 * RW + LANE:3 * RW + 2 * LANE]
        p_g = ps_ref[rows, 3 * RW + 2 * LANE:3 * RW + 3 * LANE]

        logw = -DECAY_SCALE * _sigmoid(w0_ref[...] + _dot3(jnp.tanh(p_w), wup_ref[...]))
        iclr = _sigmoid(a0_ref[...] + _dot3(p_a, aup_ref[...]))
        gate = _dot3(_sigmoid(p_g), gup_ref[...])
        kk = p_k * kk_ref[...]
        kk = kk * lax.rsqrt(headsum(kk * kk) + 1e-12)
        k2 = p_k * (1.0 + (iclr - 1.0) * ka_ref[...])
        a_v = -kk
        b_v = kk * iclr

        cum = cumsum_rows(logw)
        cum_c = cum[C - 1:C, :]
        e_neg = jnp.exp(-cum)
        e_rem = jnp.exp(cum_c - cum)
        a_t = a_v * jnp.exp(cum - logw)
        b_t = b_v * e_neg
        k_t = k2 * e_neg
        r_t = p_r * jnp.exp(cum)
        b_g = b_v * e_rem
        k_g = k2 * e_rem
        gam_c = jnp.exp(cum_c)

        def chunk_half(a_t, b_t, k_t, r_t, b_g, k_g, v, gam_c):
            ar = jnp.concatenate([a_t, r_t], axis=0)
            sb = _bd_nt(ar, b_t, bdmask)
            sk = _bd_nt(ar, k_t, bdmask)
            l_ab = jnp.where(strict, sb[:C], 0.0)
            q_b = jnp.where(incl, sb[C:], 0.0)
            m_ak = jnp.where(strict, sk[:C], 0.0)
            q_k = jnp.where(incl, sk[C:], 0.0)
            t_m = eye_l + l_ab
            pw = l_ab
            for _ in range(5):
                pw = _bd_nn(pw, pw, bdmask)
                t_m = t_m + _bd_nn(t_m, pw, bdmask)
            mq = _bd_nn(jnp.concatenate([m_ak, q_k], axis=0), v, bdmask)
            mv, qkv = mq[:C], mq[C:]
            a_h = _bd_nn(t_m, a_t, bdmask)
            u0 = _bd_nn(t_m, mv, bdmask)
            r_h = r_t + _bd_nn(q_b, a_h, bdmask)
            y0 = qkv + _bd_nn(q_b, u0, bdmask)
            g_l = _bd_tn(a_h, b_g, bdmask_f) + eye_l * gam_c
            h_l = _bd_tn(jnp.concatenate([u0, v], axis=0),
                         jnp.concatenate([b_g, k_g], axis=0), bdmask_f)
            return jnp.concatenate([r_h, y0, g_l, h_l], axis=0)

        pre = _halves(chunk_half, a_t, b_t, k_t, r_t, b_g, k_g, p_v, gam_c)
        r_h, y0, g_l, h_l = pre[0:C], pre[C:2 * C], pre[2 * C:3 * C], pre[3 * C:4 * C]

        s_prev = s_ref[...]
        y = _halves(lambda rh, s: _bd_nt(rh, s, bdmask), r_h, s_prev) + y0
        s_ref[...] = _halves(lambda s, g: _bd_nn(s, g, bdmask), s_prev, g_l) + h_l

        inv_n = 1.0 / RWKV_HEAD_DIM
        mu = headsum(y) * inv_n
        d = y - mu
        var = headsum(d * d) * inv_n
        yn = d * lax.rsqrt(var + LNX_EPS) * lg_ref[...] + lb_ref[...]
        bonus = headsum(p_r * k2 * rk_ref[...]) * p_v
        out_ref[rows, :] = ((yn + bonus) * gate).astype(BF16)


def _rwkv_call(ps, vecs, wup, aup, gup, batch, seq, nc=2):
    t = ps.shape[0]
    rows = nc * CHUNK
    steps = seq // rows
    vec_spec = pl.BlockSpec((1, RW), lambda b, j: (0, 0))
    lora_spec = pl.BlockSpec((LANE, RW), lambda b, j: (0, 0))
    return pl.pallas_call(
        functools.partial(_rwkv_kernel, nc=nc),
        grid=(batch, steps),
        in_specs=[pl.BlockSpec((rows, PS_W), lambda b, j: (b * steps + j, 0))]
                 + [vec_spec] * 7 + [lora_spec] * 3,
        out_specs=pl.BlockSpec((rows, RW), lambda b, j: (b * steps + j, 0)),
        out_shape=jax.ShapeDtypeStruct((t, RW), BF16),
        scratch_shapes=[pltpu.VMEM((CHUNK, RW), F32)],
        compiler_params=_params(("arbitrary", "arbitrary")),
        name="rwkv",
    )(ps, *vecs, wup, aup, gup)


def _mlaprep_kernel(mla_ref, cos_ref, sin_ref, qn_ref, kvn_ref, wq_ref, wqs_ref, wk_ref, wv_ref,
                    q_ref, k_ref, v_ref):
    def rms(x, g):
        return (x * lax.rsqrt(jnp.mean(x * x, axis=-1, keepdims=True) + NORM_EPS) * g).astype(BF16)

    cos = cos_ref[...]
    sin = sin_ref[...]
    lane = lax.broadcasted_iota(jnp.int32, cos.shape, 1)
    cos_q = jnp.tile(cos + (lane < QK_NOPE).astype(F32), (1, MLA_HEADS))
    sin_q = jnp.tile(sin, (1, MLA_HEADS))
    qn = rms(mla_ref[:, 0:Q_LORA], qn_ref[...])
    q = _dot(qn, wq_ref[...]) * cos_q + _dot(qn, wqs_ref[...]) * sin_q
    q_ref[...] = (q * ((QK_NOPE + QK_ROPE) ** -0.5)).astype(BF16)
    kvn = rms(mla_ref[:, Q_LORA:Q_LORA + KV_LORA], kvn_ref[...])
    o = Q_LORA + KV_LORA
    k_pe = mla_ref[:, o:o + LANE] * cos + mla_ref[:, o + LANE:o + 2 * LANE] * sin
    k_ref[...] = (_dot(kvn, wk_ref[...]) + jnp.tile(k_pe, (1, MLA_HEADS))).astype(BF16)
    v_ref[...] = _dot(kvn, wv_ref[...]).astype(BF16)


def _mlaprep_call(mla, cos, sin, qn, kvn, wq, wqs, wk, wv, tm=512):
    t = mla.shape[0]
    hw = MLA_HEADS * HEAD_PAD
    full = lambda shape: pl.BlockSpec(shape, lambda i: (0, 0))
    row = lambda w: pl.BlockSpec((tm, w), lambda i: (i, 0))
    return pl.pallas_call(
        _mlaprep_kernel,
        grid=(t // tm,),
        in_specs=[row(MLA_W), row(LANE), row(LANE), full((1, Q_LORA)), full((1, KV_LORA)),
                  full((Q_LORA, hw)), full((Q_LORA, hw)), full((KV_LORA, hw)), full((KV_LORA, hw))],
        out_specs=[row(hw)] * 3,
        out_shape=[jax.ShapeDtypeStruct((t, hw), BF16)] * 3,
        compiler_params=_params(("arbitrary",)),
        name="mlaprep",
    )(mla, cos, sin, qn, kvn, wq, wqs, wk, wv)


def _attn_kernel(q_ref, k_ref, v_ref, o_ref, *, seq, tq):
    r_i = lax.broadcasted_iota(jnp.int32, (tq, tq), 0) // CHUNK
    c_i = lax.broadcasted_iota(jnp.int32, (tq, tq), 1) // CHUNK
    mask = c_i <= r_i
    for qi in range(seq // tq):
        lo = qi * tq
        q = q_ref[lo:lo + tq, :]
        sd = jnp.where(mask, _dot_nt(q, k_ref[lo:lo + tq, :]), -1e30)
        m = jnp.max(sd, axis=-1, keepdims=True)
        if qi > 0:
            so = _dot_nt(q, k_ref[0:lo, :])
            m = jnp.maximum(m, jnp.max(so, axis=-1, keepdims=True))
        pd = jnp.exp(sd - m)
        l = jnp.sum(pd, axis=-1, keepdims=True)
        acc = _dot(pd.astype(BF16), v_ref[lo:lo + tq, :])
        if qi > 0:
            po = jnp.exp(so - m)
            l = l + jnp.sum(po, axis=-1, keepdims=True)
            acc = acc + _dot(po.astype(BF16), v_ref[0:lo, :])
        o_ref[lo:lo + tq, :] = (acc / l).astype(BF16)


def _attn_call(q, k, v, batch, seq, tq=256):
    spec = pl.BlockSpec((seq, HEAD_PAD), lambda b, h: (b, h))
    return pl.pallas_call(
        functools.partial(_attn_kernel, seq=seq, tq=tq),
        grid=(batch, MLA_HEADS),
        in_specs=[spec] * 3,
        out_specs=spec,
        out_shape=jax.ShapeDtypeStruct(q.shape, BF16),
        compiler_params=_params(("arbitrary", "arbitrary")),
        name="attn",
    )(q, k, v)


def _mixout_kernel(*refs, router):
    if router:
        (rw_ref, o_ref, gates_ref, x_ref, gate_ref, wr_ref, wm_ref, wo_ref,
         g2_ref, shift2_ref, scale2_ref, rwt_ref, rb_ref,
         xo_ref, h2_ref, ti_ref, tp_ref) = refs
    else:
        (rw_ref, o_ref, gates_ref, x_ref, gate_ref, wr_ref, wm_ref, wo_ref,
         g2_ref, shift2_ref, scale2_ref, xo_ref, h2_ref) = refs
    ya = _dot(rw_ref[...], wr_ref[...])
    yb = _dot(o_ref[...], wm_ref[...])
    merged = gates_ref[:, 0:D_MODEL].astype(F32) * ya + gates_ref[:, D_MODEL:].astype(F32) * yb
    xn = x_ref[...] + gate_ref[...] * _dot(merged.astype(BF16), wo_ref[...])
    xo_ref[...] = xn
    h2 = _norm_mod(xn, g2_ref[...], scale2_ref[...], shift2_ref[...])
    h2_ref[...] = h2.astype(h2_ref.dtype)
    if router:
        hh, hl = _split2(h2)
        wh, wl = _split2(rwt_ref[...])
        logits = _dot_nt(wh, hh) + _dot_nt(wl, hh) + _dot_nt(wh, hl) + rb_ref[...]
        e_i = lax.broadcasted_iota(jnp.int32, logits.shape, 0)
        m1 = jnp.max(logits, axis=0, keepdims=True)
        i1 = jnp.min(jnp.where(logits == m1, e_i, N_EXPERTS), axis=0, keepdims=True)
        rest = jnp.where(e_i == i1, -jnp.inf, logits)
        m2 = jnp.max(rest, axis=0, keepdims=True)
        i2 = jnp.min(jnp.where(rest == m2, e_i, N_EXPERTS), axis=0, keepdims=True)
        e2 = jnp.exp(m2 - m1)
        den = 1.0 + e2
        ti_ref[0:1, :] = i1
        ti_ref[1:2, :] = i2
        tp_ref[0:1, :] = 1.0 / den
        tp_ref[1:2, :] = e2 / den


def _mixout_call(rw, o, gates, x2, gate, wr, wm, wo, g2, shift2, scale2, seq, router=None, tm=512):
    t = x2.shape[0]
    tiles_per_seq = seq // tm
    bmap = lambda i: (i // tiles_per_seq, 0, 0)
    full = lambda shape: pl.BlockSpec(shape, lambda i: (0, 0))
    row = lambda w: pl.BlockSpec((tm, w), lambda i: (i, 0))
    mod = pl.BlockSpec((None, 1, D_MODEL), bmap)
    in_specs = [row(RW), row(MLA_HEADS * HEAD_PAD), row(GATE_W), row(D_MODEL), mod,
                full(wr.shape), full(wm.shape), full(wo.shape), full((1, D_MODEL)), mod, mod]
    args = [rw, o, gates, x2, gate, wr, wm, wo, g2, shift2, scale2]
    out_specs = [row(D_MODEL), row(D_MODEL)]
    out_shape = [jax.ShapeDtypeStruct((t, D_MODEL), F32),
                 jax.ShapeDtypeStruct((t, D_MODEL), BF16 if router is None else F32)]
    if router is not None:
        in_specs += [full((N_EXPERTS, D_MODEL)), full((N_EXPERTS, 1))]
        args += list(router)
        out_specs += [pl.BlockSpec((2, tm), lambda i: (0, i))] * 2
        out_shape += [jax.ShapeDtypeStruct((2, t), jnp.int32), jax.ShapeDtypeStruct((2, t), F32)]
    return pl.pallas_call(
        functools.partial(_mixout_kernel, router=router is not None),
        grid=(t // tm,),
        in_specs=in_specs,
        out_specs=out_specs,
        out_shape=out_shape,
        compiler_params=_params(("arbitrary",)),
        name="mixout",
    )(*args)


def _swiglu_step(h, w1_ref, w3_ref, w2_ref, y_ref, acc_ref):
    f = pl.program_id(1)

    @pl.when(f == 0)
    def _():
        acc_ref[...] = jnp.zeros_like(acc_ref)

    a1 = _dot(h, w1_ref[...])
    a3 = _dot(h, w3_ref[...])
    act = (a1 * _sigmoid(a1)) * a3
    acc_ref[...] += _dot(act.astype(BF16), w2_ref[...])

    @pl.when(f == pl.num_programs(1) - 1)
    def _():
        y_ref[...] = acc_ref[...]


def _ffn_kernel(h_ref, w1_ref, w3_ref, w2_ref, y_ref, acc_ref):
    _swiglu_step(h_ref[...], w1_ref, w3_ref, w2_ref, y_ref, acc_ref)


def _ffn_call(h, w1, w3, w2, tm=512, tf=1408):
    r = h.shape[0]
    ff = w1.shape[1]
    return pl.pallas_call(
        _ffn_kernel,
        grid=(r // tm, ff // tf),
        in_specs=[
            pl.BlockSpec((tm, D_MODEL), lambda i, f: (i, 0)),
            pl.BlockSpec((D_MODEL, tf), lambda i, f: (0, f)),
            pl.BlockSpec((D_MODEL, tf), lambda i, f: (0, f)),
            pl.BlockSpec((tf, D_MODEL), lambda i, f: (f, 0)),
        ],
        out_specs=pl.BlockSpec((tm, D_MODEL), lambda i, f: (i, 0)),
        out_shape=jax.ShapeDtypeStruct((r, D_MODEL), F32),
        scratch_shapes=[pltpu.VMEM((tm, D_MODEL), F32)],
        compiler_params=_params(("arbitrary", "arbitrary")),
        name="ffn",
    )(h, w1, w3, w2)


MOE_TM = 512
DMA_UNROLL = 8


def _route(top_i, tm):
    t = top_i.shape[1]
    e_flat = top_i.reshape(-1)
    oh = (e_flat[:, None] == jnp.arange(N_EXPERTS, dtype=jnp.int32)[None, :]).astype(jnp.int32)
    csum = jnp.cumsum(oh, axis=0)
    rank = jnp.sum(csum * oh, axis=1) - 1
    counts = csum[-1]
    padded = ((counts + tm - 1) // tm) * tm
    ends = jnp.cumsum(padded)
    pos = jnp.sum(oh * (ends - padded)[None, :], axis=1) + rank
    n_tiles = (2 * t) // tm + N_EXPERTS
    starts = jnp.arange(n_tiles, dtype=jnp.int32) * tm
    tile_e = jnp.sum((starts[:, None] >= ends[None, :]).astype(jnp.int32), axis=1)
    return pos.astype(jnp.int32), jnp.minimum(tile_e, N_EXPERTS - 1), (ends[-1:] // tm).astype(jnp.int32)


def _dispatch_kernel(pos_ref, h_ref, xs_in_ref, xs_ref, sem, *, t, td):
    del xs_in_ref
    base = pl.program_id(0) * td

    def body(j, carry):
        for u in range(DMA_UNROLL):
            tok = base + j * DMA_UNROLL + u
            for k in range(2):
                dst = pos_ref[k * t + tok]
                pltpu.make_async_copy(h_ref.at[pl.ds(tok, 1)], xs_ref.at[pl.ds(dst, 1)], sem).start()
        return carry

    lax.fori_loop(0, td // DMA_UNROLL, body, 0)
    for k in range(2):
        pltpu.make_async_copy(h_ref.at[pl.ds(0, td)], xs_ref.at[pl.ds(0, td)], sem).wait()


def _dispatch_call(pos, h, n_rows, td=1024):
    t = h.shape[0]
    any_spec = pl.BlockSpec(memory_space=pl.ANY)
    return pl.pallas_call(
        functools.partial(_dispatch_kernel, t=t, td=td),
        grid_spec=pltpu.PrefetchScalarGridSpec(
            num_scalar_prefetch=1, grid=(t // td,),
            in_specs=[any_spec, any_spec], out_specs=any_spec,
            scratch_shapes=[pltpu.SemaphoreType.DMA(())]),
        out_shape=jax.ShapeDtypeStruct((n_rows, D_MODEL), h.dtype),
        input_output_aliases={2: 0},
        compiler_params=_params(("arbitrary",)),
        name="dispatch",
    )(pos, h, jnp.zeros((n_rows, D_MODEL), h.dtype))


def _ffn_routed_kernel(te_ref, nu_ref, x_ref, w1_ref, w3_ref, w2_ref, y_ref, acc_ref):
    del te_ref
    used = pl.program_id(0) < nu_ref[0]

    @pl.when(used)
    def _():
        _swiglu_step(x_ref[...].astype(BF16), w1_ref, w3_ref, w2_ref, y_ref, acc_ref)

    @pl.when(jnp.logical_not(used))
    def _():
        y_ref[...] = jnp.zeros_like(y_ref)


def _ffn_routed_call(tile_e, n_used, xs, w1, w3, w2, tm=MOE_TM, tf=1792):
    r = xs.shape[0]
    ff = w1.shape[2]
    return pl.pallas_call(
        _ffn_routed_kernel,
        grid_spec=pltpu.PrefetchScalarGridSpec(
            num_scalar_prefetch=2, grid=(r // tm, ff // tf),
            in_specs=[
                pl.BlockSpec((tm, D_MODEL), lambda i, f, te, nu: (i, 0)),
                pl.BlockSpec((None, D_MODEL, tf), lambda i, f, te, nu: (te[i], 0, f)),
                pl.BlockSpec((None, D_MODEL, tf), lambda i, f, te, nu: (te[i], 0, f)),
                pl.BlockSpec((None, tf, D_MODEL), lambda i, f, te, nu: (te[i], f, 0)),
            ],
            out_specs=pl.BlockSpec((tm, D_MODEL), lambda i, f, te, nu: (i, 0)),
            scratch_shapes=[pltpu.VMEM((tm, D_MODEL), F32)]),
        out_shape=jax.ShapeDtypeStruct((r, D_MODEL), F32),
        compiler_params=_params(("arbitrary", "arbitrary")),
        name="ffn_routed",
    )(tile_e, n_used, xs, w1, w3, w2)


def _combine_kernel(pos_ref, x_ref, p_ref, gate_ref, gf_ref, y_ref, o_ref, buf, sem, *, t, tc, final):
    i = pl.program_id(0)

    def row_copy(step, slot, r, k):
        src = pos_ref[k * t + step * tc + r]
        return pltpu.make_async_copy(y_ref.at[pl.ds(src, 1)], buf.at[slot, k, pl.ds(r, 1)], sem.at[slot])

    def issue(step, slot):
        def body(j, carry):
            for u in range(DMA_UNROLL):
                for k in range(2):
                    row_copy(step, slot, j * DMA_UNROLL + u, k).start()
            return carry
        lax.fori_loop(0, tc // DMA_UNROLL, body, 0)

    @pl.when(i == 0)
    def _():
        issue(0, 0)

    @pl.when(i + 1 < pl.num_programs(0))
    def _():
        issue(i + 1, (i + 1) % 2)

    slot = i % 2
    for k in range(2):
        pltpu.make_async_copy(y_ref.at[pl.ds(0, tc)], buf.at[slot, k], sem.at[slot]).wait()
    p = p_ref[...]
    xn = x_ref[...] + gate_ref[...] * (p[:, 0:1] * buf[slot, 0] + p[:, 1:2] * buf[slot, 1])
    if final:
        xn = xn * lax.rsqrt(jnp.mean(xn * xn, axis=-1, keepdims=True) + NORM_EPS) * gf_ref[...]
    o_ref[...] = xn


def _combine_call(pos, x2, probs, gate, gf, y, seq, final, tc=256):
    t = x2.shape[0]
    tiles_per_seq = seq // tc
    row = pl.BlockSpec((tc, D_MODEL), lambda i, pos: (i, 0))
    return pl.pallas_call(
        functools.partial(_combine_kernel, t=t, tc=tc, final=final),
        grid_spec=pltpu.PrefetchScalarGridSpec(
            num_scalar_prefetch=1, grid=(t // tc,),
            in_specs=[row, pl.BlockSpec((tc, 2), lambda i, pos: (i, 0)),
                      pl.BlockSpec((None, 1, D_MODEL), lambda i, pos: (i // tiles_per_seq, 0, 0)),
                      pl.BlockSpec((1, D_MODEL), lambda i, pos: (0, 0)),
                      pl.BlockSpec(memory_space=pl.ANY)],
            out_specs=row,
            scratch_shapes=[pltpu.VMEM((2, 2, tc, D_MODEL), F32), pltpu.SemaphoreType.DMA((2,))]),
        out_shape=jax.ShapeDtypeStruct((t, D_MODEL), F32),
        compiler_params=_params(("arbitrary",)),
        name="combine",
    )(pos, x2, probs, gate, gf, y)


def _resid_kernel(x_ref, y_ref, gate_ref, gf_ref, o_ref, *, final):
    xn = x_ref[...] + gate_ref[...] * y_ref[...]
    if final:
        xn = xn * lax.rsqrt(jnp.mean(xn * xn, axis=-1, keepdims=True) + NORM_EPS) * gf_ref[...]
    o_ref[...] = xn


def _resid_call(x2, y, gate, gf, seq, final, tm=512):
    t = x2.shape[0]
    tiles_per_seq = seq // tm
    row = pl.BlockSpec((tm, D_MODEL), lambda i: (i, 0))
    return pl.pallas_call(
        functools.partial(_resid_kernel, final=final),
        grid=(t // tm,),
        in_specs=[row, row, pl.BlockSpec((None, 1, D_MODEL), lambda i: (i // tiles_per_seq, 0, 0)),
                  pl.BlockSpec((1, D_MODEL), lambda i: (0, 0))],
        out_specs=row,
        out_shape=jax.ShapeDtypeStruct((t, D_MODEL), F32),
        compiler_params=_params(("arbitrary",)),
        name="resid",
    )(x2, y, gate, gf)


def _pad_cols(w, n):
    return jnp.pad(w, ((0, 0), (0, n)))


def _layout_w_in(w):
    z64 = jnp.zeros((w.shape[0], 64), w.dtype)
    z32 = jnp.zeros((w.shape[0], 32), w.dtype)
    o = 3 * RW
    pw, pa, pg = w[:, o:o + 64], w[:, o + 64:o + 128], w[:, o + 128:o + 256]
    o = 3 * RW + 256
    qa, kva = w[:, o:o + Q_LORA], w[:, o + Q_LORA:o + Q_LORA + KV_LORA]
    o += Q_LORA + KV_LORA
    kr = w[:, o:o + QK_ROPE]
    kr_sw = jnp.concatenate([-kr[:, QK_ROPE // 2:], kr[:, :QK_ROPE // 2]], axis=1)
    gates = w[:, o + QK_ROPE:]
    return jnp.concatenate([w[:, :3 * RW], pw, z64, pa, z64, pg,
                            qa, kva, z64, kr, z32, z64, kr_sw, z32, gates], axis=1).astype(BF16)


def _layout_mu(mu):
    z = jnp.zeros((64,), mu.dtype)
    o = 3 * RW
    return jnp.concatenate([mu[:o + 64], z, mu[o + 64:o + 128], z, mu[o + 128:]])[None, :]


def _layout_wq(w_qb):
    hd = QK_NOPE + QK_ROPE
    w = w_qb.reshape(Q_LORA, MLA_HEADS, hd)
    half = QK_ROPE // 2
    x1, x2 = w[:, :, QK_NOPE:QK_NOPE + half], w[:, :, QK_NOPE + half:]
    zn = jnp.zeros((Q_LORA, MLA_HEADS, QK_NOPE), w.dtype)
    zp = jnp.zeros((Q_LORA, MLA_HEADS, HEAD_PAD - hd), w.dtype)
    wq = jnp.concatenate([w, zp], axis=2).reshape(Q_LORA, -1)
    wqs = jnp.concatenate([zn, -x2, x1, zp], axis=2).reshape(Q_LORA, -1)
    return wq.astype(BF16), wqs.astype(BF16)


def _layout_wkv(w_kvb):
    w = w_kvb.reshape(KV_LORA, MLA_HEADS, QK_NOPE + V_HEAD)
    z = jnp.zeros((KV_LORA, MLA_HEADS, HEAD_PAD - QK_NOPE), w.dtype)
    wk = jnp.concatenate([w[:, :, :QK_NOPE], z], axis=2).reshape(KV_LORA, -1)
    wv = jnp.concatenate([w[:, :, QK_NOPE:], z], axis=2).reshape(KV_LORA, -1)
    return wk.astype(BF16), wv.astype(BF16)


def _layout_mla_out(w):
    w = w.reshape(MLA_HEADS, V_HEAD, D_MODEL)
    z = jnp.zeros((MLA_HEADS, HEAD_PAD - V_HEAD, D_MODEL), w.dtype)
    return jnp.concatenate([w, z], axis=1).reshape(MLA_HEADS * HEAD_PAD, D_MODEL).astype(BF16)


def _rope_tables(positions):
    inv_freq = ROPE_THETA ** (-jnp.arange(0, QK_ROPE, 2, dtype=F32) / QK_ROPE)
    ang = positions.astype(F32).reshape(-1, 1) * inv_freq
    z = jnp.zeros((ang.shape[0], QK_NOPE), F32)
    zp = jnp.zeros((ang.shape[0], LANE - QK_NOPE - QK_ROPE), F32)
    cos, sin = jnp.cos(ang), jnp.sin(ang)
    return (jnp.concatenate([z, cos, cos, zp], axis=1), jnp.concatenate([z, sin, sin, zp], axis=1))


def kernel(x, c, positions, ada_w, ada_b, norm_mix, norm_ffn, norm_final, w_in, tshift_mu, w0, w_up, a0, a_up, g_up, k_k, k_a, r_k, lnx_g, lnx_b, rwkv_out, q_norm, w_qb, kv_norm, w_kvb, mla_out, w_o, ffn_w1, ffn_w3, ffn_w2, router_w, router_b, moe_w1, moe_w3, moe_w2):
    batch, seq, _ = x.shape
    t = batch * seq
    mods = _ada_call(c, ada_w.reshape(2 * DEPTH, D_MODEL, 3 * D_MODEL),
                     ada_b.reshape(2 * DEPTH, 1, 3 * D_MODEL))
    mods = mods.reshape(DEPTH, 2, batch, 3, 1, D_MODEL)
    cos, sin = _rope_tables(positions)
    x2 = x.reshape(t, D_MODEL)
    row = lambda v: v.reshape(1, -1)

    for l in range(DEPTH):
        shift, scale, gate = (mods[l, 0, :, j] for j in range(3))
        ps, mla, gates = _inproj_call(x2, shift, scale, row(norm_mix[l]), _layout_w_in(w_in[l]),
                                      _layout_mu(tshift_mu[l]), seq)
        vecs = [row(v[l]) for v in (w0, a0, k_k, k_a, r_k, lnx_g, lnx_b)]
        lora_pad = lambda w: jnp.pad(w, ((0, LANE - w.shape[0]), (0, 0)))
        rw = _rwkv_call(ps, vecs, lora_pad(w_up[l]), lora_pad(a_up[l]), g_up[l], batch, seq)
        wq, wqs = _layout_wq(w_qb[l])
        wk, wv = _layout_wkv(w_kvb[l])
        q, k, v = _mlaprep_call(mla, cos, sin, row(q_norm[l]), row(kv_norm[l]), wq, wqs, wk, wv)
        o = _attn_call(q, k, v, batch, seq)

        shift2, scale2, gate2 = (mods[l, 1, :, j] for j in range(3))
        moe = l % 2 == 1
        router = (router_w[l // 2].T, router_b[l // 2].reshape(N_EXPERTS, 1)) if moe else None
        outs = _mixout_call(rw, o, gates, x2, gate, rwkv_out[l].astype(BF16), _layout_mla_out(mla_out[l]),
                            w_o[l].astype(BF16), row(norm_ffn[l]), shift2, scale2, seq, router=router)
        final = l == DEPTH - 1
        if moe:
            x2, h2, top_i, top_p = outs
            pos, tile_e, n_used = _route(top_i, MOE_TM)
            xs = _dispatch_call(pos, h2, 2 * t + N_EXPERTS * MOE_TM)
            y = _ffn_routed_call(tile_e, n_used, xs, moe_w1[l // 2].astype(BF16),
                                 moe_w3[l // 2].astype(BF16), moe_w2[l // 2].astype(BF16))
            x2 = _combine_call(pos, x2, top_p.T, gate2, row(norm_final), y, seq, final)
        else:
            x2, h2 = outs
            y = _ffn_call(h2, ffn_w1[l // 2].astype(BF16), ffn_w3[l // 2].astype(BF16),
                          ffn_w2[l // 2].astype(BF16))
            x2 = _resid_call(x2, y, gate2, row(norm_final), seq, final)
    return x2.reshape(batch, seq, D_MODEL)
```

```python
import functools
import math

import jax
import jax.numpy as jnp
from jax import lax
from jax.experimental import pallas as pl
from jax.experimental.pallas import tpu as pltpu

F32 = jnp.float32
BF16 = jnp.bfloat16

D_MODEL = 1024
DEPTH = 2
CHUNK = 64
NORM_EPS = 1e-6
RWKV_HEADS = 8
RWKV_HEAD_DIM = 64
RW = RWKV_HEADS * RWKV_HEAD_DIM
DECAY_LORA = 64
ICLR_LORA = 64
GATE_LORA = 128
DECAY_SCALE = math.exp(-0.5)
LNX_EPS = 64e-5
MLA_HEADS = 8
Q_LORA = 384
KV_LORA = 256
QK_NOPE = 64
QK_ROPE = 32
V_HEAD = 64
ROPE_THETA = 10000.0
N_EXPERTS = 8

LANE = 128
HEAD_PAD = 128
PS_W = 3 * RW + 3 * LANE
MLA_W = Q_LORA + KV_LORA + 2 * LANE
GATE_W = 2 * D_MODEL
HALF = 256
VMEM_LIMIT = 56 * 1024 * 1024


def _sigmoid(x):
    return 1.0 / (1.0 + jnp.exp(-x))


def _dot(a, b):
    return jnp.dot(a, b, preferred_element_type=F32)


def _dot_nt(a, b):
    return lax.dot_general(a, b, (((1,), (1,)), ((), ())), preferred_element_type=F32)


def _split2(x):
    hi = x.astype(BF16)
    lo = (x - hi.astype(F32)).astype(BF16)
    return hi, lo


def _dot3(a, b):
    ah, al = _split2(a)
    bh, bl = _split2(b)
    return _dot(ah, bh) + _dot(al, bh) + _dot(ah, bl)


def _params(sem):
    return pltpu.CompilerParams(dimension_semantics=sem, vmem_limit_bytes=VMEM_LIMIT)


def _ada_kernel(c_ref, w_ref, b_ref, o_ref):
    c = c_ref[...]
    o_ref[...] = _dot3(c * _sigmoid(c), w_ref[...]) + b_ref[...]


def _ada_call(c, ada_w, ada_b):
    n = ada_w.shape[0]
    nb = c.shape[0]
    return pl.pallas_call(
        _ada_kernel,
        grid=(n, 3),
        in_specs=[
            pl.BlockSpec((nb, D_MODEL), lambda i, j: (0, 0)),
            pl.BlockSpec((None, D_MODEL, D_MODEL), lambda i, j: (i, 0, j)),
            pl.BlockSpec((None, 1, D_MODEL), lambda i, j: (i, 0, j)),
        ],
        out_specs=pl.BlockSpec((None, nb, D_MODEL), lambda i, j: (i, 0, j)),
        out_shape=jax.ShapeDtypeStruct((n, nb, 3 * D_MODEL), F32),
        compiler_params=_params(("arbitrary", "arbitrary")),
        name="ada",
    )(c, ada_w, ada_b)


def _norm_mod(x, g, scale, shift):
    var = jnp.mean(x * x, axis=-1, keepdims=True)
    return (x * lax.rsqrt(var + NORM_EPS) * g) * (1.0 + scale) + shift


def _inproj_kernel(x_ref, shift_ref, scale_ref, g_ref, w_ref, mu_ref,
                   ps_ref, mla_ref, gates_ref, carry_ref, *, tiles_per_seq, tm):
    i = pl.program_id(0)
    h = _norm_mod(x_ref[...], g_ref[...], scale_ref[...], shift_ref[...]).astype(BF16)

    @pl.when(i % tiles_per_seq == 0)
    def _():
        carry_ref[...] = jnp.zeros_like(carry_ref)

    p = _dot(h, w_ref[:, 0:PS_W])
    prev = pltpu.roll(p, 1, axis=0)
    row = lax.broadcasted_iota(jnp.int32, p.shape, 0)
    prev = jnp.where(row == 0, carry_ref[0:1, :], prev)
    carry_ref[0:1, :] = p[tm - 1:tm, :]
    ps_ref[...] = p + (prev - p) * mu_ref[...]
    mla_ref[...] = _dot(h, w_ref[:, PS_W:PS_W + MLA_W])
    gates_ref[...] = _sigmoid(_dot(h, w_ref[:, PS_W + MLA_W:])).astype(BF16)


def _inproj_call(x2, shift, scale, g, w_pad, mu_pad, seq, tm=256):
    t = x2.shape[0]
    tiles_per_seq = seq // tm
    bmap = lambda i: (i // tiles_per_seq, 0, 0)
    wtot = PS_W + MLA_W + GATE_W
    return pl.pallas_call(
        functools.partial(_inproj_kernel, tiles_per_seq=tiles_per_seq, tm=tm),
        grid=(t // tm,),
        in_specs=[
            pl.BlockSpec((tm, D_MODEL), lambda i: (i, 0)),
            pl.BlockSpec((None, 1, D_MODEL), bmap),
            pl.BlockSpec((None, 1, D_MODEL), bmap),
            pl.BlockSpec((1, D_MODEL), lambda i: (0, 0)),
            pl.BlockSpec((D_MODEL, wtot), lambda i: (0, 0)),
            pl.BlockSpec((1, PS_W), lambda i: (0, 0)),
        ],
        out_specs=[
            pl.BlockSpec((tm, PS_W), lambda i: (i, 0)),
            pl.BlockSpec((tm, MLA_W), lambda i: (i, 0)),
            pl.BlockSpec((tm, GATE_W), lambda i: (i, 0)),
        ],
        out_shape=[
            jax.ShapeDtypeStruct((t, PS_W), F32),
            jax.ShapeDtypeStruct((t, MLA_W), F32),
            jax.ShapeDtypeStruct((t, GATE_W), BF16),
        ],
        scratch_shapes=[pltpu.VMEM((8, PS_W), F32)],
        compiler_params=_params(("arbitrary",)),
        name="inproj",
    )(x2, shift, scale, g, w_pad, mu_pad)


def _blockdiag(xb, bdmask):
    return jnp.tile(xb, (HALF // CHUNK, 1)) * bdmask


def _bd_mm(dot, a, b, bdmask, passes):
    if passes == 1:
        return dot(a.astype(BF16), _blockdiag(b.astype(BF16), bdmask))
    ah, al = _split2(a)
    bh, bl = _split2(b)
    r = dot(jnp.concatenate([ah, al], axis=0), _blockdiag(bh, bdmask))
    n = a.shape[0]
    return r[:n] + r[n:] + dot(ah, _blockdiag(bl, bdmask))


def _bd_nn(a, b, bdmask, passes):
    return _bd_mm(_dot, a, b, bdmask, passes)


def _bd_nt(a, b, bdmask, passes):
    return _bd_mm(_dot_nt, a, b, bdmask, passes)


def _bd_tn(a, b, bdmask_f, passes):
    full = _dot3(a.T, b) if passes == 3 else _dot(a.T.astype(BF16), b.astype(BF16))
    return jnp.sum((full * bdmask_f).reshape(HALF // CHUNK, CHUNK, HALF), axis=0)


P_SCORE = 1
P_TINV = 1
P_APPLY = 1
P_GH = 1
P_STATE = 1


def _halves(fn, *xs):
    outs = [fn(*[x[:, h * HALF:(h + 1) * HALF] for x in xs]) for h in range(RW // HALF)]
    return jnp.concatenate(outs, axis=1)


def _rwkv_kernel(ps_ref, w0_ref, a0_ref, kk_ref, ka_ref, rk_ref, lg_ref, lb_ref,
                 wup_ref, aup_ref, gup_ref, out_ref, s_ref, *, nc):
    @pl.when(pl.program_id(1) == 0)
    def _():
        s_ref[...] = jnp.zeros_like(s_ref)

    C = CHUNK
    r_i = lax.broadcasted_iota(jnp.int32, (HALF, HALF), 0)
    c_i = lax.broadcasted_iota(jnp.int32, (HALF, HALF), 1)
    bdsame = (r_i // C) == (c_i // C)
    bdmask = bdsame.astype(BF16)
    bdmask_f = bdsame.astype(F32)
    t_i = lax.broadcasted_iota(jnp.int32, (C, HALF), 0)
    i_i = lax.broadcasted_iota(jnp.int32, (C, HALF), 1) % C
    strict = i_i < t_i
    incl = i_i <= t_i
    eye_l = (i_i == t_i).astype(F32)
    n = nc * C
    tr = lax.broadcasted_iota(jnp.int32, (n, n), 0)
    tc = lax.broadcasted_iota(jnp.int32, (n, n), 1)
    tril = ((tc <= tr) & (tc // C == tr // C)).astype(BF16)

    def headsum(x):
        def one(xh):
            hi, lo = _split2(xh)
            return _dot(hi, bdmask) + _dot(lo, bdmask)
        return _halves(one, x)

    def cumsum_rows(x):
        hi = x.astype(BF16)
        r1 = x - hi.astype(F32)
        mid = r1.astype(BF16)
        lo = (r1 - mid.astype(F32)).astype(BF16)
        return _dot(tril, hi) + _dot(tril, mid) + _dot(tril, lo)

    p_r = ps_ref[:, 0:RW]
    p_k = ps_ref[:, RW:2 * RW]
    p_v = ps_ref[:, 2 * RW:3 * RW]
    p_w = ps_ref[:, 3 * RW:3 * RW + LANE]
    p_a = ps_ref[:, 3 * RW + LANE:3 * RW + 2 * LANE]
    p_g = ps_ref[:, 3 * RW + 2 * LANE:3 * RW + 3 * LANE]
    logw = -DECAY_SCALE * _sigmoid(w0_ref[...] + _dot3(jnp.tanh(p_w), wup_ref[...]))
    iclr = _sigmoid(a0_ref[...] + _dot3(p_a, aup_ref[...]))
    gate = _dot3(_sigmoid(p_g), gup_ref[...])
    kk = p_k * kk_ref[...]
    k2 = p_k * (1.0 + (iclr - 1.0) * ka_ref[...])
    sums = headsum(jnp.concatenate([kk * kk, p_r * k2 * rk_ref[...]], axis=0))
    kk = kk * lax.rsqrt(sums[:n] + 1e-12)
    bonus = sums[n:] * p_v
    b_v = kk * iclr
    cum = cumsum_rows(logw)
    cum_c = jnp.concatenate(
        [jnp.broadcast_to(cum[(c + 1) * C - 1:(c + 1) * C, :], (C, RW)) for c in range(nc)], axis=0)
    e_neg = jnp.exp(-cum)
    e_rem = jnp.exp(cum_c - cum)
    a_all = -kk * jnp.exp(cum - logw)
    b_all = b_v * e_neg
    k_all = k2 * e_neg
    r_all = p_r * jnp.exp(cum)
    bg_all = b_v * e_rem
    kg_all = k2 * e_rem
    gam_all = jnp.exp(cum_c)

    nh = RW // HALF
    units = [(c, h) for c in range(nc) for h in range(nh)]

    def cut(x, u):
        c, h = u
        return x[c * C:(c + 1) * C, h * HALF:(h + 1) * HALF]

    each = lambda fn, *lists: [fn(*xs) for xs in zip(*lists)]
    stack = lambda x, y: jnp.concatenate([x, y], axis=0)
    top = lambda x: x[:C]
    bot = lambda x: x[C:]
    nn = lambda p: (lambda a, b: _bd_nn(a, b, bdmask, p))

    a_t, b_t, k_t, r_t, b_g, k_g, v_u = ([cut(x, u) for u in units]
                                         for x in (a_all, b_all, k_all, r_all, bg_all, kg_all, p_v))
    gam = [gam_all[c * C:c * C + 1, h * HALF:(h + 1) * HALF] for c, h in units]
    ar = each(stack, a_t, r_t)
    sb = each(lambda x, y: _bd_nt(x, y, bdmask, P_SCORE), ar, b_t)
    sk = each(lambda x, y: _bd_nt(x, y, bdmask, P_SCORE), ar, k_t)
    l_ab = [jnp.where(strict, top(x), 0.0) for x in sb]
    q_b = [jnp.where(incl, bot(x), 0.0) for x in sb]
    m_ak = [jnp.where(strict, top(x), 0.0) for x in sk]
    q_k = [jnp.where(incl, bot(x), 0.0) for x in sk]
    t_m = [eye_l + x for x in l_ab]
    pw = each(nn(P_TINV), l_ab, l_ab)
    for _ in range(4):
        tp = each(nn(P_TINV), each(stack, t_m, pw), pw)
        t_m = each(lambda t, x: t + top(x), t_m, tp)
        pw = [bot(x) for x in tp]
    t_m = each(lambda t, x: t + x, t_m, each(nn(P_TINV), t_m, pw))
    mq = each(nn(P_APPLY), each(stack, m_ak, q_k), v_u)
    a_h = each(nn(P_APPLY), t_m, a_t)
    u0 = each(nn(P_APPLY), t_m, [top(x) for x in mq])
    r_h = each(lambda r, x: r + x, r_t, each(nn(P_APPLY), q_b, a_h))
    y0 = each(lambda m, x: bot(m) + x, mq, each(nn(P_APPLY), q_b, u0))
    g_l = each(lambda a, b, g: _bd_tn(a, b, bdmask_f, P_GH) + eye_l * g, a_h, b_g, gam)
    h_l = each(lambda u, v, b, k: _bd_tn(stack(u, v), stack(b, k), bdmask_f, P_GH), u0, v_u, b_g, k_g)

    s_all = s_ref[...]
    s_cur = [s_all[:, h * HALF:(h + 1) * HALF] for h in range(nh)]
    ys = []
    for c in range(nc):
        idx = [c * nh + h for h in range(nh)]
        yc = [_bd_nt(r_h[i], s, bdmask, P_STATE) + y0[i] for i, s in zip(idx, s_cur)]
        s_cur = [_bd_nn(s, g_l[i], bdmask, P_STATE) + h_l[i] for i, s in zip(idx, s_cur)]
        ys.append(jnp.concatenate(yc, axis=1))
    s_ref[...] = jnp.concatenate(s_cur, axis=1)
    y = jnp.concatenate(ys, axis=0)
    inv_n = 1.0 / RWKV_HEAD_DIM
    mu = headsum(y) * inv_n
    d = y - mu
    var = headsum(d * d) * inv_n
    yn = d * lax.rsqrt(var + LNX_EPS) * lg_ref[...] + lb_ref[...]
    out_ref[...] = ((yn + bonus) * gate).astype(BF16)


def _rwkv_call(ps, vecs, wup, aup, gup, batch, seq, nc=4):
    t = ps.shape[0]
    rows = nc * CHUNK
    steps = seq // rows
    vec_spec = pl.BlockSpec((1, RW), lambda b, j: (0, 0))
    lora_spec = pl.BlockSpec((LANE, RW), lambda b, j: (0, 0))
    return pl.pallas_call(
        functools.partial(_rwkv_kernel, nc=nc),
        grid=(batch, steps),
        in_specs=[pl.BlockSpec((rows, PS_W), lambda b, j: (b * steps + j, 0))]
                 + [vec_spec] * 7 + [lora_spec] * 3,
        out_specs=pl.BlockSpec((rows, RW), lambda b, j: (b * steps + j, 0)),
        out_shape=jax.ShapeDtypeStruct((t, RW), BF16),
        scratch_shapes=[pltpu.VMEM((CHUNK, RW), F32)],
        compiler_params=_params(("arbitrary", "arbitrary")),
        name="rwkv",
    )(ps, *vecs, wup, aup, gup)


def _mlaprep_kernel(mla_ref, cos_ref, sin_ref, qn_ref, kvn_ref, wq_ref, wqs_ref, wk_ref, wv_ref,
                    q_ref, k_ref, v_ref):
    def rms(x, g):
        return (x * lax.rsqrt(jnp.mean(x * x, axis=-1, keepdims=True) + NORM_EPS) * g).astype(BF16)

    cos = cos_ref[...]
    sin = sin_ref[...]
    lane = lax.broadcasted_iota(jnp.int32, cos.shape, 1)
    cos_q = jnp.tile(cos + (lane < QK_NOPE).astype(F32), (1, MLA_HEADS))
    sin_q = jnp.tile(sin, (1, MLA_HEADS))
    qn = rms(mla_ref[:, 0:Q_LORA], qn_ref[...])
    q = _dot(qn, wq_ref[...]) * cos_q + _dot(qn, wqs_ref[...]) * sin_q
    q_ref[...] = (q * ((QK_NOPE + QK_ROPE) ** -0.5)).astype(BF16)
    kvn = rms(mla_ref[:, Q_LORA:Q_LORA + KV_LORA], kvn_ref[...])
    o = Q_LORA + KV_LORA
    k_pe = mla_ref[:, o:o + LANE] * cos + mla_ref[:, o + LANE:o + 2 * LANE] * sin
    k_ref[...] = (_dot(kvn, wk_ref[...]) + jnp.tile(k_pe, (1, MLA_HEADS))).astype(BF16)
    v_ref[...] = _dot(kvn, wv_ref[...]).astype(BF16)


def _mlaprep_call(mla, cos, sin, qn, kvn, wq, wqs, wk, wv, tm=512):
    t = mla.shape[0]
    hw = MLA_HEADS * HEAD_PAD
    full = lambda shape: pl.BlockSpec(shape, lambda i: (0, 0))
    row = lambda w: pl.BlockSpec((tm, w), lambda i: (i, 0))
    return pl.pallas_call(
        _mlaprep_kernel,
        grid=(t // tm,),
        in_specs=[row(MLA_W), row(LANE), row(LANE), full((1, Q_LORA)), full((1, KV_LORA)),
                  full((Q_LORA, hw)), full((Q_LORA, hw)), full((KV_LORA, hw)), full((KV_LORA, hw))],
        out_specs=[row(hw)] * 3,
        out_shape=[jax.ShapeDtypeStruct((t, hw), BF16)] * 3,
        compiler_params=_params(("arbitrary",)),
        name="mlaprep",
    )(mla, cos, sin, qn, kvn, wq, wqs, wk, wv)


def _attn_kernel(q_ref, k_ref, v_ref, o_ref, *, seq, tq):
    r_i = lax.broadcasted_iota(jnp.int32, (tq, tq), 0) // CHUNK
    c_i = lax.broadcasted_iota(jnp.int32, (tq, tq), 1) // CHUNK
    mask = c_i <= r_i
    for qi in range(seq // tq):
        lo = qi * tq
        q = q_ref[lo:lo + tq, :]
        sd = jnp.where(mask, _dot_nt(q, k_ref[lo:lo + tq, :]), -1e30)
        m = jnp.max(sd, axis=-1, keepdims=True)
        if qi > 0:
            so = _dot_nt(q, k_ref[0:lo, :])
            m = jnp.maximum(m, jnp.max(so, axis=-1, keepdims=True))
        pd = jnp.exp(sd - m)
        l = jnp.sum(pd, axis=-1, keepdims=True)
        acc = _dot(pd.astype(BF16), v_ref[lo:lo + tq, :])
        if qi > 0:
            po = jnp.exp(so - m)
            l = l + jnp.sum(po, axis=-1, keepdims=True)
            acc = acc + _dot(po.astype(BF16), v_ref[0:lo, :])
        o_ref[lo:lo + tq, :] = (acc / l).astype(BF16)


def _attn_call(q, k, v, batch, seq, tq=256):
    spec = pl.BlockSpec((seq, HEAD_PAD), lambda b, h: (b, h))
    return pl.pallas_call(
        functools.partial(_attn_kernel, seq=seq, tq=tq),
        grid=(batch, MLA_HEADS),
        in_specs=[spec] * 3,
        out_specs=spec,
        out_shape=jax.ShapeDtypeStruct(q.shape, BF16),
        compiler_params=_params(("arbitrary", "arbitrary")),
        name="attn",
    )(q, k, v)


def _mixout_kernel(*refs, router):
    if router:
        (rw_ref, o_ref, gates_ref, x_ref, gate_ref, wr_ref, wm_ref, wo_ref,
         g2_ref, shift2_ref, scale2_ref, rwt_ref, rb_ref,
         xo_ref, h2_ref, ti_ref, tp_ref) = refs
    else:
        (rw_ref, o_ref, gates_ref, x_ref, gate_ref, wr_ref, wm_ref, wo_ref,
         g2_ref, shift2_ref, scale2_ref, xo_ref, h2_ref) = refs
    ya = _dot(rw_ref[...], wr_ref[...])
    yb = _dot(o_ref[...], wm_ref[...])
    merged = gates_ref[:, 0:D_MODEL].astype(F32) * ya + gates_ref[:, D_MODEL:].astype(F32) * yb
    xn = x_ref[...] + gate_ref[...] * _dot(merged.astype(BF16), wo_ref[...])
    xo_ref[...] = xn
    h2 = _norm_mod(xn, g2_ref[...], scale2_ref[...], shift2_ref[...])
    h2_ref[...] = h2.astype(h2_ref.dtype)
    if router:
        hh, hl = _split2(h2)
        wh, wl = _split2(rwt_ref[...])
        logits = _dot_nt(wh, hh) + _dot_nt(wl, hh) + _dot_nt(wh, hl) + rb_ref[...]
        e_i = lax.broadcasted_iota(jnp.int32, logits.shape, 0)
        m1 = jnp.max(logits, axis=0, keepdims=True)
        i1 = jnp.min(jnp.where(logits == m1, e_i, N_EXPERTS), axis=0, keepdims=True)
        rest = jnp.where(e_i == i1, -jnp.inf, logits)
        m2 = jnp.max(rest, axis=0, keepdims=True)
        i2 = jnp.min(jnp.where(rest == m2, e_i, N_EXPERTS), axis=0, keepdims=True)
        e2 = jnp.exp(m2 - m1)
        den = 1.0 + e2
        ti_ref[0:1, :] = i1
        ti_ref[1:2, :] = i2
        tp_ref[0:1, :] = 1.0 / den
        tp_ref[1:2, :] = e2 / den


def _mixout_call(rw, o, gates, x2, gate, wr, wm, wo, g2, shift2, scale2, seq, router=None, tm=512):
    t = x2.shape[0]
    tiles_per_seq = seq // tm
    bmap = lambda i: (i // tiles_per_seq, 0, 0)
    full = lambda shape: pl.BlockSpec(shape, lambda i: (0, 0))
    row = lambda w: pl.BlockSpec((tm, w), lambda i: (i, 0))
    mod = pl.BlockSpec((None, 1, D_MODEL), bmap)
    in_specs = [row(RW), row(MLA_HEADS * HEAD_PAD), row(GATE_W), row(D_MODEL), mod,
                full(wr.shape), full(wm.shape), full(wo.shape), full((1, D_MODEL)), mod, mod]
    args = [rw, o, gates, x2, gate, wr, wm, wo, g2, shift2, scale2]
    out_specs = [row(D_MODEL), row(D_MODEL)]
    out_shape = [jax.ShapeDtypeStruct((t, D_MODEL), F32),
                 jax.ShapeDtypeStruct((t, D_MODEL), BF16 if router is None else F32)]
    if router is not None:
        in_specs += [full((N_EXPERTS, D_MODEL)), full((N_EXPERTS, 1))]
        args += list(router)
        out_specs += [pl.BlockSpec((2, tm), lambda i: (0, i))] * 2
        out_shape += [jax.ShapeDtypeStruct((2, t), jnp.int32), jax.ShapeDtypeStruct((2, t), F32)]
    return pl.pallas_call(
        functools.partial(_mixout_kernel, router=router is not None),
        grid=(t // tm,),
        in_specs=in_specs,
        out_specs=out_specs,
        out_shape=out_shape,
        compiler_params=_params(("arbitrary",)),
        name="mixout",
    )(*args)


def _swiglu_step(h, w1_ref, w3_ref, w2_ref, y_ref, acc_ref):
    f = pl.program_id(1)

    @pl.when(f == 0)
    def _():
        acc_ref[...] = jnp.zeros_like(acc_ref)

    a1 = _dot(h, w1_ref[...])
    a3 = _dot(h, w3_ref[...])
    act = (a1 * _sigmoid(a1)) * a3
    acc_ref[...] += _dot(act.astype(BF16), w2_ref[...])

    @pl.when(f == pl.num_programs(1) - 1)
    def _():
        y_ref[...] = acc_ref[...]


def _ffn_kernel(h_ref, w1_ref, w3_ref, w2_ref, y_ref, acc_ref):
    _swiglu_step(h_ref[...], w1_ref, w3_ref, w2_ref, y_ref, acc_ref)


def _ffn_call(h, w1, w3, w2, tm=512, tf=1408):
    r = h.shape[0]
    ff = w1.shape[1]
    return pl.pallas_call(
        _ffn_kernel,
        grid=(r // tm, ff // tf),
        in_specs=[
            pl.BlockSpec((tm, D_MODEL), lambda i, f: (i, 0)),
            pl.BlockSpec((D_MODEL, tf), lambda i, f: (0, f)),
            pl.BlockSpec((D_MODEL, tf), lambda i, f: (0, f)),
            pl.BlockSpec((tf, D_MODEL), lambda i, f: (f, 0)),
        ],
        out_specs=pl.BlockSpec((tm, D_MODEL), lambda i, f: (i, 0)),
        out_shape=jax.ShapeDtypeStruct((r, D_MODEL), F32),
        scratch_shapes=[pltpu.VMEM((tm, D_MODEL), F32)],
        compiler_params=_params(("arbitrary", "arbitrary")),
        name="ffn",
    )(h, w1, w3, w2)


MOE_TM = 512
DMA_UNROLL = 8


def _route(top_i, tm):
    t = top_i.shape[1]
    e_flat = top_i.reshape(-1)
    oh = (e_flat[:, None] == jnp.arange(N_EXPERTS, dtype=jnp.int32)[None, :]).astype(jnp.int32)
    csum = jnp.cumsum(oh, axis=0)
    rank = jnp.sum(csum * oh, axis=1) - 1
    counts = csum[-1]
    padded = ((counts + tm - 1) // tm) * tm
    ends = jnp.cumsum(padded)
    pos = jnp.sum(oh * (ends - padded)[None, :], axis=1) + rank
    n_tiles = (2 * t) // tm + N_EXPERTS
    starts = jnp.arange(n_tiles, dtype=jnp.int32) * tm
    tile_e = jnp.sum((starts[:, None] >= ends[None, :]).astype(jnp.int32), axis=1)
    return pos.astype(jnp.int32), jnp.minimum(tile_e, N_EXPERTS - 1), (ends[-1:] // tm).astype(jnp.int32)


def _dispatch_kernel(pos_ref, h_ref, xs_in_ref, xs_ref, sem, *, t, td):
    del xs_in_ref
    base = pl.program_id(0) * td

    def body(j, carry):
        for u in range(DMA_UNROLL):
            r = j * DMA_UNROLL + u
            for k in range(2):
                dst = pos_ref[k * t + base + r]
                pltpu.make_async_copy(h_ref.at[pl.ds(r, 1)], xs_ref.at[pl.ds(dst, 1)], sem).start()
        return carry

    lax.fori_loop(0, td // DMA_UNROLL, body, 0)
    for k in range(2):
        pltpu.make_async_copy(h_ref, xs_ref.at[pl.ds(0, td)], sem).wait()


def _dispatch_call(pos, h, n_rows, td=512):
    t = h.shape[0]
    any_spec = pl.BlockSpec(memory_space=pl.ANY)
    return pl.pallas_call(
        functools.partial(_dispatch_kernel, t=t, td=td),
        grid_spec=pltpu.PrefetchScalarGridSpec(
            num_scalar_prefetch=1, grid=(t // td,),
            in_specs=[pl.BlockSpec((td, D_MODEL), lambda i, pos: (i, 0)), any_spec], out_specs=any_spec,
            scratch_shapes=[pltpu.SemaphoreType.DMA(())]),
        out_shape=jax.ShapeDtypeStruct((n_rows, D_MODEL), h.dtype),
        input_output_aliases={2: 0},
        compiler_params=_params(("arbitrary",)),
        name="dispatch",
    )(pos, h, jnp.zeros((n_rows, D_MODEL), h.dtype))


def _ffn_routed_kernel(te_ref, nu_ref, x_ref, w1_ref, w3_ref, w2_ref, y_ref, acc_ref):
    del te_ref
    used = pl.program_id(0) < nu_ref[0]

    @pl.when(used)
    def _():
        _swiglu_step(x_ref[...].astype(BF16), w1_ref, w3_ref, w2_ref, y_ref, acc_ref)

    @pl.when(jnp.logical_not(used))
    def _():
        y_ref[...] = jnp.zeros_like(y_ref)


def _ffn_routed_call(tile_e, n_used, xs, w1, w3, w2, tm=MOE_TM, tf=1792):
    r = xs.shape[0]
    ff = w1.shape[2]
    return pl.pallas_call(
        _ffn_routed_kernel,
        grid_spec=pltpu.PrefetchScalarGridSpec(
            num_scalar_prefetch=2, grid=(r // tm, ff // tf),
            in_specs=[
                pl.BlockSpec((tm, D_MODEL), lambda i, f, te, nu: (i, 0)),
                pl.BlockSpec((None, D_MODEL, tf), lambda i, f, te, nu: (te[i], 0, f)),
                pl.BlockSpec((None, D_MODEL, tf), lambda i, f, te, nu: (te[i], 0, f)),
                pl.BlockSpec((None, tf, D_MODEL), lambda i, f, te, nu: (te[i], f, 0)),
            ],
            out_specs=pl.BlockSpec((tm, D_MODEL), lambda i, f, te, nu: (i, 0)),
            scratch_shapes=[pltpu.VMEM((tm, D_MODEL), F32)]),
        out_shape=jax.ShapeDtypeStruct((r, D_MODEL), F32),
        compiler_params=_params(("arbitrary", "arbitrary")),
        name="ffn_routed",
    )(tile_e, n_used, xs, w1, w3, w2)


def _combine_kernel(pos_ref, x_ref, p_ref, gate_ref, gf_ref, y_ref, o_ref, buf, sem, *, t, tc, final):
    i = pl.program_id(0)

    def row_copy(step, slot, r, k):
        src = pos_ref[k * t + step * tc + r]
        return pltpu.make_async_copy(y_ref.at[pl.ds(src, 1)], buf.at[slot, k, pl.ds(r, 1)], sem.at[slot])

    def issue(step, slot):
        def body(j, carry):
            for u in range(DMA_UNROLL):
                for k in range(2):
                    row_copy(step, slot, j * DMA_UNROLL + u, k).start()
            return carry
        lax.fori_loop(0, tc // DMA_UNROLL, body, 0)

    @pl.when(i == 0)
    def _():
        issue(0, 0)

    @pl.when(i + 1 < pl.num_programs(0))
    def _():
        issue(i + 1, (i + 1) % 2)

    slot = i % 2
    for k in range(2):
        pltpu.make_async_copy(y_ref.at[pl.ds(0, tc)], buf.at[slot, k], sem.at[slot]).wait()
    p = p_ref[...]
    xn = x_ref[...] + gate_ref[...] * (p[:, 0:1] * buf[slot, 0] + p[:, 1:2] * buf[slot, 1])
    if final:
        xn = xn * lax.rsqrt(jnp.mean(xn * xn, axis=-1, keepdims=True) + NORM_EPS) * gf_ref[...]
    o_ref[...] = xn


def _combine_call(pos, x2, probs, gate, gf, y, seq, final, tc=256):
    t = x2.shape[0]
    tiles_per_seq = seq // tc
    row = pl.BlockSpec((tc, D_MODEL), lambda i, pos: (i, 0))
    return pl.pallas_call(
        functools.partial(_combine_kernel, t=t, tc=tc, final=final),
        grid_spec=pltpu.PrefetchScalarGridSpec(
            num_scalar_prefetch=1, grid=(t // tc,),
            in_specs=[row, pl.BlockSpec((tc, 2), lambda i, pos: (i, 0)),
                      pl.BlockSpec((None, 1, D_MODEL), lambda i, pos: (i // tiles_per_seq, 0, 0)),
                      pl.BlockSpec((1, D_MODEL), lambda i, pos: (0, 0)),
                      pl.BlockSpec(memory_space=pl.ANY)],
            out_specs=row,
            scratch_shapes=[pltpu.VMEM((2, 2, tc, D_MODEL), F32), pltpu.SemaphoreType.DMA((2,))]),
        out_shape=jax.ShapeDtypeStruct((t, D_MODEL), F32),
        compiler_params=_params(("arbitrary",)),
        name="combine",
    )(pos, x2, probs, gate, gf, y)


def _resid_kernel(x_ref, y_ref, gate_ref, gf_ref, o_ref, *, final):
    xn = x_ref[...] + gate_ref[...] * y_ref[...]
    if final:
        xn = xn * lax.rsqrt(jnp.mean(xn * xn, axis=-1, keepdims=True) + NORM_EPS) * gf_ref[...]
    o_ref[...] = xn


def _resid_call(x2, y, gate, gf, seq, final, tm=512):
    t = x2.shape[0]
    tiles_per_seq = seq // tm
    row = pl.BlockSpec((tm, D_MODEL), lambda i: (i, 0))
    return pl.pallas_call(
        functools.partial(_resid_kernel, final=final),
        grid=(t // tm,),
        in_specs=[row, row, pl.BlockSpec((None, 1, D_MODEL), lambda i: (i // tiles_per_seq, 0, 0)),
                  pl.BlockSpec((1, D_MODEL), lambda i: (0, 0))],
        out_specs=row,
        out_shape=jax.ShapeDtypeStruct((t, D_MODEL), F32),
        compiler_params=_params(("arbitrary",)),
        name="resid",
    )(x2, y, gate, gf)


def _pad_cols(w, n):
    return jnp.pad(w, ((0, 0), (0, n)))


def _layout_w_in(w):
    z64 = jnp.zeros((w.shape[0], 64), w.dtype)
    z32 = jnp.zeros((w.shape[0], 32), w.dtype)
    o = 3 * RW
    pw, pa, pg = w[:, o:o + 64], w[:, o + 64:o + 128], w[:, o + 128:o + 256]
    o = 3 * RW + 256
    qa, kva = w[:, o:o + Q_LORA], w[:, o + Q_LORA:o + Q_LORA + KV_LORA]
    o += Q_LORA + KV_LORA
    kr = w[:, o:o + QK_ROPE]
    kr_sw = jnp.concatenate([-kr[:, QK_ROPE // 2:], kr[:, :QK_ROPE // 2]], axis=1)
    gates = w[:, o + QK_ROPE:]
    return jnp.concatenate([w[:, :3 * RW], pw, z64, pa, z64, pg,
                            qa, kva, z64, kr, z32, z64, kr_sw, z32, gates], axis=1).astype(BF16)


def _layout_mu(mu):
    z = jnp.zeros((64,), mu.dtype)
    o = 3 * RW
    return jnp.concatenate([mu[:o + 64], z, mu[o + 64:o + 128], z, mu[o + 128:]])[None, :]


def _layout_wq(w_qb):
    hd = QK_NOPE + QK_ROPE
    w = w_qb.reshape(Q_LORA, MLA_HEADS, hd)
    half = QK_ROPE // 2
    x1, x2 = w[:, :, QK_NOPE:QK_NOPE + half], w[:, :, QK_NOPE + half:]
    zn = jnp.zeros((Q_LORA, MLA_HEADS, QK_NOPE), w.dtype)
    zp = jnp.zeros((Q_LORA, MLA_HEADS, HEAD_PAD - hd), w.dtype)
    wq = jnp.concatenate([w, zp], axis=2).reshape(Q_LORA, -1)
    wqs = jnp.concatenate([zn, -x2, x1, zp], axis=2).reshape(Q_LORA, -1)
    return wq.astype(BF16), wqs.astype(BF16)


def _layout_wkv(w_kvb):
    w = w_kvb.reshape(KV_LORA, MLA_HEADS, QK_NOPE + V_HEAD)
    z = jnp.zeros((KV_LORA, MLA_HEADS, HEAD_PAD - QK_NOPE), w.dtype)
    wk = jnp.concatenate([w[:, :, :QK_NOPE], z], axis=2).reshape(KV_LORA, -1)
    wv = jnp.concatenate([w[:, :, QK_NOPE:], z], axis=2).reshape(KV_LORA, -1)
    return wk.astype(BF16), wv.astype(BF16)


def _layout_mla_out(w):
    w = w.reshape(MLA_HEADS, V_HEAD, D_MODEL)
    z = jnp.zeros((MLA_HEADS, HEAD_PAD - V_HEAD, D_MODEL), w.dtype)
    return jnp.concatenate([w, z], axis=1).reshape(MLA_HEADS * HEAD_PAD, D_MODEL).astype(BF16)


def _rope_tables(positions):
    inv_freq = ROPE_THETA ** (-jnp.arange(0, QK_ROPE, 2, dtype=F32) / QK_ROPE)
    ang = positions.astype(F32).reshape(-1, 1) * inv_freq
    z = jnp.zeros((ang.shape[0], QK_NOPE), F32)
    zp = jnp.zeros((ang.shape[0], LANE - QK_NOPE - QK_ROPE), F32)
    cos, sin = jnp.cos(ang), jnp.sin(ang)
    return (jnp.concatenate([z, cos, cos, zp], axis=1), jnp.concatenate([z, sin, sin, zp], axis=1))


def kernel(x, c, positions, ada_w, ada_b, norm_mix, norm_ffn, norm_final, w_in, tshift_mu, w0, w_up, a0, a_up, g_up, k_k, k_a, r_k, lnx_g, lnx_b, rwkv_out, q_norm, w_qb, kv_norm, w_kvb, mla_out, w_o, ffn_w1, ffn_w3, ffn_w2, router_w, router_b, moe_w1, moe_w3, moe_w2):
    batch, seq, _ = x.shape
    t = batch * seq
    mods = _ada_call(c, ada_w.reshape(2 * DEPTH, D_MODEL, 3 * D_MODEL),
                     ada_b.reshape(2 * DEPTH, 1, 3 * D_MODEL))
    mods = mods.reshape(DEPTH, 2, batch, 3, 1, D_MODEL)
    cos, sin = _rope_tables(positions)
    x2 = x.reshape(t, D_MODEL)
    row = lambda v: v.reshape(1, -1)

    for l in range(DEPTH):
        shift, scale, gate = (mods[l, 0, :, j] for j in range(3))
        ps, mla, gates = _inproj_call(x2, shift, scale, row(norm_mix[l]), _layout_w_in(w_in[l]),
                                      _layout_mu(tshift_mu[l]), seq)
        vecs = [row(v[l]) for v in (w0, a0, k_k, k_a, r_k, lnx_g, lnx_b)]
        lora_pad = lambda w: jnp.pad(w, ((0, LANE - w.shape[0]), (0, 0)))
        rw = _rwkv_call(ps, vecs, lora_pad(w_up[l]), lora_pad(a_up[l]), g_up[l], batch, seq)
        wq, wqs = _layout_wq(w_qb[l])
        wk, wv = _layout_wkv(w_kvb[l])
        q, k, v = _mlaprep_call(mla, cos, sin, row(q_norm[l]), row(kv_norm[l]), wq, wqs, wk, wv)
        o = _attn_call(q, k, v, batch, seq)

        shift2, scale2, gate2 = (mods[l, 1, :, j] for j in range(3))
        moe = l % 2 == 1
        router = (router_w[l // 2].T, router_b[l // 2].reshape(N_EXPERTS, 1)) if moe else None
        outs = _mixout_call(rw, o, gates, x2, gate, rwkv_out[l].astype(BF16), _layout_mla_out(mla_out[l]),
                            w_o[l].astype(BF16), row(norm_ffn[l]), shift2, scale2, seq, router=router)
        final = l == DEPTH - 1
        if moe:
            x2, h2, top_i, top_p = outs
            pos, tile_e, n_used = _route(top_i, MOE_TM)
            xs = _dispatch_call(pos, h2, 2 * t + N_EXPERTS * MOE_TM)
            y = _ffn_routed_call(tile_e, n_used, xs, moe_w1[l // 2].astype(BF16),
                                 moe_w3[l // 2].astype(BF16), moe_w2[l // 2].astype(BF16))
            x2 = _combine_call(pos, x2, top_p.T, gate2, row(norm_final), y, seq, final)
        else:
            x2, h2 = outs
            y = _ffn_call(h2, ffn_w1[l // 2].astype(BF16), ffn_w3[l // 2].astype(BF16),
                          ffn_w2[l // 2].astype(BF16))
            x2 = _resid_call(x2, y, gate2, row(norm_final), seq, final)
    return x2.reshape(batch, seq, D_MODEL)
```

```python
import functools
import math

import jax
import jax.numpy as jnp
from jax import lax
from jax.experimental import pallas as pl
from jax.experimental.pallas import tpu as pltpu

F32 = jnp.float32
BF16 = jnp.bfloat16

D_MODEL = 1024
DEPTH = 2
CHUNK = 64
NORM_EPS = 1e-6
RWKV_HEADS = 8
RWKV_HEAD_DIM = 64
RW = RWKV_HEADS * RWKV_HEAD_DIM
DECAY_LORA = 64
ICLR_LORA = 64
GATE_LORA = 128
DECAY_SCALE = math.exp(-0.5)
LNX_EPS = 64e-5
MLA_HEADS = 8
Q_LORA = 384
KV_LORA = 256
QK_NOPE = 64
QK_ROPE = 32
V_HEAD = 64
ROPE_THETA = 10000.0
N_EXPERTS = 8

LANE = 128
HEAD_PAD = 128
PS_W = 3 * RW + 3 * LANE
MLA_W = Q_LORA + KV_LORA + 2 * LANE
GATE_W = 2 * D_MODEL
HALF = 256
VMEM_LIMIT = 56 * 1024 * 1024


def _sigmoid(x):
    return 1.0 / (1.0 + jnp.exp(-x))


def _dot(a, b):
    return jnp.dot(a, b, preferred_element_type=F32)


def _dot_nt(a, b):
    return lax.dot_general(a, b, (((1,), (1,)), ((), ())), preferred_element_type=F32)


def _split2(x):
    hi = x.astype(BF16)
    lo = (x - hi.astype(F32)).astype(BF16)
    return hi, lo


def _dot3(a, b):
    ah, al = _split2(a)
    bh, bl = _split2(b)
    return _dot(ah, bh) + _dot(al, bh) + _dot(ah, bl)


def _params(sem):
    return pltpu.CompilerParams(dimension_semantics=sem, vmem_limit_bytes=VMEM_LIMIT)


def _ada_kernel(c_ref, w_ref, b_ref, o_ref):
    c = c_ref[...]
    o_ref[...] = _dot3(c * _sigmoid(c), w_ref[...]) + b_ref[...]


def _ada_call(c, ada_w, ada_b):
    n = ada_w.shape[0]
    nb = c.shape[0]
    return pl.pallas_call(
        _ada_kernel,
        grid=(n, 3),
        in_specs=[
            pl.BlockSpec((nb, D_MODEL), lambda i, j: (0, 0)),
            pl.BlockSpec((None, D_MODEL, D_MODEL), lambda i, j: (i, 0, j)),
            pl.BlockSpec((None, 1, D_MODEL), lambda i, j: (i, 0, j)),
        ],
        out_specs=pl.BlockSpec((None, nb, D_MODEL), lambda i, j: (i, 0, j)),
        out_shape=jax.ShapeDtypeStruct((n, nb, 3 * D_MODEL), F32),
        compiler_params=_params(("arbitrary", "arbitrary")),
        name="ada",
    )(c, ada_w, ada_b)


def _norm_mod(x, g, scale, shift):
    var = jnp.mean(x * x, axis=-1, keepdims=True)
    return (x * lax.rsqrt(var + NORM_EPS) * g) * (1.0 + scale) + shift


def _inproj_kernel(x_ref, shift_ref, scale_ref, g_ref, w_ref, mu_ref,
                   ps_ref, mla_ref, gates_ref, carry_ref, *, tiles_per_seq, tm):
    i = pl.program_id(0)
    h = _norm_mod(x_ref[...], g_ref[...], scale_ref[...], shift_ref[...]).astype(BF16)

    @pl.when(i % tiles_per_seq == 0)
    def _():
        carry_ref[...] = jnp.zeros_like(carry_ref)

    p = _dot(h, w_ref[:, 0:PS_W])
    prev = pltpu.roll(p, 1, axis=0)
    row = lax.broadcasted_iota(jnp.int32, p.shape, 0)
    prev = jnp.where(row == 0, carry_ref[0:1, :], prev)
    carry_ref[0:1, :] = p[tm - 1:tm, :]
    ps_ref[...] = p + (prev - p) * mu_ref[...]
    mla_ref[...] = _dot(h, w_ref[:, PS_W:PS_W + MLA_W])
    gates_ref[...] = _sigmoid(_dot(h, w_ref[:, PS_W + MLA_W:])).astype(BF16)


def _inproj_call(x2, shift, scale, g, w_pad, mu_pad, seq, tm=256):
    t = x2.shape[0]
    tiles_per_seq = seq // tm
    bmap = lambda i: (i // tiles_per_seq, 0, 0)
    wtot = PS_W + MLA_W + GATE_W
    return pl.pallas_call(
        functools.partial(_inproj_kernel, tiles_per_seq=tiles_per_seq, tm=tm),
        grid=(t // tm,),
        in_specs=[
            pl.BlockSpec((tm, D_MODEL), lambda i: (i, 0)),
            pl.BlockSpec((None, 1, D_MODEL), bmap),
            pl.BlockSpec((None, 1, D_MODEL), bmap),
            pl.BlockSpec((1, D_MODEL), lambda i: (0, 0)),
            pl.BlockSpec((D_MODEL, wtot), lambda i: (0, 0)),
            pl.BlockSpec((1, PS_W), lambda i: (0, 0)),
        ],
        out_specs=[
            pl.BlockSpec((tm, PS_W), lambda i: (i, 0)),
            pl.BlockSpec((tm, MLA_W), lambda i: (i, 0)),
            pl.BlockSpec((tm, GATE_W), lambda i: (i, 0)),
        ],
        out_shape=[
            jax.ShapeDtypeStruct((t, PS_W), F32),
            jax.ShapeDtypeStruct((t, MLA_W), F32),
            jax.ShapeDtypeStruct((t, GATE_W), BF16),
        ],
        scratch_shapes=[pltpu.VMEM((8, PS_W), F32)],
        compiler_params=_params(("arbitrary",)),
        name="inproj",
    )(x2, shift, scale, g, w_pad, mu_pad)


def _blockdiag(xb, bdmask):
    return jnp.tile(xb, (HALF // CHUNK, 1)) * bdmask


def _bd_mm(dot, a, b, bdmask, passes):
    if passes == 1:
        return dot(a.astype(BF16), _blockdiag(b.astype(BF16), bdmask))
    ah, al = _split2(a)
    bh, bl = _split2(b)
    r = dot(jnp.concatenate([ah, al], axis=0), _blockdiag(bh, bdmask))
    n = a.shape[0]
    return r[:n] + r[n:] + dot(ah, _blockdiag(bl, bdmask))


def _bd_nn(a, b, bdmask, passes):
    return _bd_mm(_dot, a, b, bdmask, passes)


def _bd_nt(a, b, bdmask, passes):
    return _bd_mm(_dot_nt, a, b, bdmask, passes)


def _bd_tn(a, b, bdmask_f, passes):
    full = _dot3(a.T, b) if passes == 3 else _dot(a.T.astype(BF16), b.astype(BF16))
    return jnp.sum((full * bdmask_f).reshape(HALF // CHUNK, CHUNK, HALF), axis=0)


P_SCORE = 1
P_TINV = 1
P_APPLY = 1
P_GH = 1
P_STATE = 1


def _halves(fn, *xs):
    outs = [fn(*[x[:, h * HALF:(h + 1) * HALF] for x in xs]) for h in range(RW // HALF)]
    return jnp.concatenate(outs, axis=1)


def _rwkv_kernel(ps_ref, w0_ref, a0_ref, kk_ref, ka_ref, rk_ref, lg_ref, lb_ref,
                 wup_ref, aup_ref, gup_ref, out_ref, s_ref, *, nc):
    @pl.when(pl.program_id(1) == 0)
    def _():
        s_ref[...] = jnp.zeros_like(s_ref)

    C = CHUNK
    r_i = lax.broadcasted_iota(jnp.int32, (HALF, HALF), 0)
    c_i = lax.broadcasted_iota(jnp.int32, (HALF, HALF), 1)
    bdsame = (r_i // C) == (c_i // C)
    bdmask = bdsame.astype(BF16)
    bdmask_f = bdsame.astype(F32)
    t_i = lax.broadcasted_iota(jnp.int32, (C, HALF), 0)
    i_i = lax.broadcasted_iota(jnp.int32, (C, HALF), 1) % C
    strict = i_i < t_i
    incl = i_i <= t_i
    eye_l = (i_i == t_i).astype(F32)
    n = nc * C
    tr = lax.broadcasted_iota(jnp.int32, (n, n), 0)
    tc = lax.broadcasted_iota(jnp.int32, (n, n), 1)
    tril = ((tc <= tr) & (tc // C == tr // C)).astype(BF16)

    def headsum(x):
        def one(xh):
            hi, lo = _split2(xh)
            return _dot(hi, bdmask) + _dot(lo, bdmask)
        return _halves(one, x)

    def cumsum_rows(x):
        hi = x.astype(BF16)
        r1 = x - hi.astype(F32)
        mid = r1.astype(BF16)
        lo = (r1 - mid.astype(F32)).astype(BF16)
        return _dot(tril, hi) + _dot(tril, mid) + _dot(tril, lo)

    p_r = ps_ref[:, 0:RW]
    p_k = ps_ref[:, RW:2 * RW]
    p_v = ps_ref[:, 2 * RW:3 * RW]
    p_w = ps_ref[:, 3 * RW:3 * RW + LANE]
    p_a = ps_ref[:, 3 * RW + LANE:3 * RW + 2 * LANE]
    p_g = ps_ref[:, 3 * RW + 2 * LANE:3 * RW + 3 * LANE]
    logw = -DECAY_SCALE * _sigmoid(w0_ref[...] + _dot3(jnp.tanh(p_w), wup_ref[...]))
    iclr = _sigmoid(a0_ref[...] + _dot3(p_a, aup_ref[...]))
    gate = _dot3(_sigmoid(p_g), gup_ref[...])
    kk = p_k * kk_ref[...]
    k2 = p_k * (1.0 + (iclr - 1.0) * ka_ref[...])
    sums = headsum(jnp.concatenate([kk * kk, p_r * k2 * rk_ref[...]], axis=0))
    kk = kk * lax.rsqrt(sums[:n] + 1e-12)
    bonus = sums[n:] * p_v
    b_v = kk * iclr
    cum = cumsum_rows(logw)
    cum_c = jnp.concatenate(
        [jnp.broadcast_to(cum[(c + 1) * C - 1:(c + 1) * C, :], (C, RW)) for c in range(nc)], axis=0)
    e_neg = jnp.exp(-cum)
    e_rem = jnp.exp(cum_c - cum)
    a_all = -kk * jnp.exp(cum - logw)
    b_all = b_v * e_neg
    k_all = k2 * e_neg
    r_all = p_r * jnp.exp(cum)
    bg_all = b_v * e_rem
    kg_all = k2 * e_rem
    gam_all = jnp.exp(cum_c)

    nh = RW // HALF
    units = [(c, h) for c in range(nc) for h in range(nh)]

    def cut(x, u):
        c, h = u
        return x[c * C:(c + 1) * C, h * HALF:(h + 1) * HALF]

    each = lambda fn, *lists: [fn(*xs) for xs in zip(*lists)]
    stack = lambda x, y: jnp.concatenate([x, y], axis=0)
    top = lambda x: x[:C]
    bot = lambda x: x[C:]
    nn = lambda p: (lambda a, b: _bd_nn(a, b, bdmask, p))

    a_t, b_t, k_t, r_t, b_g, k_g, v_u = ([cut(x, u) for u in units]
                                         for x in (a_all, b_all, k_all, r_all, bg_all, kg_all, p_v))
    gam = [gam_all[c * C:c * C + 1, h * HALF:(h + 1) * HALF] for c, h in units]
    ar = each(stack, a_t, r_t)
    sb = each(lambda x, y: _bd_nt(x, y, bdmask, P_SCORE), ar, b_t)
    sk = each(lambda x, y: _bd_nt(x, y, bdmask, P_SCORE), ar, k_t)
    l_ab = [jnp.where(strict, top(x), 0.0) for x in sb]
    q_b = [jnp.where(incl, bot(x), 0.0) for x in sb]
    m_ak = [jnp.where(strict, top(x), 0.0) for x in sk]
    q_k = [jnp.where(incl, bot(x), 0.0) for x in sk]
    t_m = [eye_l + x for x in l_ab]
    pw = each(nn(P_TINV), l_ab, l_ab)
    for _ in range(4):
        tp = each(nn(P_TINV), each(stack, t_m, pw), pw)
        t_m = each(lambda t, x: t + top(x), t_m, tp)
        pw = [bot(x) for x in tp]
    t_m = each(lambda t, x: t + x, t_m, each(nn(P_TINV), t_m, pw))
    mq = each(nn(P_APPLY), each(stack, m_ak, q_k), v_u)
    a_h = each(nn(P_APPLY), t_m, a_t)
    u0 = each(nn(P_APPLY), t_m, [top(x) for x in mq])
    r_h = each(lambda r, x: r + x, r_t, each(nn(P_APPLY), q_b, a_h))
    y0 = each(lambda m, x: bot(m) + x, mq, each(nn(P_APPLY), q_b, u0))
    g_l = each(lambda a, b, g: _bd_tn(a, b, bdmask_f, P_GH) + eye_l * g, a_h, b_g, gam)
    h_l = each(lambda u, v, b, k: _bd_tn(stack(u, v), stack(b, k), bdmask_f, P_GH), u0, v_u, b_g, k_g)

    s_all = s_ref[...]
    s_cur = [s_all[:, h * HALF:(h + 1) * HALF] for h in range(nh)]
    ys = []
    for c in range(nc):
        idx = [c * nh + h for h in range(nh)]
        yc = [_bd_nt(r_h[i], s, bdmask, P_STATE) + y0[i] for i, s in zip(idx, s_cur)]
        s_cur = [_bd_nn(s, g_l[i], bdmask, P_STATE) + h_l[i] for i, s in zip(idx, s_cur)]
        ys.append(jnp.concatenate(yc, axis=1))
    s_ref[...] = jnp.concatenate(s_cur, axis=1)
    y = jnp.concatenate(ys, axis=0)
    inv_n = 1.0 / RWKV_HEAD_DIM
    mu = headsum(y) * inv_n
    d = y - mu
    var = headsum(d * d) * inv_n
    yn = d * lax.rsqrt(var + LNX_EPS) * lg_ref[...] + lb_ref[...]
    out_ref[...] = ((yn + bonus) * gate).astype(BF16)


def _rwkv_call(ps, vecs, wup, aup, gup, batch, seq, nc=4):
    t = ps.shape[0]
    rows = nc * CHUNK
    steps = seq // rows
    vec_spec = pl.BlockSpec((1, RW), lambda b, j: (0, 0))
    lora_spec = pl.BlockSpec((LANE, RW), lambda b, j: (0, 0))
    return pl.pallas_call(
        functools.partial(_rwkv_kernel, nc=nc),
        grid=(batch, steps),
        in_specs=[pl.BlockSpec((rows, PS_W), lambda b, j: (b * steps + j, 0))]
                 + [vec_spec] * 7 + [lora_spec] * 3,
        out_specs=pl.BlockSpec((rows, RW), lambda b, j: (b * steps + j, 0)),
        out_shape=jax.ShapeDtypeStruct((t, RW), BF16),
        scratch_shapes=[pltpu.VMEM((CHUNK, RW), F32)],
        compiler_params=_params(("arbitrary", "arbitrary")),
        name="rwkv",
    )(ps, *vecs, wup, aup, gup)


def _mlaprep_kernel(mla_ref, cos_ref, sin_ref, qn_ref, kvn_ref, wq_ref, wqs_ref, wk_ref, wv_ref,
                    q_ref, k_ref, v_ref):
    def rms(x, g):
        return (x * lax.rsqrt(jnp.mean(x * x, axis=-1, keepdims=True) + NORM_EPS) * g).astype(BF16)

    cos = cos_ref[...]
    sin = sin_ref[...]
    lane = lax.broadcasted_iota(jnp.int32, cos.shape, 1)
    cos_q = jnp.tile(cos + (lane < QK_NOPE).astype(F32), (1, MLA_HEADS))
    sin_q = jnp.tile(sin, (1, MLA_HEADS))
    qn = rms(mla_ref[:, 0:Q_LORA], qn_ref[...])
    q = _dot(qn, wq_ref[...]) * cos_q + _dot(qn, wqs_ref[...]) * sin_q
    q_ref[...] = (q * ((QK_NOPE + QK_ROPE) ** -0.5)).astype(BF16)
    kvn = rms(mla_ref[:, Q_LORA:Q_LORA + KV_LORA], kvn_ref[...])
    o = Q_LORA + KV_LORA
    k_pe = mla_ref[:, o:o + LANE] * cos + mla_ref[:, o + LANE:o + 2 * LANE] * sin
    k_ref[...] = (_dot(kvn, wk_ref[...]) + jnp.tile(k_pe, (1, MLA_HEADS))).astype(BF16)
    v_ref[...] = _dot(kvn, wv_ref[...]).astype(BF16)


def _mlaprep_call(mla, cos, sin, qn, kvn, wq, wqs, wk, wv, tm=512):
    t = mla.shape[0]
    hw = MLA_HEADS * HEAD_PAD
    full = lambda shape: pl.BlockSpec(shape, lambda i: (0, 0))
    row = lambda w: pl.BlockSpec((tm, w), lambda i: (i, 0))
    return pl.pallas_call(
        _mlaprep_kernel,
        grid=(t // tm,),
        in_specs=[row(MLA_W), row(LANE), row(LANE), full((1, Q_LORA)), full((1, KV_LORA)),
                  full((Q_LORA, hw)), full((Q_LORA, hw)), full((KV_LORA, hw)), full((KV_LORA, hw))],
        out_specs=[row(hw)] * 3,
        out_shape=[jax.ShapeDtypeStruct((t, hw), BF16)] * 3,
        compiler_params=_params(("arbitrary",)),
        name="mlaprep",
    )(mla, cos, sin, qn, kvn, wq, wqs, wk, wv)


def _attn_kernel(q_ref, k_ref, v_ref, o_ref, *, seq, tq):
    r_i = lax.broadcasted_iota(jnp.int32, (tq, tq), 0) // CHUNK
    c_i = lax.broadcasted_iota(jnp.int32, (tq, tq), 1) // CHUNK
    mask = c_i <= r_i
    for qi in range(seq // tq):
        lo = qi * tq
        q = q_ref[lo:lo + tq, :]
        sd = jnp.where(mask, _dot_nt(q, k_ref[lo:lo + tq, :]), -1e30)
        m = jnp.max(sd, axis=-1, keepdims=True)
        if qi > 0:
            so = _dot_nt(q, k_ref[0:lo, :])
            m = jnp.maximum(m, jnp.max(so, axis=-1, keepdims=True))
        pd = jnp.exp(sd - m)
        l = jnp.sum(pd, axis=-1, keepdims=True)
        acc = _dot(pd.astype(BF16), v_ref[lo:lo + tq, :])
        if qi > 0:
            po = jnp.exp(so - m)
            l = l + jnp.sum(po, axis=-1, keepdims=True)
            acc = acc + _dot(po.astype(BF16), v_ref[0:lo, :])
        o_ref[lo:lo + tq, :] = (acc / l).astype(BF16)


def _attn_call(q, k, v, batch, seq, tq=256):
    spec = pl.BlockSpec((seq, HEAD_PAD), lambda b, h: (b, h))
    return pl.pallas_call(
        functools.partial(_attn_kernel, seq=seq, tq=tq),
        grid=(batch, MLA_HEADS),
        in_specs=[spec] * 3,
        out_specs=spec,
        out_shape=jax.ShapeDtypeStruct(q.shape, BF16),
        compiler_params=_params(("arbitrary", "arbitrary")),
        name="attn",
    )(q, k, v)


def _mixout_kernel(*refs, router):
    if router:
        (rw_ref, o_ref, gates_ref, x_ref, gate_ref, wr_ref, wm_ref, wo_ref,
         g2_ref, shift2_ref, scale2_ref, rwt_ref, rb_ref,
         xo_ref, h2_ref, ti_ref, tp_ref) = refs
    else:
        (rw_ref, o_ref, gates_ref, x_ref, gate_ref, wr_ref, wm_ref, wo_ref,
         g2_ref, shift2_ref, scale2_ref, xo_ref, h2_ref) = refs
    ya = _dot(rw_ref[...], wr_ref[...])
    yb = _dot(o_ref[...], wm_ref[...])
    merged = gates_ref[:, 0:D_MODEL].astype(F32) * ya + gates_ref[:, D_MODEL:].astype(F32) * yb
    xn = x_ref[...] + gate_ref[...] * _dot(merged.astype(BF16), wo_ref[...])
    xo_ref[...] = xn
    h2 = _norm_mod(xn, g2_ref[...], scale2_ref[...], shift2_ref[...])
    h2_ref[...] = h2.astype(h2_ref.dtype)
    if router:
        hh, hl = _split2(h2)
        wh, wl = _split2(rwt_ref[...])
        logits = _dot_nt(wh, hh) + _dot_nt(wl, hh) + _dot_nt(wh, hl) + rb_ref[...]
        e_i = lax.broadcasted_iota(jnp.int32, logits.shape, 0)
        m1 = jnp.max(logits, axis=0, keepdims=True)
        i1 = jnp.min(jnp.where(logits == m1, e_i, N_EXPERTS), axis=0, keepdims=True)
        rest = jnp.where(e_i == i1, -jnp.inf, logits)
        m2 = jnp.max(rest, axis=0, keepdims=True)
        i2 = jnp.min(jnp.where(rest == m2, e_i, N_EXPERTS), axis=0, keepdims=True)
        e2 = jnp.exp(m2 - m1)
        den = 1.0 + e2
        ti_ref[0:1, :] = i1
        ti_ref[1:2, :] = i2
        tp_ref[0:1, :] = 1.0 / den
        tp_ref[1:2, :] = e2 / den


def _mixout_call(rw, o, gates, x2, gate, wr, wm, wo, g2, shift2, scale2, seq, router=None, tm=512):
    t = x2.shape[0]
    tiles_per_seq = seq // tm
    bmap = lambda i: (i // tiles_per_seq, 0, 0)
    full = lambda shape: pl.BlockSpec(shape, lambda i: (0, 0))
    row = lambda w: pl.BlockSpec((tm, w), lambda i: (i, 0))
    mod = pl.BlockSpec((None, 1, D_MODEL), bmap)
    in_specs = [row(RW), row(MLA_HEADS * HEAD_PAD), row(GATE_W), row(D_MODEL), mod,
                full(wr.shape), full(wm.shape), full(wo.shape), full((1, D_MODEL)), mod, mod]
    args = [rw, o, gates, x2, gate, wr, wm, wo, g2, shift2, scale2]
    out_specs = [row(D_MODEL), row(D_MODEL)]
    out_shape = [jax.ShapeDtypeStruct((t, D_MODEL), F32),
                 jax.ShapeDtypeStruct((t, D_MODEL), BF16 if router is None else F32)]
    if router is not None:
        in_specs += [full((N_EXPERTS, D_MODEL)), full((N_EXPERTS, 1))]
        args += list(router)
        out_specs += [pl.BlockSpec((2, tm), lambda i: (0, i))] * 2
        out_shape += [jax.ShapeDtypeStruct((2, t), jnp.int32), jax.ShapeDtypeStruct((2, t), F32)]
    return pl.pallas_call(
        functools.partial(_mixout_kernel, router=router is not None),
        grid=(t // tm,),
        in_specs=in_specs,
        out_specs=out_specs,
        out_shape=out_shape,
        compiler_params=_params(("arbitrary",)),
        name="mixout",
    )(*args)


def _swiglu_step(h, w1_ref, w3_ref, w2_ref, acc_ref, finish):
    f = pl.program_id(1)

    @pl.when(f == 0)
    def _():
        acc_ref[...] = jnp.zeros_like(acc_ref)

    a1 = _dot(h, w1_ref[...])
    a3 = _dot(h, w3_ref[...])
    act = (a1 * _sigmoid(a1)) * a3
    acc_ref[...] += _dot(act.astype(BF16), w2_ref[...])

    @pl.when(f == pl.num_programs(1) - 1)
    def _():
        finish(acc_ref[...])


def _rms(x, g):
    return x * lax.rsqrt(jnp.mean(x * x, axis=-1, keepdims=True) + NORM_EPS) * g


def _ffn_kernel(h_ref, x_ref, gate_ref, gf_ref, w1_ref, w3_ref, w2_ref, o_ref, acc_ref, *, final):
    def finish(acc):
        xn = x_ref[...] + gate_ref[...] * acc
        o_ref[...] = _rms(xn, gf_ref[...]) if final else xn

    _swiglu_step(h_ref[...], w1_ref, w3_ref, w2_ref, acc_ref, finish)


def _ffn_call(h, x2, gate, gf, w1, w3, w2, seq, final, tm=512, tf=1408):
    r = h.shape[0]
    ff = w1.shape[1]
    tiles_per_seq = seq // tm
    row = pl.BlockSpec((tm, D_MODEL), lambda i, f: (i, 0))
    return pl.pallas_call(
        functools.partial(_ffn_kernel, final=final),
        grid=(r // tm, ff // tf),
        in_specs=[
            row, row,
            pl.BlockSpec((None, 1, D_MODEL), lambda i, f: (i // tiles_per_seq, 0, 0)),
            pl.BlockSpec((1, D_MODEL), lambda i, f: (0, 0)),
            pl.BlockSpec((D_MODEL, tf), lambda i, f: (0, f)),
            pl.BlockSpec((D_MODEL, tf), lambda i, f: (0, f)),
            pl.BlockSpec((tf, D_MODEL), lambda i, f: (f, 0)),
        ],
        out_specs=row,
        out_shape=jax.ShapeDtypeStruct((r, D_MODEL), F32),
        scratch_shapes=[pltpu.VMEM((tm, D_MODEL), F32)],
        compiler_params=_params(("arbitrary", "arbitrary")),
        name="ffn",
    )(h, x2, gate, gf, w1, w3, w2)


MOE_TM = 512
DMA_UNROLL = 8


def _route(top_i, tm):
    t = top_i.shape[1]
    e_flat = top_i.reshape(-1)
    oh = (e_flat[:, None] == jnp.arange(N_EXPERTS, dtype=jnp.int32)[None, :]).astype(jnp.int32)
    csum = jnp.cumsum(oh, axis=0)
    rank = jnp.sum(csum * oh, axis=1) - 1
    counts = csum[-1]
    padded = ((counts + tm - 1) // tm) * tm
    ends = jnp.cumsum(padded)
    pos = jnp.sum(oh * (ends - padded)[None, :], axis=1) + rank
    n_tiles = (2 * t) // tm + N_EXPERTS
    starts = jnp.arange(n_tiles, dtype=jnp.int32) * tm
    tile_e = jnp.sum((starts[:, None] >= ends[None, :]).astype(jnp.int32), axis=1)
    return (pos.astype(jnp.int32), ends.astype(jnp.int32), jnp.minimum(tile_e, N_EXPERTS - 1),
            (ends[-1:] // tm).astype(jnp.int32))


def _dispatch_kernel(pos_ref, ends_ref, h_ref, xs_ref, zero_ref, sem, *, t, td, tm):
    i = pl.program_id(0)
    base = i * td

    @pl.when(i == 0)
    def _():
        zero_ref[...] = jnp.zeros_like(zero_ref)
        n_tiles = xs_ref.shape[0] // tm
        fills = []
        for e in range(N_EXPERTS):
            start = ends_ref[e - 1] if e else 0
            fills.append((ends_ref[e] > start, pl.multiple_of(ends_ref[e] - tm, tm)))
            fills.append((ends_ref[N_EXPERTS - 1] // tm + e < n_tiles,
                          pl.multiple_of(ends_ref[N_EXPERTS - 1] + e * tm, tm)))
        for go, row0 in fills:
            @pl.when(go)
            def _():
                pltpu.make_async_copy(zero_ref, xs_ref.at[pl.ds(row0, tm)], sem).start()
        for go, row0 in fills:
            @pl.when(go)
            def _():
                pltpu.make_async_copy(zero_ref, xs_ref.at[pl.ds(row0, tm)], sem).wait()

    def body(j, carry):
        for u in range(DMA_UNROLL):
            r = j * DMA_UNROLL + u
            for k in range(2):
                dst = pos_ref[k * t + base + r]
                pltpu.make_async_copy(h_ref.at[pl.ds(r, 1)], xs_ref.at[pl.ds(dst, 1)], sem).start()
        return carry

    lax.fori_loop(0, td // DMA_UNROLL, body, 0)
    for k in range(2):
        pltpu.make_async_copy(h_ref, xs_ref.at[pl.ds(0, td)], sem).wait()


def _dispatch_call(pos, ends, h, n_rows, tm, td=512):
    t = h.shape[0]
    return pl.pallas_call(
        functools.partial(_dispatch_kernel, t=t, td=td, tm=tm),
        grid_spec=pltpu.PrefetchScalarGridSpec(
            num_scalar_prefetch=2, grid=(t // td,),
            in_specs=[pl.BlockSpec((td, D_MODEL), lambda i, pos, ends: (i, 0))],
            out_specs=pl.BlockSpec(memory_space=pl.ANY),
            scratch_shapes=[pltpu.VMEM((tm, D_MODEL), h.dtype), pltpu.SemaphoreType.DMA(())]),
        out_shape=jax.ShapeDtypeStruct((n_rows, D_MODEL), h.dtype),
        compiler_params=_params(("arbitrary",)),
        name="dispatch",
    )(pos, ends, h)


def _ffn_routed_kernel(te_ref, nu_ref, x_ref, w1_ref, w3_ref, w2_ref, y_ref, acc_ref):
    del te_ref
    used = pl.program_id(0) < nu_ref[0]

    def finish(acc):
        y_ref[...] = acc

    @pl.when(used)
    def _():
        _swiglu_step(x_ref[...].astype(BF16), w1_ref, w3_ref, w2_ref, acc_ref, finish)

    @pl.when(jnp.logical_not(used))
    def _():
        y_ref[...] = jnp.zeros_like(y_ref)


def _ffn_routed_call(tile_e, n_used, xs, w1, w3, w2, tm=MOE_TM, tf=1792):
    r = xs.shape[0]
    ff = w1.shape[2]
    return pl.pallas_call(
        _ffn_routed_kernel,
        grid_spec=pltpu.PrefetchScalarGridSpec(
            num_scalar_prefetch=2, grid=(r // tm, ff // tf),
            in_specs=[
                pl.BlockSpec((tm, D_MODEL), lambda i, f, te, nu: (i, 0)),
                pl.BlockSpec((None, D_MODEL, tf), lambda i, f, te, nu: (te[i], 0, f)),
                pl.BlockSpec((None, D_MODEL, tf), lambda i, f, te, nu: (te[i], 0, f)),
                pl.BlockSpec((None, tf, D_MODEL), lambda i, f, te, nu: (te[i], f, 0)),
            ],
            out_specs=pl.BlockSpec((tm, D_MODEL), lambda i, f, te, nu: (i, 0)),
            scratch_shapes=[pltpu.VMEM((tm, D_MODEL), F32)]),
        out_shape=jax.ShapeDtypeStruct((r, D_MODEL), F32),
        compiler_params=_params(("arbitrary", "arbitrary")),
        name="ffn_routed",
    )(tile_e, n_used, xs, w1, w3, w2)


def _combine_kernel(pos_ref, x_ref, p_ref, gate_ref, gf_ref, y_ref, o_ref, buf, sem, *, t, tc, final):
    i = pl.program_id(0)

    def row_copy(step, slot, r, k):
        src = pos_ref[k * t + step * tc + r]
        return pltpu.make_async_copy(y_ref.at[pl.ds(src, 1)], buf.at[slot, k, pl.ds(r, 1)], sem.at[slot])

    def issue(step, slot):
        def body(j, carry):
            for u in range(DMA_UNROLL):
                for k in range(2):
                    row_copy(step, slot, j * DMA_UNROLL + u, k).start()
            return carry
        lax.fori_loop(0, tc // DMA_UNROLL, body, 0)

    @pl.when(i == 0)
    def _():
        issue(0, 0)

    @pl.when(i + 1 < pl.num_programs(0))
    def _():
        issue(i + 1, (i + 1) % 2)

    slot = i % 2
    for k in range(2):
        pltpu.make_async_copy(y_ref.at[pl.ds(0, tc)], buf.at[slot, k], sem.at[slot]).wait()
    p = p_ref[...]
    xn = x_ref[...] + gate_ref[...] * (p[:, 0:1] * buf[slot, 0] + p[:, 1:2] * buf[slot, 1])
    o_ref[...] = _rms(xn, gf_ref[...]) if final else xn


def _combine_call(pos, x2, probs, gate, gf, y, seq, final, tc=256):
    t = x2.shape[0]
    tiles_per_seq = seq // tc
    row = pl.BlockSpec((tc, D_MODEL), lambda i, pos: (i, 0))
    return pl.pallas_call(
        functools.partial(_combine_kernel, t=t, tc=tc, final=final),
        grid_spec=pltpu.PrefetchScalarGridSpec(
            num_scalar_prefetch=1, grid=(t // tc,),
            in_specs=[row, pl.BlockSpec((tc, 2), lambda i, pos: (i, 0)),
                      pl.BlockSpec((None, 1, D_MODEL), lambda i, pos: (i // tiles_per_seq, 0, 0)),
                      pl.BlockSpec((1, D_MODEL), lambda i, pos: (0, 0)),
                      pl.BlockSpec(memory_space=pl.ANY)],
            out_specs=row,
            scratch_shapes=[pltpu.VMEM((2, 2, tc, D_MODEL), F32), pltpu.SemaphoreType.DMA((2,))]),
        out_shape=jax.ShapeDtypeStruct((t, D_MODEL), F32),
        compiler_params=_params(("arbitrary",)),
        name="combine",
    )(pos, x2, probs, gate, gf, y)


def _layout_w_in(w):
    z64 = jnp.zeros((w.shape[0], 64), w.dtype)
    z32 = jnp.zeros((w.shape[0], 32), w.dtype)
    o = 3 * RW
    pw, pa, pg = w[:, o:o + 64], w[:, o + 64:o + 128], w[:, o + 128:o + 256]
    o = 3 * RW + 256
    qa, kva = w[:, o:o + Q_LORA], w[:, o + Q_LORA:o + Q_LORA + KV_LORA]
    o += Q_LORA + KV_LORA
    kr = w[:, o:o + QK_ROPE]
    kr_sw = jnp.concatenate([-kr[:, QK_ROPE // 2:], kr[:, :QK_ROPE // 2]], axis=1)
    gates = w[:, o + QK_ROPE:]
    return jnp.concatenate([w[:, :3 * RW], pw, z64, pa, z64, pg,
                            qa, kva, z64, kr, z32, z64, kr_sw, z32, gates], axis=1).astype(BF16)


def _layout_mu(mu):
    z = jnp.zeros((64,), mu.dtype)
    o = 3 * RW
    return jnp.concatenate([mu[:o + 64], z, mu[o + 64:o + 128], z, mu[o + 128:]])[None, :]


def _layout_wq(w_qb):
    hd = QK_NOPE + QK_ROPE
    w = w_qb.reshape(Q_LORA, MLA_HEADS, hd)
    half = QK_ROPE // 2
    x1, x2 = w[:, :, QK_NOPE:QK_NOPE + half], w[:, :, QK_NOPE + half:]
    zn = jnp.zeros((Q_LORA, MLA_HEADS, QK_NOPE), w.dtype)
    zp = jnp.zeros((Q_LORA, MLA_HEADS, HEAD_PAD - hd), w.dtype)
    wq = jnp.concatenate([w, zp], axis=2).reshape(Q_LORA, -1)
    wqs = jnp.concatenate([zn, -x2, x1, zp], axis=2).reshape(Q_LORA, -1)
    return wq.astype(BF16), wqs.astype(BF16)


def _layout_wkv(w_kvb):
    w = w_kvb.reshape(KV_LORA, MLA_HEADS, QK_NOPE + V_HEAD)
    z = jnp.zeros((KV_LORA, MLA_HEADS, HEAD_PAD - QK_NOPE), w.dtype)
    wk = jnp.concatenate([w[:, :, :QK_NOPE], z], axis=2).reshape(KV_LORA, -1)
    wv = jnp.concatenate([w[:, :, QK_NOPE:], z], axis=2).reshape(KV_LORA, -1)
    return wk.astype(BF16), wv.astype(BF16)


def _layout_mla_out(w):
    w = w.reshape(MLA_HEADS, V_HEAD, D_MODEL)
    z = jnp.zeros((MLA_HEADS, HEAD_PAD - V_HEAD, D_MODEL), w.dtype)
    return jnp.concatenate([w, z], axis=1).reshape(MLA_HEADS * HEAD_PAD, D_MODEL).astype(BF16)


def _rope_tables(positions):
    inv_freq = ROPE_THETA ** (-jnp.arange(0, QK_ROPE, 2, dtype=F32) / QK_ROPE)
    ang = positions.astype(F32).reshape(-1, 1) * inv_freq
    z = jnp.zeros((ang.shape[0], QK_NOPE), F32)
    zp = jnp.zeros((ang.shape[0], LANE - QK_NOPE - QK_ROPE), F32)
    cos, sin = jnp.cos(ang), jnp.sin(ang)
    return (jnp.concatenate([z, cos, cos, zp], axis=1), jnp.concatenate([z, sin, sin, zp], axis=1))


def kernel(x, c, positions, ada_w, ada_b, norm_mix, norm_ffn, norm_final, w_in, tshift_mu, w0, w_up, a0, a_up, g_up, k_k, k_a, r_k, lnx_g, lnx_b, rwkv_out, q_norm, w_qb, kv_norm, w_kvb, mla_out, w_o, ffn_w1, ffn_w3, ffn_w2, router_w, router_b, moe_w1, moe_w3, moe_w2):
    batch, seq, _ = x.shape
    t = batch * seq
    mods = _ada_call(c, ada_w.reshape(2 * DEPTH, D_MODEL, 3 * D_MODEL),
                     ada_b.reshape(2 * DEPTH, 1, 3 * D_MODEL))
    mods = mods.reshape(DEPTH, 2, batch, 3, 1, D_MODEL)
    cos, sin = _rope_tables(positions)
    x2 = x.reshape(t, D_MODEL)
    row = lambda v: v.reshape(1, -1)

    for l in range(DEPTH):
        shift, scale, gate = (mods[l, 0, :, j] for j in range(3))
        ps, mla, gates = _inproj_call(x2, shift, scale, row(norm_mix[l]), _layout_w_in(w_in[l]),
                                      _layout_mu(tshift_mu[l]), seq)
        vecs = [row(v[l]) for v in (w0, a0, k_k, k_a, r_k, lnx_g, lnx_b)]
        lora_pad = lambda w: jnp.pad(w, ((0, LANE - w.shape[0]), (0, 0)))
        rw = _rwkv_call(ps, vecs, lora_pad(w_up[l]), lora_pad(a_up[l]), g_up[l], batch, seq)
        wq, wqs = _layout_wq(w_qb[l])
        wk, wv = _layout_wkv(w_kvb[l])
        q, k, v = _mlaprep_call(mla, cos, sin, row(q_norm[l]), row(kv_norm[l]), wq, wqs, wk, wv)
        o = _attn_call(q, k, v, batch, seq)

        shift2, scale2, gate2 = (mods[l, 1, :, j] for j in range(3))
        moe = l % 2 == 1
        router = (router_w[l // 2].T, router_b[l // 2].reshape(N_EXPERTS, 1)) if moe else None
        outs = _mixout_call(rw, o, gates, x2, gate, rwkv_out[l].astype(BF16), _layout_mla_out(mla_out[l]),
                            w_o[l].astype(BF16), row(norm_ffn[l]), shift2, scale2, seq, router=router)
        final = l == DEPTH - 1
        if moe:
            x2, h2, top_i, top_p = outs
            pos, ends, tile_e, n_used = _route(top_i, MOE_TM)
            xs = _dispatch_call(pos, ends, h2, 2 * t + N_EXPERTS * MOE_TM, MOE_TM)
            y = _ffn_routed_call(tile_e, n_used, xs, moe_w1[l // 2].astype(BF16),
                                 moe_w3[l // 2].astype(BF16), moe_w2[l // 2].astype(BF16))
            x2 = _combine_call(pos, x2, top_p.T, gate2, row(norm_final), y, seq, final)
        else:
            x2, h2 = outs
            x2 = _ffn_call(h2, x2, gate2, row(norm_final), ffn_w1[l // 2].astype(BF16),
                           ffn_w3[l // 2].astype(BF16), ffn_w2[l // 2].astype(BF16), seq, final)
    return x2.reshape(batch, seq, D_MODEL)
```

```python
import functools
import math

import jax
import jax.numpy as jnp
from jax import lax
from jax.experimental import pallas as pl
from jax.experimental.pallas import tpu as pltpu

F32 = jnp.float32
BF16 = jnp.bfloat16

D_MODEL = 1024
DEPTH = 2
CHUNK = 64
NORM_EPS = 1e-6
RWKV_HEADS = 8
RWKV_HEAD_DIM = 64
RW = RWKV_HEADS * RWKV_HEAD_DIM
DECAY_LORA = 64
ICLR_LORA = 64
GATE_LORA = 128
DECAY_SCALE = math.exp(-0.5)
LNX_EPS = 64e-5
MLA_HEADS = 8
Q_LORA = 384
KV_LORA = 256
QK_NOPE = 64
QK_ROPE = 32
V_HEAD = 64
ROPE_THETA = 10000.0
LOG2E = math.log2(math.e)
N_EXPERTS = 8

LANE = 128
HEAD_PAD = 128
PS_W = 3 * RW + 3 * LANE
MLA_W = Q_LORA + KV_LORA + 2 * LANE
GATE_W = 2 * D_MODEL
HALF = 256
VMEM_LIMIT = 56 * 1024 * 1024


def _sigmoid(x):
    return 1.0 / (1.0 + jnp.exp(-x))


def _dot(a, b):
    return jnp.dot(a, b, preferred_element_type=F32)


def _dot_nt(a, b):
    return lax.dot_general(a, b, (((1,), (1,)), ((), ())), preferred_element_type=F32)


def _split2(x):
    hi = x.astype(BF16)
    lo = (x - hi.astype(F32)).astype(BF16)
    return hi, lo


def _dot3(a, b):
    ah, al = _split2(a)
    bh, bl = _split2(b)
    return _dot(ah, bh) + _dot(al, bh) + _dot(ah, bl)


def _params(sem):
    return pltpu.CompilerParams(dimension_semantics=sem, vmem_limit_bytes=VMEM_LIMIT)


def _ada_kernel(c_ref, w_ref, b_ref, o_ref):
    c = c_ref[...]
    o_ref[...] = _dot3(c * _sigmoid(c), w_ref[...]) + b_ref[...]


def _ada_call(c, ada_w, ada_b):
    n = ada_w.shape[0]
    nb = c.shape[0]
    return pl.pallas_call(
        _ada_kernel,
        grid=(n, 3),
        in_specs=[
            pl.BlockSpec((nb, D_MODEL), lambda i, j: (0, 0)),
            pl.BlockSpec((None, D_MODEL, D_MODEL), lambda i, j: (i, 0, j)),
            pl.BlockSpec((None, 1, D_MODEL), lambda i, j: (i, 0, j)),
        ],
        out_specs=pl.BlockSpec((None, nb, D_MODEL), lambda i, j: (i, 0, j)),
        out_shape=jax.ShapeDtypeStruct((n, nb, 3 * D_MODEL), F32),
        compiler_params=_params(("arbitrary", "arbitrary")),
        name="ada",
    )(c, ada_w, ada_b)


def _norm_mod(x, g, scale, shift):
    var = jnp.mean(x * x, axis=-1, keepdims=True)
    return (x * lax.rsqrt(var + NORM_EPS) * g) * (1.0 + scale) + shift


def _inproj_kernel(x_ref, shift_ref, scale_ref, g_ref, w_ref, mu_ref,
                   ps_ref, mla_ref, gates_ref, carry_ref, *, tiles_per_seq, tm):
    i = pl.program_id(0)
    h = _norm_mod(x_ref[...], g_ref[...], scale_ref[...], shift_ref[...]).astype(BF16)

    @pl.when(i % tiles_per_seq == 0)
    def _():
        carry_ref[...] = jnp.zeros_like(carry_ref)

    p = _dot(h, w_ref[:, 0:PS_W])
    prev = pltpu.roll(p, 1, axis=0)
    row = lax.broadcasted_iota(jnp.int32, p.shape, 0)
    prev = jnp.where(row == 0, carry_ref[0:1, :], prev)
    carry_ref[0:1, :] = p[tm - 1:tm, :]
    ps_ref[...] = p + (prev - p) * mu_ref[...]
    mla_ref[...] = _dot(h, w_ref[:, PS_W:PS_W + MLA_W])
    gates_ref[...] = _sigmoid(_dot(h, w_ref[:, PS_W + MLA_W:])).astype(BF16)


def _inproj_call(x2, shift, scale, g, w_pad, mu_pad, seq, tm=256):
    t = x2.shape[0]
    tiles_per_seq = seq // tm
    bmap = lambda i: (i // tiles_per_seq, 0, 0)
    wtot = PS_W + MLA_W + GATE_W
    return pl.pallas_call(
        functools.partial(_inproj_kernel, tiles_per_seq=tiles_per_seq, tm=tm),
        grid=(t // tm,),
        in_specs=[
            pl.BlockSpec((tm, D_MODEL), lambda i: (i, 0)),
            pl.BlockSpec((None, 1, D_MODEL), bmap),
            pl.BlockSpec((None, 1, D_MODEL), bmap),
            pl.BlockSpec((1, D_MODEL), lambda i: (0, 0)),
            pl.BlockSpec((D_MODEL, wtot), lambda i: (0, 0)),
            pl.BlockSpec((1, PS_W), lambda i: (0, 0)),
        ],
        out_specs=[
            pl.BlockSpec((tm, PS_W), lambda i: (i, 0)),
            pl.BlockSpec((tm, MLA_W), lambda i: (i, 0)),
            pl.BlockSpec((tm, GATE_W), lambda i: (i, 0)),
        ],
        out_shape=[
            jax.ShapeDtypeStruct((t, PS_W), F32),
            jax.ShapeDtypeStruct((t, MLA_W), F32),
            jax.ShapeDtypeStruct((t, GATE_W), BF16),
        ],
        scratch_shapes=[pltpu.VMEM((8, PS_W), F32)],
        compiler_params=_params(("arbitrary",)),
        name="inproj",
    )(x2, shift, scale, g, w_pad, mu_pad)


def _blockdiag(xb, bdmask):
    return jnp.tile(xb, (HALF // CHUNK, 1)) * bdmask


def _bd_mm(dot, a, b, bdmask, passes):
    if passes == 1:
        return dot(a.astype(BF16), _blockdiag(b.astype(BF16), bdmask))
    ah, al = _split2(a)
    bh, bl = _split2(b)
    r = dot(jnp.concatenate([ah, al], axis=0), _blockdiag(bh, bdmask))
    n = a.shape[0]
    return r[:n] + r[n:] + dot(ah, _blockdiag(bl, bdmask))


def _bd_nn(a, b, bdmask, passes):
    return _bd_mm(_dot, a, b, bdmask, passes)


def _bd_nt(a, b, bdmask, passes):
    return _bd_mm(_dot_nt, a, b, bdmask, passes)


def _bd_tn(a, b, bdmask_f, passes):
    full = _dot3(a.T, b) if passes == 3 else _dot(a.T.astype(BF16), b.astype(BF16))
    return jnp.sum((full * bdmask_f).reshape(HALF // CHUNK, CHUNK, HALF), axis=0)


P_SCORE = 1
P_TINV = 1
P_APPLY = 1
P_GH = 1
P_STATE = 1


def _halves(fn, *xs):
    outs = [fn(*[x[:, h * HALF:(h + 1) * HALF] for x in xs]) for h in range(RW // HALF)]
    return jnp.concatenate(outs, axis=1)


def _rwkv_kernel(ps_ref, w0_ref, a0_ref, kk_ref, ka_ref, rk_ref, lg_ref, lb_ref,
                 wup_ref, aup_ref, gup_ref, out_ref, s_ref, *, nc):
    @pl.when(pl.program_id(1) == 0)
    def _():
        s_ref[...] = jnp.zeros_like(s_ref)

    C = CHUNK
    r_i = lax.broadcasted_iota(jnp.int32, (HALF, HALF), 0)
    c_i = lax.broadcasted_iota(jnp.int32, (HALF, HALF), 1)
    bdsame = (r_i // C) == (c_i // C)
    bdmask = bdsame.astype(BF16)
    bdmask_f = bdsame.astype(F32)
    t_i = lax.broadcasted_iota(jnp.int32, (C, HALF), 0)
    i_i = lax.broadcasted_iota(jnp.int32, (C, HALF), 1) % C
    strict = i_i < t_i
    incl = i_i <= t_i
    eye_l = (i_i == t_i).astype(F32)
    n = nc * C
    tr = lax.broadcasted_iota(jnp.int32, (n, n), 0)
    tc = lax.broadcasted_iota(jnp.int32, (n, n), 1)
    tril = ((tc <= tr) & (tc // C == tr // C)).astype(BF16)

    def headsum(x, split=False):
        def one(xh):
            if not split:
                return _dot(xh.astype(BF16), bdmask)
            hi, lo = _split2(xh)
            return _dot(hi, bdmask) + _dot(lo, bdmask)
        return _halves(one, x)

    def cumsum_rows(x):
        hi = x.astype(BF16)
        r1 = x - hi.astype(F32)
        mid = r1.astype(BF16)
        lo = (r1 - mid.astype(F32)).astype(BF16)
        return _dot(tril, hi) + _dot(tril, mid) + _dot(tril, lo)

    p_r = ps_ref[:, 0:RW]
    p_k = ps_ref[:, RW:2 * RW]
    p_v = ps_ref[:, 2 * RW:3 * RW]
    p_w = ps_ref[:, 3 * RW:3 * RW + LANE]
    p_a = ps_ref[:, 3 * RW + LANE:3 * RW + 2 * LANE]
    p_g = ps_ref[:, 3 * RW + 2 * LANE:3 * RW + 3 * LANE]
    logw = -DECAY_SCALE * _sigmoid(w0_ref[...] + _dot3(jnp.tanh(p_w), wup_ref[...]))
    iclr = _sigmoid(a0_ref[...] + _dot(p_a.astype(BF16), aup_ref[...].astype(BF16)))
    gate = _dot(_sigmoid(p_g).astype(BF16), gup_ref[...].astype(BF16))
    kk = p_k * kk_ref[...]
    k2 = p_k * (1.0 + (iclr - 1.0) * ka_ref[...])
    sums = headsum(jnp.concatenate([kk * kk, p_r * k2 * rk_ref[...]], axis=0))
    kk = kk * lax.rsqrt(sums[:n] + 1e-12)
    bonus = sums[n:] * p_v
    b_v = kk * iclr
    cum = cumsum_rows(logw)
    cum_c = jnp.concatenate(
        [jnp.broadcast_to(cum[(c + 1) * C - 1:(c + 1) * C, :], (C, RW)) for c in range(nc)], axis=0)
    e_neg = jnp.exp(-cum)
    e_rem = jnp.exp(cum_c - cum)
    a_all = -kk * jnp.exp(cum - logw)
    b_all = b_v * e_neg
    k_all = k2 * e_neg
    r_all = p_r * jnp.exp(cum)
    bg_all = b_v * e_rem
    kg_all = k2 * e_rem
    gam_all = jnp.exp(cum_c)

    nh = RW // HALF
    units = [(c, h) for c in range(nc) for h in range(nh)]

    def cut(x, u):
        c, h = u
        return x[c * C:(c + 1) * C, h * HALF:(h + 1) * HALF]

    each = lambda fn, *lists: [fn(*xs) for xs in zip(*lists)]
    stack = lambda x, y: jnp.concatenate([x, y], axis=0)
    top = lambda x: x[:C]
    bot = lambda x: x[C:]
    nn = lambda p: (lambda a, b: _bd_nn(a, b, bdmask, p))

    a_t, b_t, k_t, r_t, b_g, k_g, v_u = ([cut(x, u) for u in units]
                                         for x in (a_all, b_all, k_all, r_all, bg_all, kg_all, p_v))
    gam = [gam_all[c * C:c * C + 1, h * HALF:(h + 1) * HALF] for c, h in units]
    ar = each(stack, a_t, r_t)
    sb = each(lambda x, y: _bd_nt(x, y, bdmask, P_SCORE), ar, b_t)
    sk = each(lambda x, y: _bd_nt(x, y, bdmask, P_SCORE), ar, k_t)
    l_ab = [jnp.where(strict, top(x), 0.0) for x in sb]
    q_b = [jnp.where(incl, bot(x), 0.0) for x in sb]
    m_ak = [jnp.where(strict, top(x), 0.0) for x in sk]
    q_k = [jnp.where(incl, bot(x), 0.0) for x in sk]
    t_m = [eye_l + x for x in l_ab]
    pw = each(nn(P_TINV), l_ab, l_ab)
    for _ in range(4):
        tp = each(nn(P_TINV), each(stack, t_m, pw), pw)
        t_m = each(lambda t, x: t + top(x), t_m, tp)
        pw = [bot(x) for x in tp]
    t_m = each(lambda t, x: t + x, t_m, each(nn(P_TINV), t_m, pw))
    mq = each(nn(P_APPLY), each(stack, m_ak, q_k), v_u)
    a_h = each(nn(P_APPLY), t_m, a_t)
    u0 = each(nn(P_APPLY), t_m, [top(x) for x in mq])
    r_h = each(lambda r, x: r + x, r_t, each(nn(P_APPLY), q_b, a_h))
    y0 = each(lambda m, x: bot(m) + x, mq, each(nn(P_APPLY), q_b, u0))
    g_l = each(lambda a, b, g: _bd_tn(a, b, bdmask_f, P_GH) + eye_l * g, a_h, b_g, gam)
    h_l = each(lambda u, v, b, k: _bd_tn(stack(u, v), stack(b, k), bdmask_f, P_GH), u0, v_u, b_g, k_g)

    s_all = s_ref[...]
    s_cur = [s_all[:, h * HALF:(h + 1) * HALF] for h in range(nh)]
    ys = []
    for c in range(nc):
        idx = [c * nh + h for h in range(nh)]
        yc = [_bd_nt(r_h[i], s, bdmask, P_STATE) + y0[i] for i, s in zip(idx, s_cur)]
        s_cur = [_bd_nn(s, g_l[i], bdmask, P_STATE) + h_l[i] for i, s in zip(idx, s_cur)]
        ys.append(jnp.concatenate(yc, axis=1))
    s_ref[...] = jnp.concatenate(s_cur, axis=1)
    y = jnp.concatenate(ys, axis=0)
    inv_n = 1.0 / RWKV_HEAD_DIM
    mu = headsum(y, split=True) * inv_n
    d = y - mu
    var = headsum(d * d) * inv_n
    yn = d * lax.rsqrt(var + LNX_EPS) * lg_ref[...] + lb_ref[...]
    out_ref[...] = ((yn + bonus) * gate).astype(BF16)


def _rwkv_call(ps, vecs, wup, aup, gup, batch, seq, nc=4):
    t = ps.shape[0]
    rows = nc * CHUNK
    steps = seq // rows
    vec_spec = pl.BlockSpec((1, RW), lambda b, j: (0, 0))
    lora_spec = pl.BlockSpec((LANE, RW), lambda b, j: (0, 0))
    return pl.pallas_call(
        functools.partial(_rwkv_kernel, nc=nc),
        grid=(batch, steps),
        in_specs=[pl.BlockSpec((rows, PS_W), lambda b, j: (b * steps + j, 0))]
                 + [vec_spec] * 7 + [lora_spec] * 3,
        out_specs=pl.BlockSpec((rows, RW), lambda b, j: (b * steps + j, 0)),
        out_shape=jax.ShapeDtypeStruct((t, RW), BF16),
        scratch_shapes=[pltpu.VMEM((CHUNK, RW), F32)],
        compiler_params=_params(("arbitrary", "arbitrary")),
        name="rwkv",
    )(ps, *vecs, wup, aup, gup)


def _mlaprep_kernel(mla_ref, pos_ref, invf_ref, qn_ref, kvn_ref, wq_ref, wqs_ref, wk_ref, wv_ref,
                    q_ref, k_ref, v_ref):
    def rms(x, g):
        return (x * lax.rsqrt(jnp.mean(x * x, axis=-1, keepdims=True) + NORM_EPS) * g).astype(BF16)

    ang = pos_ref[...].astype(F32) * invf_ref[...]
    cos = jnp.cos(ang)
    sin = jnp.sin(ang)
    lane = lax.broadcasted_iota(jnp.int32, cos.shape, 1)
    cos_q = jnp.tile(jnp.where(lane < QK_NOPE, 1.0, cos), (1, MLA_HEADS))
    sin_q = jnp.tile(sin, (1, MLA_HEADS))
    qn = rms(mla_ref[:, 0:Q_LORA], qn_ref[...])
    q = _dot(qn, wq_ref[...]) * cos_q + _dot(qn, wqs_ref[...]) * sin_q
    q_ref[...] = (q * ((QK_NOPE + QK_ROPE) ** -0.5 * LOG2E)).astype(BF16)
    kvn = rms(mla_ref[:, Q_LORA:Q_LORA + KV_LORA], kvn_ref[...])
    o = Q_LORA + KV_LORA
    k_pe = mla_ref[:, o:o + LANE] * cos + mla_ref[:, o + LANE:o + 2 * LANE] * sin
    k_ref[...] = (_dot(kvn, wk_ref[...]) + jnp.tile(k_pe, (1, MLA_HEADS))).astype(BF16)
    v_ref[...] = _dot(kvn, wv_ref[...]).astype(BF16)


def _mlaprep_call(mla, pos, invf, qn, kvn, wq, wqs, wk, wv, tm=512):
    t = mla.shape[0]
    hw = MLA_HEADS * HEAD_PAD
    full = lambda shape: pl.BlockSpec(shape, lambda i: (0, 0))
    row = lambda w: pl.BlockSpec((tm, w), lambda i: (i, 0))
    return pl.pallas_call(
        _mlaprep_kernel,
        grid=(t // tm,),
        in_specs=[row(MLA_W), row(1), full((1, LANE)), full((1, Q_LORA)), full((1, KV_LORA)),
                  full((Q_LORA, hw)), full((Q_LORA, hw)), full((KV_LORA, hw)), full((KV_LORA, hw))],
        out_specs=[row(hw)] * 3,
        out_shape=[jax.ShapeDtypeStruct((t, hw), BF16)] * 3,
        compiler_params=_params(("arbitrary",)),
        name="mlaprep",
    )(mla, pos, invf, qn, kvn, wq, wqs, wk, wv)


def _attn_kernel(q_ref, k_ref, v_ref, o_ref, *, seq, tq):
    r_i = lax.broadcasted_iota(jnp.int32, (tq, tq), 0) // CHUNK
    c_i = lax.broadcasted_iota(jnp.int32, (tq, tq), 1) // CHUNK
    mask = c_i <= r_i
    for qi in range(seq // tq):
        lo = qi * tq
        q = q_ref[lo:lo + tq, :]
        sd = jnp.where(mask, _dot_nt(q, k_ref[lo:lo + tq, :]), -1e30)
        m = jnp.max(sd, axis=-1, keepdims=True)
        if qi > 0:
            so = _dot_nt(q, k_ref[0:lo, :])
            m = jnp.maximum(m, jnp.max(so, axis=-1, keepdims=True))
        pd = jnp.exp2(sd - m)
        l = jnp.sum(pd, axis=-1, keepdims=True)
        acc = _dot(pd.astype(BF16), v_ref[lo:lo + tq, :])
        if qi > 0:
            po = jnp.exp2(so - m)
            l = l + jnp.sum(po, axis=-1, keepdims=True)
            acc = acc + _dot(po.astype(BF16), v_ref[0:lo, :])
        o_ref[lo:lo + tq, :] = (acc / l).astype(BF16)


def _attn_call(q, k, v, batch, seq, tq=256):
    spec = pl.BlockSpec((seq, HEAD_PAD), lambda b, h: (b, h))
    return pl.pallas_call(
        functools.partial(_attn_kernel, seq=seq, tq=tq),
        grid=(batch, MLA_HEADS),
        in_specs=[spec] * 3,
        out_specs=spec,
        out_shape=jax.ShapeDtypeStruct(q.shape, BF16),
        compiler_params=_params(("arbitrary", "arbitrary")),
        name="attn",
    )(q, k, v)


def _mixout_kernel(*refs, router):
    if router:
        (rw_ref, o_ref, gates_ref, x_ref, gate_ref, wr_ref, wm_ref, wo_ref,
         g2_ref, shift2_ref, scale2_ref, rwt_ref, rb_ref,
         xo_ref, h2_ref, ti_ref, tp_ref) = refs
    else:
        (rw_ref, o_ref, gates_ref, x_ref, gate_ref, wr_ref, wm_ref, wo_ref,
         g2_ref, shift2_ref, scale2_ref, xo_ref, h2_ref) = refs
    ya = _dot(rw_ref[...], wr_ref[...])
    yb = _dot(o_ref[...], wm_ref[...])
    merged = gates_ref[:, 0:D_MODEL].astype(F32) * ya + gates_ref[:, D_MODEL:].astype(F32) * yb
    xn = x_ref[...] + gate_ref[...] * _dot(merged.astype(BF16), wo_ref[...])
    xo_ref[...] = xn
    h2 = _norm_mod(xn, g2_ref[...], scale2_ref[...], shift2_ref[...])
    h2_ref[...] = h2.astype(h2_ref.dtype)
    if router:
        hh, hl = _split2(h2)
        wh, wl = _split2(rwt_ref[...])
        logits = _dot_nt(wh, hh) + _dot_nt(wl, hh) + _dot_nt(wh, hl) + rb_ref[...]
        e_i = lax.broadcasted_iota(jnp.int32, logits.shape, 0)
        m1 = jnp.max(logits, axis=0, keepdims=True)
        i1 = jnp.min(jnp.where(logits == m1, e_i, N_EXPERTS), axis=0, keepdims=True)
        rest = jnp.where(e_i == i1, -jnp.inf, logits)
        m2 = jnp.max(rest, axis=0, keepdims=True)
        i2 = jnp.min(jnp.where(rest == m2, e_i, N_EXPERTS), axis=0, keepdims=True)
        e2 = jnp.exp(m2 - m1)
        den = 1.0 + e2
        ti_ref[0:1, :] = i1
        ti_ref[1:2, :] = i2
        tp_ref[0:1, :] = 1.0 / den
        tp_ref[1:2, :] = e2 / den


def _mixout_call(rw, o, gates, x2, gate, wr, wm, wo, g2, shift2, scale2, seq, router=None, tm=512):
    t = x2.shape[0]
    tiles_per_seq = seq // tm
    bmap = lambda i: (i // tiles_per_seq, 0, 0)
    full = lambda shape: pl.BlockSpec(shape, lambda i: (0, 0))
    row = lambda w: pl.BlockSpec((tm, w), lambda i: (i, 0))
    mod = pl.BlockSpec((None, 1, D_MODEL), bmap)
    in_specs = [row(RW), row(MLA_HEADS * HEAD_PAD), row(GATE_W), row(D_MODEL), mod,
                full(wr.shape), full(wm.shape), full(wo.shape), full((1, D_MODEL)), mod, mod]
    args = [rw, o, gates, x2, gate, wr, wm, wo, g2, shift2, scale2]
    out_specs = [row(D_MODEL), row(D_MODEL)]
    out_shape = [jax.ShapeDtypeStruct((t, D_MODEL), F32),
                 jax.ShapeDtypeStruct((t, D_MODEL), BF16 if router is None else F32)]
    if router is not None:
        in_specs += [full((N_EXPERTS, D_MODEL)), full((N_EXPERTS, 1))]
        args += list(router)
        out_specs += [pl.BlockSpec((2, tm), lambda i: (0, i))] * 2
        out_shape += [jax.ShapeDtypeStruct((2, t), jnp.int32), jax.ShapeDtypeStruct((2, t), F32)]
    return pl.pallas_call(
        functools.partial(_mixout_kernel, router=router is not None),
        grid=(t // tm,),
        in_specs=in_specs,
        out_specs=out_specs,
        out_shape=out_shape,
        compiler_params=_params(("arbitrary",)),
        name="mixout",
    )(*args)


def _swiglu_step(h, w1_ref, w3_ref, w2_ref, acc_ref, finish):
    f = pl.program_id(1)

    @pl.when(f == 0)
    def _():
        acc_ref[...] = jnp.zeros_like(acc_ref)

    a1 = _dot(h, w1_ref[...])
    a3 = _dot(h, w3_ref[...])
    act = (a1 * _sigmoid(a1)) * a3
    acc_ref[...] += _dot(act.astype(BF16), w2_ref[...])

    @pl.when(f == pl.num_programs(1) - 1)
    def _():
        finish(acc_ref[...])


def _rms(x, g):
    return x * lax.rsqrt(jnp.mean(x * x, axis=-1, keepdims=True) + NORM_EPS) * g


def _ffn_kernel(h_ref, x_ref, gate_ref, gf_ref, w1_ref, w3_ref, w2_ref, o_ref, acc_ref, *, final):
    def finish(acc):
        xn = x_ref[...] + gate_ref[...] * acc
        o_ref[...] = _rms(xn, gf_ref[...]) if final else xn

    _swiglu_step(h_ref[...], w1_ref, w3_ref, w2_ref, acc_ref, finish)


def _ffn_call(h, x2, gate, gf, w1, w3, w2, seq, final, tm=512, tf=1408):
    r = h.shape[0]
    ff = w1.shape[1]
    tiles_per_seq = seq // tm
    row = pl.BlockSpec((tm, D_MODEL), lambda i, f: (i, 0))
    return pl.pallas_call(
        functools.partial(_ffn_kernel, final=final),
        grid=(r // tm, ff // tf),
        in_specs=[
            row, row,
            pl.BlockSpec((None, 1, D_MODEL), lambda i, f: (i // tiles_per_seq, 0, 0)),
            pl.BlockSpec((1, D_MODEL), lambda i, f: (0, 0)),
            pl.BlockSpec((D_MODEL, tf), lambda i, f: (0, f)),
            pl.BlockSpec((D_MODEL, tf), lambda i, f: (0, f)),
            pl.BlockSpec((tf, D_MODEL), lambda i, f: (f, 0)),
        ],
        out_specs=row,
        out_shape=jax.ShapeDtypeStruct((r, D_MODEL), F32),
        scratch_shapes=[pltpu.VMEM((tm, D_MODEL), F32)],
        compiler_params=_params(("arbitrary", "arbitrary")),
        name="ffn",
    )(h, x2, gate, gf, w1, w3, w2)


MOE_TM = 512
DMA_UNROLL = 8


def _route(top_i, tm):
    t = top_i.shape[1]
    e_flat = top_i.reshape(-1)
    oh = (e_flat[:, None] == jnp.arange(N_EXPERTS, dtype=jnp.int32)[None, :]).astype(jnp.int32)
    csum = jnp.cumsum(oh, axis=0)
    rank = jnp.sum(csum * oh, axis=1) - 1
    counts = csum[-1]
    padded = ((counts + tm - 1) // tm) * tm
    ends = jnp.cumsum(padded)
    pos = jnp.sum(oh * (ends - padded)[None, :], axis=1) + rank
    n_tiles = (2 * t) // tm + N_EXPERTS
    starts = jnp.arange(n_tiles, dtype=jnp.int32) * tm
    tile_e = jnp.sum((starts[:, None] >= ends[None, :]).astype(jnp.int32), axis=1)
    return (pos.astype(jnp.int32), ends.astype(jnp.int32), jnp.minimum(tile_e, N_EXPERTS - 1),
            (ends[-1:] // tm).astype(jnp.int32))


def _dispatch_kernel(pos_ref, ends_ref, h_ref, xs_ref, zero_ref, sem, *, t, td, tm):
    i = pl.program_id(0)
    base = i * td

    @pl.when(i == 0)
    def _():
        zero_ref[...] = jnp.zeros_like(zero_ref)
        n_tiles = xs_ref.shape[0] // tm
        fills = []
        for e in range(N_EXPERTS):
            start = ends_ref[e - 1] if e else 0
            fills.append((ends_ref[e] > start, pl.multiple_of(ends_ref[e] - tm, tm)))
            fills.append((ends_ref[N_EXPERTS - 1] // tm + e < n_tiles,
                          pl.multiple_of(ends_ref[N_EXPERTS - 1] + e * tm, tm)))
        for go, row0 in fills:
            @pl.when(go)
            def _():
                pltpu.make_async_copy(zero_ref, xs_ref.at[pl.ds(row0, tm)], sem).start()
        for go, row0 in fills:
            @pl.when(go)
            def _():
                pltpu.make_async_copy(zero_ref, xs_ref.at[pl.ds(row0, tm)], sem).wait()

    def body(j, carry):
        for u in range(DMA_UNROLL):
            r = j * DMA_UNROLL + u
            for k in range(2):
                dst = pos_ref[k * t + base + r]
                pltpu.make_async_copy(h_ref.at[pl.ds(r, 1)], xs_ref.at[pl.ds(dst, 1)], sem).start(priority=k)
        return carry

    lax.fori_loop(0, td // DMA_UNROLL, body, 0)
    for k in range(2):
        pltpu.make_async_copy(h_ref, xs_ref.at[pl.ds(0, td)], sem).wait()


def _dispatch_call(pos, ends, h, n_rows, tm, td=512):
    t = h.shape[0]
    return pl.pallas_call(
        functools.partial(_dispatch_kernel, t=t, td=td, tm=tm),
        grid_spec=pltpu.PrefetchScalarGridSpec(
            num_scalar_prefetch=2, grid=(t // td,),
            in_specs=[pl.BlockSpec((td, D_MODEL), lambda i, pos, ends: (i, 0))],
            out_specs=pl.BlockSpec(memory_space=pl.ANY),
            scratch_shapes=[pltpu.VMEM((tm, D_MODEL), h.dtype), pltpu.SemaphoreType.DMA(())]),
        out_shape=jax.ShapeDtypeStruct((n_rows, D_MODEL), h.dtype),
        compiler_params=_params(("arbitrary",)),
        name="dispatch",
    )(pos, ends, h)


def _ffn_routed_kernel(te_ref, nu_ref, x_ref, w1_ref, w3_ref, w2_ref, y_ref, acc_ref):
    del te_ref
    used = pl.program_id(0) < nu_ref[0]

    def finish(acc):
        y_ref[...] = acc

    @pl.when(used)
    def _():
        _swiglu_step(x_ref[...].astype(BF16), w1_ref, w3_ref, w2_ref, acc_ref, finish)

    @pl.when(jnp.logical_not(used))
    def _():
        y_ref[...] = jnp.zeros_like(y_ref)


def _ffn_routed_call(tile_e, n_used, xs, w1, w3, w2, tm=MOE_TM, tf=1792):
    r = xs.shape[0]
    ff = w1.shape[2]
    return pl.pallas_call(
        _ffn_routed_kernel,
        grid_spec=pltpu.PrefetchScalarGridSpec(
            num_scalar_prefetch=2, grid=(r // tm, ff // tf),
            in_specs=[
                pl.BlockSpec((tm, D_MODEL), lambda i, f, te, nu: (i, 0)),
                pl.BlockSpec((None, D_MODEL, tf), lambda i, f, te, nu: (te[i], 0, f)),
                pl.BlockSpec((None, D_MODEL, tf), lambda i, f, te, nu: (te[i], 0, f)),
                pl.BlockSpec((None, tf, D_MODEL), lambda i, f, te, nu: (te[i], f, 0)),
            ],
            out_specs=pl.BlockSpec((tm, D_MODEL), lambda i, f, te, nu: (i, 0)),
            scratch_shapes=[pltpu.VMEM((tm, D_MODEL), F32)]),
        out_shape=jax.ShapeDtypeStruct((r, D_MODEL), F32),
        compiler_params=_params(("arbitrary", "arbitrary")),
        name="ffn_routed",
    )(tile_e, n_used, xs, w1, w3, w2)


def _combine_kernel(pos_ref, x_ref, p_ref, gate_ref, gf_ref, y_ref, o_ref, buf, sem, *, t, tc, final):
    i = pl.program_id(0)

    def row_copy(step, slot, r, k):
        src = pos_ref[k * t + step * tc + r]
        return pltpu.make_async_copy(y_ref.at[pl.ds(src, 1)], buf.at[slot, k, pl.ds(r, 1)], sem.at[slot])

    def issue(step, slot):
        def body(j, carry):
            for u in range(DMA_UNROLL):
                for k in range(2):
                    row_copy(step, slot, j * DMA_UNROLL + u, k).start(priority=k)
            return carry
        lax.fori_loop(0, tc // DMA_UNROLL, body, 0)

    @pl.when(i == 0)
    def _():
        issue(0, 0)

    @pl.when(i + 1 < pl.num_programs(0))
    def _():
        issue(i + 1, (i + 1) % 2)

    slot = i % 2
    for k in range(2):
        pltpu.make_async_copy(y_ref.at[pl.ds(0, tc)], buf.at[slot, k], sem.at[slot]).wait()
    p = p_ref[...]
    xn = x_ref[...] + gate_ref[...] * (p[:, 0:1] * buf[slot, 0] + p[:, 1:2] * buf[slot, 1])
    o_ref[...] = _rms(xn, gf_ref[...]) if final else xn


def _combine_call(pos, x2, probs, gate, gf, y, seq, final, tc=256):
    t = x2.shape[0]
    tiles_per_seq = seq // tc
    row = pl.BlockSpec((tc, D_MODEL), lambda i, pos: (i, 0))
    return pl.pallas_call(
        functools.partial(_combine_kernel, t=t, tc=tc, final=final),
        grid_spec=pltpu.PrefetchScalarGridSpec(
            num_scalar_prefetch=1, grid=(t // tc,),
            in_specs=[row, pl.BlockSpec((tc, 2), lambda i, pos: (i, 0)),
                      pl.BlockSpec((None, 1, D_MODEL), lambda i, pos: (i // tiles_per_seq, 0, 0)),
                      pl.BlockSpec((1, D_MODEL), lambda i, pos: (0, 0)),
                      pl.BlockSpec(memory_space=pl.ANY)],
            out_specs=row,
            scratch_shapes=[pltpu.VMEM((2, 2, tc, D_MODEL), F32), pltpu.SemaphoreType.DMA((2,))]),
        out_shape=jax.ShapeDtypeStruct((t, D_MODEL), F32),
        compiler_params=_params(("arbitrary",)),
        name="combine",
    )(pos, x2, probs, gate, gf, y)


def _layout_w_in(w):
    z64 = jnp.zeros((w.shape[0], 64), w.dtype)
    z32 = jnp.zeros((w.shape[0], 32), w.dtype)
    o = 3 * RW
    pw, pa, pg = w[:, o:o + 64], w[:, o + 64:o + 128], w[:, o + 128:o + 256]
    o = 3 * RW + 256
    qa, kva = w[:, o:o + Q_LORA], w[:, o + Q_LORA:o + Q_LORA + KV_LORA]
    o += Q_LORA + KV_LORA
    kr = w[:, o:o + QK_ROPE]
    kr_sw = jnp.concatenate([-kr[:, QK_ROPE // 2:], kr[:, :QK_ROPE // 2]], axis=1)
    gates = w[:, o + QK_ROPE:]
    return jnp.concatenate([w[:, :3 * RW], pw, z64, pa, z64, pg,
                            qa, kva, z64, kr, z32, z64, kr_sw, z32, gates], axis=1).astype(BF16)


def _layout_mu(mu):
    z = jnp.zeros((64,), mu.dtype)
    o = 3 * RW
    return jnp.concatenate([mu[:o + 64], z, mu[o + 64:o + 128], z, mu[o + 128:]])[None, :]


def _layout_wq(w_qb):
    hd = QK_NOPE + QK_ROPE
    w = w_qb.reshape(Q_LORA, MLA_HEADS, hd)
    half = QK_ROPE // 2
    x1, x2 = w[:, :, QK_NOPE:QK_NOPE + half], w[:, :, QK_NOPE + half:]
    zn = jnp.zeros((Q_LORA, MLA_HEADS, QK_NOPE), w.dtype)
    zp = jnp.zeros((Q_LORA, MLA_HEADS, HEAD_PAD - hd), w.dtype)
    wq = jnp.concatenate([w, zp], axis=2).reshape(Q_LORA, -1)
    wqs = jnp.concatenate([zn, -x2, x1, zp], axis=2).reshape(Q_LORA, -1)
    return wq.astype(BF16), wqs.astype(BF16)


def _layout_wkv(w_kvb):
    w = w_kvb.reshape(KV_LORA, MLA_HEADS, QK_NOPE + V_HEAD)
    z = jnp.zeros((KV_LORA, MLA_HEADS, HEAD_PAD - QK_NOPE), w.dtype)
    wk = jnp.concatenate([w[:, :, :QK_NOPE], z], axis=2).reshape(KV_LORA, -1)
    wv = jnp.concatenate([w[:, :, QK_NOPE:], z], axis=2).reshape(KV_LORA, -1)
    return wk.astype(BF16), wv.astype(BF16)


def _layout_mla_out(w):
    w = w.reshape(MLA_HEADS, V_HEAD, D_MODEL)
    z = jnp.zeros((MLA_HEADS, HEAD_PAD - V_HEAD, D_MODEL), w.dtype)
    return jnp.concatenate([w, z], axis=1).reshape(MLA_HEADS * HEAD_PAD, D_MODEL).astype(BF16)


def _rope_freq_row():
    inv_freq = ROPE_THETA ** (-jnp.arange(0, QK_ROPE, 2, dtype=F32) / QK_ROPE)
    z = jnp.zeros((QK_NOPE,), F32)
    zp = jnp.zeros((LANE - QK_NOPE - QK_ROPE,), F32)
    return jnp.concatenate([z, inv_freq, inv_freq, zp])[None, :]


def kernel(x, c, positions, ada_w, ada_b, norm_mix, norm_ffn, norm_final, w_in, tshift_mu, w0, w_up, a0, a_up, g_up, k_k, k_a, r_k, lnx_g, lnx_b, rwkv_out, q_norm, w_qb, kv_norm, w_kvb, mla_out, w_o, ffn_w1, ffn_w3, ffn_w2, router_w, router_b, moe_w1, moe_w3, moe_w2):
    batch, seq, _ = x.shape
    t = batch * seq
    mods = _ada_call(c, ada_w.reshape(2 * DEPTH, D_MODEL, 3 * D_MODEL),
                     ada_b.reshape(2 * DEPTH, 1, 3 * D_MODEL))
    mods = mods.reshape(DEPTH, 2, batch, 3, 1, D_MODEL)
    pos_col = positions.reshape(t, 1)
    invf = _rope_freq_row()
    x2 = x.reshape(t, D_MODEL)
    row = lambda v: v.reshape(1, -1)

    for l in range(DEPTH):
        shift, scale, gate = (mods[l, 0, :, j] for j in range(3))
        ps, mla, gates = _inproj_call(x2, shift, scale, row(norm_mix[l]), _layout_w_in(w_in[l]),
                                      _layout_mu(tshift_mu[l]), seq)
        vecs = [row(v[l]) for v in (w0, a0, k_k, k_a, r_k, lnx_g, lnx_b)]
        lora_pad = lambda w: jnp.pad(w, ((0, LANE - w.shape[0]), (0, 0)))
        rw = _rwkv_call(ps, vecs, lora_pad(w_up[l]), lora_pad(a_up[l]), g_up[l], batch, seq)
        wq, wqs = _layout_wq(w_qb[l])
        wk, wv = _layout_wkv(w_kvb[l])
        q, k, v = _mlaprep_call(mla, pos_col, invf, row(q_norm[l]), row(kv_norm[l]), wq, wqs, wk, wv)
        o = _attn_call(q, k, v, batch, seq)

        shift2, scale2, gate2 = (mods[l, 1, :, j] for j in range(3))
        moe = l % 2 == 1
        router = (router_w[l // 2].T, router_b[l // 2].reshape(N_EXPERTS, 1)) if moe else None
        outs = _mixout_call(rw, o, gates, x2, gate, rwkv_out[l].astype(BF16), _layout_mla_out(mla_out[l]),
                            w_o[l].astype(BF16), row(norm_ffn[l]), shift2, scale2, seq, router=router)
        final = l == DEPTH - 1
        if moe:
            x2, h2, top_i, top_p = outs
            pos, ends, tile_e, n_used = _route(top_i, MOE_TM)
            xs = _dispatch_call(pos, ends, h2, 2 * t + N_EXPERTS * MOE_TM, MOE_TM)
            y = _ffn_routed_call(tile_e, n_used, xs, moe_w1[l // 2].astype(BF16),
                                 moe_w3[l // 2].astype(BF16), moe_w2[l // 2].astype(BF16))
            x2 = _combine_call(pos, x2, top_p.T, gate2, row(norm_final), y, seq, final)
        else:
            x2, h2 = outs
            x2 = _ffn_call(h2, x2, gate2, row(norm_final), ffn_w1[l // 2].astype(BF16),
                           ffn_w3[l // 2].astype(BF16), ffn_w2[l // 2].astype(BF16), seq, final)
    return x2.reshape(batch, seq, D_MODEL)
```

```python
import functools
import math

import jax
import jax.numpy as jnp
from jax import lax
from jax.experimental import pallas as pl
from jax.experimental.pallas import tpu as pltpu

F32 = jnp.float32
BF16 = jnp.bfloat16

D_MODEL = 1024
DEPTH = 2
CHUNK = 64
NORM_EPS = 1e-6
RWKV_HEADS = 8
RWKV_HEAD_DIM = 64
RW = RWKV_HEADS * RWKV_HEAD_DIM
DECAY_LORA = 64
ICLR_LORA = 64
GATE_LORA = 128
DECAY_SCALE = math.exp(-0.5)
LNX_EPS = 64e-5
MLA_HEADS = 8
Q_LORA = 384
KV_LORA = 256
QK_NOPE = 64
QK_ROPE = 32
V_HEAD = 64
ROPE_THETA = 10000.0
LOG2E = math.log2(math.e)
N_EXPERTS = 8

LANE = 128
HEAD_PAD = 128
PS_W = 3 * RW + 3 * LANE
MLA_W = Q_LORA + KV_LORA + 2 * LANE
GATE_W = 2 * D_MODEL
HALF = 256
VMEM_LIMIT = 56 * 1024 * 1024


def _sigmoid(x):
    return 1.0 / (1.0 + jnp.exp(-x))


def _dot(a, b):
    return jnp.dot(a, b, preferred_element_type=F32)


def _dot_nt(a, b):
    return lax.dot_general(a, b, (((1,), (1,)), ((), ())), preferred_element_type=F32)


def _split2(x):
    hi = x.astype(BF16)
    lo = (x - hi.astype(F32)).astype(BF16)
    return hi, lo


def _dot3(a, b):
    ah, al = _split2(a)
    bh, bl = _split2(b)
    return _dot(ah, bh) + _dot(al, bh) + _dot(ah, bl)


def _params(sem):
    return pltpu.CompilerParams(dimension_semantics=sem, vmem_limit_bytes=VMEM_LIMIT)


def _ada_kernel(c_ref, w_ref, b_ref, o_ref):
    c = c_ref[...]
    o_ref[...] = _dot((c * _sigmoid(c)).astype(BF16), w_ref[...].astype(BF16)) + b_ref[...]


def _ada_call(c, ada_w, ada_b):
    n = ada_w.shape[0]
    nb = c.shape[0]
    return pl.pallas_call(
        _ada_kernel,
        grid=(n, 3),
        in_specs=[
            pl.BlockSpec((nb, D_MODEL), lambda i, j: (0, 0)),
            pl.BlockSpec((None, D_MODEL, D_MODEL), lambda i, j: (i, 0, j)),
            pl.BlockSpec((None, 1, D_MODEL), lambda i, j: (i, 0, j)),
        ],
        out_specs=pl.BlockSpec((None, nb, D_MODEL), lambda i, j: (i, 0, j)),
        out_shape=jax.ShapeDtypeStruct((n, nb, 3 * D_MODEL), F32),
        compiler_params=_params(("arbitrary", "arbitrary")),
        name="ada",
    )(c, ada_w, ada_b)


def _norm_mod(x, g, scale, shift):
    var = jnp.mean(x * x, axis=-1, keepdims=True)
    return (x * lax.rsqrt(var + NORM_EPS) * g) * (1.0 + scale) + shift


def _inproj_kernel(x_ref, shift_ref, scale_ref, g_ref, w_ref, mu_ref,
                   ps_ref, mla_ref, gates_ref, carry_ref, *, tiles_per_seq, tm):
    i = pl.program_id(0)

    @pl.when(i % tiles_per_seq == 0)
    def _():
        carry_ref[...] = jnp.zeros_like(carry_ref)

    carry = carry_ref[0:1, :]
    row = lax.broadcasted_iota(jnp.int32, (INPROJ_SUB, PS_W), 0)
    for s in range(0, tm, INPROJ_SUB):
        r = slice(s, s + INPROJ_SUB)
        h = _norm_mod(x_ref[r, :], g_ref[...], scale_ref[...], shift_ref[...]).astype(BF16)
        p = _dot(h, w_ref[:, 0:PS_W])
        prev = jnp.where(row == 0, carry, pltpu.roll(p, 1, axis=0))
        carry = p[INPROJ_SUB - 1:INPROJ_SUB, :]
        ps_ref[r, :] = p + (prev - p) * mu_ref[...]
        mla_ref[r, :] = _dot(h, w_ref[:, PS_W:PS_W + MLA_W])
        gates_ref[r, :] = _sigmoid(_dot(h, w_ref[:, PS_W + MLA_W:])).astype(BF16)
    carry_ref[0:1, :] = carry


INPROJ_SUB = 256


def _inproj_call(x2, shift, scale, g, w_pad, mu_pad, seq, tm=512):
    t = x2.shape[0]
    tiles_per_seq = seq // tm
    bmap = lambda i: (i // tiles_per_seq, 0, 0)
    wtot = PS_W + MLA_W + GATE_W
    return pl.pallas_call(
        functools.partial(_inproj_kernel, tiles_per_seq=tiles_per_seq, tm=tm),
        grid=(t // tm,),
        in_specs=[
            pl.BlockSpec((tm, D_MODEL), lambda i: (i, 0)),
            pl.BlockSpec((None, 1, D_MODEL), bmap),
            pl.BlockSpec((None, 1, D_MODEL), bmap),
            pl.BlockSpec((1, D_MODEL), lambda i: (0, 0)),
            pl.BlockSpec((D_MODEL, wtot), lambda i: (0, 0)),
            pl.BlockSpec((1, PS_W), lambda i: (0, 0)),
        ],
        out_specs=[
            pl.BlockSpec((tm, PS_W), lambda i: (i, 0)),
            pl.BlockSpec((tm, MLA_W), lambda i: (i, 0)),
            pl.BlockSpec((tm, GATE_W), lambda i: (i, 0)),
        ],
        out_shape=[
            jax.ShapeDtypeStruct((t, PS_W), F32),
            jax.ShapeDtypeStruct((t, MLA_W), F32),
            jax.ShapeDtypeStruct((t, GATE_W), BF16),
        ],
        scratch_shapes=[pltpu.VMEM((8, PS_W), F32)],
        compiler_params=_params(("arbitrary",)),
        name="inproj",
    )(x2, shift, scale, g, w_pad, mu_pad)


def _blockdiag(xb, bdmask):
    return jnp.tile(xb, (HALF // CHUNK, 1)) * bdmask


def _bd_mm(dot, a, b, bdmask, passes):
    if passes == 1:
        return dot(a.astype(BF16), _blockdiag(b.astype(BF16), bdmask))
    ah, al = _split2(a)
    bh, bl = _split2(b)
    r = dot(jnp.concatenate([ah, al], axis=0), _blockdiag(bh, bdmask))
    n = a.shape[0]
    return r[:n] + r[n:] + dot(ah, _blockdiag(bl, bdmask))


def _bd_nn(a, b, bdmask, passes):
    return _bd_mm(_dot, a, b, bdmask, passes)


def _bd_nt(a, b, bdmask, passes):
    return _bd_mm(_dot_nt, a, b, bdmask, passes)


def _bd_tn(a, b, bdmask_f, passes):
    full = _dot3(a.T, b) if passes == 3 else _dot(a.T.astype(BF16), b.astype(BF16))
    return jnp.sum((full * bdmask_f).reshape(HALF // CHUNK, CHUNK, HALF), axis=0)


P_SCORE = 1
P_TINV = 1
P_APPLY = 1
P_GH = 1
P_STATE = 1


def _halves(fn, *xs):
    outs = [fn(*[x[:, h * HALF:(h + 1) * HALF] for x in xs]) for h in range(RW // HALF)]
    return jnp.concatenate(outs, axis=1)


def _rwkv_kernel(ps_ref, w0_ref, a0_ref, kk_ref, ka_ref, rk_ref, lg_ref, lb_ref,
                 wup_ref, aup_ref, gup_ref, out_ref, s_ref, *, nc):
    @pl.when(pl.program_id(1) == 0)
    def _():
        s_ref[...] = jnp.zeros_like(s_ref)

    C = CHUNK
    r_i = lax.broadcasted_iota(jnp.int32, (HALF, HALF), 0)
    c_i = lax.broadcasted_iota(jnp.int32, (HALF, HALF), 1)
    bdsame = (r_i // C) == (c_i // C)
    bdmask = bdsame.astype(BF16)
    bdmask_f = bdsame.astype(F32)
    t_i = lax.broadcasted_iota(jnp.int32, (C, HALF), 0)
    i_i = lax.broadcasted_iota(jnp.int32, (C, HALF), 1) % C
    strict = i_i < t_i
    incl = i_i <= t_i
    eye_l = (i_i == t_i).astype(F32)
    n = nc * C
    tr = lax.broadcasted_iota(jnp.int32, (n, n), 0)
    tc = lax.broadcasted_iota(jnp.int32, (n, n), 1)
    tril = ((tc <= tr) & (tc // C == tr // C)).astype(BF16)

    def headsum(x, split=False):
        def one(xh):
            if not split:
                return _dot(xh.astype(BF16), bdmask)
            hi, lo = _split2(xh)
            return _dot(hi, bdmask) + _dot(lo, bdmask)
        return _halves(one, x)

    def cumsum_rows(x):
        hi = x.astype(BF16)
        r1 = x - hi.astype(F32)
        mid = r1.astype(BF16)
        lo = (r1 - mid.astype(F32)).astype(BF16)
        return _dot(tril, hi) + _dot(tril, mid) + _dot(tril, lo)

    p_r = ps_ref[:, 0:RW]
    p_k = ps_ref[:, RW:2 * RW]
    p_v = ps_ref[:, 2 * RW:3 * RW]
    p_w = ps_ref[:, 3 * RW:3 * RW + LANE]
    p_a = ps_ref[:, 3 * RW + LANE:3 * RW + 2 * LANE]
    p_g = ps_ref[:, 3 * RW + 2 * LANE:3 * RW + 3 * LANE]
    logw = -DECAY_SCALE * _sigmoid(w0_ref[...] + _dot3(jnp.tanh(p_w), wup_ref[...]))
    iclr = _sigmoid(a0_ref[...] + _dot(p_a.astype(BF16), aup_ref[...].astype(BF16)))
    gate = _dot(_sigmoid(p_g).astype(BF16), gup_ref[...].astype(BF16))
    kk = p_k * kk_ref[...]
    k2 = p_k * (1.0 + (iclr - 1.0) * ka_ref[...])
    sums = headsum(jnp.concatenate([kk * kk, p_r * k2 * rk_ref[...]], axis=0))
    kk = kk * lax.rsqrt(sums[:n] + 1e-12)
    bonus = sums[n:] * p_v
    b_v = kk * iclr
    cum = cumsum_rows(logw)
    cum_c = jnp.concatenate(
        [jnp.broadcast_to(cum[(c + 1) * C - 1:(c + 1) * C, :], (C, RW)) for c in range(nc)], axis=0)
    e_neg = jnp.exp(-cum)
    e_rem = jnp.exp(cum_c - cum)
    a_all = -kk * jnp.exp(cum - logw)
    b_all = b_v * e_neg
    k_all = k2 * e_neg
    r_all = p_r * jnp.exp(cum)
    bg_all = b_v * e_rem
    kg_all = k2 * e_rem
    gam_all = jnp.exp(cum_c)

    nh = RW // HALF
    units = [(c, h) for c in range(nc) for h in range(nh)]

    def cut(x, u):
        c, h = u
        return x[c * C:(c + 1) * C, h * HALF:(h + 1) * HALF]

    each = lambda fn, *lists: [fn(*xs) for xs in zip(*lists)]
    stack = lambda x, y: jnp.concatenate([x, y], axis=0)
    top = lambda x: x[:C]
    bot = lambda x: x[C:]
    nn = lambda p: (lambda a, b: _bd_nn(a, b, bdmask, p))

    a_t, b_t, k_t, r_t, b_g, k_g, v_u = ([cut(x, u) for u in units]
                                         for x in (a_all, b_all, k_all, r_all, bg_all, kg_all, p_v))
    gam = [gam_all[c * C:c * C + 1, h * HALF:(h + 1) * HALF] for c, h in units]
    ar = each(stack, a_t, r_t)
    sb = each(lambda x, y: _bd_nt(x, y, bdmask, P_SCORE), ar, b_t)
    sk = each(lambda x, y: _bd_nt(x, y, bdmask, P_SCORE), ar, k_t)
    l_ab = [jnp.where(strict, top(x), 0.0) for x in sb]
    q_b = [jnp.where(incl, bot(x), 0.0) for x in sb]
    m_ak = [jnp.where(strict, top(x), 0.0) for x in sk]
    q_k = [jnp.where(incl, bot(x), 0.0) for x in sk]
    t_m = [eye_l + x for x in l_ab]
    pw = each(nn(P_TINV), l_ab, l_ab)
    for _ in range(4):
        tp = each(nn(P_TINV), each(stack, t_m, pw), pw)
        t_m = each(lambda t, x: t + top(x), t_m, tp)
        pw = [bot(x) for x in tp]
    t_m = each(lambda t, x: t + x, t_m, each(nn(P_TINV), t_m, pw))
    mq = each(nn(P_APPLY), each(stack, m_ak, q_k), v_u)
    a_h = each(nn(P_APPLY), t_m, a_t)
    u0 = each(nn(P_APPLY), t_m, [top(x) for x in mq])
    r_h = each(lambda r, x: r + x, r_t, each(nn(P_APPLY), q_b, a_h))
    y0 = each(lambda m, x: bot(m) + x, mq, each(nn(P_APPLY), q_b, u0))
    g_l = each(lambda a, b, g: _bd_tn(a, b, bdmask_f, P_GH) + eye_l * g, a_h, b_g, gam)
    h_l = each(lambda u, v, b, k: _bd_tn(stack(u, v), stack(b, k), bdmask_f, P_GH), u0, v_u, b_g, k_g)

    s_all = s_ref[...]
    s_cur = [s_all[:, h * HALF:(h + 1) * HALF] for h in range(nh)]
    ys = []
    for c in range(nc):
        idx = [c * nh + h for h in range(nh)]
        yc = [_bd_nt(r_h[i], s, bdmask, P_STATE) + y0[i] for i, s in zip(idx, s_cur)]
        s_cur = [_bd_nn(s, g_l[i], bdmask, P_STATE) + h_l[i] for i, s in zip(idx, s_cur)]
        ys.append(jnp.concatenate(yc, axis=1))
    s_ref[...] = jnp.concatenate(s_cur, axis=1)
    y = jnp.concatenate(ys, axis=0)
    inv_n = 1.0 / RWKV_HEAD_DIM
    mu = headsum(y, split=True) * inv_n
    d = y - mu
    var = headsum(d * d) * inv_n
    yn = d * lax.rsqrt(var + LNX_EPS) * lg_ref[...] + lb_ref[...]
    out_ref[...] = ((yn + bonus) * gate).astype(BF16)


def _rwkv_call(ps, vecs, wup, aup, gup, batch, seq, nc=4):
    t = ps.shape[0]
    rows = nc * CHUNK
    steps = seq // rows
    vec_spec = pl.BlockSpec((1, RW), lambda b, j: (0, 0))
    lora_spec = pl.BlockSpec((LANE, RW), lambda b, j: (0, 0))
    return pl.pallas_call(
        functools.partial(_rwkv_kernel, nc=nc),
        grid=(batch, steps),
        in_specs=[pl.BlockSpec((rows, PS_W), lambda b, j: (b * steps + j, 0))]
                 + [vec_spec] * 7 + [lora_spec] * 3,
        out_specs=pl.BlockSpec((rows, RW), lambda b, j: (b * steps + j, 0)),
        out_shape=jax.ShapeDtypeStruct((t, RW), BF16),
        scratch_shapes=[pltpu.VMEM((CHUNK, RW), F32)],
        compiler_params=_params(("arbitrary", "arbitrary")),
        name="rwkv",
    )(ps, *vecs, wup, aup, gup)


def _mlaprep_kernel(mla_ref, pos_ref, invf_ref, qn_ref, kvn_ref, wq_ref, wqs_ref, wk_ref, wv_ref,
                    q_ref, k_ref, v_ref):
    def rms(x, g):
        return (x * lax.rsqrt(jnp.mean(x * x, axis=-1, keepdims=True) + NORM_EPS) * g).astype(BF16)

    ang = pos_ref[...].astype(F32) * invf_ref[...]
    cos = jnp.cos(ang)
    sin = jnp.sin(ang)
    lane = lax.broadcasted_iota(jnp.int32, cos.shape, 1)
    cos_q = jnp.tile(jnp.where(lane < QK_NOPE, 1.0, cos), (1, MLA_HEADS))
    sin_q = jnp.tile(sin, (1, MLA_HEADS))
    qn = rms(mla_ref[:, 0:Q_LORA], qn_ref[...])
    q = _dot(qn, wq_ref[...]) * cos_q + _dot(qn, wqs_ref[...]) * sin_q
    q_ref[...] = (q * ((QK_NOPE + QK_ROPE) ** -0.5 * LOG2E)).astype(BF16)
    kvn = rms(mla_ref[:, Q_LORA:Q_LORA + KV_LORA], kvn_ref[...])
    o = Q_LORA + KV_LORA
    k_pe = mla_ref[:, o:o + LANE] * cos + mla_ref[:, o + LANE:o + 2 * LANE] * sin
    k_ref[...] = (_dot(kvn, wk_ref[...]) + jnp.tile(k_pe, (1, MLA_HEADS))).astype(BF16)
    v_ref[...] = _dot(kvn, wv_ref[...]).astype(BF16)


def _mlaprep_call(mla, pos, invf, qn, kvn, wq, wqs, wk, wv, tm=512):
    t = mla.shape[0]
    hw = MLA_HEADS * HEAD_PAD
    full = lambda shape: pl.BlockSpec(shape, lambda i: (0, 0))
    row = lambda w: pl.BlockSpec((tm, w), lambda i: (i, 0))
    return pl.pallas_call(
        _mlaprep_kernel,
        grid=(t // tm,),
        in_specs=[row(MLA_W), row(1), full((1, LANE)), full((1, Q_LORA)), full((1, KV_LORA)),
                  full((Q_LORA, hw)), full((Q_LORA, hw)), full((KV_LORA, hw)), full((KV_LORA, hw))],
        out_specs=[row(hw)] * 3,
        out_shape=[jax.ShapeDtypeStruct((t, hw), BF16)] * 3,
        compiler_params=_params(("arbitrary",)),
        name="mlaprep",
    )(mla, pos, invf, qn, kvn, wq, wqs, wk, wv)


def _attn_kernel(q_ref, k_ref, v_ref, o_ref, *, seq, tq):
    r_i = lax.broadcasted_iota(jnp.int32, (tq, tq), 0) // CHUNK
    c_i = lax.broadcasted_iota(jnp.int32, (tq, tq), 1) // CHUNK
    mask = c_i <= r_i
    def scores(qi):
        lo = qi * tq
        q = q_ref[lo:lo + tq, :]
        sd = jnp.where(mask, _dot_nt(q, k_ref[lo:lo + tq, :]), -1e30)
        so = _dot_nt(q, k_ref[0:lo, :]) if qi > 0 else None
        return sd, so

    n_q = seq // tq
    nxt = scores(0)
    for qi in range(n_q):
        lo = qi * tq
        sd, so = nxt
        if qi + 1 < n_q:
            nxt = scores(qi + 1)
        m = jnp.max(sd, axis=-1, keepdims=True)
        if qi > 0:
            m = jnp.maximum(m, jnp.max(so, axis=-1, keepdims=True))
        pd = jnp.exp2(sd - m)
        l = jnp.sum(pd, axis=-1, keepdims=True)
        acc = _dot(pd.astype(BF16), v_ref[lo:lo + tq, :])
        if qi > 0:
            po = jnp.exp2(so - m)
            l = l + jnp.sum(po, axis=-1, keepdims=True)
            acc = acc + _dot(po.astype(BF16), v_ref[0:lo, :])
        o_ref[lo:lo + tq, :] = (acc / l).astype(BF16)


def _attn_call(q, k, v, batch, seq, tq=256):
    spec = pl.BlockSpec((seq, HEAD_PAD), lambda b, h: (b, h))
    return pl.pallas_call(
        functools.partial(_attn_kernel, seq=seq, tq=tq),
        grid=(batch, MLA_HEADS),
        in_specs=[spec] * 3,
        out_specs=spec,
        out_shape=jax.ShapeDtypeStruct(q.shape, BF16),
        compiler_params=_params(("arbitrary", "arbitrary")),
        name="attn",
    )(q, k, v)


MIX_SUB = 256


def _mixout_kernel(*refs, router):
    if router:
        (rw_ref, o_ref, gates_ref, x_ref, gate_ref, wr_ref, wm_ref, wo_ref,
         g2_ref, shift2_ref, scale2_ref, rwt_ref, rb_ref,
         xo_ref, h2_ref, ti_ref, tp_ref) = refs
    else:
        (rw_ref, o_ref, gates_ref, x_ref, gate_ref, wr_ref, wm_ref, wo_ref,
         g2_ref, shift2_ref, scale2_ref, xo_ref, h2_ref) = refs
    n = x_ref.shape[0]
    subs = [slice(s, s + MIX_SUB) for s in range(0, n, MIX_SUB)]
    yab = [(_dot(rw_ref[r, :], wr_ref[...]), _dot(o_ref[r, :], wm_ref[...])) for r in subs]
    merged = [gates_ref[r, 0:D_MODEL].astype(F32) * ya + gates_ref[r, D_MODEL:].astype(F32) * yb
              for r, (ya, yb) in zip(subs, yab)]
    mixed = [_dot(m.astype(BF16), wo_ref[...]) for m in merged]
    h2s = []
    for r, mx in zip(subs, mixed):
        xn = x_ref[r, :] + gate_ref[...] * mx
        xo_ref[r, :] = xn
        h2s.append(_norm_mod(xn, g2_ref[...], scale2_ref[...], shift2_ref[...]))
        h2_ref[r, :] = h2s[-1].astype(h2_ref.dtype)
    if router:
        h2 = jnp.concatenate(h2s, axis=0)
        hh, hl = _split2(h2)
        wh, wl = _split2(rwt_ref[...])
        logits = _dot_nt(wh, hh) + _dot_nt(wl, hh) + _dot_nt(wh, hl) + rb_ref[...]
        e_i = lax.broadcasted_iota(jnp.int32, logits.shape, 0)
        m1 = jnp.max(logits, axis=0, keepdims=True)
        i1 = jnp.min(jnp.where(logits == m1, e_i, N_EXPERTS), axis=0, keepdims=True)
        rest = jnp.where(e_i == i1, -jnp.inf, logits)
        m2 = jnp.max(rest, axis=0, keepdims=True)
        i2 = jnp.min(jnp.where(rest == m2, e_i, N_EXPERTS), axis=0, keepdims=True)
        e2 = jnp.exp(m2 - m1)
        den = 1.0 + e2
        ti_ref[0:1, :] = i1
        ti_ref[1:2, :] = i2
        tp_ref[0:1, :] = 1.0 / den
        tp_ref[1:2, :] = e2 / den


def _mixout_call(rw, o, gates, x2, gate, wr, wm, wo, g2, shift2, scale2, seq, router=None, tm=512):
    t = x2.shape[0]
    tiles_per_seq = seq // tm
    bmap = lambda i: (i // tiles_per_seq, 0, 0)
    full = lambda shape: pl.BlockSpec(shape, lambda i: (0, 0))
    row = lambda w: pl.BlockSpec((tm, w), lambda i: (i, 0))
    mod = pl.BlockSpec((None, 1, D_MODEL), bmap)
    in_specs = [row(RW), row(MLA_HEADS * HEAD_PAD), row(GATE_W), row(D_MODEL), mod,
                full(wr.shape), full(wm.shape), full(wo.shape), full((1, D_MODEL)), mod, mod]
    args = [rw, o, gates, x2, gate, wr, wm, wo, g2, shift2, scale2]
    out_specs = [row(D_MODEL), row(D_MODEL)]
    out_shape = [jax.ShapeDtypeStruct((t, D_MODEL), F32),
                 jax.ShapeDtypeStruct((t, D_MODEL), BF16 if router is None else F32)]
    if router is not None:
        in_specs += [full((N_EXPERTS, D_MODEL)), full((N_EXPERTS, 1))]
        args += list(router)
        out_specs += [pl.BlockSpec((2, tm), lambda i: (0, i))] * 2
        out_shape += [jax.ShapeDtypeStruct((2, t), jnp.int32), jax.ShapeDtypeStruct((2, t), F32)]
    return pl.pallas_call(
        functools.partial(_mixout_kernel, router=router is not None),
        grid=(t // tm,),
        in_specs=in_specs,
        out_specs=out_specs,
        out_shape=out_shape,
        compiler_params=_params(("arbitrary",)),
        name="mixout",
    )(*args)


def _swiglu_step(h, w1_ref, w3_ref, w2_ref, acc_ref, finish):
    f = pl.program_id(1)

    @pl.when(f == 0)
    def _():
        acc_ref[...] = jnp.zeros_like(acc_ref)

    a1 = _dot(h, w1_ref[...])
    a3 = _dot(h, w3_ref[...])
    act = (a1 * _sigmoid(a1)) * a3
    acc_ref[...] += _dot(act.astype(BF16), w2_ref[...])

    @pl.when(f == pl.num_programs(1) - 1)
    def _():
        finish(acc_ref[...])


def _rms(x, g):
    return x * lax.rsqrt(jnp.mean(x * x, axis=-1, keepdims=True) + NORM_EPS) * g


def _ffn_kernel(h_ref, x_ref, gate_ref, gf_ref, w1_ref, w3_ref, w2_ref, o_ref, acc_ref, *, final):
    def finish(acc):
        xn = x_ref[...] + gate_ref[...] * acc
        o_ref[...] = _rms(xn, gf_ref[...]) if final else xn

    _swiglu_step(h_ref[...], w1_ref, w3_ref, w2_ref, acc_ref, finish)


def _ffn_call(h, x2, gate, gf, w1, w3, w2, seq, final, tm=512, tf=1408):
    r = h.shape[0]
    ff = w1.shape[1]
    tiles_per_seq = seq // tm
    row = pl.BlockSpec((tm, D_MODEL), lambda i, f: (i, 0))
    return pl.pallas_call(
        functools.partial(_ffn_kernel, final=final),
        grid=(r // tm, ff // tf),
        in_specs=[
            row, row,
            pl.BlockSpec((None, 1, D_MODEL), lambda i, f: (i // tiles_per_seq, 0, 0)),
            pl.BlockSpec((1, D_MODEL), lambda i, f: (0, 0)),
            pl.BlockSpec((D_MODEL, tf), lambda i, f: (0, f)),
            pl.BlockSpec((D_MODEL, tf), lambda i, f: (0, f)),
            pl.BlockSpec((tf, D_MODEL), lambda i, f: (f, 0)),
        ],
        out_specs=row,
        out_shape=jax.ShapeDtypeStruct((r, D_MODEL), F32),
        scratch_shapes=[pltpu.VMEM((tm, D_MODEL), F32)],
        compiler_params=_params(("arbitrary", "arbitrary")),
        name="ffn",
    )(h, x2, gate, gf, w1, w3, w2)


MOE_TM = 512


def _route(top_i, tm):
    t = top_i.shape[1]
    e_flat = top_i.reshape(-1)
    oh = (e_flat[:, None] == jnp.arange(N_EXPERTS, dtype=jnp.int32)[None, :]).astype(jnp.int32)
    csum = jnp.cumsum(oh, axis=0)
    rank = jnp.sum(csum * oh, axis=1) - 1
    counts = csum[-1]
    padded = ((counts + tm - 1) // tm) * tm
    ends = jnp.cumsum(padded)
    pos = jnp.sum(oh * (ends - padded)[None, :], axis=1) + rank
    n_tiles = (2 * t) // tm + N_EXPERTS
    starts = jnp.arange(n_tiles, dtype=jnp.int32) * tm
    tile_e = jnp.sum((starts[:, None] >= ends[None, :]).astype(jnp.int32), axis=1)
    return (pos.astype(jnp.int32), ends.astype(jnp.int32), jnp.minimum(tile_e, N_EXPERTS - 1),
            (ends[-1:] // tm).astype(jnp.int32))


def _dispatch_kernel(pos_ref, ends_ref, h_ref, xs_ref, zero_ref, sem, *, t, td, tm):
    i = pl.program_id(0)
    base = i * td

    @pl.when(i == 0)
    def _():
        zero_ref[...] = jnp.zeros_like(zero_ref)
        n_tiles = xs_ref.shape[0] // tm
        fills = []
        for e in range(N_EXPERTS):
            start = ends_ref[e - 1] if e else 0
            fills.append((ends_ref[e] > start, pl.multiple_of(ends_ref[e] - tm, tm)))
            fills.append((ends_ref[N_EXPERTS - 1] // tm + e < n_tiles,
                          pl.multiple_of(ends_ref[N_EXPERTS - 1] + e * tm, tm)))
        for go, row0 in fills:
            @pl.when(go)
            def _():
                pltpu.make_async_copy(zero_ref, xs_ref.at[pl.ds(row0, tm)], sem).start()
        for go, row0 in fills:
            @pl.when(go)
            def _():
                pltpu.make_async_copy(zero_ref, xs_ref.at[pl.ds(row0, tm)], sem).wait()

    for r in range(td):
        for k in range(2):
            dst = pos_ref[k * t + base + r]
            pltpu.make_async_copy(h_ref.at[pl.ds(r, 1)], xs_ref.at[pl.ds(dst, 1)], sem).start()
    for k in range(2):
        pltpu.make_async_copy(h_ref, xs_ref.at[pl.ds(0, td)], sem).wait()


def _dispatch_call(pos, ends, h, n_rows, tm, td=256):
    t = h.shape[0]
    return pl.pallas_call(
        functools.partial(_dispatch_kernel, t=t, td=td, tm=tm),
        grid_spec=pltpu.PrefetchScalarGridSpec(
            num_scalar_prefetch=2, grid=(t // td,),
            in_specs=[pl.BlockSpec((td, D_MODEL), lambda i, pos, ends: (i, 0))],
            out_specs=pl.BlockSpec(memory_space=pl.ANY),
            scratch_shapes=[pltpu.VMEM((tm, D_MODEL), h.dtype), pltpu.SemaphoreType.DMA(())]),
        out_shape=jax.ShapeDtypeStruct((n_rows, D_MODEL), h.dtype),
        compiler_params=_params(("arbitrary",)),
        name="dispatch",
    )(pos, ends, h)


def _ffn_routed_kernel(te_ref, nu_ref, x_ref, w1_ref, w3_ref, w2_ref, y_ref, acc_ref):
    del te_ref
    used = pl.program_id(0) < nu_ref[0]

    def finish(acc):
        y_ref[...] = acc

    @pl.when(used)
    def _():
        _swiglu_step(x_ref[...].astype(BF16), w1_ref, w3_ref, w2_ref, acc_ref, finish)

    @pl.when(jnp.logical_not(used))
    def _():
        y_ref[...] = jnp.zeros_like(y_ref)


def _ffn_routed_call(tile_e, n_used, xs, w1, w3, w2, tm=MOE_TM, tf=1792):
    r = xs.shape[0]
    ff = w1.shape[2]
    return pl.pallas_call(
        _ffn_routed_kernel,
        grid_spec=pltpu.PrefetchScalarGridSpec(
            num_scalar_prefetch=2, grid=(r // tm, ff // tf),
            in_specs=[
                pl.BlockSpec((tm, D_MODEL), lambda i, f, te, nu: (i, 0)),
                pl.BlockSpec((None, D_MODEL, tf), lambda i, f, te, nu: (te[i], 0, f)),
                pl.BlockSpec((None, D_MODEL, tf), lambda i, f, te, nu: (te[i], 0, f)),
                pl.BlockSpec((None, tf, D_MODEL), lambda i, f, te, nu: (te[i], f, 0)),
            ],
            out_specs=pl.BlockSpec((tm, D_MODEL), lambda i, f, te, nu: (i, 0)),
            scratch_shapes=[pltpu.VMEM((tm, D_MODEL), F32)]),
        out_shape=jax.ShapeDtypeStruct((r, D_MODEL), F32),
        compiler_params=_params(("arbitrary", "arbitrary")),
        name="ffn_routed",
    )(tile_e, n_used, xs, w1, w3, w2)


def _combine_kernel(pos_ref, x_ref, p_ref, gate_ref, gf_ref, y_ref, o_ref, buf, sem, *, t, tc, final):
    i = pl.program_id(0)

    def row_copy(step, slot, r, k):
        src = pos_ref[k * t + step * tc + r]
        return pltpu.make_async_copy(y_ref.at[pl.ds(src, 1)], buf.at[slot, k, pl.ds(r, 1)], sem.at[slot])

    def issue(step, slot):
        for r in range(tc):
            for k in range(2):
                row_copy(step, slot, r, k).start()

    @pl.when(i == 0)
    def _():
        issue(0, 0)

    @pl.when(i + 1 < pl.num_programs(0))
    def _():
        issue(i + 1, (i + 1) % 2)

    slot = i % 2
    for k in range(2):
        pltpu.make_async_copy(y_ref.at[pl.ds(0, tc)], buf.at[slot, k], sem.at[slot]).wait()
    p = p_ref[...]
    xn = x_ref[...] + gate_ref[...] * (p[:, 0:1] * buf[slot, 0] + p[:, 1:2] * buf[slot, 1])
    o_ref[...] = _rms(xn, gf_ref[...]) if final else xn


def _combine_call(pos, x2, probs, gate, gf, y, seq, final, tc=256):
    t = x2.shape[0]
    tiles_per_seq = seq // tc
    row = pl.BlockSpec((tc, D_MODEL), lambda i, pos: (i, 0))
    return pl.pallas_call(
        functools.partial(_combine_kernel, t=t, tc=tc, final=final),
        grid_spec=pltpu.PrefetchScalarGridSpec(
            num_scalar_prefetch=1, grid=(t // tc,),
            in_specs=[row, pl.BlockSpec((tc, 2), lambda i, pos: (i, 0)),
                      pl.BlockSpec((None, 1, D_MODEL), lambda i, pos: (i // tiles_per_seq, 0, 0)),
                      pl.BlockSpec((1, D_MODEL), lambda i, pos: (0, 0)),
                      pl.BlockSpec(memory_space=pl.ANY)],
            out_specs=row,
            scratch_shapes=[pltpu.VMEM((2, 2, tc, D_MODEL), F32), pltpu.SemaphoreType.DMA((2,))]),
        out_shape=jax.ShapeDtypeStruct((t, D_MODEL), F32),
        compiler_params=_params(("arbitrary",)),
        name="combine",
    )(pos, x2, probs, gate, gf, y)


def _layout_w_in(w):
    z64 = jnp.zeros((w.shape[0], 64), w.dtype)
    z32 = jnp.zeros((w.shape[0], 32), w.dtype)
    o = 3 * RW
    pw, pa, pg = w[:, o:o + 64], w[:, o + 64:o + 128], w[:, o + 128:o + 256]
    o = 3 * RW + 256
    qa, kva = w[:, o:o + Q_LORA], w[:, o + Q_LORA:o + Q_LORA + KV_LORA]
    o += Q_LORA + KV_LORA
    kr = w[:, o:o + QK_ROPE]
    kr_sw = jnp.concatenate([-kr[:, QK_ROPE // 2:], kr[:, :QK_ROPE // 2]], axis=1)
    gates = w[:, o + QK_ROPE:]
    return jnp.concatenate([w[:, :3 * RW], pw, z64, pa, z64, pg,
                            qa, kva, z64, kr, z32, z64, kr_sw, z32, gates], axis=1).astype(BF16)


def _layout_mu(mu):
    z = jnp.zeros((64,), mu.dtype)
    o = 3 * RW
    return jnp.concatenate([mu[:o + 64], z, mu[o + 64:o + 128], z, mu[o + 128:]])[None, :]


def _layout_wq(w_qb):
    hd = QK_NOPE + QK_ROPE
    w = w_qb.reshape(Q_LORA, MLA_HEADS, hd)
    half = QK_ROPE // 2
    x1, x2 = w[:, :, QK_NOPE:QK_NOPE + half], w[:, :, QK_NOPE + half:]
    zn = jnp.zeros((Q_LORA, MLA_HEADS, QK_NOPE), w.dtype)
    zp = jnp.zeros((Q_LORA, MLA_HEADS, HEAD_PAD - hd), w.dtype)
    wq = jnp.concatenate([w, zp], axis=2).reshape(Q_LORA, -1)
    wqs = jnp.concatenate([zn, -x2, x1, zp], axis=2).reshape(Q_LORA, -1)
    return wq.astype(BF16), wqs.astype(BF16)


def _layout_wkv(w_kvb):
    w = w_kvb.reshape(KV_LORA, MLA_HEADS, QK_NOPE + V_HEAD)
    z = jnp.zeros((KV_LORA, MLA_HEADS, HEAD_PAD - QK_NOPE), w.dtype)
    wk = jnp.concatenate([w[:, :, :QK_NOPE], z], axis=2).reshape(KV_LORA, -1)
    wv = jnp.concatenate([w[:, :, QK_NOPE:], z], axis=2).reshape(KV_LORA, -1)
    return wk.astype(BF16), wv.astype(BF16)


def _layout_mla_out(w):
    w = w.reshape(MLA_HEADS, V_HEAD, D_MODEL)
    z = jnp.zeros((MLA_HEADS, HEAD_PAD - V_HEAD, D_MODEL), w.dtype)
    return jnp.concatenate([w, z], axis=1).reshape(MLA_HEADS * HEAD_PAD, D_MODEL).astype(BF16)


def _rope_freq_row():
    inv_freq = ROPE_THETA ** (-jnp.arange(0, QK_ROPE, 2, dtype=F32) / QK_ROPE)
    z = jnp.zeros((QK_NOPE,), F32)
    zp = jnp.zeros((LANE - QK_NOPE - QK_ROPE,), F32)
    return jnp.concatenate([z, inv_freq, inv_freq, zp])[None, :]


def kernel(x, c, positions, ada_w, ada_b, norm_mix, norm_ffn, norm_final, w_in, tshift_mu, w0, w_up, a0, a_up, g_up, k_k, k_a, r_k, lnx_g, lnx_b, rwkv_out, q_norm, w_qb, kv_norm, w_kvb, mla_out, w_o, ffn_w1, ffn_w3, ffn_w2, router_w, router_b, moe_w1, moe_w3, moe_w2):
    batch, seq, _ = x.shape
    t = batch * seq
    mods = _ada_call(c, ada_w.reshape(2 * DEPTH, D_MODEL, 3 * D_MODEL),
                     ada_b.reshape(2 * DEPTH, 1, 3 * D_MODEL))
    mods = mods.reshape(DEPTH, 2, batch, 3, 1, D_MODEL)
    pos_col = positions.reshape(t, 1)
    invf = _rope_freq_row()
    x2 = x.reshape(t, D_MODEL)
    row = lambda v: v.reshape(1, -1)

    for l in range(DEPTH):
        shift, scale, gate = (mods[l, 0, :, j] for j in range(3))
        ps, mla, gates = _inproj_call(x2, shift, scale, row(norm_mix[l]), _layout_w_in(w_in[l]),
                                      _layout_mu(tshift_mu[l]), seq)
        vecs = [row(v[l]) for v in (w0, a0, k_k, k_a, r_k, lnx_g, lnx_b)]
        lora_pad = lambda w: jnp.pad(w, ((0, LANE - w.shape[0]), (0, 0)))
        rw = _rwkv_call(ps, vecs, lora_pad(w_up[l]), lora_pad(a_up[l]), g_up[l], batch, seq)
        wq, wqs = _layout_wq(w_qb[l])
        wk, wv = _layout_wkv(w_kvb[l])
        q, k, v = _mlaprep_call(mla, pos_col, invf, row(q_norm[l]), row(kv_norm[l]), wq, wqs, wk, wv)
        o = _attn_call(q, k, v, batch, seq)

        shift2, scale2, gate2 = (mods[l, 1, :, j] for j in range(3))
        moe = l % 2 == 1
        router = (router_w[l // 2].T, router_b[l // 2].reshape(N_EXPERTS, 1)) if moe else None
        outs = _mixout_call(rw, o, gates, x2, gate, rwkv_out[l].astype(BF16), _layout_mla_out(mla_out[l]),
                            w_o[l].astype(BF16), row(norm_ffn[l]), shift2, scale2, seq, router=router)
        final = l == DEPTH - 1
        if moe:
            x2, h2, top_i, top_p = outs
            pos, ends, tile_e, n_used = _route(top_i, MOE_TM)
            xs = _dispatch_call(pos, ends, h2, 2 * t + N_EXPERTS * MOE_TM, MOE_TM)
            y = _ffn_routed_call(tile_e, n_used, xs, moe_w1[l // 2].astype(BF16),
                                 moe_w3[l // 2].astype(BF16), moe_w2[l // 2].astype(BF16))
            x2 = _combine_call(pos, x2, top_p.T, gate2, row(norm_final), y, seq, final)
        else:
            x2, h2 = outs
            x2 = _ffn_call(h2, x2, gate2, row(norm_final), ffn_w1[l // 2].astype(BF16),
                           ffn_w3[l // 2].astype(BF16), ffn_w2[l // 2].astype(BF16), seq, final)
    return x2.reshape(batch, seq, D_MODEL)
```

```python
import functools
import math

import jax
import jax.numpy as jnp
from jax import lax
from jax.experimental import pallas as pl
from jax.experimental.pallas import tpu as pltpu

F32 = jnp.float32
BF16 = jnp.bfloat16

D_MODEL = 1024
DEPTH = 2
CHUNK = 64
NORM_EPS = 1e-6
RWKV_HEADS = 8
RWKV_HEAD_DIM = 64
RW = RWKV_HEADS * RWKV_HEAD_DIM
DECAY_LORA = 64
ICLR_LORA = 64
GATE_LORA = 128
DECAY_SCALE = math.exp(-0.5)
LNX_EPS = 64e-5
MLA_HEADS = 8
Q_LORA = 384
KV_LORA = 256
QK_NOPE = 64
QK_ROPE = 32
V_HEAD = 64
ROPE_THETA = 10000.0
LOG2E = math.log2(math.e)
N_EXPERTS = 8

LANE = 128
HEAD_PAD = 128
PS_W = 3 * RW + 3 * LANE
MLA_W = Q_LORA + KV_LORA + 2 * LANE
GATE_W = 2 * D_MODEL
HALF = 256
VMEM_LIMIT = 56 * 1024 * 1024


def _sigmoid(x):
    return 1.0 / (1.0 + jnp.exp(-x))


def _dot(a, b):
    return jnp.dot(a, b, preferred_element_type=F32)


def _dot_nt(a, b):
    return lax.dot_general(a, b, (((1,), (1,)), ((), ())), preferred_element_type=F32)


def _split2(x):
    hi = x.astype(BF16)
    lo = (x - hi.astype(F32)).astype(BF16)
    return hi, lo


def _dot3(a, b):
    ah, al = _split2(a)
    bh, bl = _split2(b)
    return _dot(ah, bh) + _dot(al, bh) + _dot(ah, bl)


def _params(sem):
    return pltpu.CompilerParams(dimension_semantics=sem, vmem_limit_bytes=VMEM_LIMIT)


def _ada_kernel(c_ref, w_ref, b_ref, o_ref):
    c = c_ref[...]
    o_ref[...] = _dot((c * _sigmoid(c)).astype(BF16), w_ref[...].astype(BF16)) + b_ref[...]


def _ada_call(c, ada_w, ada_b):
    n = ada_w.shape[0]
    nb = c.shape[0]
    return pl.pallas_call(
        _ada_kernel,
        grid=(n, 3),
        in_specs=[
            pl.BlockSpec((nb, D_MODEL), lambda i, j: (0, 0)),
            pl.BlockSpec((None, D_MODEL, D_MODEL), lambda i, j: (i, 0, j)),
            pl.BlockSpec((None, 1, D_MODEL), lambda i, j: (i, 0, j)),
        ],
        out_specs=pl.BlockSpec((None, nb, D_MODEL), lambda i, j: (i, 0, j)),
        out_shape=jax.ShapeDtypeStruct((n, nb, 3 * D_MODEL), F32),
        compiler_params=_params(("arbitrary", "arbitrary")),
        name="ada",
    )(c, ada_w, ada_b)


def _norm_mod(x, g, scale, shift):
    var = jnp.mean(x * x, axis=-1, keepdims=True)
    return (x * lax.rsqrt(var + NORM_EPS) * g) * (1.0 + scale) + shift


def _inproj_kernel(x_ref, shift_ref, scale_ref, g_ref, w_ref, mu_ref,
                   ps_ref, mla_ref, gates_ref, carry_ref, *, tiles_per_seq, tm):
    i = pl.program_id(0)

    @pl.when(i % tiles_per_seq == 0)
    def _():
        carry_ref[...] = jnp.zeros_like(carry_ref)

    carry = carry_ref[0:1, :]
    row = lax.broadcasted_iota(jnp.int32, (INPROJ_SUB, PS_W), 0)
    for s in range(0, tm, INPROJ_SUB):
        r = slice(s, s + INPROJ_SUB)
        h = _norm_mod(x_ref[r, :], g_ref[...], scale_ref[...], shift_ref[...]).astype(BF16)
        p = _dot(h, w_ref[:, 0:PS_W])
        prev = jnp.where(row == 0, carry, pltpu.roll(p, 1, axis=0))
        carry = p[INPROJ_SUB - 1:INPROJ_SUB, :]
        ps_ref[r, :] = p + (prev - p) * mu_ref[...]
        mla_ref[r, :] = _dot(h, w_ref[:, PS_W:PS_W + MLA_W])
        gates_ref[r, :] = _sigmoid(_dot(h, w_ref[:, PS_W + MLA_W:])).astype(BF16)
    carry_ref[0:1, :] = carry


INPROJ_SUB = 256


def _inproj_call(x2, shift, scale, g, w_pad, mu_pad, seq, tm=512):
    t = x2.shape[0]
    tiles_per_seq = seq // tm
    bmap = lambda i: (i // tiles_per_seq, 0, 0)
    wtot = PS_W + MLA_W + GATE_W
    return pl.pallas_call(
        functools.partial(_inproj_kernel, tiles_per_seq=tiles_per_seq, tm=tm),
        grid=(t // tm,),
        in_specs=[
            pl.BlockSpec((tm, D_MODEL), lambda i: (i, 0)),
            pl.BlockSpec((None, 1, D_MODEL), bmap),
            pl.BlockSpec((None, 1, D_MODEL), bmap),
            pl.BlockSpec((1, D_MODEL), lambda i: (0, 0)),
            pl.BlockSpec((D_MODEL, wtot), lambda i: (0, 0)),
            pl.BlockSpec((1, PS_W), lambda i: (0, 0)),
        ],
        out_specs=[
            pl.BlockSpec((tm, PS_W), lambda i: (i, 0)),
            pl.BlockSpec((tm, MLA_W), lambda i: (i, 0)),
            pl.BlockSpec((tm, GATE_W), lambda i: (i, 0)),
        ],
        out_shape=[
            jax.ShapeDtypeStruct((t, PS_W), F32),
            jax.ShapeDtypeStruct((t, MLA_W), F32),
            jax.ShapeDtypeStruct((t, GATE_W), BF16),
        ],
        scratch_shapes=[pltpu.VMEM((8, PS_W), F32)],
        compiler_params=_params(("arbitrary",)),
        name="inproj",
    )(x2, shift, scale, g, w_pad, mu_pad)


def _blockdiag(xb, bdmask):
    return jnp.tile(xb, (HALF // CHUNK, 1)) * bdmask


def _bd_mm(dot, a, b, bdmask, passes):
    if passes == 1:
        return dot(a.astype(BF16), _blockdiag(b.astype(BF16), bdmask))
    ah, al = _split2(a)
    bh, bl = _split2(b)
    r = dot(jnp.concatenate([ah, al], axis=0), _blockdiag(bh, bdmask))
    n = a.shape[0]
    return r[:n] + r[n:] + dot(ah, _blockdiag(bl, bdmask))


def _bd_nn(a, b, bdmask, passes):
    return _bd_mm(_dot, a, b, bdmask, passes)


def _bd_nt(a, b, bdmask, passes):
    return _bd_mm(_dot_nt, a, b, bdmask, passes)


def _bd_tn(a, b, bdmask_f, passes):
    full = _dot3(a.T, b) if passes == 3 else _dot(a.T.astype(BF16), b.astype(BF16))
    return jnp.sum((full * bdmask_f).reshape(HALF // CHUNK, CHUNK, HALF), axis=0)


P_SCORE = 1
P_TINV = 1
P_APPLY = 1
P_GH = 1
P_STATE = 1


def _halves(fn, *xs):
    outs = [fn(*[x[:, h * HALF:(h + 1) * HALF] for x in xs]) for h in range(RW // HALF)]
    return jnp.concatenate(outs, axis=1)


def _rwkv_kernel(ps_ref, w0_ref, a0_ref, kk_ref, ka_ref, rk_ref, lg_ref, lb_ref,
                 wup_ref, aup_ref, gup_ref, out_ref, s_ref, *, nc):
    @pl.when(pl.program_id(1) == 0)
    def _():
        s_ref[...] = jnp.zeros_like(s_ref)

    C = CHUNK
    r_i = lax.broadcasted_iota(jnp.int32, (HALF, HALF), 0)
    c_i = lax.broadcasted_iota(jnp.int32, (HALF, HALF), 1)
    bdsame = (r_i // C) == (c_i // C)
    bdmask = bdsame.astype(BF16)
    bdmask_f = bdsame.astype(F32)
    t_i = lax.broadcasted_iota(jnp.int32, (C, HALF), 0)
    i_i = lax.broadcasted_iota(jnp.int32, (C, HALF), 1) % C
    strict = i_i < t_i
    incl = i_i <= t_i
    eye_l = (i_i == t_i).astype(F32)
    n = nc * C
    tr = lax.broadcasted_iota(jnp.int32, (n, n), 0)
    tc = lax.broadcasted_iota(jnp.int32, (n, n), 1)
    tril = ((tc <= tr) & (tc // C == tr // C)).astype(BF16)

    def headsum(x, split=False):
        def one(xh):
            if not split:
                return _dot(xh.astype(BF16), bdmask)
            hi, lo = _split2(xh)
            return _dot(hi, bdmask) + _dot(lo, bdmask)
        return _halves(one, x)

    def cumsum_rows(x):
        hi = x.astype(BF16)
        r1 = x - hi.astype(F32)
        mid = r1.astype(BF16)
        lo = (r1 - mid.astype(F32)).astype(BF16)
        return _dot(tril, hi) + _dot(tril, mid) + _dot(tril, lo)

    p_r = ps_ref[:, 0:RW]
    p_k = ps_ref[:, RW:2 * RW]
    p_v = ps_ref[:, 2 * RW:3 * RW]
    p_w = ps_ref[:, 3 * RW:3 * RW + LANE]
    p_a = ps_ref[:, 3 * RW + LANE:3 * RW + 2 * LANE]
    p_g = ps_ref[:, 3 * RW + 2 * LANE:3 * RW + 3 * LANE]
    logw = -DECAY_SCALE * _sigmoid(w0_ref[...] + _dot3(jnp.tanh(p_w), wup_ref[...]))
    iclr = _sigmoid(a0_ref[...] + _dot(p_a.astype(BF16), aup_ref[...].astype(BF16)))
    gate = _dot(_sigmoid(p_g).astype(BF16), gup_ref[...].astype(BF16))
    kk = p_k * kk_ref[...]
    k2 = p_k * (1.0 + (iclr - 1.0) * ka_ref[...])
    sums = headsum(jnp.concatenate([kk * kk, p_r * k2 * rk_ref[...]], axis=0))
    kk = kk * lax.rsqrt(sums[:n] + 1e-12)
    bonus = sums[n:] * p_v
    b_v = kk * iclr
    cum = cumsum_rows(logw)
    cum_c = jnp.concatenate(
        [jnp.broadcast_to(cum[(c + 1) * C - 1:(c + 1) * C, :], (C, RW)) for c in range(nc)], axis=0)
    e_neg = jnp.exp(-cum)
    e_rem = jnp.exp(cum_c - cum)
    a_all = -kk * jnp.exp(cum - logw)
    b_all = b_v * e_neg
    k_all = k2 * e_neg
    r_all = p_r * jnp.exp(cum)
    bg_all = b_v * e_rem
    kg_all = k2 * e_rem
    gam_all = jnp.exp(cum_c)

    nh = RW // HALF
    units = [(c, h) for c in range(nc) for h in range(nh)]

    def cut(x, u):
        c, h = u
        return x[c * C:(c + 1) * C, h * HALF:(h + 1) * HALF]

    each = lambda fn, *lists: [fn(*xs) for xs in zip(*lists)]
    stack = lambda x, y: jnp.concatenate([x, y], axis=0)
    top = lambda x: x[:C]
    bot = lambda x: x[C:]
    nn = lambda p: (lambda a, b: _bd_nn(a, b, bdmask, p))

    a_t, b_t, k_t, r_t, b_g, k_g, v_u = ([cut(x, u) for u in units]
                                         for x in (a_all, b_all, k_all, r_all, bg_all, kg_all, p_v))
    gam = [gam_all[c * C:c * C + 1, h * HALF:(h + 1) * HALF] for c, h in units]
    ar = each(stack, a_t, r_t)
    sb = each(lambda x, y: _bd_nt(x, y, bdmask, P_SCORE), ar, b_t)
    sk = each(lambda x, y: _bd_nt(x, y, bdmask, P_SCORE), ar, k_t)
    l_ab = [jnp.where(strict, top(x), 0.0) for x in sb]
    q_b = [jnp.where(incl, bot(x), 0.0) for x in sb]
    m_ak = [jnp.where(strict, top(x), 0.0) for x in sk]
    q_k = [jnp.where(incl, bot(x), 0.0) for x in sk]
    t_m = [eye_l + x for x in l_ab]
    pw = each(nn(P_TINV), l_ab, l_ab)
    for _ in range(4):
        tp = each(nn(P_TINV), each(stack, t_m, pw), pw)
        t_m = each(lambda t, x: t + top(x), t_m, tp)
        pw = [bot(x) for x in tp]
    t_m = each(lambda t, x: t + x, t_m, each(nn(P_TINV), t_m, pw))
    mq = each(nn(P_APPLY), each(stack, m_ak, q_k), v_u)
    a_h = each(nn(P_APPLY), t_m, a_t)
    u0 = each(nn(P_APPLY), t_m, [top(x) for x in mq])
    r_h = each(lambda r, x: r + x, r_t, each(nn(P_APPLY), q_b, a_h))
    y0 = each(lambda m, x: bot(m) + x, mq, each(nn(P_APPLY), q_b, u0))
    g_l = each(lambda a, b, g: _bd_tn(a, b, bdmask_f, P_GH) + eye_l * g, a_h, b_g, gam)
    h_l = each(lambda u, v, b, k: _bd_tn(stack(u, v), stack(b, k), bdmask_f, P_GH), u0, v_u, b_g, k_g)

    s_all = s_ref[...]
    s_cur = [s_all[:, h * HALF:(h + 1) * HALF] for h in range(nh)]
    ys = []
    for c in range(nc):
        idx = [c * nh + h for h in range(nh)]
        yc = [_bd_nt(r_h[i], s, bdmask, P_STATE) + y0[i] for i, s in zip(idx, s_cur)]
        s_cur = [_bd_nn(s, g_l[i], bdmask, P_STATE) + h_l[i] for i, s in zip(idx, s_cur)]
        ys.append(jnp.concatenate(yc, axis=1))
    s_ref[...] = jnp.concatenate(s_cur, axis=1)
    y = jnp.concatenate(ys, axis=0)
    inv_n = 1.0 / RWKV_HEAD_DIM
    mu = headsum(y, split=True) * inv_n
    d = y - mu
    var = headsum(d * d) * inv_n
    yn = d * lax.rsqrt(var + LNX_EPS) * lg_ref[...] + lb_ref[...]
    out_ref[...] = ((yn + bonus) * gate).astype(BF16)


def _rwkv_call(ps, vecs, wup, aup, gup, batch, seq, nc=4):
    t = ps.shape[0]
    rows = nc * CHUNK
    steps = seq // rows
    vec_spec = pl.BlockSpec((1, RW), lambda b, j: (0, 0))
    lora_spec = pl.BlockSpec((LANE, RW), lambda b, j: (0, 0))
    return pl.pallas_call(
        functools.partial(_rwkv_kernel, nc=nc),
        grid=(batch, steps),
        in_specs=[pl.BlockSpec((rows, PS_W), lambda b, j: (b * steps + j, 0))]
                 + [vec_spec] * 7 + [lora_spec] * 3,
        out_specs=pl.BlockSpec((rows, RW), lambda b, j: (b * steps + j, 0)),
        out_shape=jax.ShapeDtypeStruct((t, RW), BF16),
        scratch_shapes=[pltpu.VMEM((CHUNK, RW), F32)],
        compiler_params=_params(("arbitrary", "arbitrary")),
        name="rwkv",
    )(ps, *vecs, wup, aup, gup)


def _mlaprep_kernel(mla_ref, pos_ref, invf_ref, qn_ref, kvn_ref, wq_ref, wqs_ref, wk_ref, wv_ref,
                    q_ref, k_ref, v_ref):
    def rms(x, g):
        return (x * lax.rsqrt(jnp.mean(x * x, axis=-1, keepdims=True) + NORM_EPS) * g).astype(BF16)

    ang = pos_ref[...].astype(F32) * invf_ref[...]
    cos = jnp.cos(ang)
    sin = jnp.sin(ang)
    lane = lax.broadcasted_iota(jnp.int32, cos.shape, 1)
    cos_q = jnp.tile(jnp.where(lane < QK_NOPE, 1.0, cos), (1, MLA_HEADS))
    sin_q = jnp.tile(sin, (1, MLA_HEADS))
    qn = rms(mla_ref[:, 0:Q_LORA], qn_ref[...])
    q = _dot(qn, wq_ref[...]) * cos_q + _dot(qn, wqs_ref[...]) * sin_q
    q_ref[...] = (q * ((QK_NOPE + QK_ROPE) ** -0.5 * LOG2E)).astype(BF16)
    kvn = rms(mla_ref[:, Q_LORA:Q_LORA + KV_LORA], kvn_ref[...])
    o = Q_LORA + KV_LORA
    k_pe = mla_ref[:, o:o + LANE] * cos + mla_ref[:, o + LANE:o + 2 * LANE] * sin
    k_ref[...] = (_dot(kvn, wk_ref[...]) + jnp.tile(k_pe, (1, MLA_HEADS))).astype(BF16)
    v_ref[...] = _dot(kvn, wv_ref[...]).astype(BF16)


def _mlaprep_call(mla, pos, invf, qn, kvn, wq, wqs, wk, wv, tm=512):
    t = mla.shape[0]
    hw = MLA_HEADS * HEAD_PAD
    full = lambda shape: pl.BlockSpec(shape, lambda i: (0, 0))
    row = lambda w: pl.BlockSpec((tm, w), lambda i: (i, 0))
    return pl.pallas_call(
        _mlaprep_kernel,
        grid=(t // tm,),
        in_specs=[row(MLA_W), row(1), full((1, LANE)), full((1, Q_LORA)), full((1, KV_LORA)),
                  full((Q_LORA, hw)), full((Q_LORA, hw)), full((KV_LORA, hw)), full((KV_LORA, hw))],
        out_specs=[row(hw)] * 3,
        out_shape=[jax.ShapeDtypeStruct((t, hw), BF16)] * 3,
        compiler_params=_params(("arbitrary",)),
        name="mlaprep",
    )(mla, pos, invf, qn, kvn, wq, wqs, wk, wv)


def _attn_kernel(q_ref, k_ref, v_ref, o_ref, *, seq, tq):
    r_i = lax.broadcasted_iota(jnp.int32, (tq, tq), 0) // CHUNK
    c_i = lax.broadcasted_iota(jnp.int32, (tq, tq), 1) // CHUNK
    mask = c_i <= r_i
    def scores(qi):
        lo = qi * tq
        q = q_ref[lo:lo + tq, :]
        sd = jnp.where(mask, _dot_nt(q, k_ref[lo:lo + tq, :]), -1e30)
        so = _dot_nt(q, k_ref[0:lo, :]) if qi > 0 else None
        return sd, so

    n_q = seq // tq
    nxt = scores(0)
    for qi in range(n_q):
        lo = qi * tq
        sd, so = nxt
        if qi + 1 < n_q:
            nxt = scores(qi + 1)
        m = jnp.max(sd, axis=-1, keepdims=True)
        if qi > 0:
            m = jnp.maximum(m, jnp.max(so, axis=-1, keepdims=True))
        pd = jnp.exp2(sd - m)
        l = jnp.sum(pd, axis=-1, keepdims=True)
        acc = _dot(pd.astype(BF16), v_ref[lo:lo + tq, :])
        if qi > 0:
            po = jnp.exp2(so - m)
            l = l + jnp.sum(po, axis=-1, keepdims=True)
            acc = acc + _dot(po.astype(BF16), v_ref[0:lo, :])
        o_ref[lo:lo + tq, :] = (acc / l).astype(BF16)


def _attn_call(q, k, v, batch, seq, tq=256):
    spec = pl.BlockSpec((seq, HEAD_PAD), lambda b, h: (b, h))
    return pl.pallas_call(
        functools.partial(_attn_kernel, seq=seq, tq=tq),
        grid=(batch, MLA_HEADS),
        in_specs=[spec] * 3,
        out_specs=spec,
        out_shape=jax.ShapeDtypeStruct(q.shape, BF16),
        compiler_params=_params(("arbitrary", "arbitrary")),
        name="attn",
    )(q, k, v)


MIX_SUB = 256


def _mixout_kernel(*refs, router):
    if router:
        (rw_ref, o_ref, gates_ref, x_ref, gate_ref, wr_ref, wm_ref, wo_ref,
         g2_ref, shift2_ref, scale2_ref, rwt_ref, rb_ref,
         xo_ref, h2_ref, ti_ref, tp_ref) = refs
    else:
        (rw_ref, o_ref, gates_ref, x_ref, gate_ref, wr_ref, wm_ref, wo_ref,
         g2_ref, shift2_ref, scale2_ref, xo_ref, h2_ref) = refs
    n = x_ref.shape[0]
    subs = [slice(s, s + MIX_SUB) for s in range(0, n, MIX_SUB)]
    yab = [(_dot(rw_ref[r, :], wr_ref[...]), _dot(o_ref[r, :], wm_ref[...])) for r in subs]
    merged = [gates_ref[r, 0:D_MODEL].astype(F32) * ya + gates_ref[r, D_MODEL:].astype(F32) * yb
              for r, (ya, yb) in zip(subs, yab)]
    mixed = [_dot(m.astype(BF16), wo_ref[...]) for m in merged]
    h2s = []
    for r, mx in zip(subs, mixed):
        xn = x_ref[r, :] + gate_ref[...] * mx
        xo_ref[r, :] = xn
        h2s.append(_norm_mod(xn, g2_ref[...], scale2_ref[...], shift2_ref[...]))
        h2_ref[r, :] = h2s[-1].astype(h2_ref.dtype)
    if router:
        h2 = jnp.concatenate(h2s, axis=0)
        hh, hl = _split2(h2)
        wh, wl = _split2(rwt_ref[...])
        logits = _dot_nt(wh, hh) + _dot_nt(wl, hh) + _dot_nt(wh, hl) + rb_ref[...]
        e_i = lax.broadcasted_iota(jnp.int32, logits.shape, 0)
        m1 = jnp.max(logits, axis=0, keepdims=True)
        i1 = jnp.min(jnp.where(logits == m1, e_i, N_EXPERTS), axis=0, keepdims=True)
        rest = jnp.where(e_i == i1, -jnp.inf, logits)
        m2 = jnp.max(rest, axis=0, keepdims=True)
        i2 = jnp.min(jnp.where(rest == m2, e_i, N_EXPERTS), axis=0, keepdims=True)
        e2 = jnp.exp(m2 - m1)
        den = 1.0 + e2
        ti_ref[0:1, :] = i1
        ti_ref[1:2, :] = i2
        tp_ref[0:1, :] = 1.0 / den
        tp_ref[1:2, :] = e2 / den


def _mixout_call(rw, o, gates, x2, gate, wr, wm, wo, g2, shift2, scale2, seq, router=None, tm=512):
    t = x2.shape[0]
    tiles_per_seq = seq // tm
    bmap = lambda i: (i // tiles_per_seq, 0, 0)
    full = lambda shape: pl.BlockSpec(shape, lambda i: (0, 0))
    row = lambda w: pl.BlockSpec((tm, w), lambda i: (i, 0))
    mod = pl.BlockSpec((None, 1, D_MODEL), bmap)
    in_specs = [row(RW), row(MLA_HEADS * HEAD_PAD), row(GATE_W), row(D_MODEL), mod,
                full(wr.shape), full(wm.shape), full(wo.shape), full((1, D_MODEL)), mod, mod]
    args = [rw, o, gates, x2, gate, wr, wm, wo, g2, shift2, scale2]
    out_specs = [row(D_MODEL), row(D_MODEL)]
    out_shape = [jax.ShapeDtypeStruct((t, D_MODEL), F32),
                 jax.ShapeDtypeStruct((t, D_MODEL), BF16 if router is None else F32)]
    if router is not None:
        in_specs += [full((N_EXPERTS, D_MODEL)), full((N_EXPERTS, 1))]
        args += list(router)
        out_specs += [pl.BlockSpec((2, tm), lambda i: (0, i))] * 2
        out_shape += [jax.ShapeDtypeStruct((2, t), jnp.int32), jax.ShapeDtypeStruct((2, t), F32)]
    return pl.pallas_call(
        functools.partial(_mixout_kernel, router=router is not None),
        grid=(t // tm,),
        in_specs=in_specs,
        out_specs=out_specs,
        out_shape=out_shape,
        compiler_params=_params(("arbitrary",)),
        name="mixout",
    )(*args)


def _swiglu_step(h, w1_ref, w3_ref, w2_ref, acc_ref, finish):
    f = pl.program_id(1)

    @pl.when(f == 0)
    def _():
        acc_ref[...] = jnp.zeros_like(acc_ref)

    a1 = _dot(h, w1_ref[...])
    a3 = _dot(h, w3_ref[...])
    act = (a1 * _sigmoid(a1)) * a3
    acc_ref[...] += _dot(act.astype(BF16), w2_ref[...])

    @pl.when(f == pl.num_programs(1) - 1)
    def _():
        finish(acc_ref[...])


def _rms(x, g):
    return x * lax.rsqrt(jnp.mean(x * x, axis=-1, keepdims=True) + NORM_EPS) * g


def _ffn_kernel(h_ref, x_ref, gate_ref, gf_ref, w1_ref, w3_ref, w2_ref, o_ref, *, final, tf):
    h = h_ref[...]
    ff = w1_ref.shape[1]
    ups = [(_dot(h, w1_ref[:, s:s + tf]), _dot(h, w3_ref[:, s:s + tf])) for s in range(0, ff, tf)]
    acc = None
    for s, (a1, a3) in zip(range(0, ff, tf), ups):
        act = ((a1 * _sigmoid(a1)) * a3).astype(BF16)
        part = _dot(act, w2_ref[s:s + tf, :])
        acc = part if acc is None else acc + part
    xn = x_ref[...] + gate_ref[...] * acc
    o_ref[...] = _rms(xn, gf_ref[...]) if final else xn


def _ffn_call(h, x2, gate, gf, w1, w3, w2, seq, final, tm=512, tf=1408):
    r = h.shape[0]
    tiles_per_seq = seq // tm
    row = pl.BlockSpec((tm, D_MODEL), lambda i: (i, 0))
    resident = lambda a: pl.BlockSpec(a.shape, lambda i: (0, 0), pipeline_mode=pl.Buffered(1))
    return pl.pallas_call(
        functools.partial(_ffn_kernel, final=final, tf=tf),
        grid=(r // tm,),
        in_specs=[
            row, row,
            pl.BlockSpec((None, 1, D_MODEL), lambda i: (i // tiles_per_seq, 0, 0)),
            pl.BlockSpec((1, D_MODEL), lambda i: (0, 0)),
            resident(w1), resident(w3), resident(w2),
        ],
        out_specs=row,
        out_shape=jax.ShapeDtypeStruct((r, D_MODEL), F32),
        compiler_params=_params(("arbitrary",)),
        name="ffn",
    )(h, x2, gate, gf, w1, w3, w2)


MOE_TM = 512


def _route(top_i, tm):
    t = top_i.shape[1]
    e_flat = top_i.reshape(-1)
    oh = (e_flat[:, None] == jnp.arange(N_EXPERTS, dtype=jnp.int32)[None, :]).astype(jnp.int32)
    csum = jnp.cumsum(oh, axis=0)
    rank = jnp.sum(csum * oh, axis=1) - 1
    counts = csum[-1]
    padded = ((counts + tm - 1) // tm) * tm
    ends = jnp.cumsum(padded)
    pos = jnp.sum(oh * (ends - padded)[None, :], axis=1) + rank
    n_tiles = (2 * t) // tm + N_EXPERTS
    starts = jnp.arange(n_tiles, dtype=jnp.int32) * tm
    tile_e = jnp.sum((starts[:, None] >= ends[None, :]).astype(jnp.int32), axis=1)
    return (pos.astype(jnp.int32), ends.astype(jnp.int32), jnp.minimum(tile_e, N_EXPERTS - 1),
            (ends[-1:] // tm).astype(jnp.int32))


def _dispatch_kernel(pos_ref, ends_ref, h_ref, xs_ref, zero_ref, sem, *, t, td, tm):
    i = pl.program_id(0)
    base = i * td

    @pl.when(i == 0)
    def _():
        zero_ref[...] = jnp.zeros_like(zero_ref)
        n_tiles = xs_ref.shape[0] // tm
        fills = []
        for e in range(N_EXPERTS):
            start = ends_ref[e - 1] if e else 0
            fills.append((ends_ref[e] > start, pl.multiple_of(ends_ref[e] - tm, tm)))
            fills.append((ends_ref[N_EXPERTS - 1] // tm + e < n_tiles,
                          pl.multiple_of(ends_ref[N_EXPERTS - 1] + e * tm, tm)))
        for go, row0 in fills:
            @pl.when(go)
            def _():
                pltpu.make_async_copy(zero_ref, xs_ref.at[pl.ds(row0, tm)], sem).start()
        for go, row0 in fills:
            @pl.when(go)
            def _():
                pltpu.make_async_copy(zero_ref, xs_ref.at[pl.ds(row0, tm)], sem).wait()

    for r in range(td):
        for k in range(2):
            dst = pos_ref[k * t + base + r]
            pltpu.make_async_copy(h_ref.at[pl.ds(r, 1)], xs_ref.at[pl.ds(dst, 1)], sem).start()
    for k in range(2):
        pltpu.make_async_copy(h_ref, xs_ref.at[pl.ds(0, td)], sem).wait()


def _dispatch_call(pos, ends, h, n_rows, tm, td=512):
    t = h.shape[0]
    return pl.pallas_call(
        functools.partial(_dispatch_kernel, t=t, td=td, tm=tm),
        grid_spec=pltpu.PrefetchScalarGridSpec(
            num_scalar_prefetch=2, grid=(t // td,),
            in_specs=[pl.BlockSpec((td, D_MODEL), lambda i, pos, ends: (i, 0))],
            out_specs=pl.BlockSpec(memory_space=pl.ANY),
            scratch_shapes=[pltpu.VMEM((tm, D_MODEL), h.dtype), pltpu.SemaphoreType.DMA(())]),
        out_shape=jax.ShapeDtypeStruct((n_rows, D_MODEL), h.dtype),
        compiler_params=_params(("arbitrary",)),
        name="dispatch",
    )(pos, ends, h)


def _ffn_routed_kernel(te_ref, nu_ref, x_ref, w1_ref, w3_ref, w2_ref, y_ref, acc_ref):
    del te_ref
    used = pl.program_id(0) < nu_ref[0]

    def finish(acc):
        y_ref[...] = acc

    @pl.when(used)
    def _():
        _swiglu_step(x_ref[...].astype(BF16), w1_ref, w3_ref, w2_ref, acc_ref, finish)

    @pl.when(jnp.logical_not(used))
    def _():
        y_ref[...] = jnp.zeros_like(y_ref)


def _ffn_routed_call(tile_e, n_used, xs, w1, w3, w2, tm=MOE_TM, tf=1792):
    r = xs.shape[0]
    ff = w1.shape[2]
    return pl.pallas_call(
        _ffn_routed_kernel,
        grid_spec=pltpu.PrefetchScalarGridSpec(
            num_scalar_prefetch=2, grid=(r // tm, ff // tf),
            in_specs=[
                pl.BlockSpec((tm, D_MODEL), lambda i, f, te, nu: (i, 0)),
                pl.BlockSpec((None, D_MODEL, tf), lambda i, f, te, nu: (te[i], 0, f)),
                pl.BlockSpec((None, D_MODEL, tf), lambda i, f, te, nu: (te[i], 0, f)),
                pl.BlockSpec((None, tf, D_MODEL), lambda i, f, te, nu: (te[i], f, 0)),
            ],
            out_specs=pl.BlockSpec((tm, D_MODEL), lambda i, f, te, nu: (i, 0)),
            scratch_shapes=[pltpu.VMEM((tm, D_MODEL), F32)]),
        out_shape=jax.ShapeDtypeStruct((r, D_MODEL), F32),
        compiler_params=_params(("arbitrary", "arbitrary")),
        name="ffn_routed",
    )(tile_e, n_used, xs, w1, w3, w2)


def _combine_kernel(pos_ref, x_ref, p_ref, gate_ref, gf_ref, y_ref, o_ref, buf, sem, *, t, tc, final):
    i = pl.program_id(0)

    def row_copy(step, slot, r, k):
        src = pos_ref[k * t + step * tc + r]
        return pltpu.make_async_copy(y_ref.at[pl.ds(src, 1)], buf.at[slot, k, pl.ds(r, 1)], sem.at[slot])

    def issue(step, slot):
        for r in range(tc):
            for k in range(2):
                row_copy(step, slot, r, k).start()

    @pl.when(i == 0)
    def _():
        issue(0, 0)

    @pl.when(i + 1 < pl.num_programs(0))
    def _():
        issue(i + 1, (i + 1) % 2)

    slot = i % 2
    for k in range(2):
        pltpu.make_async_copy(y_ref.at[pl.ds(0, tc)], buf.at[slot, k], sem.at[slot]).wait()
    p = p_ref[...]
    xn = x_ref[...] + gate_ref[...] * (p[:, 0:1] * buf[slot, 0] + p[:, 1:2] * buf[slot, 1])
    o_ref[...] = _rms(xn, gf_ref[...]) if final else xn


def _combine_call(pos, x2, probs, gate, gf, y, seq, final, tc=256):
    t = x2.shape[0]
    tiles_per_seq = seq // tc
    row = pl.BlockSpec((tc, D_MODEL), lambda i, pos: (i, 0))
    return pl.pallas_call(
        functools.partial(_combine_kernel, t=t, tc=tc, final=final),
        grid_spec=pltpu.PrefetchScalarGridSpec(
            num_scalar_prefetch=1, grid=(t // tc,),
            in_specs=[row, pl.BlockSpec((tc, 2), lambda i, pos: (i, 0)),
                      pl.BlockSpec((None, 1, D_MODEL), lambda i, pos: (i // tiles_per_seq, 0, 0)),
                      pl.BlockSpec((1, D_MODEL), lambda i, pos: (0, 0)),
                      pl.BlockSpec(memory_space=pl.ANY)],
            out_specs=row,
            scratch_shapes=[pltpu.VMEM((2, 2, tc, D_MODEL), F32), pltpu.SemaphoreType.DMA((2,))]),
        out_shape=jax.ShapeDtypeStruct((t, D_MODEL), F32),
        compiler_params=_params(("arbitrary",)),
        name="combine",
    )(pos, x2, probs, gate, gf, y)


def _layout_w_in_kernel(w_ref, o_ref):
    half = QK_ROPE // 2
    o_ref[...] = jnp.zeros_like(o_ref)

    def put(dst, src, width, sign=1.0):
        o_ref[:, dst:dst + width] = (sign * w_ref[:, src:src + width]).astype(BF16)

    o = 3 * RW
    put(0, 0, o + DECAY_LORA)
    put(o + LANE, o + DECAY_LORA, ICLR_LORA)
    put(o + 2 * LANE, o + DECAY_LORA + ICLR_LORA, GATE_LORA)
    src = o + DECAY_LORA + ICLR_LORA + GATE_LORA
    put(PS_W, src, Q_LORA + KV_LORA)
    kr = src + Q_LORA + KV_LORA
    dst = PS_W + Q_LORA + KV_LORA
    put(dst + QK_NOPE, kr, QK_ROPE)
    put(dst + LANE + QK_NOPE, kr + half, half, -1.0)
    put(dst + LANE + QK_NOPE + half, kr, half)
    put(PS_W + MLA_W, kr + QK_ROPE, GATE_W)


def _layout_w_in(w):
    k, n = w.shape
    return pl.pallas_call(
        _layout_w_in_kernel,
        grid=(k // 256,),
        in_specs=[pl.BlockSpec((256, n), lambda i: (i, 0))],
        out_specs=pl.BlockSpec((256, PS_W + MLA_W + GATE_W), lambda i: (i, 0)),
        out_shape=jax.ShapeDtypeStruct((k, PS_W + MLA_W + GATE_W), BF16),
        compiler_params=_params(("arbitrary",)),
        name="layout_w_in",
    )(w)


def _layout_mu(mu):
    z = jnp.zeros((64,), mu.dtype)
    o = 3 * RW
    return jnp.concatenate([mu[:o + 64], z, mu[o + 64:o + 128], z, mu[o + 128:]])[None, :]


def _layout_wq(w_qb):
    hd = QK_NOPE + QK_ROPE
    w = w_qb.reshape(Q_LORA, MLA_HEADS, hd)
    half = QK_ROPE // 2
    x1, x2 = w[:, :, QK_NOPE:QK_NOPE + half], w[:, :, QK_NOPE + half:]
    zn = jnp.zeros((Q_LORA, MLA_HEADS, QK_NOPE), w.dtype)
    zp = jnp.zeros((Q_LORA, MLA_HEADS, HEAD_PAD - hd), w.dtype)
    wq = jnp.concatenate([w, zp], axis=2).reshape(Q_LORA, -1)
    wqs = jnp.concatenate([zn, -x2, x1, zp], axis=2).reshape(Q_LORA, -1)
    return wq.astype(BF16), wqs.astype(BF16)


def _layout_wkv(w_kvb):
    w = w_kvb.reshape(KV_LORA, MLA_HEADS, QK_NOPE + V_HEAD)
    z = jnp.zeros((KV_LORA, MLA_HEADS, HEAD_PAD - QK_NOPE), w.dtype)
    wk = jnp.concatenate([w[:, :, :QK_NOPE], z], axis=2).reshape(KV_LORA, -1)
    wv = jnp.concatenate([w[:, :, QK_NOPE:], z], axis=2).reshape(KV_LORA, -1)
    return wk.astype(BF16), wv.astype(BF16)


def _layout_mla_out(w):
    w = w.reshape(MLA_HEADS, V_HEAD, D_MODEL)
    z = jnp.zeros((MLA_HEADS, HEAD_PAD - V_HEAD, D_MODEL), w.dtype)
    return jnp.concatenate([w, z], axis=1).reshape(MLA_HEADS * HEAD_PAD, D_MODEL).astype(BF16)


def _rope_freq_row():
    inv_freq = ROPE_THETA ** (-jnp.arange(0, QK_ROPE, 2, dtype=F32) / QK_ROPE)
    z = jnp.zeros((QK_NOPE,), F32)
    zp = jnp.zeros((LANE - QK_NOPE - QK_ROPE,), F32)
    return jnp.concatenate([z, inv_freq, inv_freq, zp])[None, :]


def kernel(x, c, positions, ada_w, ada_b, norm_mix, norm_ffn, norm_final, w_in, tshift_mu, w0, w_up, a0, a_up, g_up, k_k, k_a, r_k, lnx_g, lnx_b, rwkv_out, q_norm, w_qb, kv_norm, w_kvb, mla_out, w_o, ffn_w1, ffn_w3, ffn_w2, router_w, router_b, moe_w1, moe_w3, moe_w2):
    batch, seq, _ = x.shape
    t = batch * seq
    mods = _ada_call(c, ada_w.reshape(2 * DEPTH, D_MODEL, 3 * D_MODEL),
                     ada_b.reshape(2 * DEPTH, 1, 3 * D_MODEL))
    mods = mods.reshape(DEPTH, 2, batch, 3, 1, D_MODEL)
    pos_col = positions.reshape(t, 1)
    invf = _rope_freq_row()
    x2 = x.reshape(t, D_MODEL)
    row = lambda v: v.reshape(1, -1)

    for l in range(DEPTH):
        shift, scale, gate = (mods[l, 0, :, j] for j in range(3))
        ps, mla, gates = _inproj_call(x2, shift, scale, row(norm_mix[l]), _layout_w_in(w_in[l]),
                                      _layout_mu(tshift_mu[l]), seq)
        vecs = [row(v[l]) for v in (w0, a0, k_k, k_a, r_k, lnx_g, lnx_b)]
        lora_pad = lambda w: jnp.pad(w, ((0, LANE - w.shape[0]), (0, 0)))
        rw = _rwkv_call(ps, vecs, lora_pad(w_up[l]), lora_pad(a_up[l]), g_up[l], batch, seq)
        wq, wqs = _layout_wq(w_qb[l])
        wk, wv = _layout_wkv(w_kvb[l])
        q, k, v = _mlaprep_call(mla, pos_col, invf, row(q_norm[l]), row(kv_norm[l]), wq, wqs, wk, wv)
        o = _attn_call(q, k, v, batch, seq)

        shift2, scale2, gate2 = (mods[l, 1, :, j] for j in range(3))
        moe = l % 2 == 1
        router = (router_w[l // 2].T, router_b[l // 2].reshape(N_EXPERTS, 1)) if moe else None
        outs = _mixout_call(rw, o, gates, x2, gate, rwkv_out[l].astype(BF16), _layout_mla_out(mla_out[l]),
                            w_o[l].astype(BF16), row(norm_ffn[l]), shift2, scale2, seq, router=router)
        final = l == DEPTH - 1
        if moe:
            x2, h2, top_i, top_p = outs
            pos, ends, tile_e, n_used = _route(top_i, MOE_TM)
            xs = _dispatch_call(pos, ends, h2, 2 * t + N_EXPERTS * MOE_TM, MOE_TM)
            y = _ffn_routed_call(tile_e, n_used, xs, moe_w1[l // 2].astype(BF16),
                                 moe_w3[l // 2].astype(BF16), moe_w2[l // 2].astype(BF16))
            x2 = _combine_call(pos, x2, top_p.T, gate2, row(norm_final), y, seq, final)
        else:
            x2, h2 = outs
            x2 = _ffn_call(h2, x2, gate2, row(norm_final), ffn_w1[l // 2].astype(BF16),
                           ffn_w3[l // 2].astype(BF16), ffn_w2[l // 2].astype(BF16), seq, final)
    return x2.reshape(batch, seq, D_MODEL)
```

```python
import functools
import math

import jax
import jax.numpy as jnp
from jax import lax
from jax.experimental import pallas as pl
from jax.experimental.pallas import tpu as pltpu

F32 = jnp.float32
BF16 = jnp.bfloat16

D_MODEL = 1024
DEPTH = 2
CHUNK = 64
NORM_EPS = 1e-6
RWKV_HEADS = 8
RWKV_HEAD_DIM = 64
RW = RWKV_HEADS * RWKV_HEAD_DIM
DECAY_LORA = 64
ICLR_LORA = 64
GATE_LORA = 128
DECAY_SCALE = math.exp(-0.5)
LNX_EPS = 64e-5
MLA_HEADS = 8
Q_LORA = 384
KV_LORA = 256
QK_NOPE = 64
QK_ROPE = 32
V_HEAD = 64
ROPE_THETA = 10000.0
LOG2E = math.log2(math.e)
N_EXPERTS = 8

LANE = 128
HEAD_PAD = 128
PS_W = 3 * RW + 3 * LANE
MLA_W = Q_LORA + KV_LORA + 2 * LANE
GATE_W = 2 * D_MODEL
HALF = 256
VMEM_LIMIT = 56 * 1024 * 1024


def _sigmoid(x):
    return 1.0 / (1.0 + jnp.exp(-x))


def _dot(a, b):
    return jnp.dot(a, b, preferred_element_type=F32)


def _dot_nt(a, b):
    return lax.dot_general(a, b, (((1,), (1,)), ((), ())), preferred_element_type=F32)


def _split2(x):
    hi = x.astype(BF16)
    lo = (x - hi.astype(F32)).astype(BF16)
    return hi, lo


def _dot3(a, b):
    ah, al = _split2(a)
    bh, bl = _split2(b)
    return _dot(ah, bh) + _dot(al, bh) + _dot(ah, bl)


def _params(sem):
    return pltpu.CompilerParams(dimension_semantics=sem, vmem_limit_bytes=VMEM_LIMIT)


def _ada_kernel(c_ref, w_ref, b_ref, o_ref):
    c = c_ref[...]
    o_ref[...] = _dot((c * _sigmoid(c)).astype(BF16), w_ref[...].astype(BF16)) + b_ref[...]


def _ada_call(c, ada_w, ada_b):
    n = ada_w.shape[0]
    nb = c.shape[0]
    return pl.pallas_call(
        _ada_kernel,
        grid=(n, 3),
        in_specs=[
            pl.BlockSpec((nb, D_MODEL), lambda i, j: (0, 0)),
            pl.BlockSpec((None, D_MODEL, D_MODEL), lambda i, j: (i, 0, j)),
            pl.BlockSpec((None, 1, D_MODEL), lambda i, j: (i, 0, j)),
        ],
        out_specs=pl.BlockSpec((None, nb, D_MODEL), lambda i, j: (i, 0, j)),
        out_shape=jax.ShapeDtypeStruct((n, nb, 3 * D_MODEL), F32),
        compiler_params=_params(("arbitrary", "arbitrary")),
        name="ada",
    )(c, ada_w, ada_b)


def _norm_mod(x, g, scale, shift):
    var = jnp.mean(x * x, axis=-1, keepdims=True)
    return (x * lax.rsqrt(var + NORM_EPS) * g) * (1.0 + scale) + shift


def _inproj_kernel(x_ref, shift_ref, scale_ref, g_ref, w_ref, mu_ref,
                   ps_ref, mla_ref, gates_ref, carry_ref, *, tiles_per_seq, tm):
    i = pl.program_id(0)

    @pl.when(i % tiles_per_seq == 0)
    def _():
        carry_ref[...] = jnp.zeros_like(carry_ref)

    carry = carry_ref[0:1, :]
    row = lax.broadcasted_iota(jnp.int32, (INPROJ_SUB, PS_W), 0)
    for s in range(0, tm, INPROJ_SUB):
        r = slice(s, s + INPROJ_SUB)
        h = _norm_mod(x_ref[r, :], g_ref[...], scale_ref[...], shift_ref[...]).astype(BF16)
        p = _dot(h, w_ref[:, 0:PS_W])
        prev = jnp.where(row == 0, carry, pltpu.roll(p, 1, axis=0))
        carry = p[INPROJ_SUB - 1:INPROJ_SUB, :]
        ps_ref[r, :] = p + (prev - p) * mu_ref[...]
        mla_ref[r, :] = _dot(h, w_ref[:, PS_W:PS_W + MLA_W])
        gates_ref[r, :] = _sigmoid(_dot(h, w_ref[:, PS_W + MLA_W:])).astype(BF16)
    carry_ref[0:1, :] = carry


INPROJ_SUB = 256


def _inproj_call(x2, shift, scale, g, w_pad, mu_pad, seq, tm=512):
    t = x2.shape[0]
    tiles_per_seq = seq // tm
    bmap = lambda i: (i // tiles_per_seq, 0, 0)
    wtot = PS_W + MLA_W + GATE_W
    return pl.pallas_call(
        functools.partial(_inproj_kernel, tiles_per_seq=tiles_per_seq, tm=tm),
        grid=(t // tm,),
        in_specs=[
            pl.BlockSpec((tm, D_MODEL), lambda i: (i, 0)),
            pl.BlockSpec((None, 1, D_MODEL), bmap),
            pl.BlockSpec((None, 1, D_MODEL), bmap),
            pl.BlockSpec((1, D_MODEL), lambda i: (0, 0)),
            pl.BlockSpec((D_MODEL, wtot), lambda i: (0, 0)),
            pl.BlockSpec((1, PS_W), lambda i: (0, 0)),
        ],
        out_specs=[
            pl.BlockSpec((tm, PS_W), lambda i: (i, 0)),
            pl.BlockSpec((tm, MLA_W), lambda i: (i, 0)),
            pl.BlockSpec((tm, GATE_W), lambda i: (i, 0)),
        ],
        out_shape=[
            jax.ShapeDtypeStruct((t, PS_W), F32),
            jax.ShapeDtypeStruct((t, MLA_W), F32),
            jax.ShapeDtypeStruct((t, GATE_W), BF16),
        ],
        scratch_shapes=[pltpu.VMEM((8, PS_W), F32)],
        compiler_params=_params(("arbitrary",)),
        name="inproj",
    )(x2, shift, scale, g, w_pad, mu_pad)


def _blockdiag(xb, bdmask):
    return jnp.tile(xb, (HALF // CHUNK, 1)) * bdmask


def _bd_mm(dot, a, b, bdmask, passes):
    if passes == 1:
        return dot(a.astype(BF16), _blockdiag(b.astype(BF16), bdmask))
    ah, al = _split2(a)
    bh, bl = _split2(b)
    r = dot(jnp.concatenate([ah, al], axis=0), _blockdiag(bh, bdmask))
    n = a.shape[0]
    return r[:n] + r[n:] + dot(ah, _blockdiag(bl, bdmask))


def _bd_nn(a, b, bdmask, passes):
    return _bd_mm(_dot, a, b, bdmask, passes)


def _bd_nt(a, b, bdmask, passes):
    return _bd_mm(_dot_nt, a, b, bdmask, passes)


def _bd_tn(a, b, bdmask_f, passes):
    full = _dot3(a.T, b) if passes == 3 else _dot(a.T.astype(BF16), b.astype(BF16))
    return jnp.sum((full * bdmask_f).reshape(HALF // CHUNK, CHUNK, HALF), axis=0)


P_SCORE = 1
P_TINV = 1
P_APPLY = 1
P_GH = 1
P_STATE = 1


def _halves(fn, *xs):
    outs = [fn(*[x[:, h * HALF:(h + 1) * HALF] for x in xs]) for h in range(RW // HALF)]
    return jnp.concatenate(outs, axis=1)


def _rwkv_kernel(ps_ref, w0_ref, a0_ref, kk_ref, ka_ref, rk_ref, lg_ref, lb_ref,
                 wup_ref, aup_ref, gup_ref, out_ref, s_ref, *, nc):
    @pl.when(pl.program_id(1) == 0)
    def _():
        s_ref[...] = jnp.zeros_like(s_ref)

    C = CHUNK
    r_i = lax.broadcasted_iota(jnp.int32, (HALF, HALF), 0)
    c_i = lax.broadcasted_iota(jnp.int32, (HALF, HALF), 1)
    bdsame = (r_i // C) == (c_i // C)
    bdmask = bdsame.astype(BF16)
    bdmask_f = bdsame.astype(F32)
    t_i = lax.broadcasted_iota(jnp.int32, (C, HALF), 0)
    i_i = lax.broadcasted_iota(jnp.int32, (C, HALF), 1) % C
    strict = i_i < t_i
    incl = i_i <= t_i
    eye_l = (i_i == t_i).astype(F32)
    n = nc * C
    tr = lax.broadcasted_iota(jnp.int32, (n, n), 0)
    tc = lax.broadcasted_iota(jnp.int32, (n, n), 1)
    tril = ((tc <= tr) & (tc // C == tr // C)).astype(BF16)

    def headsum(x, split=False):
        def one(xh):
            if not split:
                return _dot(xh.astype(BF16), bdmask)
            hi, lo = _split2(xh)
            return _dot(hi, bdmask) + _dot(lo, bdmask)
        return _halves(one, x)

    def cumsum_rows(x):
        hi = x.astype(BF16)
        r1 = x - hi.astype(F32)
        mid = r1.astype(BF16)
        lo = (r1 - mid.astype(F32)).astype(BF16)
        return _dot(tril, hi) + _dot(tril, mid) + _dot(tril, lo)

    p_r = ps_ref[:, 0:RW]
    p_k = ps_ref[:, RW:2 * RW]
    p_v = ps_ref[:, 2 * RW:3 * RW]
    p_w = ps_ref[:, 3 * RW:3 * RW + LANE]
    p_a = ps_ref[:, 3 * RW + LANE:3 * RW + 2 * LANE]
    p_g = ps_ref[:, 3 * RW + 2 * LANE:3 * RW + 3 * LANE]
    logw = -DECAY_SCALE * _sigmoid(w0_ref[...] + _dot3(jnp.tanh(p_w), wup_ref[...]))
    iclr = _sigmoid(a0_ref[...] + _dot(p_a.astype(BF16), aup_ref[...].astype(BF16)))
    gate = _dot(_sigmoid(p_g).astype(BF16), gup_ref[...].astype(BF16))
    kk = p_k * kk_ref[...]
    k2 = p_k * (1.0 + (iclr - 1.0) * ka_ref[...])
    sums = headsum(jnp.concatenate([kk * kk, p_r * k2 * rk_ref[...]], axis=0))
    kk = kk * lax.rsqrt(sums[:n] + 1e-12)
    bonus = sums[n:] * p_v
    b_v = kk * iclr
    cum = cumsum_rows(logw)
    cum_c = jnp.concatenate(
        [jnp.broadcast_to(cum[(c + 1) * C - 1:(c + 1) * C, :], (C, RW)) for c in range(nc)], axis=0)
    e_neg = jnp.exp(-cum)
    e_rem = jnp.exp(cum_c - cum)
    a_all = -kk * jnp.exp(cum - logw)
    b_all = b_v * e_neg
    k_all = k2 * e_neg
    r_all = p_r * jnp.exp(cum)
    bg_all = b_v * e_rem
    kg_all = k2 * e_rem
    gam_all = jnp.exp(cum_c)

    nh = RW // HALF
    units = [(c, h) for c in range(nc) for h in range(nh)]

    def cut(x, u):
        c, h = u
        return x[c * C:(c + 1) * C, h * HALF:(h + 1) * HALF]

    each = lambda fn, *lists: [fn(*xs) for xs in zip(*lists)]
    stack = lambda x, y: jnp.concatenate([x, y], axis=0)
    top = lambda x: x[:C]
    bot = lambda x: x[C:]
    nn = lambda p: (lambda a, b: _bd_nn(a, b, bdmask, p))

    a_t, b_t, k_t, r_t, b_g, k_g, v_u = ([cut(x, u) for u in units]
                                         for x in (a_all, b_all, k_all, r_all, bg_all, kg_all, p_v))
    gam = [gam_all[c * C:c * C + 1, h * HALF:(h + 1) * HALF] for c, h in units]
    ar = each(stack, a_t, r_t)
    sb = each(lambda x, y: _bd_nt(x, y, bdmask, P_SCORE), ar, b_t)
    sk = each(lambda x, y: _bd_nt(x, y, bdmask, P_SCORE), ar, k_t)
    l_ab = [jnp.where(strict, top(x), 0.0) for x in sb]
    q_b = [jnp.where(incl, bot(x), 0.0) for x in sb]
    m_ak = [jnp.where(strict, top(x), 0.0) for x in sk]
    q_k = [jnp.where(incl, bot(x), 0.0) for x in sk]
    t_m = [eye_l + x for x in l_ab]
    pw = each(nn(P_TINV), l_ab, l_ab)
    for _ in range(4):
        tp = each(nn(P_TINV), each(stack, t_m, pw), pw)
        t_m = each(lambda t, x: t + top(x), t_m, tp)
        pw = [bot(x) for x in tp]
    t_m = each(lambda t, x: t + x, t_m, each(nn(P_TINV), t_m, pw))
    mq = each(nn(P_APPLY), each(stack, m_ak, q_k), v_u)
    a_h = each(nn(P_APPLY), t_m, a_t)
    u0 = each(nn(P_APPLY), t_m, [top(x) for x in mq])
    r_h = each(lambda r, x: r + x, r_t, each(nn(P_APPLY), q_b, a_h))
    y0 = each(lambda m, x: bot(m) + x, mq, each(nn(P_APPLY), q_b, u0))
    g_l = each(lambda a, b, g: _bd_tn(a, b, bdmask_f, P_GH) + eye_l * g, a_h, b_g, gam)
    h_l = each(lambda u, v, b, k: _bd_tn(stack(u, v), stack(b, k), bdmask_f, P_GH), u0, v_u, b_g, k_g)

    s_all = s_ref[...]
    s_cur = [s_all[:, h * HALF:(h + 1) * HALF] for h in range(nh)]
    ys = []
    for c in range(nc):
        idx = [c * nh + h for h in range(nh)]
        yc = [_bd_nt(r_h[i], s, bdmask, P_STATE) + y0[i] for i, s in zip(idx, s_cur)]
        s_cur = [_bd_nn(s, g_l[i], bdmask, P_STATE) + h_l[i] for i, s in zip(idx, s_cur)]
        ys.append(jnp.concatenate(yc, axis=1))
    s_ref[...] = jnp.concatenate(s_cur, axis=1)
    y = jnp.concatenate(ys, axis=0)
    inv_n = 1.0 / RWKV_HEAD_DIM
    mu = headsum(y, split=True) * inv_n
    d = y - mu
    var = headsum(d * d) * inv_n
    yn = d * lax.rsqrt(var + LNX_EPS) * lg_ref[...] + lb_ref[...]
    out_ref[...] = ((yn + bonus) * gate).astype(BF16)


def _rwkv_call(ps, vecs, wup, aup, gup, batch, seq, nc=4):
    t = ps.shape[0]
    rows = nc * CHUNK
    steps = seq // rows
    vec_spec = pl.BlockSpec((1, RW), lambda b, j: (0, 0))
    lora_spec = pl.BlockSpec((LANE, RW), lambda b, j: (0, 0))
    return pl.pallas_call(
        functools.partial(_rwkv_kernel, nc=nc),
        grid=(batch, steps),
        in_specs=[pl.BlockSpec((rows, PS_W), lambda b, j: (b * steps + j, 0))]
                 + [vec_spec] * 7 + [lora_spec] * 3,
        out_specs=pl.BlockSpec((rows, RW), lambda b, j: (b * steps + j, 0)),
        out_shape=jax.ShapeDtypeStruct((t, RW), BF16),
        scratch_shapes=[pltpu.VMEM((CHUNK, RW), F32)],
        compiler_params=_params(("arbitrary", "arbitrary")),
        name="rwkv",
    )(ps, *vecs, wup, aup, gup)


def _mlaprep_kernel(mla_ref, pos_ref, invf_ref, qn_ref, kvn_ref, wq_ref, wqs_ref, wk_ref, wv_ref,
                    q_ref, k_ref, v_ref):
    def rms(x, g):
        return (x * lax.rsqrt(jnp.mean(x * x, axis=-1, keepdims=True) + NORM_EPS) * g).astype(BF16)

    ang = pos_ref[...].astype(F32) * invf_ref[...]
    cos = jnp.cos(ang)
    sin = jnp.sin(ang)
    lane = lax.broadcasted_iota(jnp.int32, cos.shape, 1)
    cos_q = jnp.tile(jnp.where(lane < QK_NOPE, 1.0, cos), (1, MLA_HEADS))
    sin_q = jnp.tile(sin, (1, MLA_HEADS))
    qn = rms(mla_ref[:, 0:Q_LORA], qn_ref[...])
    q = _dot(qn, wq_ref[...]) * cos_q + _dot(qn, wqs_ref[...]) * sin_q
    q_ref[...] = (q * ((QK_NOPE + QK_ROPE) ** -0.5 * LOG2E)).astype(BF16)
    kvn = rms(mla_ref[:, Q_LORA:Q_LORA + KV_LORA], kvn_ref[...])
    o = Q_LORA + KV_LORA
    k_pe = mla_ref[:, o:o + LANE] * cos + mla_ref[:, o + LANE:o + 2 * LANE] * sin
    k_ref[...] = (_dot(kvn, wk_ref[...]) + jnp.tile(k_pe, (1, MLA_HEADS))).astype(BF16)
    v_ref[...] = _dot(kvn, wv_ref[...]).astype(BF16)


def _mlaprep_call(mla, pos, invf, qn, kvn, wq, wqs, wk, wv, tm=512):
    t = mla.shape[0]
    hw = MLA_HEADS * HEAD_PAD
    vw = MLA_HEADS * V_HEAD
    full = lambda shape: pl.BlockSpec(shape, lambda i: (0, 0))
    row = lambda w: pl.BlockSpec((tm, w), lambda i: (i, 0))
    return pl.pallas_call(
        _mlaprep_kernel,
        grid=(t // tm,),
        in_specs=[row(MLA_W), row(1), full((1, LANE)), full((1, Q_LORA)), full((1, KV_LORA)),
                  full((Q_LORA, hw)), full((Q_LORA, hw)), full((KV_LORA, hw)), full((KV_LORA, vw))],
        out_specs=[row(hw), row(hw), row(vw)],
        out_shape=[jax.ShapeDtypeStruct((t, w), BF16) for w in (hw, hw, vw)],
        compiler_params=_params(("arbitrary",)),
        name="mlaprep",
    )(mla, pos, invf, qn, kvn, wq, wqs, wk, wv)


def _attn_kernel(q_ref, k_ref, v_ref, o_ref, *, seq, tq):
    r_i = lax.broadcasted_iota(jnp.int32, (tq, tq), 0) // CHUNK
    c_i = lax.broadcasted_iota(jnp.int32, (tq, tq), 1) // CHUNK
    mask = c_i <= r_i
    def scores(item):
        qi, hh = item
        lo = qi * tq
        qk = slice(hh * HEAD_PAD, (hh + 1) * HEAD_PAD)
        q = q_ref[lo:lo + tq, qk]
        sd = jnp.where(mask, _dot_nt(q, k_ref[lo:lo + tq, qk]), -1e30)
        so = _dot_nt(q, k_ref[0:lo, qk]) if qi > 0 else None
        return sd, so

    items = [(qi, hh) for qi in range(seq // tq) for hh in range(ATTN_HEADS_PER_STEP)]
    nxt = scores(items[0])
    for n, (qi, hh) in enumerate(items):
        lo = qi * tq
        vo = slice(hh * V_HEAD, (hh + 1) * V_HEAD)
        sd, so = nxt
        if n + 1 < len(items):
            nxt = scores(items[n + 1])
        m = jnp.max(sd, axis=-1, keepdims=True)
        if qi > 0:
            m = jnp.maximum(m, jnp.max(so, axis=-1, keepdims=True))
        pd = jnp.exp2(sd - m)
        l = jnp.sum(pd, axis=-1, keepdims=True)
        acc = _dot(pd.astype(BF16), v_ref[lo:lo + tq, vo])
        if qi > 0:
            po = jnp.exp2(so - m)
            l = l + jnp.sum(po, axis=-1, keepdims=True)
            acc = acc + _dot(po.astype(BF16), v_ref[0:lo, vo])
        o_ref[lo:lo + tq, vo] = (acc / l).astype(BF16)


ATTN_HEADS_PER_STEP = LANE // V_HEAD


def _attn_call(q, k, v, batch, seq, tq=256):
    hp = ATTN_HEADS_PER_STEP
    qk_spec = pl.BlockSpec((seq, hp * HEAD_PAD), lambda b, h: (b, h))
    vo_spec = pl.BlockSpec((seq, hp * V_HEAD), lambda b, h: (b, h))
    return pl.pallas_call(
        functools.partial(_attn_kernel, seq=seq, tq=tq),
        grid=(batch, MLA_HEADS // hp),
        in_specs=[qk_spec, qk_spec, vo_spec],
        out_specs=vo_spec,
        out_shape=jax.ShapeDtypeStruct(v.shape, BF16),
        compiler_params=_params(("arbitrary", "arbitrary")),
        name="attn",
    )(q, k, v)


MIX_SUB = 256


def _mixout_kernel(*refs, router):
    if router:
        (rw_ref, o_ref, gates_ref, x_ref, gate_ref, wr_ref, wm_ref, wo_ref,
         g2_ref, shift2_ref, scale2_ref, rwt_ref, rb_ref,
         xo_ref, h2_ref, ti_ref, tp_ref) = refs
    else:
        (rw_ref, o_ref, gates_ref, x_ref, gate_ref, wr_ref, wm_ref, wo_ref,
         g2_ref, shift2_ref, scale2_ref, xo_ref, h2_ref) = refs
    n = x_ref.shape[0]
    subs = [slice(s, s + MIX_SUB) for s in range(0, n, MIX_SUB)]
    yab = [(_dot(rw_ref[r, :], wr_ref[...]), _dot(o_ref[r, :], wm_ref[...])) for r in subs]
    merged = [gates_ref[r, 0:D_MODEL].astype(F32) * ya + gates_ref[r, D_MODEL:].astype(F32) * yb
              for r, (ya, yb) in zip(subs, yab)]
    mixed = [_dot(m.astype(BF16), wo_ref[...]) for m in merged]
    h2s = []
    for r, mx in zip(subs, mixed):
        xn = x_ref[r, :] + gate_ref[...] * mx
        xo_ref[r, :] = xn
        h2s.append(_norm_mod(xn, g2_ref[...], scale2_ref[...], shift2_ref[...]))
        h2_ref[r, :] = h2s[-1].astype(h2_ref.dtype)
    if router:
        h2 = jnp.concatenate(h2s, axis=0)
        hh, hl = _split2(h2)
        wh, wl = _split2(rwt_ref[...])
        logits = _dot_nt(wh, hh) + _dot_nt(wl, hh) + _dot_nt(wh, hl) + rb_ref[...]
        e_i = lax.broadcasted_iota(jnp.int32, logits.shape, 0)
        m1 = jnp.max(logits, axis=0, keepdims=True)
        i1 = jnp.min(jnp.where(logits == m1, e_i, N_EXPERTS), axis=0, keepdims=True)
        rest = jnp.where(e_i == i1, -jnp.inf, logits)
        m2 = jnp.max(rest, axis=0, keepdims=True)
        i2 = jnp.min(jnp.where(rest == m2, e_i, N_EXPERTS), axis=0, keepdims=True)
        e2 = jnp.exp(m2 - m1)
        den = 1.0 + e2
        ti_ref[0:1, :] = i1
        ti_ref[1:2, :] = i2
        tp_ref[0:1, :] = 1.0 / den
        tp_ref[1:2, :] = e2 / den


def _mixout_call(rw, o, gates, x2, gate, wr, wm, wo, g2, shift2, scale2, seq, router=None, tm=512):
    t = x2.shape[0]
    tiles_per_seq = seq // tm
    bmap = lambda i: (i // tiles_per_seq, 0, 0)
    full = lambda shape: pl.BlockSpec(shape, lambda i: (0, 0))
    row = lambda w: pl.BlockSpec((tm, w), lambda i: (i, 0))
    mod = pl.BlockSpec((None, 1, D_MODEL), bmap)
    in_specs = [row(RW), row(MLA_HEADS * V_HEAD), row(GATE_W), row(D_MODEL), mod,
                full(wr.shape), full(wm.shape), full(wo.shape), full((1, D_MODEL)), mod, mod]
    args = [rw, o, gates, x2, gate, wr, wm, wo, g2, shift2, scale2]
    out_specs = [row(D_MODEL), row(D_MODEL)]
    out_shape = [jax.ShapeDtypeStruct((t, D_MODEL), F32),
                 jax.ShapeDtypeStruct((t, D_MODEL), BF16 if router is None else F32)]
    if router is not None:
        in_specs += [full((N_EXPERTS, D_MODEL)), full((N_EXPERTS, 1))]
        args += list(router)
        out_specs += [pl.BlockSpec((2, tm), lambda i: (0, i))] * 2
        out_shape += [jax.ShapeDtypeStruct((2, t), jnp.int32), jax.ShapeDtypeStruct((2, t), F32)]
    return pl.pallas_call(
        functools.partial(_mixout_kernel, router=router is not None),
        grid=(t // tm,),
        in_specs=in_specs,
        out_specs=out_specs,
        out_shape=out_shape,
        compiler_params=_params(("arbitrary",)),
        name="mixout",
    )(*args)


def _swiglu_step(h, w1_ref, w3_ref, w2_ref, acc_ref, finish):
    f = pl.program_id(1)

    @pl.when(f == 0)
    def _():
        acc_ref[...] = jnp.zeros_like(acc_ref)

    a1 = _dot(h, w1_ref[...])
    a3 = _dot(h, w3_ref[...])
    act = (a1 * _sigmoid(a1)) * a3
    acc_ref[...] += _dot(act.astype(BF16), w2_ref[...])

    @pl.when(f == pl.num_programs(1) - 1)
    def _():
        finish(acc_ref[...])


def _rms(x, g):
    return x * lax.rsqrt(jnp.mean(x * x, axis=-1, keepdims=True) + NORM_EPS) * g


def _ffn_kernel(h_ref, x_ref, gate_ref, gf_ref, w1_ref, w3_ref, w2_ref, o_ref, *, final, tf):
    h = h_ref[...]
    ff = w1_ref.shape[1]
    ups = [(_dot(h, w1_ref[:, s:s + tf]), _dot(h, w3_ref[:, s:s + tf])) for s in range(0, ff, tf)]
    acc = None
    for s, (a1, a3) in zip(range(0, ff, tf), ups):
        act = ((a1 * _sigmoid(a1)) * a3).astype(BF16)
        part = _dot(act, w2_ref[s:s + tf, :])
        acc = part if acc is None else acc + part
    xn = x_ref[...] + gate_ref[...] * acc
    o_ref[...] = _rms(xn, gf_ref[...]) if final else xn


def _ffn_call(h, x2, gate, gf, w1, w3, w2, seq, final, tm=512, tf=1408):
    r = h.shape[0]
    tiles_per_seq = seq // tm
    row = pl.BlockSpec((tm, D_MODEL), lambda i: (i, 0))
    resident = lambda a: pl.BlockSpec(a.shape, lambda i: (0, 0), pipeline_mode=pl.Buffered(1))
    return pl.pallas_call(
        functools.partial(_ffn_kernel, final=final, tf=tf),
        grid=(r // tm,),
        in_specs=[
            row, row,
            pl.BlockSpec((None, 1, D_MODEL), lambda i: (i // tiles_per_seq, 0, 0)),
            pl.BlockSpec((1, D_MODEL), lambda i: (0, 0)),
            resident(w1), resident(w3), resident(w2),
        ],
        out_specs=row,
        out_shape=jax.ShapeDtypeStruct((r, D_MODEL), F32),
        compiler_params=_params(("arbitrary",)),
        name="ffn",
    )(h, x2, gate, gf, w1, w3, w2)


MOE_TM = 512


def _route(top_i, tm):
    t = top_i.shape[1]
    e_flat = top_i.reshape(-1)
    oh = (e_flat[:, None] == jnp.arange(N_EXPERTS, dtype=jnp.int32)[None, :]).astype(jnp.int32)
    csum = jnp.cumsum(oh, axis=0)
    rank = jnp.sum(csum * oh, axis=1) - 1
    counts = csum[-1]
    padded = ((counts + tm - 1) // tm) * tm
    ends = jnp.cumsum(padded)
    pos = jnp.sum(oh * (ends - padded)[None, :], axis=1) + rank
    n_tiles = (2 * t) // tm + N_EXPERTS
    starts = jnp.arange(n_tiles, dtype=jnp.int32) * tm
    tile_e = jnp.sum((starts[:, None] >= ends[None, :]).astype(jnp.int32), axis=1)
    return (pos.astype(jnp.int32), ends.astype(jnp.int32), jnp.minimum(tile_e, N_EXPERTS - 1),
            (ends[-1:] // tm).astype(jnp.int32))


def _dispatch_kernel(pos_ref, ends_ref, h_ref, xs_ref, zero_ref, sem, *, t, td, tm):
    i = pl.program_id(0)
    base = i * td

    @pl.when(i == 0)
    def _():
        zero_ref[...] = jnp.zeros_like(zero_ref)
        n_tiles = xs_ref.shape[0] // tm
        fills = []
        for e in range(N_EXPERTS):
            start = ends_ref[e - 1] if e else 0
            fills.append((ends_ref[e] > start, pl.multiple_of(ends_ref[e] - tm, tm)))
            fills.append((ends_ref[N_EXPERTS - 1] // tm + e < n_tiles,
                          pl.multiple_of(ends_ref[N_EXPERTS - 1] + e * tm, tm)))
        for go, row0 in fills:
            @pl.when(go)
            def _():
                pltpu.make_async_copy(zero_ref, xs_ref.at[pl.ds(row0, tm)], sem).start()
        for go, row0 in fills:
            @pl.when(go)
            def _():
                pltpu.make_async_copy(zero_ref, xs_ref.at[pl.ds(row0, tm)], sem).wait()

    for r in range(td):
        for k in range(2):
            dst = pos_ref[k * t + base + r]
            pltpu.make_async_copy(h_ref.at[pl.ds(r, 1)], xs_ref.at[pl.ds(dst, 1)], sem).start()
    for k in range(2):
        pltpu.make_async_copy(h_ref, xs_ref.at[pl.ds(0, td)], sem).wait()


def _dispatch_call(pos, ends, h, n_rows, tm, td=512):
    t = h.shape[0]
    return pl.pallas_call(
        functools.partial(_dispatch_kernel, t=t, td=td, tm=tm),
        grid_spec=pltpu.PrefetchScalarGridSpec(
            num_scalar_prefetch=2, grid=(t // td,),
            in_specs=[pl.BlockSpec((td, D_MODEL), lambda i, pos, ends: (i, 0))],
            out_specs=pl.BlockSpec(memory_space=pl.ANY),
            scratch_shapes=[pltpu.VMEM((tm, D_MODEL), h.dtype), pltpu.SemaphoreType.DMA(())]),
        out_shape=jax.ShapeDtypeStruct((n_rows, D_MODEL), h.dtype),
        compiler_params=_params(("arbitrary",)),
        name="dispatch",
    )(pos, ends, h)


def _ffn_routed_kernel(te_ref, nu_ref, x_ref, w1_ref, w3_ref, w2_ref, y_ref, acc_ref):
    del te_ref
    used = pl.program_id(0) < nu_ref[0]

    def finish(acc):
        y_ref[...] = acc

    @pl.when(used)
    def _():
        _swiglu_step(x_ref[...].astype(BF16), w1_ref, w3_ref, w2_ref, acc_ref, finish)

    @pl.when(jnp.logical_not(used))
    def _():
        y_ref[...] = jnp.zeros_like(y_ref)


def _ffn_routed_call(tile_e, n_used, xs, w1, w3, w2, tm=MOE_TM, tf=1792):
    r = xs.shape[0]
    ff = w1.shape[2]
    return pl.pallas_call(
        _ffn_routed_kernel,
        grid_spec=pltpu.PrefetchScalarGridSpec(
            num_scalar_prefetch=2, grid=(r // tm, ff // tf),
            in_specs=[
                pl.BlockSpec((tm, D_MODEL), lambda i, f, te, nu: (i, 0)),
                pl.BlockSpec((None, D_MODEL, tf), lambda i, f, te, nu: (te[i], 0, f)),
                pl.BlockSpec((None, D_MODEL, tf), lambda i, f, te, nu: (te[i], 0, f)),
                pl.BlockSpec((None, tf, D_MODEL), lambda i, f, te, nu: (te[i], f, 0)),
            ],
            out_specs=pl.BlockSpec((tm, D_MODEL), lambda i, f, te, nu: (i, 0)),
            scratch_shapes=[pltpu.VMEM((tm, D_MODEL), F32)]),
        out_shape=jax.ShapeDtypeStruct((r, D_MODEL), F32),
        compiler_params=_params(("arbitrary", "arbitrary")),
        name="ffn_routed",
    )(tile_e, n_used, xs, w1, w3, w2)


def _combine_kernel(pos_ref, x_ref, p_ref, gate_ref, gf_ref, y_ref, o_ref, buf, sem, *, t, tc, final):
    i = pl.program_id(0)

    def row_copy(step, slot, r, k):
        src = pos_ref[k * t + step * tc + r]
        return pltpu.make_async_copy(y_ref.at[pl.ds(src, 1)], buf.at[slot, k, pl.ds(r, 1)], sem.at[slot])

    def issue(step, slot):
        for r in range(tc):
            for k in range(2):
                row_copy(step, slot, r, k).start()

    @pl.when(i == 0)
    def _():
        issue(0, 0)

    @pl.when(i + 1 < pl.num_programs(0))
    def _():
        issue(i + 1, (i + 1) % 2)

    slot = i % 2
    for k in range(2):
        pltpu.make_async_copy(y_ref.at[pl.ds(0, tc)], buf.at[slot, k], sem.at[slot]).wait()
    p = p_ref[...]
    xn = x_ref[...] + gate_ref[...] * (p[:, 0:1] * buf[slot, 0] + p[:, 1:2] * buf[slot, 1])
    o_ref[...] = _rms(xn, gf_ref[...]) if final else xn


def _combine_call(pos, x2, probs, gate, gf, y, seq, final, tc=256):
    t = x2.shape[0]
    tiles_per_seq = seq // tc
    row = pl.BlockSpec((tc, D_MODEL), lambda i, pos: (i, 0))
    return pl.pallas_call(
        functools.partial(_combine_kernel, t=t, tc=tc, final=final),
        grid_spec=pltpu.PrefetchScalarGridSpec(
            num_scalar_prefetch=1, grid=(t // tc,),
            in_specs=[row, pl.BlockSpec((tc, 2), lambda i, pos: (i, 0)),
                      pl.BlockSpec((None, 1, D_MODEL), lambda i, pos: (i // tiles_per_seq, 0, 0)),
                      pl.BlockSpec((1, D_MODEL), lambda i, pos: (0, 0)),
                      pl.BlockSpec(memory_space=pl.ANY)],
            out_specs=row,
            scratch_shapes=[pltpu.VMEM((2, 2, tc, D_MODEL), F32), pltpu.SemaphoreType.DMA((2,))]),
        out_shape=jax.ShapeDtypeStruct((t, D_MODEL), F32),
        compiler_params=_params(("arbitrary",)),
        name="combine",
    )(pos, x2, probs, gate, gf, y)


def _layout_w_in_kernel(w_ref, o_ref):
    half = QK_ROPE // 2
    o_ref[...] = jnp.zeros_like(o_ref)

    def put(dst, src, width, sign=1.0):
        o_ref[:, dst:dst + width] = (sign * w_ref[:, src:src + width]).astype(BF16)

    o = 3 * RW
    put(0, 0, o + DECAY_LORA)
    put(o + LANE, o + DECAY_LORA, ICLR_LORA)
    put(o + 2 * LANE, o + DECAY_LORA + ICLR_LORA, GATE_LORA)
    src = o + DECAY_LORA + ICLR_LORA + GATE_LORA
    put(PS_W, src, Q_LORA + KV_LORA)
    kr = src + Q_LORA + KV_LORA
    dst = PS_W + Q_LORA + KV_LORA
    put(dst + QK_NOPE, kr, QK_ROPE)
    put(dst + LANE + QK_NOPE, kr + half, half, -1.0)
    put(dst + LANE + QK_NOPE + half, kr, half)
    put(PS_W + MLA_W, kr + QK_ROPE, GATE_W)


def _layout_w_in(w, layer):
    _, k, n = w.shape
    return pl.pallas_call(
        _layout_w_in_kernel,
        grid=(k // 256,),
        in_specs=[pl.BlockSpec((None, 256, n), lambda i: (layer, i, 0))],
        out_specs=pl.BlockSpec((256, PS_W + MLA_W + GATE_W), lambda i: (i, 0)),
        out_shape=jax.ShapeDtypeStruct((k, PS_W + MLA_W + GATE_W), BF16),
        compiler_params=_params(("arbitrary",)),
        name="layout_w_in",
    )(w)


def _layout_mu(mu):
    z = jnp.zeros((64,), mu.dtype)
    o = 3 * RW
    return jnp.concatenate([mu[:o + 64], z, mu[o + 64:o + 128], z, mu[o + 128:]])[None, :]


def _layout_wq(w_qb):
    hd = QK_NOPE + QK_ROPE
    w = w_qb.reshape(Q_LORA, MLA_HEADS, hd)
    half = QK_ROPE // 2
    x1, x2 = w[:, :, QK_NOPE:QK_NOPE + half], w[:, :, QK_NOPE + half:]
    zn = jnp.zeros((Q_LORA, MLA_HEADS, QK_NOPE), w.dtype)
    zp = jnp.zeros((Q_LORA, MLA_HEADS, HEAD_PAD - hd), w.dtype)
    wq = jnp.concatenate([w, zp], axis=2).reshape(Q_LORA, -1)
    wqs = jnp.concatenate([zn, -x2, x1, zp], axis=2).reshape(Q_LORA, -1)
    return wq.astype(BF16), wqs.astype(BF16)


def _layout_wkv(w_kvb):
    w = w_kvb.reshape(KV_LORA, MLA_HEADS, QK_NOPE + V_HEAD)
    z = jnp.zeros((KV_LORA, MLA_HEADS, HEAD_PAD - QK_NOPE), w.dtype)
    wk = jnp.concatenate([w[:, :, :QK_NOPE], z], axis=2).reshape(KV_LORA, -1)
    wv = w[:, :, QK_NOPE:].reshape(KV_LORA, -1)
    return wk.astype(BF16), wv.astype(BF16)


def _rope_freq_row():
    inv_freq = ROPE_THETA ** (-jnp.arange(0, QK_ROPE, 2, dtype=F32) / QK_ROPE)
    z = jnp.zeros((QK_NOPE,), F32)
    zp = jnp.zeros((LANE - QK_NOPE - QK_ROPE,), F32)
    return jnp.concatenate([z, inv_freq, inv_freq, zp])[None, :]


def kernel(x, c, positions, ada_w, ada_b, norm_mix, norm_ffn, norm_final, w_in, tshift_mu, w0, w_up, a0, a_up, g_up, k_k, k_a, r_k, lnx_g, lnx_b, rwkv_out, q_norm, w_qb, kv_norm, w_kvb, mla_out, w_o, ffn_w1, ffn_w3, ffn_w2, router_w, router_b, moe_w1, moe_w3, moe_w2):
    batch, seq, _ = x.shape
    t = batch * seq
    mods = _ada_call(c, ada_w.reshape(2 * DEPTH, D_MODEL, 3 * D_MODEL),
                     ada_b.reshape(2 * DEPTH, 1, 3 * D_MODEL))
    mods = mods.reshape(DEPTH, 2, batch, 3, 1, D_MODEL)
    pos_col = positions.reshape(t, 1)
    invf = _rope_freq_row()
    x2 = x.reshape(t, D_MODEL)
    row = lambda v: v.reshape(1, -1)

    for l in range(DEPTH):
        shift, scale, gate = (mods[l, 0, :, j] for j in range(3))
        ps, mla, gates = _inproj_call(x2, shift, scale, row(norm_mix[l]), _layout_w_in(w_in, l),
                                      _layout_mu(tshift_mu[l]), seq)
        vecs = [row(v[l]) for v in (w0, a0, k_k, k_a, r_k, lnx_g, lnx_b)]
        lora_pad = lambda w: jnp.pad(w, ((0, LANE - w.shape[0]), (0, 0)))
        rw = _rwkv_call(ps, vecs, lora_pad(w_up[l]), lora_pad(a_up[l]), g_up[l], batch, seq)
        wq, wqs = _layout_wq(w_qb[l])
        wk, wv = _layout_wkv(w_kvb[l])
        q, k, v = _mlaprep_call(mla, pos_col, invf, row(q_norm[l]), row(kv_norm[l]), wq, wqs, wk, wv)
        o = _attn_call(q, k, v, batch, seq)

        shift2, scale2, gate2 = (mods[l, 1, :, j] for j in range(3))
        moe = l % 2 == 1
        router = (router_w[l // 2].T, router_b[l // 2].reshape(N_EXPERTS, 1)) if moe else None
        outs = _mixout_call(rw, o, gates, x2, gate, rwkv_out[l].astype(BF16), mla_out[l].astype(BF16),
                            w_o[l].astype(BF16), row(norm_ffn[l]), shift2, scale2, seq, router=router)
        final = l == DEPTH - 1
        if moe:
            x2, h2, top_i, top_p = outs
            pos, ends, tile_e, n_used = _route(top_i, MOE_TM)
            xs = _dispatch_call(pos, ends, h2, 2 * t + N_EXPERTS * MOE_TM, MOE_TM)
            y = _ffn_routed_call(tile_e, n_used, xs, moe_w1[l // 2].astype(BF16),
                                 moe_w3[l // 2].astype(BF16), moe_w2[l // 2].astype(BF16))
            x2 = _combine_call(pos, x2, top_p.T, gate2, row(norm_final), y, seq, final)
        else:
            x2, h2 = outs
            x2 = _ffn_call(h2, x2, gate2, row(norm_final), ffn_w1[l // 2].astype(BF16),
                           ffn_w3[l // 2].astype(BF16), ffn_w2[l // 2].astype(BF16), seq, final)
    return x2.reshape(batch, seq, D_MODEL)
```

```python
import functools
import math

import jax
import jax.numpy as jnp
from jax import lax
from jax.experimental import pallas as pl
from jax.experimental.pallas import tpu as pltpu

F32 = jnp.float32
BF16 = jnp.bfloat16

D_MODEL = 1024
DEPTH = 2
CHUNK = 64
NORM_EPS = 1e-6
RWKV_HEADS = 8
RWKV_HEAD_DIM = 64
RW = RWKV_HEADS * RWKV_HEAD_DIM
DECAY_LORA = 64
ICLR_LORA = 64
GATE_LORA = 128
DECAY_SCALE = math.exp(-0.5)
LNX_EPS = 64e-5
MLA_HEADS = 8
Q_LORA = 384
KV_LORA = 256
QK_NOPE = 64
QK_ROPE = 32
V_HEAD = 64
ROPE_THETA = 10000.0
LOG2E = math.log2(math.e)
N_EXPERTS = 8

LANE = 128
HEAD_PAD = 128
PS_W = 3 * RW + 3 * LANE
MLA_W = Q_LORA + KV_LORA + 2 * LANE
GATE_W = 2 * D_MODEL
HALF = 256
VMEM_LIMIT = 56 * 1024 * 1024


def _sigmoid(x):
    return 1.0 / (1.0 + jnp.exp(-x))


def _dot(a, b):
    return jnp.dot(a, b, preferred_element_type=F32)


def _dot_nt(a, b):
    return lax.dot_general(a, b, (((1,), (1,)), ((), ())), preferred_element_type=F32)


def _split2(x):
    hi = x.astype(BF16)
    lo = (x - hi.astype(F32)).astype(BF16)
    return hi, lo


def _dot3(a, b):
    ah, al = _split2(a)
    bh, bl = _split2(b)
    return _dot(ah, bh) + _dot(al, bh) + _dot(ah, bl)


def _params(sem):
    return pltpu.CompilerParams(dimension_semantics=sem, vmem_limit_bytes=VMEM_LIMIT)


def _ada_kernel(c_ref, w_ref, b_ref, o_ref):
    c = c_ref[...]
    o_ref[...] = _dot((c * _sigmoid(c)).astype(BF16), w_ref[...].astype(BF16)) + b_ref[...]


def _ada_call(c, ada_w, ada_b):
    n = ada_w.shape[0]
    nb = c.shape[0]
    return pl.pallas_call(
        _ada_kernel,
        grid=(n, 3),
        in_specs=[
            pl.BlockSpec((nb, D_MODEL), lambda i, j: (0, 0)),
            pl.BlockSpec((None, D_MODEL, D_MODEL), lambda i, j: (i, 0, j)),
            pl.BlockSpec((None, 1, D_MODEL), lambda i, j: (i, 0, j)),
        ],
        out_specs=pl.BlockSpec((None, nb, D_MODEL), lambda i, j: (i, 0, j)),
        out_shape=jax.ShapeDtypeStruct((n, nb, 3 * D_MODEL), F32),
        compiler_params=_params(("arbitrary", "arbitrary")),
        name="ada",
    )(c, ada_w, ada_b)


def _norm_mod(x, g, scale, shift):
    var = jnp.mean(x * x, axis=-1, keepdims=True)
    return (x * lax.rsqrt(var + NORM_EPS) * g) * (1.0 + scale) + shift


def _inproj_kernel(x_ref, shift_ref, scale_ref, g_ref, w_ref, mu_ref,
                   ps_ref, mla_ref, gates_ref, carry_ref, *, tiles_per_seq, tm):
    i = pl.program_id(0)

    @pl.when(i % tiles_per_seq == 0)
    def _():
        carry_ref[...] = jnp.zeros_like(carry_ref)

    carry = carry_ref[0:1, :]
    row = lax.broadcasted_iota(jnp.int32, (INPROJ_SUB, PS_W), 0)
    for s in range(0, tm, INPROJ_SUB):
        r = slice(s, s + INPROJ_SUB)
        h = _norm_mod(x_ref[r, :], g_ref[...], scale_ref[...], shift_ref[...]).astype(BF16)
        p = _dot(h, w_ref[:, 0:PS_W])
        prev = jnp.where(row == 0, carry, pltpu.roll(p, 1, axis=0))
        carry = p[INPROJ_SUB - 1:INPROJ_SUB, :]
        ps_ref[r, :] = p + (prev - p) * mu_ref[...]
        mla_ref[r, :] = _dot(h, w_ref[:, PS_W:PS_W + MLA_W])
        gates_ref[r, :] = _sigmoid(_dot(h, w_ref[:, PS_W + MLA_W:])).astype(BF16)
    carry_ref[0:1, :] = carry


INPROJ_SUB = 256


def _inproj_call(x2, shift, scale, g, w_pad, mu_pad, seq, tm=512):
    t = x2.shape[0]
    tiles_per_seq = seq // tm
    bmap = lambda i: (i // tiles_per_seq, 0, 0)
    wtot = PS_W + MLA_W + GATE_W
    return pl.pallas_call(
        functools.partial(_inproj_kernel, tiles_per_seq=tiles_per_seq, tm=tm),
        grid=(t // tm,),
        in_specs=[
            pl.BlockSpec((tm, D_MODEL), lambda i: (i, 0)),
            pl.BlockSpec((None, 1, D_MODEL), bmap),
            pl.BlockSpec((None, 1, D_MODEL), bmap),
            pl.BlockSpec((1, D_MODEL), lambda i: (0, 0)),
            pl.BlockSpec((D_MODEL, wtot), lambda i: (0, 0)),
            pl.BlockSpec((1, PS_W), lambda i: (0, 0)),
        ],
        out_specs=[
            pl.BlockSpec((tm, PS_W), lambda i: (i, 0)),
            pl.BlockSpec((tm, MLA_W), lambda i: (i, 0)),
            pl.BlockSpec((tm, GATE_W), lambda i: (i, 0)),
        ],
        out_shape=[
            jax.ShapeDtypeStruct((t, PS_W), F32),
            jax.ShapeDtypeStruct((t, MLA_W), F32),
            jax.ShapeDtypeStruct((t, GATE_W), BF16),
        ],
        scratch_shapes=[pltpu.VMEM((8, PS_W), F32)],
        compiler_params=_params(("arbitrary",)),
        name="inproj",
    )(x2, shift, scale, g, w_pad, mu_pad)


def _blockdiag(xb, bdmask):
    return jnp.tile(xb, (HALF // CHUNK, 1)) * bdmask


def _bd_mm(dot, a, b, bdmask, passes):
    if passes == 1:
        return dot(a.astype(BF16), _blockdiag(b.astype(BF16), bdmask))
    ah, al = _split2(a)
    bh, bl = _split2(b)
    r = dot(jnp.concatenate([ah, al], axis=0), _blockdiag(bh, bdmask))
    n = a.shape[0]
    return r[:n] + r[n:] + dot(ah, _blockdiag(bl, bdmask))


def _bd_nn(a, b, bdmask, passes):
    return _bd_mm(_dot, a, b, bdmask, passes)


def _bd_nt(a, b, bdmask, passes):
    return _bd_mm(_dot_nt, a, b, bdmask, passes)


def _bd_tn(a, b, bdmask_f, passes):
    full = _dot3(a.T, b) if passes == 3 else _dot(a.T.astype(BF16), b.astype(BF16))
    return jnp.sum((full * bdmask_f).reshape(HALF // CHUNK, CHUNK, HALF), axis=0)


P_SCORE = 1
P_TINV = 1
P_APPLY = 1
P_GH = 1
P_STATE = 1


def _halves(fn, *xs):
    outs = [fn(*[x[:, h * HALF:(h + 1) * HALF] for x in xs]) for h in range(RW // HALF)]
    return jnp.concatenate(outs, axis=1)


def _rwkv_kernel(ps_ref, w0_ref, a0_ref, kk_ref, ka_ref, rk_ref, lg_ref, lb_ref,
                 wup_ref, aup_ref, gup_ref, out_ref, s_ref, *, nc):
    @pl.when(pl.program_id(1) == 0)
    def _():
        s_ref[...] = jnp.zeros_like(s_ref)

    C = CHUNK
    r_i = lax.broadcasted_iota(jnp.int32, (HALF, HALF), 0)
    c_i = lax.broadcasted_iota(jnp.int32, (HALF, HALF), 1)
    bdsame = (r_i // C) == (c_i // C)
    bdmask = bdsame.astype(BF16)
    bdmask_f = bdsame.astype(F32)
    t_i = lax.broadcasted_iota(jnp.int32, (C, HALF), 0)
    i_i = lax.broadcasted_iota(jnp.int32, (C, HALF), 1) % C
    strict = i_i < t_i
    incl = i_i <= t_i
    eye_l = (i_i == t_i).astype(F32)
    n = nc * C
    tr = lax.broadcasted_iota(jnp.int32, (n, n), 0)
    tc = lax.broadcasted_iota(jnp.int32, (n, n), 1)
    tril = ((tc <= tr) & (tc // C == tr // C)).astype(BF16)

    def headsum(x, split=False):
        def one(xh):
            if not split:
                return _dot(xh.astype(BF16), bdmask)
            hi, lo = _split2(xh)
            return _dot(hi, bdmask) + _dot(lo, bdmask)
        return _halves(one, x)

    def cumsum_rows(x):
        hi = x.astype(BF16)
        r1 = x - hi.astype(F32)
        mid = r1.astype(BF16)
        lo = (r1 - mid.astype(F32)).astype(BF16)
        return _dot(tril, hi) + _dot(tril, mid) + _dot(tril, lo)

    p_r = ps_ref[:, 0:RW]
    p_k = ps_ref[:, RW:2 * RW]
    p_v = ps_ref[:, 2 * RW:3 * RW]
    p_w = ps_ref[:, 3 * RW:3 * RW + LANE]
    p_a = ps_ref[:, 3 * RW + LANE:3 * RW + 2 * LANE]
    p_g = ps_ref[:, 3 * RW + 2 * LANE:3 * RW + 3 * LANE]
    logw = -DECAY_SCALE * _sigmoid(w0_ref[...] + _dot3(jnp.tanh(p_w), wup_ref[...]))
    iclr = _sigmoid(a0_ref[...] + _dot(p_a.astype(BF16), aup_ref[...].astype(BF16)))
    gate = _dot(_sigmoid(p_g).astype(BF16), gup_ref[...].astype(BF16))
    kk = p_k * kk_ref[...]
    k2 = p_k * (1.0 + (iclr - 1.0) * ka_ref[...])
    sums = headsum(jnp.concatenate([kk * kk, p_r * k2 * rk_ref[...]], axis=0))
    kk = kk * lax.rsqrt(sums[:n] + 1e-12)
    bonus = sums[n:] * p_v
    b_v = kk * iclr
    cum = cumsum_rows(logw)
    cum_c = jnp.concatenate(
        [jnp.broadcast_to(cum[(c + 1) * C - 1:(c + 1) * C, :], (C, RW)) for c in range(nc)], axis=0)
    e_neg = jnp.exp(-cum)
    e_rem = jnp.exp(cum_c - cum)
    a_all = -kk * jnp.exp(cum - logw)
    b_all = b_v * e_neg
    k_all = k2 * e_neg
    r_all = p_r * jnp.exp(cum)
    bg_all = b_v * e_rem
    kg_all = k2 * e_rem
    gam_all = jnp.exp(cum_c)

    nh = RW // HALF
    units = [(c, h) for c in range(nc) for h in range(nh)]

    def cut(x, u):
        c, h = u
        return x[c * C:(c + 1) * C, h * HALF:(h + 1) * HALF]

    each = lambda fn, *lists: [fn(*xs) for xs in zip(*lists)]
    stack = lambda x, y: jnp.concatenate([x, y], axis=0)
    top = lambda x: x[:C]
    bot = lambda x: x[C:]
    nn = lambda p: (lambda a, b: _bd_nn(a, b, bdmask, p))

    a_t, b_t, k_t, r_t, b_g, k_g, v_u = ([cut(x, u) for u in units]
                                         for x in (a_all, b_all, k_all, r_all, bg_all, kg_all, p_v))
    gam = [gam_all[c * C:c * C + 1, h * HALF:(h + 1) * HALF] for c, h in units]
    ar = each(stack, a_t, r_t)
    sb = each(lambda x, y: _bd_nt(x, y, bdmask, P_SCORE), ar, b_t)
    sk = each(lambda x, y: _bd_nt(x, y, bdmask, P_SCORE), ar, k_t)
    l_ab = [jnp.where(strict, top(x), 0.0) for x in sb]
    q_b = [jnp.where(incl, bot(x), 0.0) for x in sb]
    m_ak = [jnp.where(strict, top(x), 0.0) for x in sk]
    q_k = [jnp.where(incl, bot(x), 0.0) for x in sk]
    t_m = [eye_l + x for x in l_ab]
    pw = each(nn(P_TINV), l_ab, l_ab)
    for _ in range(4):
        tp = each(nn(P_TINV), each(stack, t_m, pw), pw)
        t_m = each(lambda t, x: t + top(x), t_m, tp)
        pw = [bot(x) for x in tp]
    t_m = each(lambda t, x: t + x, t_m, each(nn(P_TINV), t_m, pw))
    mq = each(nn(P_APPLY), each(stack, m_ak, q_k), v_u)
    a_h = each(nn(P_APPLY), t_m, a_t)
    u0 = each(nn(P_APPLY), t_m, [top(x) for x in mq])
    r_h = each(lambda r, x: r + x, r_t, each(nn(P_APPLY), q_b, a_h))
    y0 = each(lambda m, x: bot(m) + x, mq, each(nn(P_APPLY), q_b, u0))
    g_l = each(lambda a, b, g: _bd_tn(a, b, bdmask_f, P_GH) + eye_l * g, a_h, b_g, gam)
    h_l = each(lambda u, v, b, k: _bd_tn(stack(u, v), stack(b, k), bdmask_f, P_GH), u0, v_u, b_g, k_g)

    s_all = s_ref[...]
    s_cur = [s_all[:, h * HALF:(h + 1) * HALF] for h in range(nh)]
    ys = []
    for c in range(nc):
        idx = [c * nh + h for h in range(nh)]
        yc = [_bd_nt(r_h[i], s, bdmask, P_STATE) + y0[i] for i, s in zip(idx, s_cur)]
        s_cur = [_bd_nn(s, g_l[i], bdmask, P_STATE) + h_l[i] for i, s in zip(idx, s_cur)]
        ys.append(jnp.concatenate(yc, axis=1))
    s_ref[...] = jnp.concatenate(s_cur, axis=1)
    y = jnp.concatenate(ys, axis=0)
    inv_n = 1.0 / RWKV_HEAD_DIM
    mu = headsum(y, split=True) * inv_n
    d = y - mu
    var = headsum(d * d) * inv_n
    yn = d * lax.rsqrt(var + LNX_EPS) * lg_ref[...] + lb_ref[...]
    out_ref[...] = ((yn + bonus) * gate).astype(BF16)


def _rwkv_call(ps, vecs, wup, aup, gup, batch, seq, nc=4):
    t = ps.shape[0]
    rows = nc * CHUNK
    steps = seq // rows
    vec_spec = pl.BlockSpec((1, RW), lambda b, j: (0, 0))
    lora_spec = pl.BlockSpec((LANE, RW), lambda b, j: (0, 0))
    return pl.pallas_call(
        functools.partial(_rwkv_kernel, nc=nc),
        grid=(batch, steps),
        in_specs=[pl.BlockSpec((rows, PS_W), lambda b, j: (b * steps + j, 0))]
                 + [vec_spec] * 7 + [lora_spec] * 3,
        out_specs=pl.BlockSpec((rows, RW), lambda b, j: (b * steps + j, 0)),
        out_shape=jax.ShapeDtypeStruct((t, RW), BF16),
        scratch_shapes=[pltpu.VMEM((CHUNK, RW), F32)],
        compiler_params=_params(("arbitrary", "arbitrary")),
        name="rwkv",
    )(ps, *vecs, wup, aup, gup)


def _mlaprep_kernel(mla_ref, pos_ref, invf_ref, qn_ref, kvn_ref, wq_ref, wqs_ref, wk_ref, wv_ref,
                    q_ref, k_ref, v_ref):
    def rms(x, g):
        return (x * lax.rsqrt(jnp.mean(x * x, axis=-1, keepdims=True) + NORM_EPS) * g).astype(BF16)

    ang = pos_ref[...].astype(F32) * invf_ref[...]
    cos = jnp.cos(ang)
    sin = jnp.sin(ang)
    lane = lax.broadcasted_iota(jnp.int32, cos.shape, 1)
    qs = (QK_NOPE + QK_ROPE) ** -0.5 * LOG2E
    cos_q = jnp.tile(jnp.where(lane < QK_NOPE, qs, cos * qs), (1, MLA_HEADS))
    sin_q = jnp.tile(sin * qs, (1, MLA_HEADS))
    qn = rms(mla_ref[:, 0:Q_LORA], qn_ref[...])
    q_ref[...] = (_dot(qn, wq_ref[...]) * cos_q + _dot(qn, wqs_ref[...]) * sin_q).astype(BF16)
    kvn = rms(mla_ref[:, Q_LORA:Q_LORA + KV_LORA], kvn_ref[...])
    o = Q_LORA + KV_LORA
    k_pe = mla_ref[:, o:o + LANE] * cos + mla_ref[:, o + LANE:o + 2 * LANE] * sin
    k_ref[...] = (_dot(kvn, wk_ref[...]) + jnp.tile(k_pe, (1, MLA_HEADS))).astype(BF16)
    v_ref[...] = _dot(kvn, wv_ref[...]).astype(BF16)


def _mlaprep_call(mla, pos, invf, qn, kvn, wq, wqs, wk, wv, tm=512):
    t = mla.shape[0]
    hw = MLA_HEADS * HEAD_PAD
    vw = MLA_HEADS * V_HEAD
    full = lambda shape: pl.BlockSpec(shape, lambda i: (0, 0))
    row = lambda w: pl.BlockSpec((tm, w), lambda i: (i, 0))
    return pl.pallas_call(
        _mlaprep_kernel,
        grid=(t // tm,),
        in_specs=[row(MLA_W), row(1), full((1, LANE)), full((1, Q_LORA)), full((1, KV_LORA)),
                  full((Q_LORA, hw)), full((Q_LORA, hw)), full((KV_LORA, hw)), full((KV_LORA, vw))],
        out_specs=[row(hw), row(hw), row(vw)],
        out_shape=[jax.ShapeDtypeStruct((t, w), BF16) for w in (hw, hw, vw)],
        compiler_params=_params(("arbitrary",)),
        name="mlaprep",
    )(mla, pos, invf, qn, kvn, wq, wqs, wk, wv)


def _attn_kernel(q_ref, k_ref, v_ref, o_ref, *, seq, tq):
    r_i = lax.broadcasted_iota(jnp.int32, (tq, tq), 0) // CHUNK
    c_i = lax.broadcasted_iota(jnp.int32, (tq, tq), 1) // CHUNK
    mask = c_i <= r_i
    def scores(item):
        qi, hh = item
        lo = qi * tq
        qk = slice(hh * HEAD_PAD, (hh + 1) * HEAD_PAD)
        q = q_ref[lo:lo + tq, qk]
        sd = jnp.where(mask, _dot_nt(q, k_ref[lo:lo + tq, qk]), -1e30)
        so = _dot_nt(q, k_ref[0:lo, qk]) if qi > 0 else None
        return sd, so

    items = [(qi, hh) for qi in range(seq // tq) for hh in range(ATTN_HEADS_PER_STEP)]
    nxt = scores(items[0])
    for n, (qi, hh) in enumerate(items):
        lo = qi * tq
        vo = slice(hh * V_HEAD, (hh + 1) * V_HEAD)
        sd, so = nxt
        if n + 1 < len(items):
            nxt = scores(items[n + 1])
        m = jnp.max(sd, axis=-1, keepdims=True)
        if qi > 0:
            m = jnp.maximum(m, jnp.max(so, axis=-1, keepdims=True))
        pd = jnp.exp2(sd - m)
        l = jnp.sum(pd, axis=-1, keepdims=True)
        acc = _dot(pd.astype(BF16), v_ref[lo:lo + tq, vo])
        if qi > 0:
            po = jnp.exp2(so - m)
            l = l + jnp.sum(po, axis=-1, keepdims=True)
            acc = acc + _dot(po.astype(BF16), v_ref[0:lo, vo])
        o_ref[lo:lo + tq, vo] = (acc / l).astype(BF16)


ATTN_HEADS_PER_STEP = LANE // V_HEAD


def _attn_call(q, k, v, batch, seq, tq=256):
    hp = ATTN_HEADS_PER_STEP
    qk_spec = pl.BlockSpec((seq, hp * HEAD_PAD), lambda b, h: (b, h))
    vo_spec = pl.BlockSpec((seq, hp * V_HEAD), lambda b, h: (b, h))
    return pl.pallas_call(
        functools.partial(_attn_kernel, seq=seq, tq=tq),
        grid=(batch, MLA_HEADS // hp),
        in_specs=[qk_spec, qk_spec, vo_spec],
        out_specs=vo_spec,
        out_shape=jax.ShapeDtypeStruct(v.shape, BF16),
        compiler_params=_params(("arbitrary", "arbitrary")),
        name="attn",
    )(q, k, v)


MIX_SUB = 256


def _mixout_kernel(*refs, router):
    if router:
        (rw_ref, o_ref, gates_ref, x_ref, gate_ref, wr_ref, wm_ref, wo_ref,
         g2_ref, shift2_ref, scale2_ref, rwt_ref, rb_ref,
         xo_ref, h2_ref, ti_ref, tp_ref) = refs
    else:
        (rw_ref, o_ref, gates_ref, x_ref, gate_ref, wr_ref, wm_ref, wo_ref,
         g2_ref, shift2_ref, scale2_ref, xo_ref, h2_ref) = refs
    n = x_ref.shape[0]
    subs = [slice(s, s + MIX_SUB) for s in range(0, n, MIX_SUB)]
    yab = [(_dot(rw_ref[r, :], wr_ref[...]), _dot(o_ref[r, :], wm_ref[...])) for r in subs]
    merged = [gates_ref[r, 0:D_MODEL].astype(F32) * ya + gates_ref[r, D_MODEL:].astype(F32) * yb
              for r, (ya, yb) in zip(subs, yab)]
    mixed = [_dot(m.astype(BF16), wo_ref[...]) for m in merged]
    h2s = []
    for r, mx in zip(subs, mixed):
        xn = x_ref[r, :] + gate_ref[...] * mx
        xo_ref[r, :] = xn
        h2s.append(_norm_mod(xn, g2_ref[...], scale2_ref[...], shift2_ref[...]))
        h2_ref[r, :] = h2s[-1].astype(h2_ref.dtype)
    if router:
        h2 = jnp.concatenate(h2s, axis=0)
        hh, hl = _split2(h2)
        wh, wl = _split2(rwt_ref[...])
        logits = _dot_nt(wh, hh) + _dot_nt(wl, hh) + _dot_nt(wh, hl) + rb_ref[...]
        e_i = lax.broadcasted_iota(jnp.int32, logits.shape, 0)
        m1 = jnp.max(logits, axis=0, keepdims=True)
        i1 = jnp.min(jnp.where(logits == m1, e_i, N_EXPERTS), axis=0, keepdims=True)
        rest = jnp.where(e_i == i1, -jnp.inf, logits)
        m2 = jnp.max(rest, axis=0, keepdims=True)
        i2 = jnp.min(jnp.where(rest == m2, e_i, N_EXPERTS), axis=0, keepdims=True)
        e2 = jnp.exp(m2 - m1)
        den = 1.0 + e2
        ti_ref[0:1, :] = i1
        ti_ref[1:2, :] = i2
        tp_ref[0:1, :] = 1.0 / den
        tp_ref[1:2, :] = e2 / den


def _mixout_call(rw, o, gates, x2, gate, wr, wm, wo, g2, shift2, scale2, seq, router=None, tm=512):
    t = x2.shape[0]
    tiles_per_seq = seq // tm
    bmap = lambda i: (i // tiles_per_seq, 0, 0)
    full = lambda shape: pl.BlockSpec(shape, lambda i: (0, 0))
    row = lambda w: pl.BlockSpec((tm, w), lambda i: (i, 0))
    mod = pl.BlockSpec((None, 1, D_MODEL), bmap)
    in_specs = [row(RW), row(MLA_HEADS * V_HEAD), row(GATE_W), row(D_MODEL), mod,
                full(wr.shape), full(wm.shape), full(wo.shape), full((1, D_MODEL)), mod, mod]
    args = [rw, o, gates, x2, gate, wr, wm, wo, g2, shift2, scale2]
    out_specs = [row(D_MODEL), row(D_MODEL)]
    out_shape = [jax.ShapeDtypeStruct((t, D_MODEL), F32),
                 jax.ShapeDtypeStruct((t, D_MODEL), BF16 if router is None else F32)]
    if router is not None:
        in_specs += [full((N_EXPERTS, D_MODEL)), full((N_EXPERTS, 1))]
        args += list(router)
        out_specs += [pl.BlockSpec((2, tm), lambda i: (0, i))] * 2
        out_shape += [jax.ShapeDtypeStruct((2, t), jnp.int32), jax.ShapeDtypeStruct((2, t), F32)]
    return pl.pallas_call(
        functools.partial(_mixout_kernel, router=router is not None),
        grid=(t // tm,),
        in_specs=in_specs,
        out_specs=out_specs,
        out_shape=out_shape,
        compiler_params=_params(("arbitrary",)),
        name="mixout",
    )(*args)


def _swiglu_step(h, w1_ref, w3_ref, w2_ref, acc_ref, finish):
    f = pl.program_id(1)

    @pl.when(f == 0)
    def _():
        acc_ref[...] = jnp.zeros_like(acc_ref)

    a1 = _dot(h, w1_ref[...])
    a3 = _dot(h, w3_ref[...])
    act = (a1 * _sigmoid(a1)) * a3
    acc_ref[...] += _dot(act.astype(BF16), w2_ref[...])

    @pl.when(f == pl.num_programs(1) - 1)
    def _():
        finish(acc_ref[...])


def _rms(x, g):
    return x * lax.rsqrt(jnp.mean(x * x, axis=-1, keepdims=True) + NORM_EPS) * g


def _ffn_kernel(h_ref, x_ref, gate_ref, gf_ref, w1_ref, w3_ref, w2_ref, o_ref, *, final, tf):
    h = h_ref[...]
    ff = w1_ref.shape[1]
    ups = [(_dot(h, w1_ref[:, s:s + tf]), _dot(h, w3_ref[:, s:s + tf])) for s in range(0, ff, tf)]
    acc = None
    for s, (a1, a3) in zip(range(0, ff, tf), ups):
        act = ((a1 * _sigmoid(a1)) * a3).astype(BF16)
        part = _dot(act, w2_ref[s:s + tf, :])
        acc = part if acc is None else acc + part
    xn = x_ref[...] + gate_ref[...] * acc
    o_ref[...] = _rms(xn, gf_ref[...]) if final else xn


def _ffn_call(h, x2, gate, gf, w1, w3, w2, seq, final, tm=512, tf=1408):
    r = h.shape[0]
    tiles_per_seq = seq // tm
    row = pl.BlockSpec((tm, D_MODEL), lambda i: (i, 0))
    resident = lambda a: pl.BlockSpec(a.shape, lambda i: (0, 0), pipeline_mode=pl.Buffered(1))
    return pl.pallas_call(
        functools.partial(_ffn_kernel, final=final, tf=tf),
        grid=(r // tm,),
        in_specs=[
            row, row,
            pl.BlockSpec((None, 1, D_MODEL), lambda i: (i // tiles_per_seq, 0, 0)),
            pl.BlockSpec((1, D_MODEL), lambda i: (0, 0)),
            resident(w1), resident(w3), resident(w2),
        ],
        out_specs=row,
        out_shape=jax.ShapeDtypeStruct((r, D_MODEL), F32),
        compiler_params=_params(("arbitrary",)),
        name="ffn",
    )(h, x2, gate, gf, w1, w3, w2)


MOE_TM = 512


def _route(top_i, tm):
    t = top_i.shape[1]
    e_flat = top_i.reshape(-1)
    oh = (e_flat[:, None] == jnp.arange(N_EXPERTS, dtype=jnp.int32)[None, :]).astype(jnp.int32)
    csum = jnp.cumsum(oh, axis=0)
    rank = jnp.sum(csum * oh, axis=1) - 1
    counts = csum[-1]
    padded = ((counts + tm - 1) // tm) * tm
    ends = jnp.cumsum(padded)
    pos = jnp.sum(oh * (ends - padded)[None, :], axis=1) + rank
    n_tiles = (2 * t) // tm + N_EXPERTS
    starts = jnp.arange(n_tiles, dtype=jnp.int32) * tm
    tile_e = jnp.sum((starts[:, None] >= ends[None, :]).astype(jnp.int32), axis=1)
    return (pos.astype(jnp.int32), ends.astype(jnp.int32), jnp.minimum(tile_e, N_EXPERTS - 1),
            (ends[-1:] // tm).astype(jnp.int32))


def _dispatch_kernel(pos_ref, ends_ref, h_ref, xs_ref, zero_ref, sem, *, t, td, tm):
    i = pl.program_id(0)
    base = i * td

    @pl.when(i == 0)
    def _():
        zero_ref[...] = jnp.zeros_like(zero_ref)
        n_tiles = xs_ref.shape[0] // tm
        fills = []
        for e in range(N_EXPERTS):
            start = ends_ref[e - 1] if e else 0
            fills.append((ends_ref[e] > start, pl.multiple_of(ends_ref[e] - tm, tm)))
            fills.append((ends_ref[N_EXPERTS - 1] // tm + e < n_tiles,
                          pl.multiple_of(ends_ref[N_EXPERTS - 1] + e * tm, tm)))
        for go, row0 in fills:
            @pl.when(go)
            def _():
                pltpu.make_async_copy(zero_ref, xs_ref.at[pl.ds(row0, tm)], sem).start()
        for go, row0 in fills:
            @pl.when(go)
            def _():
                pltpu.make_async_copy(zero_ref, xs_ref.at[pl.ds(row0, tm)], sem).wait()

    for r in range(td):
        for k in range(2):
            dst = pos_ref[k * t + base + r]
            pltpu.make_async_copy(h_ref.at[pl.ds(r, 1)], xs_ref.at[pl.ds(dst, 1)], sem).start(priority=k)
    for k in range(2):
        pltpu.make_async_copy(h_ref, xs_ref.at[pl.ds(0, td)], sem).wait()


def _dispatch_call(pos, ends, h, n_rows, tm, td=512):
    t = h.shape[0]
    return pl.pallas_call(
        functools.partial(_dispatch_kernel, t=t, td=td, tm=tm),
        grid_spec=pltpu.PrefetchScalarGridSpec(
            num_scalar_prefetch=2, grid=(t // td,),
            in_specs=[pl.BlockSpec((td, D_MODEL), lambda i, pos, ends: (i, 0))],
            out_specs=pl.BlockSpec(memory_space=pl.ANY),
            scratch_shapes=[pltpu.VMEM((tm, D_MODEL), h.dtype), pltpu.SemaphoreType.DMA(())]),
        out_shape=jax.ShapeDtypeStruct((n_rows, D_MODEL), h.dtype),
        compiler_params=_params(("arbitrary",)),
        name="dispatch",
    )(pos, ends, h)


def _ffn_routed_kernel(te_ref, nu_ref, x_ref, w1_ref, w3_ref, w2_ref, y_ref, acc_ref):
    del te_ref
    used = pl.program_id(0) < nu_ref[0]

    def finish(acc):
        y_ref[...] = acc

    @pl.when(used)
    def _():
        _swiglu_step(x_ref[...].astype(BF16), w1_ref, w3_ref, w2_ref, acc_ref, finish)

    @pl.when(jnp.logical_not(used))
    def _():
        y_ref[...] = jnp.zeros_like(y_ref)


def _ffn_routed_call(tile_e, n_used, xs, w1, w3, w2, tm=MOE_TM, tf=1792):
    r = xs.shape[0]
    ff = w1.shape[2]
    return pl.pallas_call(
        _ffn_routed_kernel,
        grid_spec=pltpu.PrefetchScalarGridSpec(
            num_scalar_prefetch=2, grid=(r // tm, ff // tf),
            in_specs=[
                pl.BlockSpec((tm, D_MODEL), lambda i, f, te, nu: (i, 0)),
                pl.BlockSpec((None, D_MODEL, tf), lambda i, f, te, nu: (te[i], 0, f)),
                pl.BlockSpec((None, D_MODEL, tf), lambda i, f, te, nu: (te[i], 0, f)),
                pl.BlockSpec((None, tf, D_MODEL), lambda i, f, te, nu: (te[i], f, 0)),
            ],
            out_specs=pl.BlockSpec((tm, D_MODEL), lambda i, f, te, nu: (i, 0)),
            scratch_shapes=[pltpu.VMEM((tm, D_MODEL), F32)]),
        out_shape=jax.ShapeDtypeStruct((r, D_MODEL), F32),
        compiler_params=_params(("arbitrary", "arbitrary")),
        name="ffn_routed",
    )(tile_e, n_used, xs, w1, w3, w2)


def _combine_kernel(pos_ref, x_ref, p_ref, gate_ref, gf_ref, y_ref, o_ref, buf, sem, *, t, tc, final):
    i = pl.program_id(0)

    def row_copy(step, slot, r, k):
        src = pos_ref[k * t + step * tc + r]
        return pltpu.make_async_copy(y_ref.at[pl.ds(src, 1)], buf.at[slot, k, pl.ds(r, 1)], sem.at[slot])

    def issue(step, slot):
        for r in range(tc):
            for k in range(2):
                row_copy(step, slot, r, k).start(priority=k)

    @pl.when(i == 0)
    def _():
        issue(0, 0)

    @pl.when(i + 1 < pl.num_programs(0))
    def _():
        issue(i + 1, (i + 1) % 2)

    slot = i % 2
    for k in range(2):
        pltpu.make_async_copy(y_ref.at[pl.ds(0, tc)], buf.at[slot, k], sem.at[slot]).wait()
    p = p_ref[...]
    xn = x_ref[...] + gate_ref[...] * (p[:, 0:1] * buf[slot, 0] + p[:, 1:2] * buf[slot, 1])
    o_ref[...] = _rms(xn, gf_ref[...]) if final else xn


def _combine_call(pos, x2, probs, gate, gf, y, seq, final, tc=256):
    t = x2.shape[0]
    tiles_per_seq = seq // tc
    row = pl.BlockSpec((tc, D_MODEL), lambda i, pos: (i, 0))
    return pl.pallas_call(
        functools.partial(_combine_kernel, t=t, tc=tc, final=final),
        grid_spec=pltpu.PrefetchScalarGridSpec(
            num_scalar_prefetch=1, grid=(t // tc,),
            in_specs=[row, pl.BlockSpec((tc, 2), lambda i, pos: (i, 0)),
                      pl.BlockSpec((None, 1, D_MODEL), lambda i, pos: (i // tiles_per_seq, 0, 0)),
                      pl.BlockSpec((1, D_MODEL), lambda i, pos: (0, 0)),
                      pl.BlockSpec(memory_space=pl.ANY)],
            out_specs=row,
            scratch_shapes=[pltpu.VMEM((2, 2, tc, D_MODEL), F32), pltpu.SemaphoreType.DMA((2,))]),
        out_shape=jax.ShapeDtypeStruct((t, D_MODEL), F32),
        compiler_params=_params(("arbitrary",)),
        name="combine",
    )(pos, x2, probs, gate, gf, y)


def _layout_w_in_kernel(w_ref, o_ref):
    half = QK_ROPE // 2
    o_ref[...] = jnp.zeros_like(o_ref)

    def put(dst, src, width, sign=1.0):
        o_ref[:, dst:dst + width] = (sign * w_ref[:, src:src + width]).astype(BF16)

    o = 3 * RW
    put(0, 0, o + DECAY_LORA)
    put(o + LANE, o + DECAY_LORA, ICLR_LORA)
    put(o + 2 * LANE, o + DECAY_LORA + ICLR_LORA, GATE_LORA)
    src = o + DECAY_LORA + ICLR_LORA + GATE_LORA
    put(PS_W, src, Q_LORA + KV_LORA)
    kr = src + Q_LORA + KV_LORA
    dst = PS_W + Q_LORA + KV_LORA
    put(dst + QK_NOPE, kr, QK_ROPE)
    put(dst + LANE + QK_NOPE, kr + half, half, -1.0)
    put(dst + LANE + QK_NOPE + half, kr, half)
    put(PS_W + MLA_W, kr + QK_ROPE, GATE_W)


def _layout_w_in(w, layer):
    _, k, n = w.shape
    return pl.pallas_call(
        _layout_w_in_kernel,
        grid=(k // 256,),
        in_specs=[pl.BlockSpec((None, 256, n), lambda i: (layer, i, 0))],
        out_specs=pl.BlockSpec((256, PS_W + MLA_W + GATE_W), lambda i: (i, 0)),
        out_shape=jax.ShapeDtypeStruct((k, PS_W + MLA_W + GATE_W), BF16),
        compiler_params=_params(("arbitrary",)),
        name="layout_w_in",
    )(w)


def _layout_mu(mu):
    z = jnp.zeros((64,), mu.dtype)
    o = 3 * RW
    return jnp.concatenate([mu[:o + 64], z, mu[o + 64:o + 128], z, mu[o + 128:]])[None, :]


def _layout_wq(w_qb):
    hd = QK_NOPE + QK_ROPE
    w = w_qb.reshape(Q_LORA, MLA_HEADS, hd)
    half = QK_ROPE // 2
    x1, x2 = w[:, :, QK_NOPE:QK_NOPE + half], w[:, :, QK_NOPE + half:]
    zn = jnp.zeros((Q_LORA, MLA_HEADS, QK_NOPE), w.dtype)
    zp = jnp.zeros((Q_LORA, MLA_HEADS, HEAD_PAD - hd), w.dtype)
    wq = jnp.concatenate([w, zp], axis=2).reshape(Q_LORA, -1)
    wqs = jnp.concatenate([zn, -x2, x1, zp], axis=2).reshape(Q_LORA, -1)
    return wq.astype(BF16), wqs.astype(BF16)


def _layout_wkv(w_kvb):
    w = w_kvb.reshape(KV_LORA, MLA_HEADS, QK_NOPE + V_HEAD)
    z = jnp.zeros((KV_LORA, MLA_HEADS, HEAD_PAD - QK_NOPE), w.dtype)
    wk = jnp.concatenate([w[:, :, :QK_NOPE], z], axis=2).reshape(KV_LORA, -1)
    wv = w[:, :, QK_NOPE:].reshape(KV_LORA, -1)
    return wk.astype(BF16), wv.astype(BF16)


def _rope_freq_row():
    inv_freq = ROPE_THETA ** (-jnp.arange(0, QK_ROPE, 2, dtype=F32) / QK_ROPE)
    z = jnp.zeros((QK_NOPE,), F32)
    zp = jnp.zeros((LANE - QK_NOPE - QK_ROPE,), F32)
    return jnp.concatenate([z, inv_freq, inv_freq, zp])[None, :]


def kernel(x, c, positions, ada_w, ada_b, norm_mix, norm_ffn, norm_final, w_in, tshift_mu, w0, w_up, a0, a_up, g_up, k_k, k_a, r_k, lnx_g, lnx_b, rwkv_out, q_norm, w_qb, kv_norm, w_kvb, mla_out, w_o, ffn_w1, ffn_w3, ffn_w2, router_w, router_b, moe_w1, moe_w3, moe_w2):
    batch, seq, _ = x.shape
    t = batch * seq
    mods = _ada_call(c, ada_w.reshape(2 * DEPTH, D_MODEL, 3 * D_MODEL),
                     ada_b.reshape(2 * DEPTH, 1, 3 * D_MODEL))
    mods = mods.reshape(DEPTH, 2, batch, 3, 1, D_MODEL)
    pos_col = positions.reshape(t, 1)
    invf = _rope_freq_row()
    x2 = x.reshape(t, D_MODEL)
    row = lambda v: v.reshape(1, -1)

    for l in range(DEPTH):
        shift, scale, gate = (mods[l, 0, :, j] for j in range(3))
        ps, mla, gates = _inproj_call(x2, shift, scale, row(norm_mix[l]), _layout_w_in(w_in, l),
                                      _layout_mu(tshift_mu[l]), seq)
        vecs = [row(v[l]) for v in (w0, a0, k_k, k_a, r_k, lnx_g, lnx_b)]
        lora_pad = lambda w: jnp.pad(w, ((0, LANE - w.shape[0]), (0, 0)))
        rw = _rwkv_call(ps, vecs, lora_pad(w_up[l]), lora_pad(a_up[l]), g_up[l], batch, seq)
        wq, wqs = _layout_wq(w_qb[l])
        wk, wv = _layout_wkv(w_kvb[l])
        q, k, v = _mlaprep_call(mla, pos_col, invf, row(q_norm[l]), row(kv_norm[l]), wq, wqs, wk, wv)
        o = _attn_call(q, k, v, batch, seq)

        shift2, scale2, gate2 = (mods[l, 1, :, j] for j in range(3))
        moe = l % 2 == 1
        router = (router_w[l // 2].T, router_b[l // 2].reshape(N_EXPERTS, 1)) if moe else None
        outs = _mixout_call(rw, o, gates, x2, gate, rwkv_out[l].astype(BF16), mla_out[l].astype(BF16),
                            w_o[l].astype(BF16), row(norm_ffn[l]), shift2, scale2, seq, router=router)
        final = l == DEPTH - 1
        if moe:
            x2, h2, top_i, top_p = outs
            pos, ends, tile_e, n_used = _route(top_i, MOE_TM)
            xs = _dispatch_call(pos, ends, h2, 2 * t + N_EXPERTS * MOE_TM, MOE_TM)
            y = _ffn_routed_call(tile_e, n_used, xs, moe_w1[l // 2].astype(BF16),
                                 moe_w3[l // 2].astype(BF16), moe_w2[l // 2].astype(BF16))
            x2 = _combine_call(pos, x2, top_p.T, gate2, row(norm_final), y, seq, final)
        else:
            x2, h2 = outs
            x2 = _ffn_call(h2, x2, gate2, row(norm_final), ffn_w1[l // 2].astype(BF16),
                           ffn_w3[l // 2].astype(BF16), ffn_w2[l // 2].astype(BF16), seq, final)
    return x2.reshape(batch, seq, D_MODEL)
```

```python
import functools
import math

import jax
import jax.numpy as jnp
from jax import lax
from jax.experimental import pallas as pl
from jax.experimental.pallas import tpu as pltpu

F32 = jnp.float32
BF16 = jnp.bfloat16

D_MODEL = 1024
DEPTH = 2
CHUNK = 64
NORM_EPS = 1e-6
RWKV_HEADS = 8
RWKV_HEAD_DIM = 64
RW = RWKV_HEADS * RWKV_HEAD_DIM
DECAY_LORA = 64
ICLR_LORA = 64
GATE_LORA = 128
DECAY_SCALE = math.exp(-0.5)
LNX_EPS = 64e-5
MLA_HEADS = 8
Q_LORA = 384
KV_LORA = 256
QK_NOPE = 64
QK_ROPE = 32
V_HEAD = 64
ROPE_THETA = 10000.0
LOG2E = math.log2(math.e)
N_EXPERTS = 8

LANE = 128
HEAD_PAD = 128
PS_W = 3 * RW + 3 * LANE
MLA_W = Q_LORA + KV_LORA + 2 * LANE
GATE_W = 2 * D_MODEL
HALF = 256
VMEM_LIMIT = 56 * 1024 * 1024


def _sigmoid(x):
    return 1.0 / (1.0 + jnp.exp(-x))


def _dot(a, b):
    return jnp.dot(a, b, preferred_element_type=F32)


def _dot_nt(a, b):
    return lax.dot_general(a, b, (((1,), (1,)), ((), ())), preferred_element_type=F32)


def _split2(x):
    hi = x.astype(BF16)
    lo = (x - hi.astype(F32)).astype(BF16)
    return hi, lo


def _dot3(a, b):
    ah, al = _split2(a)
    bh, bl = _split2(b)
    return _dot(ah, bh) + _dot(al, bh) + _dot(ah, bl)


def _params(sem):
    return pltpu.CompilerParams(dimension_semantics=sem, vmem_limit_bytes=VMEM_LIMIT)


def _ada_kernel(c_ref, w_ref, b_ref, o_ref):
    c = c_ref[...]
    o_ref[...] = _dot((c * _sigmoid(c)).astype(BF16), w_ref[...].astype(BF16)) + b_ref[...]


def _ada_call(c, ada_w, ada_b):
    n = ada_w.shape[0]
    nb = c.shape[0]
    return pl.pallas_call(
        _ada_kernel,
        grid=(n, 3),
        in_specs=[
            pl.BlockSpec((nb, D_MODEL), lambda i, j: (0, 0)),
            pl.BlockSpec((None, D_MODEL, D_MODEL), lambda i, j: (i, 0, j)),
            pl.BlockSpec((None, 1, D_MODEL), lambda i, j: (i, 0, j)),
        ],
        out_specs=pl.BlockSpec((None, nb, D_MODEL), lambda i, j: (i, 0, j)),
        out_shape=jax.ShapeDtypeStruct((n, nb, 3 * D_MODEL), F32),
        compiler_params=_params(("arbitrary", "arbitrary")),
        name="ada",
    )(c, ada_w, ada_b)


def _norm_mod(x, g, scale, shift):
    var = jnp.mean(x * x, axis=-1, keepdims=True)
    return (x * lax.rsqrt(var + NORM_EPS) * g) * (1.0 + scale) + shift


def _inproj_kernel(x_ref, shift_ref, scale_ref, g_ref, w_ref, mu_ref,
                   ps_ref, mla_ref, gates_ref, carry_ref, *, tiles_per_seq, tm):
    i = pl.program_id(0)

    @pl.when(i % tiles_per_seq == 0)
    def _():
        carry_ref[...] = jnp.zeros_like(carry_ref)

    carry = carry_ref[0:1, :]
    row = lax.broadcasted_iota(jnp.int32, (INPROJ_SUB, PS_W), 0)
    for s in range(0, tm, INPROJ_SUB):
        r = slice(s, s + INPROJ_SUB)
        h = _norm_mod(x_ref[r, :], g_ref[...], scale_ref[...], shift_ref[...]).astype(BF16)
        p = _dot(h, w_ref[:, 0:PS_W])
        prev = jnp.where(row == 0, carry, pltpu.roll(p, 1, axis=0))
        carry = p[INPROJ_SUB - 1:INPROJ_SUB, :]
        ps_ref[r, :] = p + (prev - p) * mu_ref[...]
        mla_ref[r, :] = _dot(h, w_ref[:, PS_W:PS_W + MLA_W])
        gates_ref[r, :] = _sigmoid(_dot(h, w_ref[:, PS_W + MLA_W:])).astype(BF16)
    carry_ref[0:1, :] = carry


INPROJ_SUB = 256


def _inproj_call(x2, shift, scale, g, w_pad, mu_pad, seq, tm=512):
    t = x2.shape[0]
    tiles_per_seq = seq // tm
    bmap = lambda i: (i // tiles_per_seq, 0, 0)
    wtot = PS_W + MLA_W + GATE_W
    return pl.pallas_call(
        functools.partial(_inproj_kernel, tiles_per_seq=tiles_per_seq, tm=tm),
        grid=(t // tm,),
        in_specs=[
            pl.BlockSpec((tm, D_MODEL), lambda i: (i, 0)),
            pl.BlockSpec((None, 1, D_MODEL), bmap),
            pl.BlockSpec((None, 1, D_MODEL), bmap),
            pl.BlockSpec((1, D_MODEL), lambda i: (0, 0)),
            pl.BlockSpec((D_MODEL, wtot), lambda i: (0, 0)),
            pl.BlockSpec((1, PS_W), lambda i: (0, 0)),
        ],
        out_specs=[
            pl.BlockSpec((tm, PS_W), lambda i: (i, 0)),
            pl.BlockSpec((tm, MLA_W), lambda i: (i, 0)),
            pl.BlockSpec((tm, GATE_W), lambda i: (i, 0)),
        ],
        out_shape=[
            jax.ShapeDtypeStruct((t, PS_W), F32),
            jax.ShapeDtypeStruct((t, MLA_W), F32),
            jax.ShapeDtypeStruct((t, GATE_W), BF16),
        ],
        scratch_shapes=[pltpu.VMEM((8, PS_W), F32)],
        compiler_params=_params(("arbitrary",)),
        name="inproj",
    )(x2, shift, scale, g, w_pad, mu_pad)


def _blockdiag(xb, bdmask):
    return jnp.tile(xb, (HALF // CHUNK, 1)) * bdmask


def _bd_mm(dot, a, b, bdmask, passes):
    if passes == 1:
        return dot(a.astype(BF16), _blockdiag(b.astype(BF16), bdmask))
    ah, al = _split2(a)
    bh, bl = _split2(b)
    r = dot(jnp.concatenate([ah, al], axis=0), _blockdiag(bh, bdmask))
    n = a.shape[0]
    return r[:n] + r[n:] + dot(ah, _blockdiag(bl, bdmask))


def _bd_nn(a, b, bdmask, passes):
    return _bd_mm(_dot, a, b, bdmask, passes)


def _bd_nt(a, b, bdmask, passes):
    return _bd_mm(_dot_nt, a, b, bdmask, passes)


def _bd_tn(a, b, bdmask_f, passes):
    full = _dot3(a.T, b) if passes == 3 else _dot(a.T.astype(BF16), b.astype(BF16))
    return jnp.sum((full * bdmask_f).reshape(HALF // CHUNK, CHUNK, HALF), axis=0)


P_SCORE = 1
P_TINV = 1
P_APPLY = 1
P_GH = 1
P_STATE = 1


def _halves(fn, *xs):
    outs = [fn(*[x[:, h * HALF:(h + 1) * HALF] for x in xs]) for h in range(RW // HALF)]
    return jnp.concatenate(outs, axis=1)


def _rwkv_kernel(ps_ref, w0_ref, a0_ref, kk_ref, ka_ref, rk_ref, lg_ref, lb_ref,
                 wup_ref, aup_ref, gup_ref, out_ref, s_ref, *, nc):
    @pl.when(pl.program_id(1) == 0)
    def _():
        s_ref[...] = jnp.zeros_like(s_ref)

    C = CHUNK
    r_i = lax.broadcasted_iota(jnp.int32, (HALF, HALF), 0)
    c_i = lax.broadcasted_iota(jnp.int32, (HALF, HALF), 1)
    bdsame = (r_i // C) == (c_i // C)
    bdmask = bdsame.astype(BF16)
    bdmask_f = bdsame.astype(F32)
    t_i = lax.broadcasted_iota(jnp.int32, (C, HALF), 0)
    i_i = lax.broadcasted_iota(jnp.int32, (C, HALF), 1) % C
    strict = i_i < t_i
    incl = i_i <= t_i
    eye_l = (i_i == t_i).astype(F32)
    n = nc * C
    tr = lax.broadcasted_iota(jnp.int32, (n, n), 0)
    tc = lax.broadcasted_iota(jnp.int32, (n, n), 1)
    tril = ((tc <= tr) & (tc // C == tr // C)).astype(BF16)

    def headsum(x, split=False):
        def one(xh):
            if not split:
                return _dot(xh.astype(BF16), bdmask)
            hi, lo = _split2(xh)
            return _dot(hi, bdmask) + _dot(lo, bdmask)
        return _halves(one, x)

    def cumsum_rows(x):
        hi = x.astype(BF16)
        r1 = x - hi.astype(F32)
        mid = r1.astype(BF16)
        lo = (r1 - mid.astype(F32)).astype(BF16)
        return _dot(tril, hi) + _dot(tril, mid) + _dot(tril, lo)

    p_r = ps_ref[:, 0:RW]
    p_k = ps_ref[:, RW:2 * RW]
    p_v = ps_ref[:, 2 * RW:3 * RW]
    p_w = ps_ref[:, 3 * RW:3 * RW + LANE]
    p_a = ps_ref[:, 3 * RW + LANE:3 * RW + 2 * LANE]
    p_g = ps_ref[:, 3 * RW + 2 * LANE:3 * RW + 3 * LANE]
    logw = -DECAY_SCALE * _sigmoid(w0_ref[...] + _dot3(jnp.tanh(p_w), wup_ref[...]))
    iclr = _sigmoid(a0_ref[...] + _dot(p_a.astype(BF16), aup_ref[...].astype(BF16)))
    gate = _dot(_sigmoid(p_g).astype(BF16), gup_ref[...].astype(BF16))
    kk = p_k * kk_ref[...]
    k2 = p_k * (1.0 + (iclr - 1.0) * ka_ref[...])
    sums = headsum(jnp.concatenate([kk * kk, p_r * k2 * rk_ref[...]], axis=0))
    kk = kk * lax.rsqrt(sums[:n] + 1e-12)
    bonus = sums[n:] * p_v
    b_v = kk * iclr
    cum = cumsum_rows(logw)
    cum_c = jnp.concatenate(
        [jnp.broadcast_to(cum[(c + 1) * C - 1:(c + 1) * C, :], (C, RW)) for c in range(nc)], axis=0)
    e_neg = jnp.exp(-cum)
    e_rem = jnp.exp(cum_c - cum)
    a_all = -kk * jnp.exp(cum - logw)
    b_all = b_v * e_neg
    k_all = k2 * e_neg
    r_all = p_r * jnp.exp(cum)
    bg_all = b_v * e_rem
    kg_all = k2 * e_rem
    gam_all = jnp.exp(cum_c)

    nh = RW // HALF
    units = [(c, h) for c in range(nc) for h in range(nh)]

    def cut(x, u):
        c, h = u
        return x[c * C:(c + 1) * C, h * HALF:(h + 1) * HALF]

    each = lambda fn, *lists: [fn(*xs) for xs in zip(*lists)]
    stack = lambda x, y: jnp.concatenate([x, y], axis=0)
    top = lambda x: x[:C]
    bot = lambda x: x[C:]
    nn = lambda p: (lambda a, b: _bd_nn(a, b, bdmask, p))

    a_t, b_t, k_t, r_t, b_g, k_g, v_u = ([cut(x, u) for u in units]
                                         for x in (a_all, b_all, k_all, r_all, bg_all, kg_all, p_v))
    gam = [gam_all[c * C:c * C + 1, h * HALF:(h + 1) * HALF] for c, h in units]
    ar = each(stack, a_t, r_t)
    sb = each(lambda x, y: _bd_nt(x, y, bdmask, P_SCORE), ar, b_t)
    sk = each(lambda x, y: _bd_nt(x, y, bdmask, P_SCORE), ar, k_t)
    l_ab = [jnp.where(strict, top(x), 0.0) for x in sb]
    q_b = [jnp.where(incl, bot(x), 0.0) for x in sb]
    m_ak = [jnp.where(strict, top(x), 0.0) for x in sk]
    q_k = [jnp.where(incl, bot(x), 0.0) for x in sk]
    t_m = [eye_l + x for x in l_ab]
    pw = each(nn(P_TINV), l_ab, l_ab)
    for _ in range(4):
        tp = each(nn(P_TINV), each(stack, t_m, pw), pw)
        t_m = each(lambda t, x: t + top(x), t_m, tp)
        pw = [bot(x) for x in tp]
    t_m = each(lambda t, x: t + x, t_m, each(nn(P_TINV), t_m, pw))
    mq = each(nn(P_APPLY), each(stack, m_ak, q_k), v_u)
    a_h = each(nn(P_APPLY), t_m, a_t)
    u0 = each(nn(P_APPLY), t_m, [top(x) for x in mq])
    r_h = each(lambda r, x: r + x, r_t, each(nn(P_APPLY), q_b, a_h))
    y0 = each(lambda m, x: bot(m) + x, mq, each(nn(P_APPLY), q_b, u0))
    g_l = each(lambda a, b, g: _bd_tn(a, b, bdmask_f, P_GH) + eye_l * g, a_h, b_g, gam)
    h_l = each(lambda u, v, b, k: _bd_tn(stack(u, v), stack(b, k), bdmask_f, P_GH), u0, v_u, b_g, k_g)

    s_all = s_ref[...]
    s_cur = [s_all[:, h * HALF:(h + 1) * HALF] for h in range(nh)]
    ys = []
    for c in range(nc):
        idx = [c * nh + h for h in range(nh)]
        yc = [_bd_nt(r_h[i], s, bdmask, P_STATE) + y0[i] for i, s in zip(idx, s_cur)]
        s_cur = [_bd_nn(s, g_l[i], bdmask, P_STATE) + h_l[i] for i, s in zip(idx, s_cur)]
        ys.append(jnp.concatenate(yc, axis=1))
    s_ref[...] = jnp.concatenate(s_cur, axis=1)
    y = jnp.concatenate(ys, axis=0)
    inv_n = 1.0 / RWKV_HEAD_DIM
    mu = headsum(y, split=True) * inv_n
    d = y - mu
    var = headsum(d * d) * inv_n
    yn = d * lax.rsqrt(var + LNX_EPS) * lg_ref[...] + lb_ref[...]
    out_ref[...] = ((yn + bonus) * gate).astype(BF16)


def _rwkv_call(ps, vecs, wup, aup, gup, batch, seq, nc=4):
    t = ps.shape[0]
    rows = nc * CHUNK
    steps = seq // rows
    vec_spec = pl.BlockSpec((1, RW), lambda b, j: (0, 0))
    lora_spec = pl.BlockSpec((LANE, RW), lambda b, j: (0, 0))
    return pl.pallas_call(
        functools.partial(_rwkv_kernel, nc=nc),
        grid=(batch, steps),
        in_specs=[pl.BlockSpec((rows, PS_W), lambda b, j: (b * steps + j, 0))]
                 + [vec_spec] * 7 + [lora_spec] * 3,
        out_specs=pl.BlockSpec((rows, RW), lambda b, j: (b * steps + j, 0)),
        out_shape=jax.ShapeDtypeStruct((t, RW), BF16),
        scratch_shapes=[pltpu.VMEM((CHUNK, RW), F32)],
        compiler_params=_params(("arbitrary", "arbitrary")),
        name="rwkv",
    )(ps, *vecs, wup, aup, gup)


def _mlaprep_kernel(mla_ref, pos_ref, invf_ref, qn_ref, kvn_ref, wq_ref, wqs_ref, wk_ref, wv_ref,
                    q_ref, k_ref, v_ref):
    def rms(x, g):
        return (x * lax.rsqrt(jnp.mean(x * x, axis=-1, keepdims=True) + NORM_EPS) * g).astype(BF16)

    ang = pos_ref[...].astype(F32) * invf_ref[...]
    cos = jnp.cos(ang)
    sin = jnp.sin(ang)
    lane = lax.broadcasted_iota(jnp.int32, cos.shape, 1)
    cos_q = jnp.tile(jnp.where(lane < QK_NOPE, 1.0, cos), (1, MLA_HEADS))
    sin_q = jnp.tile(sin, (1, MLA_HEADS))
    qn = rms(mla_ref[:, 0:Q_LORA], qn_ref[...])
    q = _dot(qn, wq_ref[...]) * cos_q + _dot(qn, wqs_ref[...]) * sin_q
    q_ref[...] = (q * ((QK_NOPE + QK_ROPE) ** -0.5 * LOG2E)).astype(BF16)
    kvn = rms(mla_ref[:, Q_LORA:Q_LORA + KV_LORA], kvn_ref[...])
    o = Q_LORA + KV_LORA
    k_pe = mla_ref[:, o:o + LANE] * cos + mla_ref[:, o + LANE:o + 2 * LANE] * sin
    k_ref[...] = (_dot(kvn, wk_ref[...]) + jnp.tile(k_pe, (1, MLA_HEADS))).astype(BF16)
    v_ref[...] = _dot(kvn, wv_ref[...]).astype(BF16)


def _mlaprep_call(mla, pos, invf, qn, kvn, wq, wqs, wk, wv, tm=512):
    t = mla.shape[0]
    hw = MLA_HEADS * HEAD_PAD
    vw = MLA_HEADS * V_HEAD
    full = lambda shape: pl.BlockSpec(shape, lambda i: (0, 0))
    row = lambda w: pl.BlockSpec((tm, w), lambda i: (i, 0))
    return pl.pallas_call(
        _mlaprep_kernel,
        grid=(t // tm,),
        in_specs=[row(MLA_W), row(1), full((1, LANE)), full((1, Q_LORA)), full((1, KV_LORA)),
                  full((Q_LORA, hw)), full((Q_LORA, hw)), full((KV_LORA, hw)), full((KV_LORA, vw))],
        out_specs=[row(hw), row(hw), row(vw)],
        out_shape=[jax.ShapeDtypeStruct((t, w), BF16) for w in (hw, hw, vw)],
        compiler_params=_params(("arbitrary",)),
        name="mlaprep",
    )(mla, pos, invf, qn, kvn, wq, wqs, wk, wv)


def _attn_kernel(q_ref, k_ref, v_ref, o_ref, *, seq, tq):
    r_i = lax.broadcasted_iota(jnp.int32, (tq, tq), 0) // CHUNK
    c_i = lax.broadcasted_iota(jnp.int32, (tq, tq), 1) // CHUNK
    mask = c_i <= r_i
    def scores(item):
        qi, hh = item
        lo = qi * tq
        qk = slice(hh * HEAD_PAD, (hh + 1) * HEAD_PAD)
        q = q_ref[lo:lo + tq, qk]
        sd = jnp.where(mask, _dot_nt(q, k_ref[lo:lo + tq, qk]), -1e30)
        so = _dot_nt(q, k_ref[0:lo, qk]) if qi > 0 else None
        return sd, so

    items = [(qi, hh) for qi in range(seq // tq) for hh in range(ATTN_HEADS_PER_STEP)]
    nxt = scores(items[0])
    for n, (qi, hh) in enumerate(items):
        lo = qi * tq
        vo = slice(hh * V_HEAD, (hh + 1) * V_HEAD)
        sd, so = nxt
        if n + 1 < len(items):
            nxt = scores(items[n + 1])
        m = jnp.max(sd, axis=-1, keepdims=True)
        if qi > 0:
            m = jnp.maximum(m, jnp.max(so, axis=-1, keepdims=True))
        pd = jnp.exp2(sd - m)
        l = jnp.sum(pd, axis=-1, keepdims=True)
        acc = _dot(pd.astype(BF16), v_ref[lo:lo + tq, vo])
        if qi > 0:
            po = jnp.exp2(so - m)
            l = l + jnp.sum(po, axis=-1, keepdims=True)
            acc = acc + _dot(po.astype(BF16), v_ref[0:lo, vo])
        o_ref[lo:lo + tq, vo] = (acc / l).astype(BF16)


ATTN_HEADS_PER_STEP = LANE // V_HEAD


def _attn_call(q, k, v, batch, seq, tq=256):
    hp = ATTN_HEADS_PER_STEP
    qk_spec = pl.BlockSpec((seq, hp * HEAD_PAD), lambda b, h: (b, h))
    vo_spec = pl.BlockSpec((seq, hp * V_HEAD), lambda b, h: (b, h))
    return pl.pallas_call(
        functools.partial(_attn_kernel, seq=seq, tq=tq),
        grid=(batch, MLA_HEADS // hp),
        in_specs=[qk_spec, qk_spec, vo_spec],
        out_specs=vo_spec,
        out_shape=jax.ShapeDtypeStruct(v.shape, BF16),
        compiler_params=_params(("arbitrary", "arbitrary")),
        name="attn",
    )(q, k, v)


MIX_SUB = 256


def _mixout_kernel(*refs, router):
    if router:
        (rw_ref, o_ref, gates_ref, x_ref, gate_ref, wr_ref, wm_ref, wo_ref,
         g2_ref, shift2_ref, scale2_ref, rwt_ref, rb_ref,
         xo_ref, h2_ref, ti_ref, tp_ref) = refs
    else:
        (rw_ref, o_ref, gates_ref, x_ref, gate_ref, wr_ref, wm_ref, wo_ref,
         g2_ref, shift2_ref, scale2_ref, xo_ref, h2_ref) = refs
    n = x_ref.shape[0]
    subs = [slice(s, s + MIX_SUB) for s in range(0, n, MIX_SUB)]
    yab = [(_dot(rw_ref[r, :], wr_ref[...]), _dot(o_ref[r, :], wm_ref[...])) for r in subs]
    merged = [gates_ref[r, 0:D_MODEL].astype(F32) * ya + gates_ref[r, D_MODEL:].astype(F32) * yb
              for r, (ya, yb) in zip(subs, yab)]
    mixed = [_dot(m.astype(BF16), wo_ref[...]) for m in merged]
    h2s = []
    for r, mx in zip(subs, mixed):
        xn = x_ref[r, :] + gate_ref[...] * mx
        xo_ref[r, :] = xn
        h2s.append(_norm_mod(xn, g2_ref[...], scale2_ref[...], shift2_ref[...]))
        h2_ref[r, :] = h2s[-1].astype(h2_ref.dtype)
    if router:
        h2 = jnp.concatenate(h2s, axis=0)
        hh, hl = _split2(h2)
        wh, wl = _split2(rwt_ref[...])
        logits = _dot_nt(wh, hh) + _dot_nt(wl, hh) + _dot_nt(wh, hl) + rb_ref[...]
        e_i = lax.broadcasted_iota(jnp.int32, logits.shape, 0)
        m1 = jnp.max(logits, axis=0, keepdims=True)
        i1 = jnp.min(jnp.where(logits == m1, e_i, N_EXPERTS), axis=0, keepdims=True)
        rest = jnp.where(e_i == i1, -jnp.inf, logits)
        m2 = jnp.max(rest, axis=0, keepdims=True)
        i2 = jnp.min(jnp.where(rest == m2, e_i, N_EXPERTS), axis=0, keepdims=True)
        e2 = jnp.exp(m2 - m1)
        den = 1.0 + e2
        ti_ref[0:1, :] = i1
        ti_ref[1:2, :] = i2
        tp_ref[0:1, :] = 1.0 / den
        tp_ref[1:2, :] = e2 / den


def _mixout_call(rw, o, gates, x2, gate, wr, wm, wo, g2, shift2, scale2, seq, router=None, tm=512):
    t = x2.shape[0]
    tiles_per_seq = seq // tm
    bmap = lambda i: (i // tiles_per_seq, 0, 0)
    full = lambda shape: pl.BlockSpec(shape, lambda i: (0, 0))
    row = lambda w: pl.BlockSpec((tm, w), lambda i: (i, 0))
    mod = pl.BlockSpec((None, 1, D_MODEL), bmap)
    in_specs = [row(RW), row(MLA_HEADS * V_HEAD), row(GATE_W), row(D_MODEL), mod,
                full(wr.shape), full(wm.shape), full(wo.shape), full((1, D_MODEL)), mod, mod]
    args = [rw, o, gates, x2, gate, wr, wm, wo, g2, shift2, scale2]
    out_specs = [row(D_MODEL), row(D_MODEL)]
    out_shape = [jax.ShapeDtypeStruct((t, D_MODEL), F32),
                 jax.ShapeDtypeStruct((t, D_MODEL), BF16 if router is None else F32)]
    if router is not None:
        in_specs += [full((N_EXPERTS, D_MODEL)), full((N_EXPERTS, 1))]
        args += list(router)
        out_specs += [pl.BlockSpec((2, tm), lambda i: (0, i))] * 2
        out_shape += [jax.ShapeDtypeStruct((2, t), jnp.int32), jax.ShapeDtypeStruct((2, t), F32)]
    return pl.pallas_call(
        functools.partial(_mixout_kernel, router=router is not None),
        grid=(t // tm,),
        in_specs=in_specs,
        out_specs=out_specs,
        out_shape=out_shape,
        compiler_params=_params(("arbitrary",)),
        name="mixout",
    )(*args)


def _swiglu_step(h, w1_ref, w3_ref, w2_ref, acc_ref, finish):
    f = pl.program_id(1)

    @pl.when(f == 0)
    def _():
        acc_ref[...] = jnp.zeros_like(acc_ref)

    a1 = _dot(h, w1_ref[...])
    a3 = _dot(h, w3_ref[...])
    act = (a1 * _sigmoid(a1)) * a3
    acc_ref[...] += _dot(act.astype(BF16), w2_ref[...])

    @pl.when(f == pl.num_programs(1) - 1)
    def _():
        finish(acc_ref[...])


def _rms(x, g):
    return x * lax.rsqrt(jnp.mean(x * x, axis=-1, keepdims=True) + NORM_EPS) * g


def _ffn_kernel(h_ref, x_ref, gate_ref, gf_ref, w1_ref, w3_ref, w2_ref, o_ref, *, final, tf):
    h = h_ref[...]
    ff = w1_ref.shape[1]
    ups = [(_dot(h, w1_ref[:, s:s + tf]), _dot(h, w3_ref[:, s:s + tf])) for s in range(0, ff, tf)]
    acc = None
    for s, (a1, a3) in zip(range(0, ff, tf), ups):
        act = ((a1 * _sigmoid(a1)) * a3).astype(BF16)
        part = _dot(act, w2_ref[s:s + tf, :])
        acc = part if acc is None else acc + part
    xn = x_ref[...] + gate_ref[...] * acc
    o_ref[...] = _rms(xn, gf_ref[...]) if final else xn


def _ffn_call(h, x2, gate, gf, w1, w3, w2, seq, final, tm=512, tf=1408):
    r = h.shape[0]
    tiles_per_seq = seq // tm
    row = pl.BlockSpec((tm, D_MODEL), lambda i: (i, 0))
    resident = lambda a: pl.BlockSpec(a.shape, lambda i: (0, 0), pipeline_mode=pl.Buffered(1))
    return pl.pallas_call(
        functools.partial(_ffn_kernel, final=final, tf=tf),
        grid=(r // tm,),
        in_specs=[
            row, row,
            pl.BlockSpec((None, 1, D_MODEL), lambda i: (i // tiles_per_seq, 0, 0)),
            pl.BlockSpec((1, D_MODEL), lambda i: (0, 0)),
            resident(w1), resident(w3), resident(w2),
        ],
        out_specs=row,
        out_shape=jax.ShapeDtypeStruct((r, D_MODEL), F32),
        compiler_params=_params(("arbitrary",)),
        name="ffn",
    )(h, x2, gate, gf, w1, w3, w2)


MOE_TM = 512
ROW_TILE = (D_MODEL // LANE, LANE)


def _route(top_i, tm):
    t = top_i.shape[1]
    e_flat = top_i.reshape(-1)
    oh = (e_flat[:, None] == jnp.arange(N_EXPERTS, dtype=jnp.int32)[None, :]).astype(jnp.int32)
    csum = jnp.cumsum(oh, axis=0)
    rank = jnp.sum(csum * oh, axis=1) - 1
    counts = csum[-1]
    padded = ((counts + tm - 1) // tm) * tm
    ends = jnp.cumsum(padded)
    pos = jnp.sum(oh * (ends - padded)[None, :], axis=1) + rank
    n_tiles = (2 * t) // tm + N_EXPERTS
    starts = jnp.arange(n_tiles, dtype=jnp.int32) * tm
    tile_e = jnp.sum((starts[:, None] >= ends[None, :]).astype(jnp.int32), axis=1)
    return (pos.astype(jnp.int32), ends.astype(jnp.int32), jnp.minimum(tile_e, N_EXPERTS - 1),
            (ends[-1:] // tm).astype(jnp.int32))


def _dispatch_kernel(pos_ref, ends_ref, h_ref, xs_ref, zero_ref, sem, *, t, td, tm):
    i = pl.program_id(0)
    base = i * td

    @pl.when(i == 0)
    def _():
        zero_ref[...] = jnp.zeros_like(zero_ref)
        n_tiles = xs_ref.shape[0] // tm
        fills = []
        for e in range(N_EXPERTS):
            start = ends_ref[e - 1] if e else 0
            fills.append((ends_ref[e] > start, pl.multiple_of(ends_ref[e] - tm, tm)))
            fills.append((ends_ref[N_EXPERTS - 1] // tm + e < n_tiles,
                          pl.multiple_of(ends_ref[N_EXPERTS - 1] + e * tm, tm)))
        for go, row0 in fills:
            @pl.when(go)
            def _():
                pltpu.make_async_copy(zero_ref, xs_ref.at[pl.ds(row0, tm)], sem).start()
        for go, row0 in fills:
            @pl.when(go)
            def _():
                pltpu.make_async_copy(zero_ref, xs_ref.at[pl.ds(row0, tm)], sem).wait()

    for r in range(td):
        for k in range(2):
            dst = pos_ref[k * t + base + r]
            pltpu.make_async_copy(h_ref.at[pl.ds(r, 1)], xs_ref.at[pl.ds(dst, 1)], sem).start(priority=k)
    for k in range(2):
        pltpu.make_async_copy(h_ref, xs_ref.at[pl.ds(0, td)], sem).wait()


def _dispatch_call(pos, ends, h, n_rows, tm, td=512):
    t = h.shape[0]
    return pl.pallas_call(
        functools.partial(_dispatch_kernel, t=t, td=td, tm=tm),
        grid_spec=pltpu.PrefetchScalarGridSpec(
            num_scalar_prefetch=2, grid=(t // td,),
            in_specs=[pl.BlockSpec((td, D_MODEL), lambda i, pos, ends: (i, 0))],
            out_specs=pl.BlockSpec(memory_space=pl.ANY),
            scratch_shapes=[pltpu.VMEM((tm, D_MODEL), h.dtype), pltpu.SemaphoreType.DMA(())]),
        out_shape=jax.ShapeDtypeStruct((n_rows, D_MODEL), h.dtype),
        compiler_params=_params(("arbitrary",)),
        name="dispatch",
    )(pos, ends, h)


def _ffn_routed_kernel(te_ref, nu_ref, x_ref, w1_ref, w3_ref, w2_ref, y_ref, acc_ref):
    del te_ref
    used = pl.program_id(0) < nu_ref[0]

    def finish(acc):
        y_ref[...] = acc.reshape(y_ref.shape)

    @pl.when(used)
    def _():
        _swiglu_step(x_ref[...].astype(BF16), w1_ref, w3_ref, w2_ref, acc_ref, finish)

    @pl.when(jnp.logical_not(used))
    def _():
        y_ref[...] = jnp.zeros_like(y_ref)


def _ffn_routed_call(tile_e, n_used, xs, w1, w3, w2, tm=MOE_TM, tf=1792):
    r = xs.shape[0]
    ff = w1.shape[2]
    return pl.pallas_call(
        _ffn_routed_kernel,
        grid_spec=pltpu.PrefetchScalarGridSpec(
            num_scalar_prefetch=2, grid=(r // tm, ff // tf),
            in_specs=[
                pl.BlockSpec((tm, D_MODEL), lambda i, f, te, nu: (i, 0)),
                pl.BlockSpec((None, D_MODEL, tf), lambda i, f, te, nu: (te[i], 0, f)),
                pl.BlockSpec((None, D_MODEL, tf), lambda i, f, te, nu: (te[i], 0, f)),
                pl.BlockSpec((None, tf, D_MODEL), lambda i, f, te, nu: (te[i], f, 0)),
            ],
            out_specs=pl.BlockSpec((tm,) + ROW_TILE, lambda i, f, te, nu: (i, 0, 0)),
            scratch_shapes=[pltpu.VMEM((tm, D_MODEL), F32)]),
        out_shape=jax.ShapeDtypeStruct((r,) + ROW_TILE, F32),
        compiler_params=_params(("arbitrary", "arbitrary")),
        name="ffn_routed",
    )(tile_e, n_used, xs, w1, w3, w2)


def _combine_kernel(pos_ref, x_ref, p_ref, gate_ref, gf_ref, y_ref, o_ref, buf, sem, *, t, tc, final):
    i = pl.program_id(0)

    def row_copy(step, slot, r, k):
        src = pos_ref[k * t + step * tc + r]
        return pltpu.make_async_copy(y_ref.at[pl.ds(src, 1)], buf.at[slot, k, pl.ds(r, 1)], sem.at[slot])

    def issue(step, slot):
        for r in range(tc):
            for k in range(2):
                row_copy(step, slot, r, k).start(priority=k)

    @pl.when(i == 0)
    def _():
        issue(0, 0)

    @pl.when(i + 1 < pl.num_programs(0))
    def _():
        issue(i + 1, (i + 1) % 2)

    slot = i % 2
    for k in range(2):
        pltpu.make_async_copy(y_ref.at[pl.ds(0, tc)], buf.at[slot, k], sem.at[slot]).wait()
    p = p_ref[...]
    y0 = buf[slot, 0].reshape(tc, D_MODEL)
    y1 = buf[slot, 1].reshape(tc, D_MODEL)
    xn = x_ref[...] + gate_ref[...] * (p[:, 0:1] * y0 + p[:, 1:2] * y1)
    o_ref[...] = _rms(xn, gf_ref[...]) if final else xn


def _combine_call(pos, x2, probs, gate, gf, y, seq, final, tc=256):
    t = x2.shape[0]
    tiles_per_seq = seq // tc
    row = pl.BlockSpec((tc, D_MODEL), lambda i, pos: (i, 0))
    return pl.pallas_call(
        functools.partial(_combine_kernel, t=t, tc=tc, final=final),
        grid_spec=pltpu.PrefetchScalarGridSpec(
            num_scalar_prefetch=1, grid=(t // tc,),
            in_specs=[row, pl.BlockSpec((tc, 2), lambda i, pos: (i, 0)),
                      pl.BlockSpec((None, 1, D_MODEL), lambda i, pos: (i // tiles_per_seq, 0, 0)),
                      pl.BlockSpec((1, D_MODEL), lambda i, pos: (0, 0)),
                      pl.BlockSpec(memory_space=pl.ANY)],
            out_specs=row,
            scratch_shapes=[pltpu.VMEM((2, 2, tc) + ROW_TILE, F32), pltpu.SemaphoreType.DMA((2,))]),
        out_shape=jax.ShapeDtypeStruct((t, D_MODEL), F32),
        compiler_params=_params(("arbitrary",)),
        name="combine",
    )(pos, x2, probs, gate, gf, y)


def _layout_w_in_kernel(w_ref, o_ref):
    half = QK_ROPE // 2
    o_ref[...] = jnp.zeros_like(o_ref)

    def put(dst, src, width, sign=1.0):
        o_ref[:, dst:dst + width] = (sign * w_ref[:, src:src + width]).astype(BF16)

    o = 3 * RW
    put(0, 0, o + DECAY_LORA)
    put(o + LANE, o + DECAY_LORA, ICLR_LORA)
    put(o + 2 * LANE, o + DECAY_LORA + ICLR_LORA, GATE_LORA)
    src = o + DECAY_LORA + ICLR_LORA + GATE_LORA
    put(PS_W, src, Q_LORA + KV_LORA)
    kr = src + Q_LORA + KV_LORA
    dst = PS_W + Q_LORA + KV_LORA
    put(dst + QK_NOPE, kr, QK_ROPE)
    put(dst + LANE + QK_NOPE, kr + half, half, -1.0)
    put(dst + LANE + QK_NOPE + half, kr, half)
    put(PS_W + MLA_W, kr + QK_ROPE, GATE_W)


def _layout_w_in(w, layer):
    _, k, n = w.shape
    return pl.pallas_call(
        _layout_w_in_kernel,
        grid=(k // 256,),
        in_specs=[pl.BlockSpec((None, 256, n), lambda i: (layer, i, 0))],
        out_specs=pl.BlockSpec((256, PS_W + MLA_W + GATE_W), lambda i: (i, 0)),
        out_shape=jax.ShapeDtypeStruct((k, PS_W + MLA_W + GATE_W), BF16),
        compiler_params=_params(("arbitrary",)),
        name="layout_w_in",
    )(w)


def _layout_mu(mu):
    z = jnp.zeros((64,), mu.dtype)
    o = 3 * RW
    return jnp.concatenate([mu[:o + 64], z, mu[o + 64:o + 128], z, mu[o + 128:]])[None, :]


def _layout_wq(w_qb):
    hd = QK_NOPE + QK_ROPE
    w = w_qb.reshape(Q_LORA, MLA_HEADS, hd)
    half = QK_ROPE // 2
    x1, x2 = w[:, :, QK_NOPE:QK_NOPE + half], w[:, :, QK_NOPE + half:]
    zn = jnp.zeros((Q_LORA, MLA_HEADS, QK_NOPE), w.dtype)
    zp = jnp.zeros((Q_LORA, MLA_HEADS, HEAD_PAD - hd), w.dtype)
    wq = jnp.concatenate([w, zp], axis=2).reshape(Q_LORA, -1)
    wqs = jnp.concatenate([zn, -x2, x1, zp], axis=2).reshape(Q_LORA, -1)
    return wq.astype(BF16), wqs.astype(BF16)


def _layout_wkv(w_kvb):
    w = w_kvb.reshape(KV_LORA, MLA_HEADS, QK_NOPE + V_HEAD)
    z = jnp.zeros((KV_LORA, MLA_HEADS, HEAD_PAD - QK_NOPE), w.dtype)
    wk = jnp.concatenate([w[:, :, :QK_NOPE], z], axis=2).reshape(KV_LORA, -1)
    wv = w[:, :, QK_NOPE:].reshape(KV_LORA, -1)
    return wk.astype(BF16), wv.astype(BF16)


def _rope_freq_row():
    inv_freq = ROPE_THETA ** (-jnp.arange(0, QK_ROPE, 2, dtype=F32) / QK_ROPE)
    z = jnp.zeros((QK_NOPE,), F32)
    zp = jnp.zeros((LANE - QK_NOPE - QK_ROPE,), F32)
    return jnp.concatenate([z, inv_freq, inv_freq, zp])[None, :]


def kernel(x, c, positions, ada_w, ada_b, norm_mix, norm_ffn, norm_final, w_in, tshift_mu, w0, w_up, a0, a_up, g_up, k_k, k_a, r_k, lnx_g, lnx_b, rwkv_out, q_norm, w_qb, kv_norm, w_kvb, mla_out, w_o, ffn_w1, ffn_w3, ffn_w2, router_w, router_b, moe_w1, moe_w3, moe_w2):
    batch, seq, _ = x.shape
    t = batch * seq
    mods = _ada_call(c, ada_w.reshape(2 * DEPTH, D_MODEL, 3 * D_MODEL),
                     ada_b.reshape(2 * DEPTH, 1, 3 * D_MODEL))
    mods = mods.reshape(DEPTH, 2, batch, 3, 1, D_MODEL)
    pos_col = positions.reshape(t, 1)
    invf = _rope_freq_row()
    x2 = x.reshape(t, D_MODEL)
    row = lambda v: v.reshape(1, -1)

    for l in range(DEPTH):
        shift, scale, gate = (mods[l, 0, :, j] for j in range(3))
        ps, mla, gates = _inproj_call(x2, shift, scale, row(norm_mix[l]), _layout_w_in(w_in, l),
                                      _layout_mu(tshift_mu[l]), seq)
        vecs = [row(v[l]) for v in (w0, a0, k_k, k_a, r_k, lnx_g, lnx_b)]
        lora_pad = lambda w: jnp.pad(w, ((0, LANE - w.shape[0]), (0, 0)))
        rw = _rwkv_call(ps, vecs, lora_pad(w_up[l]), lora_pad(a_up[l]), g_up[l], batch, seq)
        wq, wqs = _layout_wq(w_qb[l])
        wk, wv = _layout_wkv(w_kvb[l])
        q, k, v = _mlaprep_call(mla, pos_col, invf, row(q_norm[l]), row(kv_norm[l]), wq, wqs, wk, wv)
        o = _attn_call(q, k, v, batch, seq)

        shift2, scale2, gate2 = (mods[l, 1, :, j] for j in range(3))
        moe = l % 2 == 1
        router = (router_w[l // 2].T, router_b[l // 2].reshape(N_EXPERTS, 1)) if moe else None
        outs = _mixout_call(rw, o, gates, x2, gate, rwkv_out[l].astype(BF16), mla_out[l].astype(BF16),
                            w_o[l].astype(BF16), row(norm_ffn[l]), shift2, scale2, seq, router=router)
        final = l == DEPTH - 1
        if moe:
            x2, h2, top_i, top_p = outs
            pos, ends, tile_e, n_used = _route(top_i, MOE_TM)
            xs = _dispatch_call(pos, ends, h2, 2 * t + N_EXPERTS * MOE_TM, MOE_TM)
            y = _ffn_routed_call(tile_e, n_used, xs, moe_w1[l // 2].astype(BF16),
                                 moe_w3[l // 2].astype(BF16), moe_w2[l // 2].astype(BF16))
            x2 = _combine_call(pos, x2, top_p.T, gate2, row(norm_final), y, seq, final)
        else:
            x2, h2 = outs
            x2 = _ffn_call(h2, x2, gate2, row(norm_final), ffn_w1[l // 2].astype(BF16),
                           ffn_w3[l // 2].astype(BF16), ffn_w2[l // 2].astype(BF16), seq, final)
    return x2.reshape(batch, seq, D_MODEL)
```

```python
import functools
import math

import jax
import jax.numpy as jnp
from jax import lax
from jax.experimental import pallas as pl
from jax.experimental.pallas import tpu as pltpu

F32 = jnp.float32
BF16 = jnp.bfloat16

D_MODEL = 1024
DEPTH = 2
CHUNK = 64
NORM_EPS = 1e-6
RWKV_HEADS = 8
RWKV_HEAD_DIM = 64
RW = RWKV_HEADS * RWKV_HEAD_DIM
DECAY_LORA = 64
ICLR_LORA = 64
GATE_LORA = 128
DECAY_SCALE = math.exp(-0.5)
LNX_EPS = 64e-5
MLA_HEADS = 8
Q_LORA = 384
KV_LORA = 256
QK_NOPE = 64
QK_ROPE = 32
V_HEAD = 64
ROPE_THETA = 10000.0
LOG2E = math.log2(math.e)
N_EXPERTS = 8

LANE = 128
HEAD_PAD = 128
PS_W = 3 * RW + 3 * LANE
MLA_W = Q_LORA + KV_LORA + 2 * LANE
GATE_W = 2 * D_MODEL
HALF = 256
VMEM_LIMIT = 56 * 1024 * 1024


def _sigmoid(x):
    return 1.0 / (1.0 + jnp.exp(-x))


def _dot(a, b):
    return jnp.dot(a, b, preferred_element_type=F32)


def _dot_nt(a, b):
    return lax.dot_general(a, b, (((1,), (1,)), ((), ())), preferred_element_type=F32)


def _split2(x):
    hi = x.astype(BF16)
    lo = (x - hi.astype(F32)).astype(BF16)
    return hi, lo


def _dot3(a, b):
    ah, al = _split2(a)
    bh, bl = _split2(b)
    return _dot(ah, bh) + _dot(al, bh) + _dot(ah, bl)


def _params(sem):
    return pltpu.CompilerParams(dimension_semantics=sem, vmem_limit_bytes=VMEM_LIMIT)


def _ada_kernel(c_ref, w_ref, b_ref, o_ref):
    c = c_ref[...]
    o_ref[...] = _dot((c * _sigmoid(c)).astype(BF16), w_ref[...].astype(BF16)) + b_ref[...]


def _ada_call(c, ada_w, ada_b):
    n = ada_w.shape[0]
    nb = c.shape[0]
    return pl.pallas_call(
        _ada_kernel,
        grid=(n, 3),
        in_specs=[
            pl.BlockSpec((nb, D_MODEL), lambda i, j: (0, 0)),
            pl.BlockSpec((None, D_MODEL, D_MODEL), lambda i, j: (i, 0, j)),
            pl.BlockSpec((None, 1, D_MODEL), lambda i, j: (i, 0, j)),
        ],
        out_specs=pl.BlockSpec((None, nb, D_MODEL), lambda i, j: (i, 0, j)),
        out_shape=jax.ShapeDtypeStruct((n, nb, 3 * D_MODEL), F32),
        compiler_params=_params(("arbitrary", "arbitrary")),
        name="ada",
    )(c, ada_w, ada_b)


def _norm_mod(x, g, scale, shift):
    var = jnp.mean(x * x, axis=-1, keepdims=True)
    return (x * lax.rsqrt(var + NORM_EPS) * g) * (1.0 + scale) + shift


def _inproj_kernel(x_ref, shift_ref, scale_ref, g_ref, w_ref, mu_ref,
                   ps_ref, mla_ref, gates_ref, carry_ref, *, tiles_per_seq, tm):
    i = pl.program_id(0)

    @pl.when(i % tiles_per_seq == 0)
    def _():
        carry_ref[...] = jnp.zeros_like(carry_ref)

    carry = carry_ref[0:1, :]
    row = lax.broadcasted_iota(jnp.int32, (INPROJ_SUB, PS_W), 0)
    for s in range(0, tm, INPROJ_SUB):
        r = slice(s, s + INPROJ_SUB)
        h = _norm_mod(x_ref[r, :], g_ref[...], scale_ref[...], shift_ref[...]).astype(BF16)
        p = _dot(h, w_ref[:, 0:PS_W])
        prev = jnp.where(row == 0, carry, pltpu.roll(p, 1, axis=0))
        carry = p[INPROJ_SUB - 1:INPROJ_SUB, :]
        ps_ref[r, :] = p + (prev - p) * mu_ref[...]
        mla_ref[r, :] = _dot(h, w_ref[:, PS_W:PS_W + MLA_W])
        gates_ref[r, :] = _sigmoid(_dot(h, w_ref[:, PS_W + MLA_W:])).astype(BF16)
    carry_ref[0:1, :] = carry


INPROJ_SUB = 256


def _inproj_call(x2, shift, scale, g, w_pad, mu_pad, seq, tm=512):
    t = x2.shape[0]
    tiles_per_seq = seq // tm
    bmap = lambda i: (i // tiles_per_seq, 0, 0)
    wtot = PS_W + MLA_W + GATE_W
    return pl.pallas_call(
        functools.partial(_inproj_kernel, tiles_per_seq=tiles_per_seq, tm=tm),
        grid=(t // tm,),
        in_specs=[
            pl.BlockSpec((tm, D_MODEL), lambda i: (i, 0)),
            pl.BlockSpec((None, 1, D_MODEL), bmap),
            pl.BlockSpec((None, 1, D_MODEL), bmap),
            pl.BlockSpec((1, D_MODEL), lambda i: (0, 0)),
            pl.BlockSpec((D_MODEL, wtot), lambda i: (0, 0)),
            pl.BlockSpec((1, PS_W), lambda i: (0, 0)),
        ],
        out_specs=[
            pl.BlockSpec((tm, PS_W), lambda i: (i, 0)),
            pl.BlockSpec((tm, MLA_W), lambda i: (i, 0)),
            pl.BlockSpec((tm, GATE_W), lambda i: (i, 0)),
        ],
        out_shape=[
            jax.ShapeDtypeStruct((t, PS_W), F32),
            jax.ShapeDtypeStruct((t, MLA_W), F32),
            jax.ShapeDtypeStruct((t, GATE_W), BF16),
        ],
        scratch_shapes=[pltpu.VMEM((8, PS_W), F32)],
        compiler_params=_params(("arbitrary",)),
        name="inproj",
    )(x2, shift, scale, g, w_pad, mu_pad)


def _blockdiag(xb, bdmask):
    return jnp.tile(xb, (HALF // CHUNK, 1)) * bdmask


def _bd_mm(dot, a, b, bdmask, passes):
    if passes == 1:
        return dot(a.astype(BF16), _blockdiag(b.astype(BF16), bdmask))
    ah, al = _split2(a)
    bh, bl = _split2(b)
    r = dot(jnp.concatenate([ah, al], axis=0), _blockdiag(bh, bdmask))
    n = a.shape[0]
    return r[:n] + r[n:] + dot(ah, _blockdiag(bl, bdmask))


def _bd_nn(a, b, bdmask, passes):
    return _bd_mm(_dot, a, b, bdmask, passes)


def _bd_nt(a, b, bdmask, passes):
    return _bd_mm(_dot_nt, a, b, bdmask, passes)


def _bd_tn(a, b, bdmask_f, passes):
    full = _dot3(a.T, b) if passes == 3 else _dot(a.T.astype(BF16), b.astype(BF16))
    return jnp.sum((full * bdmask_f).reshape(HALF // CHUNK, CHUNK, HALF), axis=0)


P_SCORE = 1
P_TINV = 1
P_APPLY = 1
P_GH = 1
P_STATE = 1


def _halves(fn, *xs):
    outs = [fn(*[x[:, h * HALF:(h + 1) * HALF] for x in xs]) for h in range(RW // HALF)]
    return jnp.concatenate(outs, axis=1)


def _rwkv_kernel(ps_ref, w0_ref, a0_ref, kk_ref, ka_ref, rk_ref, lg_ref, lb_ref,
                 wup_ref, aup_ref, gup_ref, out_ref, s_ref, *, nc):
    @pl.when(pl.program_id(1) == 0)
    def _():
        s_ref[...] = jnp.zeros_like(s_ref)

    C = CHUNK
    r_i = lax.broadcasted_iota(jnp.int32, (HALF, HALF), 0)
    c_i = lax.broadcasted_iota(jnp.int32, (HALF, HALF), 1)
    bdsame = (r_i // C) == (c_i // C)
    bdmask = bdsame.astype(BF16)
    bdmask_f = bdsame.astype(F32)
    t_i = lax.broadcasted_iota(jnp.int32, (C, HALF), 0)
    i_i = lax.broadcasted_iota(jnp.int32, (C, HALF), 1) % C
    strict = i_i < t_i
    incl = i_i <= t_i
    eye_l = (i_i == t_i).astype(F32)
    n = nc * C
    tr = lax.broadcasted_iota(jnp.int32, (n, n), 0)
    tc = lax.broadcasted_iota(jnp.int32, (n, n), 1)
    tril = ((tc <= tr) & (tc // C == tr // C)).astype(BF16)

    def headsum(x, split=False):
        def one(xh):
            if not split:
                return _dot(xh.astype(BF16), bdmask)
            hi, lo = _split2(xh)
            return _dot(hi, bdmask) + _dot(lo, bdmask)
        return _halves(one, x)

    def cumsum_rows(x):
        hi = x.astype(BF16)
        r1 = x - hi.astype(F32)
        mid = r1.astype(BF16)
        lo = (r1 - mid.astype(F32)).astype(BF16)
        return _dot(tril, hi) + _dot(tril, mid) + _dot(tril, lo)

    p_r = ps_ref[:, 0:RW]
    p_k = ps_ref[:, RW:2 * RW]
    p_v = ps_ref[:, 2 * RW:3 * RW]
    p_w = ps_ref[:, 3 * RW:3 * RW + LANE]
    p_a = ps_ref[:, 3 * RW + LANE:3 * RW + 2 * LANE]
    p_g = ps_ref[:, 3 * RW + 2 * LANE:3 * RW + 3 * LANE]
    logw = -DECAY_SCALE * _sigmoid(w0_ref[...] + _dot3(jnp.tanh(p_w), wup_ref[...]))
    iclr = _sigmoid(a0_ref[...] + _dot(p_a.astype(BF16), aup_ref[...].astype(BF16)))
    gate = _dot(_sigmoid(p_g).astype(BF16), gup_ref[...].astype(BF16))
    kk = p_k * kk_ref[...]
    k2 = p_k * (1.0 + (iclr - 1.0) * ka_ref[...])
    sums = headsum(jnp.concatenate([kk * kk, p_r * k2 * rk_ref[...]], axis=0))
    kk = kk * lax.rsqrt(sums[:n] + 1e-12)
    bonus = sums[n:] * p_v
    b_v = kk * iclr
    cum = cumsum_rows(logw)
    cum_c = jnp.concatenate(
        [jnp.broadcast_to(cum[(c + 1) * C - 1:(c + 1) * C, :], (C, RW)) for c in range(nc)], axis=0)
    e_neg = jnp.exp(-cum)
    e_rem = jnp.exp(cum_c - cum)
    a_all = -kk * jnp.exp(cum - logw)
    b_all = b_v * e_neg
    k_all = k2 * e_neg
    r_all = p_r * jnp.exp(cum)
    bg_all = b_v * e_rem
    kg_all = k2 * e_rem
    gam_all = jnp.exp(cum_c)

    nh = RW // HALF
    units = [(c, h) for c in range(nc) for h in range(nh)]

    def cut(x, u):
        c, h = u
        return x[c * C:(c + 1) * C, h * HALF:(h + 1) * HALF]

    each = lambda fn, *lists: [fn(*xs) for xs in zip(*lists)]
    stack = lambda x, y: jnp.concatenate([x, y], axis=0)
    top = lambda x: x[:C]
    bot = lambda x: x[C:]
    nn = lambda p: (lambda a, b: _bd_nn(a, b, bdmask, p))

    a_t, b_t, k_t, r_t, b_g, k_g, v_u = ([cut(x, u) for u in units]
                                         for x in (a_all, b_all, k_all, r_all, bg_all, kg_all, p_v))
    gam = [gam_all[c * C:c * C + 1, h * HALF:(h + 1) * HALF] for c, h in units]
    ar = each(stack, a_t, r_t)
    sb = each(lambda x, y: _bd_nt(x, y, bdmask, P_SCORE), ar, b_t)
    sk = each(lambda x, y: _bd_nt(x, y, bdmask, P_SCORE), ar, k_t)
    l_ab = [jnp.where(strict, top(x), 0.0) for x in sb]
    q_b = [jnp.where(incl, bot(x), 0.0) for x in sb]
    m_ak = [jnp.where(strict, top(x), 0.0) for x in sk]
    q_k = [jnp.where(incl, bot(x), 0.0) for x in sk]
    t_m = [eye_l + x for x in l_ab]
    pw = each(nn(P_TINV), l_ab, l_ab)
    for _ in range(4):
        tp = each(nn(P_TINV), each(stack, t_m, pw), pw)
        t_m = each(lambda t, x: t + top(x), t_m, tp)
        pw = [bot(x) for x in tp]
    t_m = each(lambda t, x: t + x, t_m, each(nn(P_TINV), t_m, pw))
    mq = each(nn(P_APPLY), each(stack, m_ak, q_k), v_u)
    a_h = each(nn(P_APPLY), t_m, a_t)
    u0 = each(nn(P_APPLY), t_m, [top(x) for x in mq])
    r_h = each(lambda r, x: r + x, r_t, each(nn(P_APPLY), q_b, a_h))
    y0 = each(lambda m, x: bot(m) + x, mq, each(nn(P_APPLY), q_b, u0))
    g_l = each(lambda a, b, g: _bd_tn(a, b, bdmask_f, P_GH) + eye_l * g, a_h, b_g, gam)
    h_l = each(lambda u, v, b, k: _bd_tn(stack(u, v), stack(b, k), bdmask_f, P_GH), u0, v_u, b_g, k_g)

    s_all = s_ref[...]
    s_cur = [s_all[:, h * HALF:(h + 1) * HALF] for h in range(nh)]
    ys = []
    for c in range(nc):
        idx = [c * nh + h for h in range(nh)]
        yc = [_bd_nt(r_h[i], s, bdmask, P_STATE) + y0[i] for i, s in zip(idx, s_cur)]
        s_cur = [_bd_nn(s, g_l[i], bdmask, P_STATE) + h_l[i] for i, s in zip(idx, s_cur)]
        ys.append(jnp.concatenate(yc, axis=1))
    s_ref[...] = jnp.concatenate(s_cur, axis=1)
    y = jnp.concatenate(ys, axis=0)
    inv_n = 1.0 / RWKV_HEAD_DIM
    mu = headsum(y, split=True) * inv_n
    d = y - mu
    var = headsum(d * d) * inv_n
    yn = d * lax.rsqrt(var + LNX_EPS) * lg_ref[...] + lb_ref[...]
    out_ref[...] = ((yn + bonus) * gate).astype(BF16)


def _rwkv_call(ps, vecs, wup, aup, gup, batch, seq, nc=4):
    t = ps.shape[0]
    rows = nc * CHUNK
    steps = seq // rows
    vec_spec = pl.BlockSpec((1, RW), lambda b, j: (0, 0))
    lora_spec = pl.BlockSpec((LANE, RW), lambda b, j: (0, 0))
    return pl.pallas_call(
        functools.partial(_rwkv_kernel, nc=nc),
        grid=(batch, steps),
        in_specs=[pl.BlockSpec((rows, PS_W), lambda b, j: (b * steps + j, 0))]
                 + [vec_spec] * 7 + [lora_spec] * 3,
        out_specs=pl.BlockSpec((rows, RW), lambda b, j: (b * steps + j, 0)),
        out_shape=jax.ShapeDtypeStruct((t, RW), BF16),
        scratch_shapes=[pltpu.VMEM((CHUNK, RW), F32)],
        compiler_params=_params(("arbitrary", "arbitrary")),
        name="rwkv",
    )(ps, *vecs, wup, aup, gup)


def _mlaprep_kernel(mla_ref, pos_ref, invf_ref, qn_ref, kvn_ref, wq_ref, wqs_ref, wk_ref, wv_ref,
                    q_ref, k_ref, v_ref):
    def rms(x, g):
        return (x * lax.rsqrt(jnp.mean(x * x, axis=-1, keepdims=True) + NORM_EPS) * g).astype(BF16)

    ang = pos_ref[...].astype(F32) * invf_ref[...]
    cos = jnp.cos(ang)
    sin = jnp.sin(ang)
    lane = lax.broadcasted_iota(jnp.int32, cos.shape, 1)
    cos_q = jnp.tile(jnp.where(lane < QK_NOPE, 1.0, cos), (1, MLA_HEADS))
    sin_q = jnp.tile(sin, (1, MLA_HEADS))
    qn = rms(mla_ref[:, 0:Q_LORA], qn_ref[...])
    q = _dot(qn, wq_ref[...]) * cos_q + _dot(qn, wqs_ref[...]) * sin_q
    q_ref[...] = (q * ((QK_NOPE + QK_ROPE) ** -0.5 * LOG2E)).astype(BF16)
    kvn = rms(mla_ref[:, Q_LORA:Q_LORA + KV_LORA], kvn_ref[...])
    o = Q_LORA + KV_LORA
    k_pe = mla_ref[:, o:o + LANE] * cos + mla_ref[:, o + LANE:o + 2 * LANE] * sin
    k_ref[...] = (_dot(kvn, wk_ref[...]) + jnp.tile(k_pe, (1, MLA_HEADS))).astype(BF16)
    v_ref[...] = _dot(kvn, wv_ref[...]).astype(BF16)


def _mlaprep_call(mla, pos, invf, qn, kvn, wq, wqs, wk, wv, tm=512):
    t = mla.shape[0]
    hw = MLA_HEADS * HEAD_PAD
    vw = MLA_HEADS * V_HEAD
    full = lambda shape: pl.BlockSpec(shape, lambda i: (0, 0))
    row = lambda w: pl.BlockSpec((tm, w), lambda i: (i, 0))
    return pl.pallas_call(
        _mlaprep_kernel,
        grid=(t // tm,),
        in_specs=[row(MLA_W), row(1), full((1, LANE)), full((1, Q_LORA)), full((1, KV_LORA)),
                  full((Q_LORA, hw)), full((Q_LORA, hw)), full((KV_LORA, hw)), full((KV_LORA, vw))],
        out_specs=[row(hw), row(hw), row(vw)],
        out_shape=[jax.ShapeDtypeStruct((t, w), BF16) for w in (hw, hw, vw)],
        compiler_params=_params(("arbitrary",)),
        name="mlaprep",
    )(mla, pos, invf, qn, kvn, wq, wqs, wk, wv)


def _attn_kernel(q_ref, k_ref, v_ref, o_ref, *, seq, tq):
    r_i = lax.broadcasted_iota(jnp.int32, (tq, tq), 0) // CHUNK
    c_i = lax.broadcasted_iota(jnp.int32, (tq, tq), 1) // CHUNK
    mask = c_i <= r_i
    def scores(item):
        qi, hh = item
        lo = qi * tq
        qk = slice(hh * HEAD_PAD, (hh + 1) * HEAD_PAD)
        q = q_ref[lo:lo + tq, qk]
        sd = jnp.where(mask, _dot_nt(q, k_ref[lo:lo + tq, qk]), -1e30)
        so = _dot_nt(q, k_ref[0:lo, qk]) if qi > 0 else None
        return sd, so

    items = [(qi, hh) for qi in range(seq // tq) for hh in range(ATTN_HEADS_PER_STEP)]
    nxt = scores(items[0])
    for n, (qi, hh) in enumerate(items):
        lo = qi * tq
        vo = slice(hh * V_HEAD, (hh + 1) * V_HEAD)
        sd, so = nxt
        if n + 1 < len(items):
            nxt = scores(items[n + 1])
        m = jnp.max(sd, axis=-1, keepdims=True)
        if qi > 0:
            m = jnp.maximum(m, jnp.max(so, axis=-1, keepdims=True))
        pd = jnp.exp2(sd - m)
        l = jnp.sum(pd, axis=-1, keepdims=True)
        acc = _dot(pd.astype(BF16), v_ref[lo:lo + tq, vo])
        if qi > 0:
            po = jnp.exp2(so - m)
            l = l + jnp.sum(po, axis=-1, keepdims=True)
            acc = acc + _dot(po.astype(BF16), v_ref[0:lo, vo])
        o_ref[lo:lo + tq, vo] = (acc / l).astype(BF16)


ATTN_HEADS_PER_STEP = LANE // V_HEAD


def _attn_call(q, k, v, batch, seq, tq=256):
    hp = ATTN_HEADS_PER_STEP
    qk_spec = pl.BlockSpec((seq, hp * HEAD_PAD), lambda b, h: (b, h))
    vo_spec = pl.BlockSpec((seq, hp * V_HEAD), lambda b, h: (b, h))
    return pl.pallas_call(
        functools.partial(_attn_kernel, seq=seq, tq=tq),
        grid=(batch, MLA_HEADS // hp),
        in_specs=[qk_spec, qk_spec, vo_spec],
        out_specs=vo_spec,
        out_shape=jax.ShapeDtypeStruct(v.shape, BF16),
        compiler_params=_params(("arbitrary", "arbitrary")),
        name="attn",
    )(q, k, v)


MIX_SUB = 256


def _mixout_kernel(*refs, router):
    if router:
        (rw_ref, o_ref, gates_ref, x_ref, gate_ref, wr_ref, wm_ref, wo_ref,
         g2_ref, shift2_ref, scale2_ref, rwt_ref, rb_ref,
         xo_ref, h2_ref, ti_ref, tp_ref) = refs
    else:
        (rw_ref, o_ref, gates_ref, x_ref, gate_ref, wr_ref, wm_ref, wo_ref,
         g2_ref, shift2_ref, scale2_ref, xo_ref, h2_ref) = refs
    n = x_ref.shape[0]
    subs = [slice(s, s + MIX_SUB) for s in range(0, n, MIX_SUB)]
    yab = [(_dot(rw_ref[r, :], wr_ref[...]), _dot(o_ref[r, :], wm_ref[...])) for r in subs]
    merged = [gates_ref[r, 0:D_MODEL].astype(F32) * ya + gates_ref[r, D_MODEL:].astype(F32) * yb
              for r, (ya, yb) in zip(subs, yab)]
    mixed = [_dot(m.astype(BF16), wo_ref[...]) for m in merged]
    h2s = []
    for r, mx in zip(subs, mixed):
        xn = x_ref[r, :] + gate_ref[...] * mx
        xo_ref[r, :] = xn
        h2s.append(_norm_mod(xn, g2_ref[...], scale2_ref[...], shift2_ref[...]))
        h2_ref[r, :] = h2s[-1].astype(h2_ref.dtype)
    if router:
        h2 = jnp.concatenate(h2s, axis=0)
        hh, hl = _split2(h2)
        wh, wl = _split2(rwt_ref[...])
        logits = _dot_nt(wh, hh) + _dot_nt(wl, hh) + _dot_nt(wh, hl) + rb_ref[...]
        e_i = lax.broadcasted_iota(jnp.int32, logits.shape, 0)
        m1 = jnp.max(logits, axis=0, keepdims=True)
        i1 = jnp.min(jnp.where(logits == m1, e_i, N_EXPERTS), axis=0, keepdims=True)
        rest = jnp.where(e_i == i1, -jnp.inf, logits)
        m2 = jnp.max(rest, axis=0, keepdims=True)
        i2 = jnp.min(jnp.where(rest == m2, e_i, N_EXPERTS), axis=0, keepdims=True)
        e2 = jnp.exp(m2 - m1)
        den = 1.0 + e2
        ti_ref[0:1, :] = i1
        ti_ref[1:2, :] = i2
        tp_ref[0:1, :] = 1.0 / den
        tp_ref[1:2, :] = e2 / den


def _mixout_call(rw, o, gates, x2, gate, wr, wm, wo, g2, shift2, scale2, seq, router=None, tm=512):
    t = x2.shape[0]
    tiles_per_seq = seq // tm
    bmap = lambda i: (i // tiles_per_seq, 0, 0)
    full = lambda shape: pl.BlockSpec(shape, lambda i: (0, 0))
    row = lambda w: pl.BlockSpec((tm, w), lambda i: (i, 0))
    mod = pl.BlockSpec((None, 1, D_MODEL), bmap)
    in_specs = [row(RW), row(MLA_HEADS * V_HEAD), row(GATE_W), row(D_MODEL), mod,
                full(wr.shape), full(wm.shape), full(wo.shape), full((1, D_MODEL)), mod, mod]
    args = [rw, o, gates, x2, gate, wr, wm, wo, g2, shift2, scale2]
    out_specs = [row(D_MODEL), row(D_MODEL)]
    out_shape = [jax.ShapeDtypeStruct((t, D_MODEL), F32),
                 jax.ShapeDtypeStruct((t, D_MODEL), BF16 if router is None else F32)]
    if router is not None:
        in_specs += [full((N_EXPERTS, D_MODEL)), full((N_EXPERTS, 1))]
        args += list(router)
        out_specs += [pl.BlockSpec((2, tm), lambda i: (0, i))] * 2
        out_shape += [jax.ShapeDtypeStruct((2, t), jnp.int32), jax.ShapeDtypeStruct((2, t), F32)]
    return pl.pallas_call(
        functools.partial(_mixout_kernel, router=router is not None),
        grid=(t // tm,),
        in_specs=in_specs,
        out_specs=out_specs,
        out_shape=out_shape,
        compiler_params=_params(("arbitrary",)),
        name="mixout",
    )(*args)


def _swiglu_step(h, w1_ref, w3_ref, w2_ref, acc_ref, finish):
    f = pl.program_id(1)

    @pl.when(f == 0)
    def _():
        acc_ref[...] = jnp.zeros_like(acc_ref)

    a1 = _dot(h, w1_ref[...])
    a3 = _dot(h, w3_ref[...])
    act = (a1 * _sigmoid(a1)) * a3
    acc_ref[...] += _dot(act.astype(BF16), w2_ref[...].astype(BF16))

    @pl.when(f == pl.num_programs(1) - 1)
    def _():
        finish(acc_ref[...])


def _rms(x, g):
    return x * lax.rsqrt(jnp.mean(x * x, axis=-1, keepdims=True) + NORM_EPS) * g


def _ffn_kernel(h_ref, x_ref, gate_ref, gf_ref, w1_ref, w3_ref, w2_ref, o_ref, *, final, tf):
    h = h_ref[...]
    ff = w1_ref.shape[1]
    ups = [(_dot(h, w1_ref[:, s:s + tf]), _dot(h, w3_ref[:, s:s + tf])) for s in range(0, ff, tf)]
    acc = None
    for s, (a1, a3) in zip(range(0, ff, tf), ups):
        act = ((a1 * _sigmoid(a1)) * a3).astype(BF16)
        part = _dot(act, w2_ref[s:s + tf, :])
        acc = part if acc is None else acc + part
    xn = x_ref[...] + gate_ref[...] * acc
    o_ref[...] = _rms(xn, gf_ref[...]) if final else xn


def _ffn_call(h, x2, gate, gf, w1, w3, w2, seq, final, tm=512, tf=1408):
    r = h.shape[0]
    tiles_per_seq = seq // tm
    row = pl.BlockSpec((tm, D_MODEL), lambda i: (i, 0))
    resident = lambda a: pl.BlockSpec(a.shape, lambda i: (0, 0), pipeline_mode=pl.Buffered(1))
    return pl.pallas_call(
        functools.partial(_ffn_kernel, final=final, tf=tf),
        grid=(r // tm,),
        in_specs=[
            row, row,
            pl.BlockSpec((None, 1, D_MODEL), lambda i: (i // tiles_per_seq, 0, 0)),
            pl.BlockSpec((1, D_MODEL), lambda i: (0, 0)),
            resident(w1), resident(w3), resident(w2),
        ],
        out_specs=row,
        out_shape=jax.ShapeDtypeStruct((r, D_MODEL), F32),
        compiler_params=_params(("arbitrary",)),
        name="ffn",
    )(h, x2, gate, gf, w1, w3, w2)


MOE_TM = 512
ROW_TILE = (D_MODEL // LANE, LANE)


def _route(top_i, tm):
    t = top_i.shape[1]
    e_flat = top_i.reshape(-1)
    oh = (e_flat[:, None] == jnp.arange(N_EXPERTS, dtype=jnp.int32)[None, :]).astype(jnp.int32)
    csum = jnp.cumsum(oh, axis=0)
    rank = jnp.sum(csum * oh, axis=1) - 1
    counts = csum[-1]
    padded = ((counts + tm - 1) // tm) * tm
    ends = jnp.cumsum(padded)
    pos = jnp.sum(oh * (ends - padded)[None, :], axis=1) + rank
    n_tiles = (2 * t) // tm + N_EXPERTS
    starts = jnp.arange(n_tiles, dtype=jnp.int32) * tm
    tile_e = jnp.sum((starts[:, None] >= ends[None, :]).astype(jnp.int32), axis=1)
    return (pos.astype(jnp.int32), ends.astype(jnp.int32), jnp.minimum(tile_e, N_EXPERTS - 1),
            (ends[-1:] // tm).astype(jnp.int32))


def _dispatch_kernel(pos_ref, ends_ref, h_ref, xs_ref, zero_ref, sem, *, t, td, tm):
    i = pl.program_id(0)
    base = i * td

    @pl.when(i == 0)
    def _():
        zero_ref[...] = jnp.zeros_like(zero_ref)
        n_tiles = xs_ref.shape[0] // tm
        fills = []
        for e in range(N_EXPERTS):
            start = ends_ref[e - 1] if e else 0
            fills.append((ends_ref[e] > start, pl.multiple_of(ends_ref[e] - tm, tm)))
            fills.append((ends_ref[N_EXPERTS - 1] // tm + e < n_tiles,
                          pl.multiple_of(ends_ref[N_EXPERTS - 1] + e * tm, tm)))
        for go, row0 in fills:
            @pl.when(go)
            def _():
                pltpu.make_async_copy(zero_ref, xs_ref.at[pl.ds(row0, tm)], sem).start()
        for go, row0 in fills:
            @pl.when(go)
            def _():
                pltpu.make_async_copy(zero_ref, xs_ref.at[pl.ds(row0, tm)], sem).wait()

    for r in range(td):
        for k in range(2):
            dst = pos_ref[k * t + base + r]
            pltpu.make_async_copy(h_ref.at[pl.ds(r, 1)], xs_ref.at[pl.ds(dst, 1)], sem).start(priority=k)
    for k in range(2):
        pltpu.make_async_copy(h_ref, xs_ref.at[pl.ds(0, td)], sem).wait()


def _dispatch_call(pos, ends, h, n_rows, tm, td=512):
    t = h.shape[0]
    return pl.pallas_call(
        functools.partial(_dispatch_kernel, t=t, td=td, tm=tm),
        grid_spec=pltpu.PrefetchScalarGridSpec(
            num_scalar_prefetch=2, grid=(t // td,),
            in_specs=[pl.BlockSpec((td, D_MODEL), lambda i, pos, ends: (i, 0))],
            out_specs=pl.BlockSpec(memory_space=pl.ANY),
            scratch_shapes=[pltpu.VMEM((tm, D_MODEL), h.dtype), pltpu.SemaphoreType.DMA(())]),
        out_shape=jax.ShapeDtypeStruct((n_rows, D_MODEL), h.dtype),
        compiler_params=_params(("arbitrary",)),
        name="dispatch",
    )(pos, ends, h)


def _ffn_routed_kernel(te_ref, nu_ref, x_ref, w1_ref, w3_ref, w2_ref, y_ref, acc_ref):
    del te_ref
    used = pl.program_id(0) < nu_ref[0]

    def finish(acc):
        y_ref[...] = acc.reshape(y_ref.shape)

    @pl.when(used)
    def _():
        _swiglu_step(x_ref[...].astype(BF16), w1_ref, w3_ref, w2_ref, acc_ref, finish)

    @pl.when(jnp.logical_not(used))
    def _():
        y_ref[...] = jnp.zeros_like(y_ref)


def _ffn_routed_call(tile_e, n_used, xs, w1, w3, w2, tm=MOE_TM, tf=1792):
    r = xs.shape[0]
    ff = w1.shape[2]
    return pl.pallas_call(
        _ffn_routed_kernel,
        grid_spec=pltpu.PrefetchScalarGridSpec(
            num_scalar_prefetch=2, grid=(r // tm, ff // tf),
            in_specs=[
                pl.BlockSpec((tm, D_MODEL), lambda i, f, te, nu: (i, 0)),
                pl.BlockSpec((None, D_MODEL, tf), lambda i, f, te, nu: (te[i], 0, f)),
                pl.BlockSpec((None, D_MODEL, tf), lambda i, f, te, nu: (te[i], 0, f)),
                pl.BlockSpec((None, tf, D_MODEL), lambda i, f, te, nu: (te[i], f, 0)),
            ],
            out_specs=pl.BlockSpec((tm,) + ROW_TILE, lambda i, f, te, nu: (i, 0, 0)),
            scratch_shapes=[pltpu.VMEM((tm, D_MODEL), F32)]),
        out_shape=jax.ShapeDtypeStruct((r,) + ROW_TILE, F32),
        compiler_params=_params(("arbitrary", "arbitrary")),
        name="ffn_routed",
    )(tile_e, n_used, xs, w1, w3, w2)


def _combine_kernel(pos_ref, x_ref, p_ref, gate_ref, gf_ref, y_ref, o_ref, buf, sem, *, t, tc, final):
    i = pl.program_id(0)

    def row_copy(step, slot, r, k):
        src = pos_ref[k * t + step * tc + r]
        return pltpu.make_async_copy(y_ref.at[pl.ds(src, 1)], buf.at[slot, k, pl.ds(r, 1)], sem.at[slot])

    def issue(step, slot):
        for r in range(tc):
            for k in range(2):
                row_copy(step, slot, r, k).start(priority=k)

    @pl.when(i == 0)
    def _():
        issue(0, 0)

    @pl.when(i + 1 < pl.num_programs(0))
    def _():
        issue(i + 1, (i + 1) % 2)

    slot = i % 2
    for k in range(2):
        pltpu.make_async_copy(y_ref.at[pl.ds(0, tc)], buf.at[slot, k], sem.at[slot]).wait()
    p = p_ref[...]
    y0 = buf[slot, 0].reshape(tc, D_MODEL)
    y1 = buf[slot, 1].reshape(tc, D_MODEL)
    xn = x_ref[...] + gate_ref[...] * (p[:, 0:1] * y0 + p[:, 1:2] * y1)
    o_ref[...] = _rms(xn, gf_ref[...]) if final else xn


def _combine_call(pos, x2, probs, gate, gf, y, seq, final, tc=256):
    t = x2.shape[0]
    tiles_per_seq = seq // tc
    row = pl.BlockSpec((tc, D_MODEL), lambda i, pos: (i, 0))
    return pl.pallas_call(
        functools.partial(_combine_kernel, t=t, tc=tc, final=final),
        grid_spec=pltpu.PrefetchScalarGridSpec(
            num_scalar_prefetch=1, grid=(t // tc,),
            in_specs=[row, pl.BlockSpec((tc, 2), lambda i, pos: (i, 0)),
                      pl.BlockSpec((None, 1, D_MODEL), lambda i, pos: (i // tiles_per_seq, 0, 0)),
                      pl.BlockSpec((1, D_MODEL), lambda i, pos: (0, 0)),
                      pl.BlockSpec(memory_space=pl.ANY)],
            out_specs=row,
            scratch_shapes=[pltpu.VMEM((2, 2, tc) + ROW_TILE, F32), pltpu.SemaphoreType.DMA((2,))]),
        out_shape=jax.ShapeDtypeStruct((t, D_MODEL), F32),
        compiler_params=_params(("arbitrary",)),
        name="combine",
    )(pos, x2, probs, gate, gf, y)


def _layout_w_in_kernel(w_ref, o_ref):
    half = QK_ROPE // 2
    o_ref[...] = jnp.zeros_like(o_ref)

    def put(dst, src, width, sign=1.0):
        o_ref[:, dst:dst + width] = (sign * w_ref[:, src:src + width]).astype(BF16)

    o = 3 * RW
    put(0, 0, o + DECAY_LORA)
    put(o + LANE, o + DECAY_LORA, ICLR_LORA)
    put(o + 2 * LANE, o + DECAY_LORA + ICLR_LORA, GATE_LORA)
    src = o + DECAY_LORA + ICLR_LORA + GATE_LORA
    put(PS_W, src, Q_LORA + KV_LORA)
    kr = src + Q_LORA + KV_LORA
    dst = PS_W + Q_LORA + KV_LORA
    put(dst + QK_NOPE, kr, QK_ROPE)
    put(dst + LANE + QK_NOPE, kr + half, half, -1.0)
    put(dst + LANE + QK_NOPE + half, kr, half)
    put(PS_W + MLA_W, kr + QK_ROPE, GATE_W)


def _layout_w_in(w, layer):
    _, k, n = w.shape
    return pl.pallas_call(
        _layout_w_in_kernel,
        grid=(k // 256,),
        in_specs=[pl.BlockSpec((None, 256, n), lambda i: (layer, i, 0))],
        out_specs=pl.BlockSpec((256, PS_W + MLA_W + GATE_W), lambda i: (i, 0)),
        out_shape=jax.ShapeDtypeStruct((k, PS_W + MLA_W + GATE_W), BF16),
        compiler_params=_params(("arbitrary",)),
        name="layout_w_in",
    )(w)


def _layout_mu(mu):
    z = jnp.zeros((64,), mu.dtype)
    o = 3 * RW
    return jnp.concatenate([mu[:o + 64], z, mu[o + 64:o + 128], z, mu[o + 128:]])[None, :]


def _layout_wq(w_qb):
    hd = QK_NOPE + QK_ROPE
    w = w_qb.reshape(Q_LORA, MLA_HEADS, hd)
    half = QK_ROPE // 2
    x1, x2 = w[:, :, QK_NOPE:QK_NOPE + half], w[:, :, QK_NOPE + half:]
    zn = jnp.zeros((Q_LORA, MLA_HEADS, QK_NOPE), w.dtype)
    zp = jnp.zeros((Q_LORA, MLA_HEADS, HEAD_PAD - hd), w.dtype)
    wq = jnp.concatenate([w, zp], axis=2).reshape(Q_LORA, -1)
    wqs = jnp.concatenate([zn, -x2, x1, zp], axis=2).reshape(Q_LORA, -1)
    return wq.astype(BF16), wqs.astype(BF16)


def _layout_wkv(w_kvb):
    w = w_kvb.reshape(KV_LORA, MLA_HEADS, QK_NOPE + V_HEAD)
    z = jnp.zeros((KV_LORA, MLA_HEADS, HEAD_PAD - QK_NOPE), w.dtype)
    wk = jnp.concatenate([w[:, :, :QK_NOPE], z], axis=2).reshape(KV_LORA, -1)
    wv = w[:, :, QK_NOPE:].reshape(KV_LORA, -1)
    return wk.astype(BF16), wv.astype(BF16)


def _rope_freq_row():
    inv_freq = ROPE_THETA ** (-jnp.arange(0, QK_ROPE, 2, dtype=F32) / QK_ROPE)
    z = jnp.zeros((QK_NOPE,), F32)
    zp = jnp.zeros((LANE - QK_NOPE - QK_ROPE,), F32)
    return jnp.concatenate([z, inv_freq, inv_freq, zp])[None, :]


def kernel(x, c, positions, ada_w, ada_b, norm_mix, norm_ffn, norm_final, w_in, tshift_mu, w0, w_up, a0, a_up, g_up, k_k, k_a, r_k, lnx_g, lnx_b, rwkv_out, q_norm, w_qb, kv_norm, w_kvb, mla_out, w_o, ffn_w1, ffn_w3, ffn_w2, router_w, router_b, moe_w1, moe_w3, moe_w2):
    batch, seq, _ = x.shape
    t = batch * seq
    mods = _ada_call(c, ada_w.reshape(2 * DEPTH, D_MODEL, 3 * D_MODEL),
                     ada_b.reshape(2 * DEPTH, 1, 3 * D_MODEL))
    mods = mods.reshape(DEPTH, 2, batch, 3, 1, D_MODEL)
    pos_col = positions.reshape(t, 1)
    invf = _rope_freq_row()
    x2 = x.reshape(t, D_MODEL)
    row = lambda v: v.reshape(1, -1)

    for l in range(DEPTH):
        shift, scale, gate = (mods[l, 0, :, j] for j in range(3))
        ps, mla, gates = _inproj_call(x2, shift, scale, row(norm_mix[l]), _layout_w_in(w_in, l),
                                      _layout_mu(tshift_mu[l]), seq)
        vecs = [row(v[l]) for v in (w0, a0, k_k, k_a, r_k, lnx_g, lnx_b)]
        lora_pad = lambda w: jnp.pad(w, ((0, LANE - w.shape[0]), (0, 0)))
        rw = _rwkv_call(ps, vecs, lora_pad(w_up[l]), lora_pad(a_up[l]), g_up[l], batch, seq)
        wq, wqs = _layout_wq(w_qb[l])
        wk, wv = _layout_wkv(w_kvb[l])
        q, k, v = _mlaprep_call(mla, pos_col, invf, row(q_norm[l]), row(kv_norm[l]), wq, wqs, wk, wv)
        o = _attn_call(q, k, v, batch, seq)

        shift2, scale2, gate2 = (mods[l, 1, :, j] for j in range(3))
        moe = l % 2 == 1
        router = (router_w[l // 2].T, router_b[l // 2].reshape(N_EXPERTS, 1)) if moe else None
        outs = _mixout_call(rw, o, gates, x2, gate, rwkv_out[l].astype(BF16), mla_out[l].astype(BF16),
                            w_o[l].astype(BF16), row(norm_ffn[l]), shift2, scale2, seq, router=router)
        final = l == DEPTH - 1
        if moe:
            x2, h2, top_i, top_p = outs
            pos, ends, tile_e, n_used = _route(top_i, MOE_TM)
            xs = _dispatch_call(pos, ends, h2, 2 * t + N_EXPERTS * MOE_TM, MOE_TM)
            y = _ffn_routed_call(tile_e, n_used, xs, moe_w1[l // 2].astype(BF16),
                                 moe_w3[l // 2].astype(BF16), moe_w2[l // 2])
            x2 = _combine_call(pos, x2, top_p.T, gate2, row(norm_final), y, seq, final)
        else:
            x2, h2 = outs
            x2 = _ffn_call(h2, x2, gate2, row(norm_final), ffn_w1[l // 2].astype(BF16),
                           ffn_w3[l // 2].astype(BF16), ffn_w2[l // 2].astype(BF16), seq, final)
    return x2.reshape(batch, seq, D_MODEL)
```

```python
import functools
import math

import jax
import jax.numpy as jnp
from jax import lax
from jax.experimental import pallas as pl
from jax.experimental.pallas import tpu as pltpu

F32 = jnp.float32
BF16 = jnp.bfloat16

D_MODEL = 1024
DEPTH = 2
CHUNK = 64
NORM_EPS = 1e-6
RWKV_HEADS = 8
RWKV_HEAD_DIM = 64
RW = RWKV_HEADS * RWKV_HEAD_DIM
DECAY_LORA = 64
ICLR_LORA = 64
GATE_LORA = 128
DECAY_SCALE = math.exp(-0.5)
LNX_EPS = 64e-5
MLA_HEADS = 8
Q_LORA = 384
KV_LORA = 256
QK_NOPE = 64
QK_ROPE = 32
V_HEAD = 64
ROPE_THETA = 10000.0
LOG2E = math.log2(math.e)
N_EXPERTS = 8

LANE = 128
HEAD_PAD = 128
PS_W = 3 * RW + 3 * LANE
MLA_W = Q_LORA + KV_LORA + 2 * LANE
GATE_W = 2 * D_MODEL
HALF = 256
VMEM_LIMIT = 56 * 1024 * 1024


def _sigmoid(x):
    return 1.0 / (1.0 + jnp.exp(-x))


def _dot(a, b):
    return jnp.dot(a, b, preferred_element_type=F32)


def _dot_nt(a, b):
    return lax.dot_general(a, b, (((1,), (1,)), ((), ())), preferred_element_type=F32)


def _split2(x):
    hi = x.astype(BF16)
    lo = (x - hi.astype(F32)).astype(BF16)
    return hi, lo


def _dot3(a, b):
    ah, al = _split2(a)
    bh, bl = _split2(b)
    return _dot(ah, bh) + _dot(al, bh) + _dot(ah, bl)


def _params(sem):
    return pltpu.CompilerParams(dimension_semantics=sem, vmem_limit_bytes=VMEM_LIMIT)


def _ada_kernel(c_ref, w_ref, b_ref, o_ref):
    c = c_ref[...]
    o_ref[...] = _dot((c * _sigmoid(c)).astype(BF16), w_ref[...].astype(BF16)) + b_ref[...]


def _ada_call(c, ada_w, ada_b):
    n = ada_w.shape[0]
    nb = c.shape[0]
    return pl.pallas_call(
        _ada_kernel,
        grid=(n, 3),
        in_specs=[
            pl.BlockSpec((nb, D_MODEL), lambda i, j: (0, 0)),
            pl.BlockSpec((None, D_MODEL, D_MODEL), lambda i, j: (i, 0, j)),
            pl.BlockSpec((None, 1, D_MODEL), lambda i, j: (i, 0, j)),
        ],
        out_specs=pl.BlockSpec((None, nb, D_MODEL), lambda i, j: (i, 0, j)),
        out_shape=jax.ShapeDtypeStruct((n, nb, 3 * D_MODEL), F32),
        compiler_params=_params(("arbitrary", "arbitrary")),
        name="ada",
    )(c, ada_w, ada_b)


def _norm_mod(x, g, scale, shift):
    var = jnp.mean(x * x, axis=-1, keepdims=True)
    return (x * lax.rsqrt(var + NORM_EPS) * g) * (1.0 + scale) + shift


def _inproj_kernel(x_ref, shift_ref, scale_ref, g_ref, w_ref, mu_ref, side_ref,
                   ps_ref, mla_ref, gates_ref, side_out_ref, carry_ref, *, tiles_per_seq, tm):
    i = pl.program_id(0)
    side_out_ref[...] = side_ref[...].astype(BF16)

    @pl.when(i % tiles_per_seq == 0)
    def _():
        carry_ref[...] = jnp.zeros_like(carry_ref)

    carry = carry_ref[0:1, :]
    row = lax.broadcasted_iota(jnp.int32, (INPROJ_SUB, PS_W), 0)
    for s in range(0, tm, INPROJ_SUB):
        r = slice(s, s + INPROJ_SUB)
        h = _norm_mod(x_ref[r, :], g_ref[...], scale_ref[...], shift_ref[...]).astype(BF16)
        p = _dot(h, w_ref[:, 0:PS_W])
        prev = jnp.where(row == 0, carry, pltpu.roll(p, 1, axis=0))
        carry = p[INPROJ_SUB - 1:INPROJ_SUB, :]
        ps_ref[r, :] = p + (prev - p) * mu_ref[...]
        mla_ref[r, :] = _dot(h, w_ref[:, PS_W:PS_W + MLA_W])
        gates_ref[r, :] = _sigmoid(_dot(h, w_ref[:, PS_W + MLA_W:])).astype(BF16)
    carry_ref[0:1, :] = carry


INPROJ_SUB = 256


def _inproj_call(x2, shift, scale, g, w_pad, mu_pad, side, seq, tm=512):
    t = x2.shape[0]
    steps = t // tm
    tiles_per_seq = seq // tm
    bmap = lambda i: (i // tiles_per_seq, 0, 0)
    wtot = PS_W + MLA_W + GATE_W
    side_spec = pl.BlockSpec((side.shape[0] // steps, side.shape[1]), lambda i: (i, 0))
    return pl.pallas_call(
        functools.partial(_inproj_kernel, tiles_per_seq=tiles_per_seq, tm=tm),
        grid=(steps,),
        in_specs=[
            pl.BlockSpec((tm, D_MODEL), lambda i: (i, 0)),
            pl.BlockSpec((None, 1, D_MODEL), bmap),
            pl.BlockSpec((None, 1, D_MODEL), bmap),
            pl.BlockSpec((1, D_MODEL), lambda i: (0, 0)),
            pl.BlockSpec((D_MODEL, wtot), lambda i: (0, 0), pipeline_mode=pl.Buffered(1)),
            pl.BlockSpec((1, PS_W), lambda i: (0, 0)),
            side_spec,
        ],
        out_specs=[
            pl.BlockSpec((tm, PS_W), lambda i: (i, 0)),
            pl.BlockSpec((tm, MLA_W), lambda i: (i, 0)),
            pl.BlockSpec((tm, GATE_W), lambda i: (i, 0)),
            side_spec,
        ],
        out_shape=[
            jax.ShapeDtypeStruct((t, PS_W), F32),
            jax.ShapeDtypeStruct((t, MLA_W), F32),
            jax.ShapeDtypeStruct((t, GATE_W), BF16),
            jax.ShapeDtypeStruct(side.shape, BF16),
        ],
        scratch_shapes=[pltpu.VMEM((8, PS_W), F32)],
        compiler_params=_params(("arbitrary",)),
        name="inproj",
    )(x2, shift, scale, g, w_pad, mu_pad, side)


def _blockdiag(xb, bdmask):
    return jnp.tile(xb, (HALF // CHUNK, 1)) * bdmask


def _bd_mm(dot, a, b, bdmask, passes):
    if passes == 1:
        return dot(a.astype(BF16), _blockdiag(b.astype(BF16), bdmask))
    ah, al = _split2(a)
    bh, bl = _split2(b)
    r = dot(jnp.concatenate([ah, al], axis=0), _blockdiag(bh, bdmask))
    n = a.shape[0]
    return r[:n] + r[n:] + dot(ah, _blockdiag(bl, bdmask))


def _bd_nn(a, b, bdmask, passes):
    return _bd_mm(_dot, a, b, bdmask, passes)


def _bd_nt(a, b, bdmask, passes):
    return _bd_mm(_dot_nt, a, b, bdmask, passes)


def _bd_tn(a, b, bdmask_f, passes):
    full = _dot3(a.T, b) if passes == 3 else _dot(a.T.astype(BF16), b.astype(BF16))
    return jnp.sum((full * bdmask_f).reshape(HALF // CHUNK, CHUNK, HALF), axis=0)


P_SCORE = 1
P_TINV = 1
P_APPLY = 1
P_GH = 1
P_STATE = 1


def _halves(fn, *xs):
    outs = [fn(*[x[:, h * HALF:(h + 1) * HALF] for x in xs]) for h in range(RW // HALF)]
    return jnp.concatenate(outs, axis=1)


def _rwkv_kernel(ps_ref, w0_ref, a0_ref, kk_ref, ka_ref, rk_ref, lg_ref, lb_ref,
                 wup_ref, aup_ref, gup_ref, out_ref, s_ref, *, nc):
    @pl.when(pl.program_id(1) == 0)
    def _():
        s_ref[...] = jnp.zeros_like(s_ref)

    C = CHUNK
    r_i = lax.broadcasted_iota(jnp.int32, (HALF, HALF), 0)
    c_i = lax.broadcasted_iota(jnp.int32, (HALF, HALF), 1)
    bdsame = (r_i // C) == (c_i // C)
    bdmask = bdsame.astype(BF16)
    bdmask_f = bdsame.astype(F32)
    t_i = lax.broadcasted_iota(jnp.int32, (C, HALF), 0)
    i_i = lax.broadcasted_iota(jnp.int32, (C, HALF), 1) % C
    strict = i_i < t_i
    incl = i_i <= t_i
    eye_l = (i_i == t_i).astype(F32)
    n = nc * C
    tr = lax.broadcasted_iota(jnp.int32, (n, n), 0)
    tc = lax.broadcasted_iota(jnp.int32, (n, n), 1)
    tril = ((tc <= tr) & (tc // C == tr // C)).astype(BF16)

    def headsum(x, split=False):
        def one(xh):
            if not split:
                return _dot(xh.astype(BF16), bdmask)
            hi, lo = _split2(xh)
            return _dot(hi, bdmask) + _dot(lo, bdmask)
        return _halves(one, x)

    def cumsum_rows(x):
        hi = x.astype(BF16)
        r1 = x - hi.astype(F32)
        mid = r1.astype(BF16)
        lo = (r1 - mid.astype(F32)).astype(BF16)
        return _dot(tril, hi) + _dot(tril, mid) + _dot(tril, lo)

    p_r = ps_ref[:, 0:RW]
    p_k = ps_ref[:, RW:2 * RW]
    p_v = ps_ref[:, 2 * RW:3 * RW]
    p_w = ps_ref[:, 3 * RW:3 * RW + LANE]
    p_a = ps_ref[:, 3 * RW + LANE:3 * RW + 2 * LANE]
    p_g = ps_ref[:, 3 * RW + 2 * LANE:3 * RW + 3 * LANE]
    logw = -DECAY_SCALE * _sigmoid(w0_ref[...] + _dot3(jnp.tanh(p_w), wup_ref[...]))
    iclr = _sigmoid(a0_ref[...] + _dot(p_a.astype(BF16), aup_ref[...].astype(BF16)))
    gate = _dot(_sigmoid(p_g).astype(BF16), gup_ref[...].astype(BF16))
    kk = p_k * kk_ref[...]
    k2 = p_k * (1.0 + (iclr - 1.0) * ka_ref[...])
    sums = headsum(jnp.concatenate([kk * kk, p_r * k2 * rk_ref[...]], axis=0))
    kk = kk * lax.rsqrt(sums[:n] + 1e-12)
    bonus = sums[n:] * p_v
    b_v = kk * iclr
    cum = cumsum_rows(logw)
    cum_c = jnp.concatenate(
        [jnp.broadcast_to(cum[(c + 1) * C - 1:(c + 1) * C, :], (C, RW)) for c in range(nc)], axis=0)
    e_neg = jnp.exp(-cum)
    e_rem = jnp.exp(cum_c - cum)
    a_all = -kk * jnp.exp(cum - logw)
    b_all = b_v * e_neg
    k_all = k2 * e_neg
    r_all = p_r * jnp.exp(cum)
    bg_all = b_v * e_rem
    kg_all = k2 * e_rem
    gam_all = jnp.exp(cum_c)

    nh = RW // HALF
    units = [(c, h) for c in range(nc) for h in range(nh)]

    def cut(x, u):
        c, h = u
        return x[c * C:(c + 1) * C, h * HALF:(h + 1) * HALF]

    each = lambda fn, *lists: [fn(*xs) for xs in zip(*lists)]
    stack = lambda x, y: jnp.concatenate([x, y], axis=0)
    top = lambda x: x[:C]
    bot = lambda x: x[C:]
    nn = lambda p: (lambda a, b: _bd_nn(a, b, bdmask, p))

    a_t, b_t, k_t, r_t, b_g, k_g, v_u = ([cut(x, u) for u in units]
                                         for x in (a_all, b_all, k_all, r_all, bg_all, kg_all, p_v))
    gam = [gam_all[c * C:c * C + 1, h * HALF:(h + 1) * HALF] for c, h in units]
    ar = each(stack, a_t, r_t)
    sb = each(lambda x, y: _bd_nt(x, y, bdmask, P_SCORE), ar, b_t)
    sk = each(lambda x, y: _bd_nt(x, y, bdmask, P_SCORE), ar, k_t)
    l_ab = [jnp.where(strict, top(x), 0.0) for x in sb]
    q_b = [jnp.where(incl, bot(x), 0.0) for x in sb]
    m_ak = [jnp.where(strict, top(x), 0.0) for x in sk]
    q_k = [jnp.where(incl, bot(x), 0.0) for x in sk]
    t_m = [eye_l + x for x in l_ab]
    pw = each(nn(P_TINV), l_ab, l_ab)
    for _ in range(4):
        tp = each(nn(P_TINV), each(stack, t_m, pw), pw)
        t_m = each(lambda t, x: t + top(x), t_m, tp)
        pw = [bot(x) for x in tp]
    t_m = each(lambda t, x: t + x, t_m, each(nn(P_TINV), t_m, pw))
    mq = each(nn(P_APPLY), each(stack, m_ak, q_k), v_u)
    a_h = each(nn(P_APPLY), t_m, a_t)
    u0 = each(nn(P_APPLY), t_m, [top(x) for x in mq])
    r_h = each(lambda r, x: r + x, r_t, each(nn(P_APPLY), q_b, a_h))
    y0 = each(lambda m, x: bot(m) + x, mq, each(nn(P_APPLY), q_b, u0))
    g_l = each(lambda a, b, g: _bd_tn(a, b, bdmask_f, P_GH) + eye_l * g, a_h, b_g, gam)
    h_l = each(lambda u, v, b, k: _bd_tn(stack(u, v), stack(b, k), bdmask_f, P_GH), u0, v_u, b_g, k_g)

    s_all = s_ref[...]
    s_cur = [s_all[:, h * HALF:(h + 1) * HALF] for h in range(nh)]
    ys = []
    for c in range(nc):
        idx = [c * nh + h for h in range(nh)]
        yc = [_bd_nt(r_h[i], s, bdmask, P_STATE) + y0[i] for i, s in zip(idx, s_cur)]
        s_cur = [_bd_nn(s, g_l[i], bdmask, P_STATE) + h_l[i] for i, s in zip(idx, s_cur)]
        ys.append(jnp.concatenate(yc, axis=1))
    s_ref[...] = jnp.concatenate(s_cur, axis=1)
    y = jnp.concatenate(ys, axis=0)
    inv_n = 1.0 / RWKV_HEAD_DIM
    mu = headsum(y, split=True) * inv_n
    d = y - mu
    var = headsum(d * d) * inv_n
    yn = d * lax.rsqrt(var + LNX_EPS) * lg_ref[...] + lb_ref[...]
    out_ref[...] = ((yn + bonus) * gate).astype(BF16)


def _rwkv_call(ps, vecs, wup, aup, gup, batch, seq, nc=4):
    t = ps.shape[0]
    rows = nc * CHUNK
    steps = seq // rows
    vec_spec = pl.BlockSpec((1, RW), lambda b, j: (0, 0))
    lora_spec = pl.BlockSpec((LANE, RW), lambda b, j: (0, 0))
    return pl.pallas_call(
        functools.partial(_rwkv_kernel, nc=nc),
        grid=(batch, steps),
        in_specs=[pl.BlockSpec((rows, PS_W), lambda b, j: (b * steps + j, 0))]
                 + [vec_spec] * 7 + [lora_spec] * 3,
        out_specs=pl.BlockSpec((rows, RW), lambda b, j: (b * steps + j, 0)),
        out_shape=jax.ShapeDtypeStruct((t, RW), BF16),
        scratch_shapes=[pltpu.VMEM((CHUNK, RW), F32)],
        compiler_params=_params(("arbitrary", "arbitrary")),
        name="rwkv",
    )(ps, *vecs, wup, aup, gup)


def _mlaprep_kernel(mla_ref, pos_ref, invf_ref, qn_ref, kvn_ref, wq_ref, wqs_ref, wk_ref, wv_ref,
                    q_ref, k_ref, v_ref):
    def rms(x, g):
        return (x * lax.rsqrt(jnp.mean(x * x, axis=-1, keepdims=True) + NORM_EPS) * g).astype(BF16)

    ang = pos_ref[...].astype(F32) * invf_ref[...]
    cos = jnp.cos(ang)
    sin = jnp.sin(ang)
    lane = lax.broadcasted_iota(jnp.int32, cos.shape, 1)
    cos_q = jnp.tile(jnp.where(lane < QK_NOPE, 1.0, cos), (1, MLA_HEADS))
    sin_q = jnp.tile(sin, (1, MLA_HEADS))
    qn = rms(mla_ref[:, 0:Q_LORA], qn_ref[...])
    q = _dot(qn, wq_ref[...]) * cos_q + _dot(qn, wqs_ref[...]) * sin_q
    q_ref[...] = (q * ((QK_NOPE + QK_ROPE) ** -0.5 * LOG2E)).astype(BF16)
    kvn = rms(mla_ref[:, Q_LORA:Q_LORA + KV_LORA], kvn_ref[...])
    o = Q_LORA + KV_LORA
    k_pe = mla_ref[:, o:o + LANE] * cos + mla_ref[:, o + LANE:o + 2 * LANE] * sin
    k_ref[...] = (_dot(kvn, wk_ref[...]) + jnp.tile(k_pe, (1, MLA_HEADS))).astype(BF16)
    v_ref[...] = _dot(kvn, wv_ref[...]).astype(BF16)


def _mlaprep_call(mla, pos, invf, qn, kvn, wq, wqs, wk, wv, tm=512):
    t = mla.shape[0]
    hw = MLA_HEADS * HEAD_PAD
    vw = MLA_HEADS * V_HEAD
    full = lambda shape: pl.BlockSpec(shape, lambda i: (0, 0))
    row = lambda w: pl.BlockSpec((tm, w), lambda i: (i, 0))
    return pl.pallas_call(
        _mlaprep_kernel,
        grid=(t // tm,),
        in_specs=[row(MLA_W), row(1), full((1, LANE)), full((1, Q_LORA)), full((1, KV_LORA)),
                  full((Q_LORA, hw)), full((Q_LORA, hw)), full((KV_LORA, hw)), full((KV_LORA, vw))],
        out_specs=[row(hw), row(hw), row(vw)],
        out_shape=[jax.ShapeDtypeStruct((t, w), BF16) for w in (hw, hw, vw)],
        compiler_params=_params(("arbitrary",)),
        name="mlaprep",
    )(mla, pos, invf, qn, kvn, wq, wqs, wk, wv)


def _attn_kernel(q_ref, k_ref, v_ref, o_ref, *, seq, tq):
    r_i = lax.broadcasted_iota(jnp.int32, (tq, tq), 0) // CHUNK
    c_i = lax.broadcasted_iota(jnp.int32, (tq, tq), 1) // CHUNK
    mask = c_i <= r_i
    def scores(item):
        qi, hh = item
        lo = qi * tq
        qk = slice(hh * HEAD_PAD, (hh + 1) * HEAD_PAD)
        q = q_ref[lo:lo + tq, qk]
        sd = jnp.where(mask, _dot_nt(q, k_ref[lo:lo + tq, qk]), -1e30)
        so = _dot_nt(q, k_ref[0:lo, qk]) if qi > 0 else None
        return sd, so

    items = [(qi, hh) for qi in range(seq // tq) for hh in range(ATTN_HEADS_PER_STEP)]
    nxt = scores(items[0])
    for n, (qi, hh) in enumerate(items):
        lo = qi * tq
        vo = slice(hh * V_HEAD, (hh + 1) * V_HEAD)
        sd, so = nxt
        if n + 1 < len(items):
            nxt = scores(items[n + 1])
        m = jnp.max(sd, axis=-1, keepdims=True)
        if qi > 0:
            m = jnp.maximum(m, jnp.max(so, axis=-1, keepdims=True))
        pd = jnp.exp2(sd - m)
        l = jnp.sum(pd, axis=-1, keepdims=True)
        acc = _dot(pd.astype(BF16), v_ref[lo:lo + tq, vo])
        if qi > 0:
            po = jnp.exp2(so - m)
            l = l + jnp.sum(po, axis=-1, keepdims=True)
            acc = acc + _dot(po.astype(BF16), v_ref[0:lo, vo])
        o_ref[lo:lo + tq, vo] = (acc / l).astype(BF16)


ATTN_HEADS_PER_STEP = LANE // V_HEAD


def _attn_call(q, k, v, batch, seq, tq=256):
    hp = ATTN_HEADS_PER_STEP
    qk_spec = pl.BlockSpec((seq, hp * HEAD_PAD), lambda b, h: (b, h))
    vo_spec = pl.BlockSpec((seq, hp * V_HEAD), lambda b, h: (b, h))
    return pl.pallas_call(
        functools.partial(_attn_kernel, seq=seq, tq=tq),
        grid=(batch, MLA_HEADS // hp),
        in_specs=[qk_spec, qk_spec, vo_spec],
        out_specs=vo_spec,
        out_shape=jax.ShapeDtypeStruct(v.shape, BF16),
        compiler_params=_params(("arbitrary", "arbitrary")),
        name="attn",
    )(q, k, v)


MIX_SUB = 256


def _mixout_kernel(*refs, router):
    if router:
        (rw_ref, o_ref, gates_ref, x_ref, gate_ref, wr_ref, wm_ref, wo_ref,
         g2_ref, shift2_ref, scale2_ref, rwt_ref, rb_ref,
         xo_ref, h2_ref, ti_ref, tp_ref) = refs
    else:
        (rw_ref, o_ref, gates_ref, x_ref, gate_ref, wr_ref, wm_ref, wo_ref,
         g2_ref, shift2_ref, scale2_ref, xo_ref, h2_ref) = refs
    n = x_ref.shape[0]
    subs = [slice(s, s + MIX_SUB) for s in range(0, n, MIX_SUB)]
    yab = [(_dot(rw_ref[r, :], wr_ref[...]), _dot(o_ref[r, :], wm_ref[...])) for r in subs]
    merged = [gates_ref[r, 0:D_MODEL].astype(F32) * ya + gates_ref[r, D_MODEL:].astype(F32) * yb
              for r, (ya, yb) in zip(subs, yab)]
    mixed = [_dot(m.astype(BF16), wo_ref[...]) for m in merged]
    h2s = []
    for r, mx in zip(subs, mixed):
        xn = x_ref[r, :] + gate_ref[...] * mx
        xo_ref[r, :] = xn
        h2s.append(_norm_mod(xn, g2_ref[...], scale2_ref[...], shift2_ref[...]))
        h2_ref[r, :] = h2s[-1].astype(h2_ref.dtype)
    if router:
        h2 = jnp.concatenate(h2s, axis=0)
        hh, hl = _split2(h2)
        wh, wl = _split2(rwt_ref[...])
        logits = _dot_nt(wh, hh) + _dot_nt(wl, hh) + _dot_nt(wh, hl) + rb_ref[...]
        e_i = lax.broadcasted_iota(jnp.int32, logits.shape, 0)
        m1 = jnp.max(logits, axis=0, keepdims=True)
        i1 = jnp.min(jnp.where(logits == m1, e_i, N_EXPERTS), axis=0, keepdims=True)
        rest = jnp.where(e_i == i1, -jnp.inf, logits)
        m2 = jnp.max(rest, axis=0, keepdims=True)
        i2 = jnp.min(jnp.where(rest == m2, e_i, N_EXPERTS), axis=0, keepdims=True)
        e2 = jnp.exp(m2 - m1)
        den = 1.0 + e2
        ti_ref[0:1, :] = i1
        ti_ref[1:2, :] = i2
        tp_ref[0:1, :] = 1.0 / den
        tp_ref[1:2, :] = e2 / den


def _mixout_call(rw, o, gates, x2, gate, wr, wm, wo, g2, shift2, scale2, seq, router=None, tm=512):
    t = x2.shape[0]
    tiles_per_seq = seq // tm
    bmap = lambda i: (i // tiles_per_seq, 0, 0)
    full = lambda shape: pl.BlockSpec(shape, lambda i: (0, 0))
    row = lambda w: pl.BlockSpec((tm, w), lambda i: (i, 0))
    mod = pl.BlockSpec((None, 1, D_MODEL), bmap)
    in_specs = [row(RW), row(MLA_HEADS * V_HEAD), row(GATE_W), row(D_MODEL), mod,
                full(wr.shape), full(wm.shape), full(wo.shape), full((1, D_MODEL)), mod, mod]
    args = [rw, o, gates, x2, gate, wr, wm, wo, g2, shift2, scale2]
    out_specs = [row(D_MODEL), row(D_MODEL)]
    out_shape = [jax.ShapeDtypeStruct((t, D_MODEL), F32),
                 jax.ShapeDtypeStruct((t, D_MODEL), BF16 if router is None else F32)]
    if router is not None:
        in_specs += [full((N_EXPERTS, D_MODEL)), full((N_EXPERTS, 1))]
        args += list(router)
        out_specs += [pl.BlockSpec((2, tm), lambda i: (0, i))] * 2
        out_shape += [jax.ShapeDtypeStruct((2, t), jnp.int32), jax.ShapeDtypeStruct((2, t), F32)]
    return pl.pallas_call(
        functools.partial(_mixout_kernel, router=router is not None),
        grid=(t // tm,),
        in_specs=in_specs,
        out_specs=out_specs,
        out_shape=out_shape,
        compiler_params=_params(("arbitrary",)),
        name="mixout",
    )(*args)


def _swiglu_step(h, w1_ref, w3_ref, w2_ref, acc_ref, finish):
    f = pl.program_id(1)

    @pl.when(f == 0)
    def _():
        acc_ref[...] = jnp.zeros_like(acc_ref)

    a1 = _dot(h, w1_ref[...])
    a3 = _dot(h, w3_ref[...])
    act = (a1 * _sigmoid(a1)) * a3
    acc_ref[...] += _dot(act.astype(BF16), w2_ref[...].astype(BF16))

    @pl.when(f == pl.num_programs(1) - 1)
    def _():
        finish(acc_ref[...])


def _rms(x, g):
    return x * lax.rsqrt(jnp.mean(x * x, axis=-1, keepdims=True) + NORM_EPS) * g


def _ffn_kernel(h_ref, x_ref, gate_ref, gf_ref, w1_ref, w3_ref, w2_ref, o_ref, *, final, tf):
    h = h_ref[...]
    ff = w1_ref.shape[1]
    ups = [(_dot(h, w1_ref[:, s:s + tf]), _dot(h, w3_ref[:, s:s + tf])) for s in range(0, ff, tf)]
    acc = None
    for s, (a1, a3) in zip(range(0, ff, tf), ups):
        act = ((a1 * _sigmoid(a1)) * a3).astype(BF16)
        part = _dot(act, w2_ref[s:s + tf, :])
        acc = part if acc is None else acc + part
    xn = x_ref[...] + gate_ref[...] * acc
    o_ref[...] = _rms(xn, gf_ref[...]) if final else xn


def _ffn_call(h, x2, gate, gf, w1, w3, w2, seq, final, tm=512, tf=1408):
    r = h.shape[0]
    tiles_per_seq = seq // tm
    row = pl.BlockSpec((tm, D_MODEL), lambda i: (i, 0))
    resident = lambda a: pl.BlockSpec(a.shape, lambda i: (0, 0), pipeline_mode=pl.Buffered(1))
    return pl.pallas_call(
        functools.partial(_ffn_kernel, final=final, tf=tf),
        grid=(r // tm,),
        in_specs=[
            row, row,
            pl.BlockSpec((None, 1, D_MODEL), lambda i: (i // tiles_per_seq, 0, 0)),
            pl.BlockSpec((1, D_MODEL), lambda i: (0, 0)),
            resident(w1), resident(w3), resident(w2),
        ],
        out_specs=row,
        out_shape=jax.ShapeDtypeStruct((r, D_MODEL), F32),
        compiler_params=_params(("arbitrary",)),
        name="ffn",
    )(h, x2, gate, gf, w1, w3, w2)


MOE_TM = 512
ROW_TILE = (D_MODEL // LANE, LANE)


def _route(top_i, tm):
    t = top_i.shape[1]
    e_flat = top_i.reshape(-1)
    oh = (e_flat[:, None] == jnp.arange(N_EXPERTS, dtype=jnp.int32)[None, :]).astype(jnp.int32)
    csum = jnp.cumsum(oh, axis=0)
    rank = jnp.sum(csum * oh, axis=1) - 1
    counts = csum[-1]
    padded = ((counts + tm - 1) // tm) * tm
    ends = jnp.cumsum(padded)
    pos = jnp.sum(oh * (ends - padded)[None, :], axis=1) + rank
    n_tiles = (2 * t) // tm + N_EXPERTS
    starts = jnp.arange(n_tiles, dtype=jnp.int32) * tm
    tile_e = jnp.sum((starts[:, None] >= ends[None, :]).astype(jnp.int32), axis=1)
    return (pos.astype(jnp.int32), ends.astype(jnp.int32), jnp.minimum(tile_e, N_EXPERTS - 1),
            (ends[-1:] // tm).astype(jnp.int32))


def _dispatch_kernel(pos_ref, ends_ref, h_ref, xs_ref, zero_ref, sem, *, t, td, tm):
    i = pl.program_id(0)
    base = i * td

    @pl.when(i == 0)
    def _():
        zero_ref[...] = jnp.zeros_like(zero_ref)
        n_tiles = xs_ref.shape[0] // tm
        fills = []
        for e in range(N_EXPERTS):
            start = ends_ref[e - 1] if e else 0
            fills.append((ends_ref[e] > start, pl.multiple_of(ends_ref[e] - tm, tm)))
            fills.append((ends_ref[N_EXPERTS - 1] // tm + e < n_tiles,
                          pl.multiple_of(ends_ref[N_EXPERTS - 1] + e * tm, tm)))
        for go, row0 in fills:
            @pl.when(go)
            def _():
                pltpu.make_async_copy(zero_ref, xs_ref.at[pl.ds(row0, tm)], sem).start()
        for go, row0 in fills:
            @pl.when(go)
            def _():
                pltpu.make_async_copy(zero_ref, xs_ref.at[pl.ds(row0, tm)], sem).wait()

    for r in range(td):
        for k in range(2):
            dst = pos_ref[k * t + base + r]
            pltpu.make_async_copy(h_ref.at[pl.ds(r, 1)], xs_ref.at[pl.ds(dst, 1)], sem).start(priority=k)
    for k in range(2):
        pltpu.make_async_copy(h_ref, xs_ref.at[pl.ds(0, td)], sem).wait()


def _dispatch_call(pos, ends, h, n_rows, tm, td=512):
    t = h.shape[0]
    return pl.pallas_call(
        functools.partial(_dispatch_kernel, t=t, td=td, tm=tm),
        grid_spec=pltpu.PrefetchScalarGridSpec(
            num_scalar_prefetch=2, grid=(t // td,),
            in_specs=[pl.BlockSpec((td, D_MODEL), lambda i, pos, ends: (i, 0))],
            out_specs=pl.BlockSpec(memory_space=pl.ANY),
            scratch_shapes=[pltpu.VMEM((tm, D_MODEL), h.dtype), pltpu.SemaphoreType.DMA(())]),
        out_shape=jax.ShapeDtypeStruct((n_rows, D_MODEL), h.dtype),
        compiler_params=_params(("arbitrary",)),
        name="dispatch",
    )(pos, ends, h)


def _ffn_routed_kernel(te_ref, nu_ref, x_ref, w1_ref, w3_ref, w2_ref, y_ref, acc_ref):
    del te_ref
    used = pl.program_id(0) < nu_ref[0]

    def finish(acc):
        y_ref[...] = acc.reshape(y_ref.shape)

    @pl.when(used)
    def _():
        _swiglu_step(x_ref[...].astype(BF16), w1_ref, w3_ref, w2_ref, acc_ref, finish)

    @pl.when(jnp.logical_not(used))
    def _():
        y_ref[...] = jnp.zeros_like(y_ref)


def _ffn_routed_call(tile_e, n_used, xs, w1, w3, w2, tm=MOE_TM, tf=1792):
    r = xs.shape[0]
    ff = w1.shape[2]
    return pl.pallas_call(
        _ffn_routed_kernel,
        grid_spec=pltpu.PrefetchScalarGridSpec(
            num_scalar_prefetch=2, grid=(r // tm, ff // tf),
            in_specs=[
                pl.BlockSpec((tm, D_MODEL), lambda i, f, te, nu: (i, 0)),
                pl.BlockSpec((None, D_MODEL, tf), lambda i, f, te, nu: (te[i], 0, f)),
                pl.BlockSpec((None, D_MODEL, tf), lambda i, f, te, nu: (te[i], 0, f)),
                pl.BlockSpec((None, tf, D_MODEL), lambda i, f, te, nu: (te[i], f, 0)),
            ],
            out_specs=pl.BlockSpec((tm,) + ROW_TILE, lambda i, f, te, nu: (i, 0, 0)),
            scratch_shapes=[pltpu.VMEM((tm, D_MODEL), F32)]),
        out_shape=jax.ShapeDtypeStruct((r,) + ROW_TILE, F32),
        compiler_params=_params(("arbitrary", "arbitrary")),
        name="ffn_routed",
    )(tile_e, n_used, xs, w1, w3, w2)


def _combine_kernel(pos_ref, x_ref, p_ref, gate_ref, gf_ref, y_ref, o_ref, buf, sem, *, t, tc, final):
    i = pl.program_id(0)

    def row_copy(step, slot, r, k):
        src = pos_ref[k * t + step * tc + r]
        return pltpu.make_async_copy(y_ref.at[pl.ds(src, 1)], buf.at[slot, k, pl.ds(r, 1)], sem.at[slot])

    def issue(step, slot):
        for r in range(tc):
            for k in range(2):
                row_copy(step, slot, r, k).start(priority=k)

    @pl.when(i == 0)
    def _():
        issue(0, 0)

    @pl.when(i + 1 < pl.num_programs(0))
    def _():
        issue(i + 1, (i + 1) % 2)

    slot = i % 2
    for k in range(2):
        pltpu.make_async_copy(y_ref.at[pl.ds(0, tc)], buf.at[slot, k], sem.at[slot]).wait()
    p = p_ref[...]
    y0 = buf[slot, 0].reshape(tc, D_MODEL)
    y1 = buf[slot, 1].reshape(tc, D_MODEL)
    xn = x_ref[...] + gate_ref[...] * (p[:, 0:1] * y0 + p[:, 1:2] * y1)
    o_ref[...] = _rms(xn, gf_ref[...]) if final else xn


def _combine_call(pos, x2, probs, gate, gf, y, seq, final, tc=256):
    t = x2.shape[0]
    tiles_per_seq = seq // tc
    row = pl.BlockSpec((tc, D_MODEL), lambda i, pos: (i, 0))
    return pl.pallas_call(
        functools.partial(_combine_kernel, t=t, tc=tc, final=final),
        grid_spec=pltpu.PrefetchScalarGridSpec(
            num_scalar_prefetch=1, grid=(t // tc,),
            in_specs=[row, pl.BlockSpec((tc, 2), lambda i, pos: (i, 0)),
                      pl.BlockSpec((None, 1, D_MODEL), lambda i, pos: (i // tiles_per_seq, 0, 0)),
                      pl.BlockSpec((1, D_MODEL), lambda i, pos: (0, 0)),
                      pl.BlockSpec(memory_space=pl.ANY)],
            out_specs=row,
            scratch_shapes=[pltpu.VMEM((2, 2, tc) + ROW_TILE, F32), pltpu.SemaphoreType.DMA((2,))]),
        out_shape=jax.ShapeDtypeStruct((t, D_MODEL), F32),
        compiler_params=_params(("arbitrary",)),
        name="combine",
    )(pos, x2, probs, gate, gf, y)


def _layout_w_in_kernel(w_ref, o_ref):
    half = QK_ROPE // 2
    o_ref[...] = jnp.zeros_like(o_ref)

    def put(dst, src, width, sign=1.0):
        o_ref[:, dst:dst + width] = (sign * w_ref[:, src:src + width]).astype(BF16)

    o = 3 * RW
    put(0, 0, o + DECAY_LORA)
    put(o + LANE, o + DECAY_LORA, ICLR_LORA)
    put(o + 2 * LANE, o + DECAY_LORA + ICLR_LORA, GATE_LORA)
    src = o + DECAY_LORA + ICLR_LORA + GATE_LORA
    put(PS_W, src, Q_LORA + KV_LORA)
    kr = src + Q_LORA + KV_LORA
    dst = PS_W + Q_LORA + KV_LORA
    put(dst + QK_NOPE, kr, QK_ROPE)
    put(dst + LANE + QK_NOPE, kr + half, half, -1.0)
    put(dst + LANE + QK_NOPE + half, kr, half)
    put(PS_W + MLA_W, kr + QK_ROPE, GATE_W)


def _layout_w_in(w, layer):
    _, k, n = w.shape
    return pl.pallas_call(
        _layout_w_in_kernel,
        grid=(k // 256,),
        in_specs=[pl.BlockSpec((None, 256, n), lambda i: (layer, i, 0))],
        out_specs=pl.BlockSpec((256, PS_W + MLA_W + GATE_W), lambda i: (i, 0)),
        out_shape=jax.ShapeDtypeStruct((k, PS_W + MLA_W + GATE_W), BF16),
        compiler_params=_params(("arbitrary",)),
        name="layout_w_in",
    )(w)


def _layout_mu(mu):
    z = jnp.zeros((64,), mu.dtype)
    o = 3 * RW
    return jnp.concatenate([mu[:o + 64], z, mu[o + 64:o + 128], z, mu[o + 128:]])[None, :]


def _layout_wq(w_qb):
    hd = QK_NOPE + QK_ROPE
    w = w_qb.reshape(Q_LORA, MLA_HEADS, hd)
    half = QK_ROPE // 2
    x1, x2 = w[:, :, QK_NOPE:QK_NOPE + half], w[:, :, QK_NOPE + half:]
    zn = jnp.zeros((Q_LORA, MLA_HEADS, QK_NOPE), w.dtype)
    zp = jnp.zeros((Q_LORA, MLA_HEADS, HEAD_PAD - hd), w.dtype)
    wq = jnp.concatenate([w, zp], axis=2).reshape(Q_LORA, -1)
    wqs = jnp.concatenate([zn, -x2, x1, zp], axis=2).reshape(Q_LORA, -1)
    return wq.astype(BF16), wqs.astype(BF16)


def _layout_wkv(w_kvb):
    w = w_kvb.reshape(KV_LORA, MLA_HEADS, QK_NOPE + V_HEAD)
    z = jnp.zeros((KV_LORA, MLA_HEADS, HEAD_PAD - QK_NOPE), w.dtype)
    wk = jnp.concatenate([w[:, :, :QK_NOPE], z], axis=2).reshape(KV_LORA, -1)
    wv = w[:, :, QK_NOPE:].reshape(KV_LORA, -1)
    return wk.astype(BF16), wv.astype(BF16)


def _rope_freq_row():
    inv_freq = ROPE_THETA ** (-jnp.arange(0, QK_ROPE, 2, dtype=F32) / QK_ROPE)
    z = jnp.zeros((QK_NOPE,), F32)
    zp = jnp.zeros((LANE - QK_NOPE - QK_ROPE,), F32)
    return jnp.concatenate([z, inv_freq, inv_freq, zp])[None, :]


def kernel(x, c, positions, ada_w, ada_b, norm_mix, norm_ffn, norm_final, w_in, tshift_mu, w0, w_up, a0, a_up, g_up, k_k, k_a, r_k, lnx_g, lnx_b, rwkv_out, q_norm, w_qb, kv_norm, w_kvb, mla_out, w_o, ffn_w1, ffn_w3, ffn_w2, router_w, router_b, moe_w1, moe_w3, moe_w2):
    batch, seq, _ = x.shape
    t = batch * seq
    mods = _ada_call(c, ada_w.reshape(2 * DEPTH, D_MODEL, 3 * D_MODEL),
                     ada_b.reshape(2 * DEPTH, 1, 3 * D_MODEL))
    mods = mods.reshape(DEPTH, 2, batch, 3, 1, D_MODEL)
    pos_col = positions.reshape(t, 1)
    invf = _rope_freq_row()
    x2 = x.reshape(t, D_MODEL)
    row = lambda v: v.reshape(1, -1)

    side_jobs = [moe_w1.reshape(-1, moe_w1.shape[-1]), moe_w3.reshape(-1, moe_w3.shape[-1])]
    moe_up = []

    for l in range(DEPTH):
        shift, scale, gate = (mods[l, 0, :, j] for j in range(3))
        ps, mla, gates, cast = _inproj_call(x2, shift, scale, row(norm_mix[l]), _layout_w_in(w_in, l),
                                            _layout_mu(tshift_mu[l]), side_jobs[l], seq)
        moe_up.append(cast.reshape(moe_w1.shape))
        vecs = [row(v[l]) for v in (w0, a0, k_k, k_a, r_k, lnx_g, lnx_b)]
        lora_pad = lambda w: jnp.pad(w, ((0, LANE - w.shape[0]), (0, 0)))
        rw = _rwkv_call(ps, vecs, lora_pad(w_up[l]), lora_pad(a_up[l]), g_up[l], batch, seq)
        wq, wqs = _layout_wq(w_qb[l])
        wk, wv = _layout_wkv(w_kvb[l])
        q, k, v = _mlaprep_call(mla, pos_col, invf, row(q_norm[l]), row(kv_norm[l]), wq, wqs, wk, wv)
        o = _attn_call(q, k, v, batch, seq)

        shift2, scale2, gate2 = (mods[l, 1, :, j] for j in range(3))
        moe = l % 2 == 1
        router = (router_w[l // 2].T, router_b[l // 2].reshape(N_EXPERTS, 1)) if moe else None
        outs = _mixout_call(rw, o, gates, x2, gate, rwkv_out[l].astype(BF16), mla_out[l].astype(BF16),
                            w_o[l].astype(BF16), row(norm_ffn[l]), shift2, scale2, seq, router=router)
        final = l == DEPTH - 1
        if moe:
            x2, h2, top_i, top_p = outs
            pos, ends, tile_e, n_used = _route(top_i, MOE_TM)
            xs = _dispatch_call(pos, ends, h2, 2 * t + N_EXPERTS * MOE_TM, MOE_TM)
            y = _ffn_routed_call(tile_e, n_used, xs, moe_up[0][l // 2], moe_up[1][l // 2], moe_w2[l // 2])
            x2 = _combine_call(pos, x2, top_p.T, gate2, row(norm_final), y, seq, final)
        else:
            x2, h2 = outs
            x2 = _ffn_call(h2, x2, gate2, row(norm_final), ffn_w1[l // 2].astype(BF16),
                           ffn_w3[l // 2].astype(BF16), ffn_w2[l // 2].astype(BF16), seq, final)
    return x2.reshape(batch, seq, D_MODEL)
```

```python
import functools
import math

import jax
import jax.numpy as jnp
from jax import lax
from jax.experimental import pallas as pl
from jax.experimental.pallas import tpu as pltpu

F32 = jnp.float32
BF16 = jnp.bfloat16

D_MODEL = 1024
DEPTH = 2
CHUNK = 64
NORM_EPS = 1e-6
RWKV_HEADS = 8
RWKV_HEAD_DIM = 64
RW = RWKV_HEADS * RWKV_HEAD_DIM
DECAY_LORA = 64
ICLR_LORA = 64
GATE_LORA = 128
DECAY_SCALE = math.exp(-0.5)
LNX_EPS = 64e-5
MLA_HEADS = 8
Q_LORA = 384
KV_LORA = 256
QK_NOPE = 64
QK_ROPE = 32
V_HEAD = 64
ROPE_THETA = 10000.0
LOG2E = math.log2(math.e)
N_EXPERTS = 8

LANE = 128
HEAD_PAD = 128
PS_W = 3 * RW + 3 * LANE
MLA_W = Q_LORA + KV_LORA + 2 * LANE
GATE_W = 2 * D_MODEL
HALF = 256
VMEM_LIMIT = 56 * 1024 * 1024


def _sigmoid(x):
    return 1.0 / (1.0 + jnp.exp(-x))


def _dot(a, b):
    return jnp.dot(a, b, preferred_element_type=F32)


def _dot_nt(a, b):
    return lax.dot_general(a, b, (((1,), (1,)), ((), ())), preferred_element_type=F32)


def _split2(x):
    hi = x.astype(BF16)
    lo = (x - hi.astype(F32)).astype(BF16)
    return hi, lo


def _dot3(a, b):
    ah, al = _split2(a)
    bh, bl = _split2(b)
    return _dot(ah, bh) + _dot(al, bh) + _dot(ah, bl)


def _params(sem):
    return pltpu.CompilerParams(dimension_semantics=sem, vmem_limit_bytes=VMEM_LIMIT)


def _ada_kernel(c_ref, w_ref, b_ref, o_ref):
    c = c_ref[...]
    o_ref[...] = _dot((c * _sigmoid(c)).astype(BF16), w_ref[...].astype(BF16)) + b_ref[...]


def _ada_call(c, ada_w, ada_b):
    n = ada_w.shape[0]
    nb = c.shape[0]
    return pl.pallas_call(
        _ada_kernel,
        grid=(n, 3),
        in_specs=[
            pl.BlockSpec((nb, D_MODEL), lambda i, j: (0, 0)),
            pl.BlockSpec((None, D_MODEL, D_MODEL), lambda i, j: (i, 0, j)),
            pl.BlockSpec((None, 1, D_MODEL), lambda i, j: (i, 0, j)),
        ],
        out_specs=pl.BlockSpec((None, nb, D_MODEL), lambda i, j: (i, 0, j)),
        out_shape=jax.ShapeDtypeStruct((n, nb, 3 * D_MODEL), F32),
        compiler_params=_params(("arbitrary", "arbitrary")),
        name="ada",
    )(c, ada_w, ada_b)


def _norm_mod(x, g, scale, shift):
    var = jnp.mean(x * x, axis=-1, keepdims=True)
    return (x * lax.rsqrt(var + NORM_EPS) * g) * (1.0 + scale) + shift


def _inproj_kernel(x_ref, shift_ref, scale_ref, g_ref, w_ref, mu_ref, side_ref,
                   ps_ref, mla_ref, gates_ref, side_out_ref, carry_ref, *, tiles_per_seq, tm):
    i = pl.program_id(0)
    side_out_ref[...] = side_ref[...].astype(BF16)

    @pl.when(i % tiles_per_seq == 0)
    def _():
        carry_ref[...] = jnp.zeros_like(carry_ref)

    carry = carry_ref[0:1, :]
    row = lax.broadcasted_iota(jnp.int32, (INPROJ_SUB, PS_W), 0)
    for s in range(0, tm, INPROJ_SUB):
        r = slice(s, s + INPROJ_SUB)
        h = _norm_mod(x_ref[r, :], g_ref[...], scale_ref[...], shift_ref[...]).astype(BF16)
        p = _dot(h, w_ref[:, 0:PS_W])
        prev = jnp.where(row == 0, carry, pltpu.roll(p, 1, axis=0))
        carry = p[INPROJ_SUB - 1:INPROJ_SUB, :]
        ps_ref[r, :] = p + (prev - p) * mu_ref[...]
        mla_ref[r, :] = _dot(h, w_ref[:, PS_W:PS_W + MLA_W])
        gates_ref[r, :] = _sigmoid(_dot(h, w_ref[:, PS_W + MLA_W:])).astype(BF16)
    carry_ref[0:1, :] = carry


INPROJ_SUB = 256


def _inproj_call(x2, shift, scale, g, w_pad, mu_pad, side, seq, tm=512):
    t = x2.shape[0]
    steps = t // tm
    tiles_per_seq = seq // tm
    bmap = lambda i: (i // tiles_per_seq, 0, 0)
    wtot = PS_W + MLA_W + GATE_W
    side_spec = pl.BlockSpec((side.shape[0] // steps, side.shape[1]), lambda i: (i, 0))
    return pl.pallas_call(
        functools.partial(_inproj_kernel, tiles_per_seq=tiles_per_seq, tm=tm),
        grid=(steps,),
        in_specs=[
            pl.BlockSpec((tm, D_MODEL), lambda i: (i, 0)),
            pl.BlockSpec((None, 1, D_MODEL), bmap),
            pl.BlockSpec((None, 1, D_MODEL), bmap),
            pl.BlockSpec((1, D_MODEL), lambda i: (0, 0)),
            pl.BlockSpec((D_MODEL, wtot), lambda i: (0, 0), pipeline_mode=pl.Buffered(1)),
            pl.BlockSpec((1, PS_W), lambda i: (0, 0)),
            side_spec,
        ],
        out_specs=[
            pl.BlockSpec((tm, PS_W), lambda i: (i, 0)),
            pl.BlockSpec((tm, MLA_W), lambda i: (i, 0)),
            pl.BlockSpec((tm, GATE_W), lambda i: (i, 0)),
            side_spec,
        ],
        out_shape=[
            jax.ShapeDtypeStruct((t, PS_W), F32),
            jax.ShapeDtypeStruct((t, MLA_W), F32),
            jax.ShapeDtypeStruct((t, GATE_W), BF16),
            jax.ShapeDtypeStruct(side.shape, BF16),
        ],
        scratch_shapes=[pltpu.VMEM((8, PS_W), F32)],
        compiler_params=_params(("arbitrary",)),
        name="inproj",
    )(x2, shift, scale, g, w_pad, mu_pad, side)


def _blockdiag(xb, bdmask):
    return jnp.tile(xb, (HALF // CHUNK, 1)) * bdmask


def _bd_nn(a, b, bdmask):
    return _dot(a.astype(BF16), _blockdiag(b.astype(BF16), bdmask))


def _bd_nt(a, b, bdmask):
    return _dot_nt(a.astype(BF16), _blockdiag(b.astype(BF16), bdmask))


def _bd_tn(a, b, lane_head):
    full = _dot(a.T.astype(BF16), b.astype(BF16))
    out = full[(HALF // CHUNK - 1) * CHUNK:, :]
    for h in range(HALF // CHUNK - 2, -1, -1):
        out = jnp.where(lane_head == h, full[h * CHUNK:(h + 1) * CHUNK, :], out)
    return out


def _halves(fn, *xs):
    outs = [fn(*[x[:, h * HALF:(h + 1) * HALF] for x in xs]) for h in range(RW // HALF)]
    return jnp.concatenate(outs, axis=1)


def _rwkv_kernel(ps_ref, w0_ref, a0_ref, kk_ref, ka_ref, rk_ref, lg_ref, lb_ref,
                 wup_ref, aup_ref, gup_ref, out_ref, s_ref, *, nc):
    @pl.when(pl.program_id(1) == 0)
    def _():
        s_ref[...] = jnp.zeros_like(s_ref)

    C = CHUNK
    r_i = lax.broadcasted_iota(jnp.int32, (HALF, HALF), 0)
    c_i = lax.broadcasted_iota(jnp.int32, (HALF, HALF), 1)
    bdmask = ((r_i // C) == (c_i // C)).astype(BF16)
    t_i = lax.broadcasted_iota(jnp.int32, (C, HALF), 0)
    lane_head = lax.broadcasted_iota(jnp.int32, (C, HALF), 1) // C
    i_i = lax.broadcasted_iota(jnp.int32, (C, HALF), 1) % C
    strict = i_i < t_i
    incl = i_i <= t_i
    eye_l = (i_i == t_i).astype(F32)
    n = nc * C
    tr = lax.broadcasted_iota(jnp.int32, (n, n), 0)
    tc = lax.broadcasted_iota(jnp.int32, (n, n), 1)
    tril = ((tc <= tr) & (tc // C == tr // C)).astype(BF16)

    def headsum(x, split=False):
        def one(xh):
            if not split:
                return _dot(xh.astype(BF16), bdmask)
            hi, lo = _split2(xh)
            return _dot(hi, bdmask) + _dot(lo, bdmask)
        return _halves(one, x)

    def cumsum_rows(x):
        hi = x.astype(BF16)
        r1 = x - hi.astype(F32)
        mid = r1.astype(BF16)
        lo = (r1 - mid.astype(F32)).astype(BF16)
        return _dot(tril, hi) + _dot(tril, mid) + _dot(tril, lo)

    p_r = ps_ref[:, 0:RW]
    p_k = ps_ref[:, RW:2 * RW]
    p_v = ps_ref[:, 2 * RW:3 * RW]
    p_w = ps_ref[:, 3 * RW:3 * RW + LANE]
    p_a = ps_ref[:, 3 * RW + LANE:3 * RW + 2 * LANE]
    p_g = ps_ref[:, 3 * RW + 2 * LANE:3 * RW + 3 * LANE]
    logw = -DECAY_SCALE * _sigmoid(w0_ref[...] + _dot3(jnp.tanh(p_w), wup_ref[...]))
    iclr = _sigmoid(a0_ref[...] + _dot(p_a.astype(BF16), aup_ref[...].astype(BF16)))
    gate = _dot(_sigmoid(p_g).astype(BF16), gup_ref[...].astype(BF16))
    kk = p_k * kk_ref[...]
    k2 = p_k * (1.0 + (iclr - 1.0) * ka_ref[...])
    sums = headsum(jnp.concatenate([kk * kk, p_r * k2 * rk_ref[...]], axis=0))
    kk = kk * lax.rsqrt(sums[:n] + 1e-12)
    bonus = sums[n:] * p_v
    b_v = kk * iclr
    cum = cumsum_rows(logw)
    cum_c = jnp.concatenate(
        [jnp.broadcast_to(cum[(c + 1) * C - 1:(c + 1) * C, :], (C, RW)) for c in range(nc)], axis=0)
    e_neg = jnp.exp(-cum)
    e_rem = jnp.exp(cum_c - cum)
    a_all = -kk * jnp.exp(cum - logw)
    b_all = b_v * e_neg
    k_all = k2 * e_neg
    r_all = p_r * jnp.exp(cum)
    bg_all = b_v * e_rem
    kg_all = k2 * e_rem
    gam_all = jnp.exp(cum_c)

    nh = RW // HALF
    units = [(c, h) for c in range(nc) for h in range(nh)]

    def cut(x, u):
        c, h = u
        return x[c * C:(c + 1) * C, h * HALF:(h + 1) * HALF]

    each = lambda fn, *lists: [fn(*xs) for xs in zip(*lists)]
    stack = lambda x, y: jnp.concatenate([x, y], axis=0)
    top = lambda x: x[:C]
    bot = lambda x: x[C:]
    nn = lambda a, b: _bd_nn(a, b, bdmask)
    nt = lambda a, b: _bd_nt(a, b, bdmask)

    a_t, b_t, k_t, r_t, b_g, k_g, v_u = ([cut(x, u) for u in units]
                                         for x in (a_all, b_all, k_all, r_all, bg_all, kg_all, p_v))
    gam = [gam_all[c * C:c * C + 1, h * HALF:(h + 1) * HALF] for c, h in units]
    ar = each(stack, a_t, r_t)
    sb = each(nt, ar, b_t)
    sk = each(nt, ar, k_t)
    l_ab = [jnp.where(strict, top(x), 0.0) for x in sb]
    q_b = [jnp.where(incl, bot(x), 0.0) for x in sb]
    m_ak = [jnp.where(strict, top(x), 0.0) for x in sk]
    q_k = [jnp.where(incl, bot(x), 0.0) for x in sk]
    t_m = [eye_l + x for x in l_ab]
    pw = each(nn, l_ab, l_ab)
    for _ in range(4):
        tp = each(nn, each(stack, t_m, pw), pw)
        t_m = each(lambda t, x: t + top(x), t_m, tp)
        pw = [bot(x) for x in tp]
    t_m = each(lambda t, x: t + x, t_m, each(nn, t_m, pw))
    mq = each(nn, each(stack, m_ak, q_k), v_u)
    a_h = each(nn, t_m, a_t)
    u0 = each(nn, t_m, [top(x) for x in mq])
    r_h = each(lambda r, x: r + x, r_t, each(nn, q_b, a_h))
    y0 = each(lambda m, x: bot(m) + x, mq, each(nn, q_b, u0))
    g_l = each(lambda a, b, g: _bd_tn(a, b, lane_head) + eye_l * g, a_h, b_g, gam)
    h_l = each(lambda u, v, b, k: _bd_tn(stack(u, v), stack(b, k), lane_head), u0, v_u, b_g, k_g)

    s_all = s_ref[...]
    s_cur = [s_all[:, h * HALF:(h + 1) * HALF] for h in range(nh)]
    ys = []
    for c in range(nc):
        idx = [c * nh + h for h in range(nh)]
        yc = [nt(r_h[i], s) + y0[i] for i, s in zip(idx, s_cur)]
        s_cur = [nn(s, g_l[i]) + h_l[i] for i, s in zip(idx, s_cur)]
        ys.append(jnp.concatenate(yc, axis=1))
    s_ref[...] = jnp.concatenate(s_cur, axis=1)
    y = jnp.concatenate(ys, axis=0)
    inv_n = 1.0 / RWKV_HEAD_DIM
    mu = headsum(y, split=True) * inv_n
    d = y - mu
    var = headsum(d * d) * inv_n
    yn = d * lax.rsqrt(var + LNX_EPS) * lg_ref[...] + lb_ref[...]
    out_ref[...] = ((yn + bonus) * gate).astype(BF16)


def _rwkv_call(ps, vecs, wup, aup, gup, batch, seq, nc=4):
    t = ps.shape[0]
    rows = nc * CHUNK
    steps = seq // rows
    vec_spec = pl.BlockSpec((1, RW), lambda b, j: (0, 0))
    lora_spec = pl.BlockSpec((LANE, RW), lambda b, j: (0, 0))
    return pl.pallas_call(
        functools.partial(_rwkv_kernel, nc=nc),
        grid=(batch, steps),
        in_specs=[pl.BlockSpec((rows, PS_W), lambda b, j: (b * steps + j, 0))]
                 + [vec_spec] * 7 + [lora_spec] * 3,
        out_specs=pl.BlockSpec((rows, RW), lambda b, j: (b * steps + j, 0)),
        out_shape=jax.ShapeDtypeStruct((t, RW), BF16),
        scratch_shapes=[pltpu.VMEM((CHUNK, RW), F32)],
        compiler_params=_params(("arbitrary", "arbitrary")),
        name="rwkv",
    )(ps, *vecs, wup, aup, gup)


def _mlaprep_kernel(mla_ref, pos_ref, invf_ref, qn_ref, kvn_ref, wq_ref, wqs_ref, wk_ref, wv_ref,
                    q_ref, k_ref, v_ref):
    def rms(x, g):
        return (x * lax.rsqrt(jnp.mean(x * x, axis=-1, keepdims=True) + NORM_EPS) * g).astype(BF16)

    ang = pos_ref[...].astype(F32) * invf_ref[...]
    cos = jnp.cos(ang)
    sin = jnp.sin(ang)
    lane = lax.broadcasted_iota(jnp.int32, cos.shape, 1)
    cos_q = jnp.tile(jnp.where(lane < QK_NOPE, 1.0, cos), (1, MLA_HEADS))
    sin_q = jnp.tile(sin, (1, MLA_HEADS))
    qn = rms(mla_ref[:, 0:Q_LORA], qn_ref[...])
    q = _dot(qn, wq_ref[...]) * cos_q + _dot(qn, wqs_ref[...]) * sin_q
    q_ref[...] = (q * ((QK_NOPE + QK_ROPE) ** -0.5 * LOG2E)).astype(BF16)
    kvn = rms(mla_ref[:, Q_LORA:Q_LORA + KV_LORA], kvn_ref[...])
    o = Q_LORA + KV_LORA
    k_pe = mla_ref[:, o:o + LANE] * cos + mla_ref[:, o + LANE:o + 2 * LANE] * sin
    k_ref[...] = (_dot(kvn, wk_ref[...]) + jnp.tile(k_pe, (1, MLA_HEADS))).astype(BF16)
    v_ref[...] = _dot(kvn, wv_ref[...]).astype(BF16)


def _mlaprep_call(mla, pos, invf, qn, kvn, wq, wqs, wk, wv, tm=512):
    t = mla.shape[0]
    hw = MLA_HEADS * HEAD_PAD
    vw = MLA_HEADS * V_HEAD
    full = lambda shape: pl.BlockSpec(shape, lambda i: (0, 0))
    row = lambda w: pl.BlockSpec((tm, w), lambda i: (i, 0))
    return pl.pallas_call(
        _mlaprep_kernel,
        grid=(t // tm,),
        in_specs=[row(MLA_W), row(1), full((1, LANE)), full((1, Q_LORA)), full((1, KV_LORA)),
                  full((Q_LORA, hw)), full((Q_LORA, hw)), full((KV_LORA, hw)), full((KV_LORA, vw))],
        out_specs=[row(hw), row(hw), row(vw)],
        out_shape=[jax.ShapeDtypeStruct((t, w), BF16) for w in (hw, hw, vw)],
        compiler_params=_params(("arbitrary",)),
        name="mlaprep",
    )(mla, pos, invf, qn, kvn, wq, wqs, wk, wv)


def _attn_kernel(q_ref, k_ref, v_ref, o_ref, *, seq, tq):
    r_i = lax.broadcasted_iota(jnp.int32, (tq, tq), 0) // CHUNK
    c_i = lax.broadcasted_iota(jnp.int32, (tq, tq), 1) // CHUNK
    mask = c_i <= r_i
    def scores(item):
        qi, hh = item
        lo = qi * tq
        qk = slice(hh * HEAD_PAD, (hh + 1) * HEAD_PAD)
        q = q_ref[lo:lo + tq, qk]
        sd = jnp.where(mask, _dot_nt(q, k_ref[lo:lo + tq, qk]), -1e30)
        so = _dot_nt(q, k_ref[0:lo, qk]) if qi > 0 else None
        return sd, so

    items = [(qi, hh) for qi in range(seq // tq) for hh in range(ATTN_HEADS_PER_STEP)]
    nxt = scores(items[0])
    for n, (qi, hh) in enumerate(items):
        lo = qi * tq
        vo = slice(hh * V_HEAD, (hh + 1) * V_HEAD)
        sd, so = nxt
        if n + 1 < len(items):
            nxt = scores(items[n + 1])
        m = jnp.max(sd, axis=-1, keepdims=True)
        if qi > 0:
            m = jnp.maximum(m, jnp.max(so, axis=-1, keepdims=True))
        pd = jnp.exp2(sd - m)
        l = jnp.sum(pd, axis=-1, keepdims=True)
        acc = _dot(pd.astype(BF16), v_ref[lo:lo + tq, vo])
        if qi > 0:
            po = jnp.exp2(so - m)
            l = l + jnp.sum(po, axis=-1, keepdims=True)
            acc = acc + _dot(po.astype(BF16), v_ref[0:lo, vo])
        o_ref[lo:lo + tq, vo] = (acc / l).astype(BF16)


ATTN_HEADS_PER_STEP = LANE // V_HEAD


def _attn_call(q, k, v, batch, seq, tq=256):
    hp = ATTN_HEADS_PER_STEP
    qk_spec = pl.BlockSpec((seq, hp * HEAD_PAD), lambda b, h: (b, h))
    vo_spec = pl.BlockSpec((seq, hp * V_HEAD), lambda b, h: (b, h))
    return pl.pallas_call(
        functools.partial(_attn_kernel, seq=seq, tq=tq),
        grid=(batch, MLA_HEADS // hp),
        in_specs=[qk_spec, qk_spec, vo_spec],
        out_specs=vo_spec,
        out_shape=jax.ShapeDtypeStruct(v.shape, BF16),
        compiler_params=_params(("arbitrary", "arbitrary")),
        name="attn",
    )(q, k, v)


MIX_SUB = 256


def _mixout_kernel(*refs, router):
    if router:
        (rw_ref, o_ref, gates_ref, x_ref, gate_ref, wr_ref, wm_ref, wo_ref,
         g2_ref, shift2_ref, scale2_ref, rwt_ref, rb_ref,
         xo_ref, h2_ref, ti_ref, tp_ref) = refs
    else:
        (rw_ref, o_ref, gates_ref, x_ref, gate_ref, wr_ref, wm_ref, wo_ref,
         g2_ref, shift2_ref, scale2_ref, xo_ref, h2_ref) = refs
    n = x_ref.shape[0]
    subs = [slice(s, s + MIX_SUB) for s in range(0, n, MIX_SUB)]
    yab = [(_dot(rw_ref[r, :], wr_ref[...]), _dot(o_ref[r, :], wm_ref[...])) for r in subs]
    merged = [gates_ref[r, 0:D_MODEL].astype(F32) * ya + gates_ref[r, D_MODEL:].astype(F32) * yb
              for r, (ya, yb) in zip(subs, yab)]
    mixed = [_dot(m.astype(BF16), wo_ref[...]) for m in merged]
    h2s = []
    for r, mx in zip(subs, mixed):
        xn = x_ref[r, :] + gate_ref[...] * mx
        xo_ref[r, :] = xn
        h2s.append(_norm_mod(xn, g2_ref[...], scale2_ref[...], shift2_ref[...]))
        h2_ref[r, :] = h2s[-1].astype(h2_ref.dtype)
    if router:
        h2 = jnp.concatenate(h2s, axis=0)
        hh, hl = _split2(h2)
        wh, wl = _split2(rwt_ref[...])
        logits = _dot_nt(wh, hh) + _dot_nt(wl, hh) + _dot_nt(wh, hl) + rb_ref[...]
        e_i = lax.broadcasted_iota(jnp.int32, logits.shape, 0)
        m1 = jnp.max(logits, axis=0, keepdims=True)
        i1 = jnp.min(jnp.where(logits == m1, e_i, N_EXPERTS), axis=0, keepdims=True)
        rest = jnp.where(e_i == i1, -jnp.inf, logits)
        m2 = jnp.max(rest, axis=0, keepdims=True)
        i2 = jnp.min(jnp.where(rest == m2, e_i, N_EXPERTS), axis=0, keepdims=True)
        e2 = jnp.exp(m2 - m1)
        den = 1.0 + e2
        ti_ref[0:1, :] = i1
        ti_ref[1:2, :] = i2
        tp_ref[0:1, :] = 1.0 / den
        tp_ref[1:2, :] = e2 / den


def _mixout_call(rw, o, gates, x2, gate, wr, wm, wo, g2, shift2, scale2, seq, router=None, tm=512):
    t = x2.shape[0]
    tiles_per_seq = seq // tm
    bmap = lambda i: (i // tiles_per_seq, 0, 0)
    full = lambda shape: pl.BlockSpec(shape, lambda i: (0, 0))
    row = lambda w: pl.BlockSpec((tm, w), lambda i: (i, 0))
    mod = pl.BlockSpec((None, 1, D_MODEL), bmap)
    in_specs = [row(RW), row(MLA_HEADS * V_HEAD), row(GATE_W), row(D_MODEL), mod,
                full(wr.shape), full(wm.shape), full(wo.shape), full((1, D_MODEL)), mod, mod]
    args = [rw, o, gates, x2, gate, wr, wm, wo, g2, shift2, scale2]
    out_specs = [row(D_MODEL), row(D_MODEL)]
    out_shape = [jax.ShapeDtypeStruct((t, D_MODEL), F32),
                 jax.ShapeDtypeStruct((t, D_MODEL), BF16 if router is None else F32)]
    if router is not None:
        in_specs += [full((N_EXPERTS, D_MODEL)), full((N_EXPERTS, 1))]
        args += list(router)
        out_specs += [pl.BlockSpec((2, tm), lambda i: (0, i))] * 2
        out_shape += [jax.ShapeDtypeStruct((2, t), jnp.int32), jax.ShapeDtypeStruct((2, t), F32)]
    return pl.pallas_call(
        functools.partial(_mixout_kernel, router=router is not None),
        grid=(t // tm,),
        in_specs=in_specs,
        out_specs=out_specs,
        out_shape=out_shape,
        compiler_params=_params(("arbitrary",)),
        name="mixout",
    )(*args)


def _swiglu_step(h, w1_ref, w3_ref, w2_ref, acc_ref, finish):
    f = pl.program_id(1)

    @pl.when(f == 0)
    def _():
        acc_ref[...] = jnp.zeros_like(acc_ref)

    a1 = _dot(h, w1_ref[...])
    a3 = _dot(h, w3_ref[...])
    act = (a1 * _sigmoid(a1)) * a3
    acc_ref[...] += _dot(act.astype(BF16), w2_ref[...].astype(BF16))

    @pl.when(f == pl.num_programs(1) - 1)
    def _():
        finish(acc_ref[...])


def _rms(x, g):
    return x * lax.rsqrt(jnp.mean(x * x, axis=-1, keepdims=True) + NORM_EPS) * g


def _ffn_kernel(h_ref, x_ref, gate_ref, gf_ref, w1_ref, w3_ref, w2_ref, o_ref, *, final, tf):
    h = h_ref[...]
    ff = w1_ref.shape[1]
    ups = [(_dot(h, w1_ref[:, s:s + tf]), _dot(h, w3_ref[:, s:s + tf])) for s in range(0, ff, tf)]
    acc = None
    for s, (a1, a3) in zip(range(0, ff, tf), ups):
        act = ((a1 * _sigmoid(a1)) * a3).astype(BF16)
        part = _dot(act, w2_ref[s:s + tf, :])
        acc = part if acc is None else acc + part
    xn = x_ref[...] + gate_ref[...] * acc
    o_ref[...] = _rms(xn, gf_ref[...]) if final else xn


def _ffn_call(h, x2, gate, gf, w1, w3, w2, seq, final, tm=512, tf=1408):
    r = h.shape[0]
    tiles_per_seq = seq // tm
    row = pl.BlockSpec((tm, D_MODEL), lambda i: (i, 0))
    resident = lambda a: pl.BlockSpec(a.shape, lambda i: (0, 0), pipeline_mode=pl.Buffered(1))
    return pl.pallas_call(
        functools.partial(_ffn_kernel, final=final, tf=tf),
        grid=(r // tm,),
        in_specs=[
            row, row,
            pl.BlockSpec((None, 1, D_MODEL), lambda i: (i // tiles_per_seq, 0, 0)),
            pl.BlockSpec((1, D_MODEL), lambda i: (0, 0)),
            resident(w1), resident(w3), resident(w2),
        ],
        out_specs=row,
        out_shape=jax.ShapeDtypeStruct((r, D_MODEL), F32),
        compiler_params=_params(("arbitrary",)),
        name="ffn",
    )(h, x2, gate, gf, w1, w3, w2)


MOE_TM = 512
ROW_TILE = (D_MODEL // LANE, LANE)


def _route(top_i, tm):
    t = top_i.shape[1]
    e_flat = top_i.reshape(-1)
    oh = (e_flat[:, None] == jnp.arange(N_EXPERTS, dtype=jnp.int32)[None, :]).astype(jnp.int32)
    csum = jnp.cumsum(oh, axis=0)
    rank = jnp.sum(csum * oh, axis=1) - 1
    counts = csum[-1]
    padded = ((counts + tm - 1) // tm) * tm
    ends = jnp.cumsum(padded)
    pos = jnp.sum(oh * (ends - padded)[None, :], axis=1) + rank
    n_tiles = (2 * t) // tm + N_EXPERTS
    starts = jnp.arange(n_tiles, dtype=jnp.int32) * tm
    tile_e = jnp.sum((starts[:, None] >= ends[None, :]).astype(jnp.int32), axis=1)
    return (pos.astype(jnp.int32), ends.astype(jnp.int32), jnp.minimum(tile_e, N_EXPERTS - 1),
            (ends[-1:] // tm).astype(jnp.int32))


def _dispatch_kernel(pos_ref, ends_ref, h_ref, xs_ref, zero_ref, sem, *, t, td, tm):
    i = pl.program_id(0)
    base = i * td

    @pl.when(i == 0)
    def _():
        zero_ref[...] = jnp.zeros_like(zero_ref)
        n_tiles = xs_ref.shape[0] // tm
        fills = []
        for e in range(N_EXPERTS):
            start = ends_ref[e - 1] if e else 0
            fills.append((ends_ref[e] > start, pl.multiple_of(ends_ref[e] - tm, tm)))
            fills.append((ends_ref[N_EXPERTS - 1] // tm + e < n_tiles,
                          pl.multiple_of(ends_ref[N_EXPERTS - 1] + e * tm, tm)))
        for go, row0 in fills:
            @pl.when(go)
            def _():
                pltpu.make_async_copy(zero_ref, xs_ref.at[pl.ds(row0, tm)], sem).start()
        for go, row0 in fills:
            @pl.when(go)
            def _():
                pltpu.make_async_copy(zero_ref, xs_ref.at[pl.ds(row0, tm)], sem).wait()

    for r in range(td):
        for k in range(2):
            dst = pos_ref[k * t + base + r]
            pltpu.make_async_copy(h_ref.at[pl.ds(r, 1)], xs_ref.at[pl.ds(dst, 1)], sem).start(priority=k)
    for k in range(2):
        pltpu.make_async_copy(h_ref, xs_ref.at[pl.ds(0, td)], sem).wait()


def _dispatch_call(pos, ends, h, n_rows, tm, td=512):
    t = h.shape[0]
    return pl.pallas_call(
        functools.partial(_dispatch_kernel, t=t, td=td, tm=tm),
        grid_spec=pltpu.PrefetchScalarGridSpec(
            num_scalar_prefetch=2, grid=(t // td,),
            in_specs=[pl.BlockSpec((td, D_MODEL), lambda i, pos, ends: (i, 0))],
            out_specs=pl.BlockSpec(memory_space=pl.ANY),
            scratch_shapes=[pltpu.VMEM((tm, D_MODEL), h.dtype), pltpu.SemaphoreType.DMA(())]),
        out_shape=jax.ShapeDtypeStruct((n_rows, D_MODEL), h.dtype),
        compiler_params=_params(("arbitrary",)),
        name="dispatch",
    )(pos, ends, h)


def _ffn_routed_kernel(te_ref, nu_ref, x_ref, w1_ref, w3_ref, w2_ref, y_ref, acc_ref):
    del te_ref
    used = pl.program_id(0) < nu_ref[0]

    def finish(acc):
        y_ref[...] = acc.reshape(y_ref.shape)

    @pl.when(used)
    def _():
        _swiglu_step(x_ref[...].astype(BF16), w1_ref, w3_ref, w2_ref, acc_ref, finish)

    @pl.when(jnp.logical_not(used))
    def _():
        y_ref[...] = jnp.zeros_like(y_ref)


def _ffn_routed_call(tile_e, n_used, xs, w1, w3, w2, tm=MOE_TM, tf=1792):
    r = xs.shape[0]
    ff = w1.shape[2]
    return pl.pallas_call(
        _ffn_routed_kernel,
        grid_spec=pltpu.PrefetchScalarGridSpec(
            num_scalar_prefetch=2, grid=(r // tm, ff // tf),
            in_specs=[
                pl.BlockSpec((tm, D_MODEL), lambda i, f, te, nu: (i, 0)),
                pl.BlockSpec((None, D_MODEL, tf), lambda i, f, te, nu: (te[i], 0, f)),
                pl.BlockSpec((None, D_MODEL, tf), lambda i, f, te, nu: (te[i], 0, f)),
                pl.BlockSpec((None, tf, D_MODEL), lambda i, f, te, nu: (te[i], f, 0)),
            ],
            out_specs=pl.BlockSpec((tm,) + ROW_TILE, lambda i, f, te, nu: (i, 0, 0)),
            scratch_shapes=[pltpu.VMEM((tm, D_MODEL), F32)]),
        out_shape=jax.ShapeDtypeStruct((r,) + ROW_TILE, F32),
        compiler_params=_params(("arbitrary", "arbitrary")),
        name="ffn_routed",
    )(tile_e, n_used, xs, w1, w3, w2)


def _combine_kernel(pos_ref, x_ref, p_ref, gate_ref, gf_ref, y_ref, o_ref, buf, sem, *, t, tc, final):
    i = pl.program_id(0)

    def row_copy(step, slot, r, k):
        src = pos_ref[k * t + step * tc + r]
        return pltpu.make_async_copy(y_ref.at[pl.ds(src, 1)], buf.at[slot, k, pl.ds(r, 1)], sem.at[slot])

    def issue(step, slot):
        for r in range(tc):
            for k in range(2):
                row_copy(step, slot, r, k).start(priority=k)

    @pl.when(i == 0)
    def _():
        issue(0, 0)

    @pl.when(i + 1 < pl.num_programs(0))
    def _():
        issue(i + 1, (i + 1) % 2)

    slot = i % 2
    for k in range(2):
        pltpu.make_async_copy(y_ref.at[pl.ds(0, tc)], buf.at[slot, k], sem.at[slot]).wait()
    p = p_ref[...]
    y0 = buf[slot, 0].reshape(tc, D_MODEL)
    y1 = buf[slot, 1].reshape(tc, D_MODEL)
    xn = x_ref[...] + gate_ref[...] * (p[:, 0:1] * y0 + p[:, 1:2] * y1)
    o_ref[...] = _rms(xn, gf_ref[...]) if final else xn


def _combine_call(pos, x2, probs, gate, gf, y, seq, final, tc=256):
    t = x2.shape[0]
    tiles_per_seq = seq // tc
    row = pl.BlockSpec((tc, D_MODEL), lambda i, pos: (i, 0))
    return pl.pallas_call(
        functools.partial(_combine_kernel, t=t, tc=tc, final=final),
        grid_spec=pltpu.PrefetchScalarGridSpec(
            num_scalar_prefetch=1, grid=(t // tc,),
            in_specs=[row, pl.BlockSpec((tc, 2), lambda i, pos: (i, 0)),
                      pl.BlockSpec((None, 1, D_MODEL), lambda i, pos: (i // tiles_per_seq, 0, 0)),
                      pl.BlockSpec((1, D_MODEL), lambda i, pos: (0, 0)),
                      pl.BlockSpec(memory_space=pl.ANY)],
            out_specs=row,
            scratch_shapes=[pltpu.VMEM((2, 2, tc) + ROW_TILE, F32), pltpu.SemaphoreType.DMA((2,))]),
        out_shape=jax.ShapeDtypeStruct((t, D_MODEL), F32),
        compiler_params=_params(("arbitrary",)),
        name="combine",
    )(pos, x2, probs, gate, gf, y)


def _layout_w_in_kernel(w_ref, o_ref):
    half = QK_ROPE // 2
    o_ref[...] = jnp.zeros_like(o_ref)

    def put(dst, src, width, sign=1.0):
        o_ref[:, dst:dst + width] = (sign * w_ref[:, src:src + width]).astype(BF16)

    o = 3 * RW
    put(0, 0, o + DECAY_LORA)
    put(o + LANE, o + DECAY_LORA, ICLR_LORA)
    put(o + 2 * LANE, o + DECAY_LORA + ICLR_LORA, GATE_LORA)
    src = o + DECAY_LORA + ICLR_LORA + GATE_LORA
    put(PS_W, src, Q_LORA + KV_LORA)
    kr = src + Q_LORA + KV_LORA
    dst = PS_W + Q_LORA + KV_LORA
    put(dst + QK_NOPE, kr, QK_ROPE)
    put(dst + LANE + QK_NOPE, kr + half, half, -1.0)
    put(dst + LANE + QK_NOPE + half, kr, half)
    put(PS_W + MLA_W, kr + QK_ROPE, GATE_W)


def _layout_w_in(w, layer):
    _, k, n = w.shape
    return pl.pallas_call(
        _layout_w_in_kernel,
        grid=(k // 256,),
        in_specs=[pl.BlockSpec((None, 256, n), lambda i: (layer, i, 0))],
        out_specs=pl.BlockSpec((256, PS_W + MLA_W + GATE_W), lambda i: (i, 0)),
        out_shape=jax.ShapeDtypeStruct((k, PS_W + MLA_W + GATE_W), BF16),
        compiler_params=_params(("arbitrary",)),
        name="layout_w_in",
    )(w)


def _layout_mu(mu):
    z = jnp.zeros((64,), mu.dtype)
    o = 3 * RW
    return jnp.concatenate([mu[:o + 64], z, mu[o + 64:o + 128], z, mu[o + 128:]])[None, :]


def _layout_wq(w_qb):
    hd = QK_NOPE + QK_ROPE
    w = w_qb.reshape(Q_LORA, MLA_HEADS, hd)
    half = QK_ROPE // 2
    x1, x2 = w[:, :, QK_NOPE:QK_NOPE + half], w[:, :, QK_NOPE + half:]
    zn = jnp.zeros((Q_LORA, MLA_HEADS, QK_NOPE), w.dtype)
    zp = jnp.zeros((Q_LORA, MLA_HEADS, HEAD_PAD - hd), w.dtype)
    wq = jnp.concatenate([w, zp], axis=2).reshape(Q_LORA, -1)
    wqs = jnp.concatenate([zn, -x2, x1, zp], axis=2).reshape(Q_LORA, -1)
    return wq.astype(BF16), wqs.astype(BF16)


def _layout_wkv(w_kvb):
    w = w_kvb.reshape(KV_LORA, MLA_HEADS, QK_NOPE + V_HEAD)
    z = jnp.zeros((KV_LORA, MLA_HEADS, HEAD_PAD - QK_NOPE), w.dtype)
    wk = jnp.concatenate([w[:, :, :QK_NOPE], z], axis=2).reshape(KV_LORA, -1)
    wv = w[:, :, QK_NOPE:].reshape(KV_LORA, -1)
    return wk.astype(BF16), wv.astype(BF16)


def _rope_freq_row():
    inv_freq = ROPE_THETA ** (-jnp.arange(0, QK_ROPE, 2, dtype=F32) / QK_ROPE)
    z = jnp.zeros((QK_NOPE,), F32)
    zp = jnp.zeros((LANE - QK_NOPE - QK_ROPE,), F32)
    return jnp.concatenate([z, inv_freq, inv_freq, zp])[None, :]


def kernel(x, c, positions, ada_w, ada_b, norm_mix, norm_ffn, norm_final, w_in, tshift_mu, w0, w_up, a0, a_up, g_up, k_k, k_a, r_k, lnx_g, lnx_b, rwkv_out, q_norm, w_qb, kv_norm, w_kvb, mla_out, w_o, ffn_w1, ffn_w3, ffn_w2, router_w, router_b, moe_w1, moe_w3, moe_w2):
    batch, seq, _ = x.shape
    t = batch * seq
    mods = _ada_call(c, ada_w.reshape(2 * DEPTH, D_MODEL, 3 * D_MODEL),
                     ada_b.reshape(2 * DEPTH, 1, 3 * D_MODEL))
    mods = mods.reshape(DEPTH, 2, batch, 3, 1, D_MODEL)
    pos_col = positions.reshape(t, 1)
    invf = _rope_freq_row()
    x2 = x.reshape(t, D_MODEL)
    row = lambda v: v.reshape(1, -1)

    side_jobs = [moe_w1.reshape(-1, moe_w1.shape[-1]), moe_w3.reshape(-1, moe_w3.shape[-1])]
    moe_up = []

    for l in range(DEPTH):
        shift, scale, gate = (mods[l, 0, :, j] for j in range(3))
        ps, mla, gates, cast = _inproj_call(x2, shift, scale, row(norm_mix[l]), _layout_w_in(w_in, l),
                                            _layout_mu(tshift_mu[l]), side_jobs[l], seq)
        moe_up.append(cast.reshape(moe_w1.shape))
        vecs = [row(v[l]) for v in (w0, a0, k_k, k_a, r_k, lnx_g, lnx_b)]
        lora_pad = lambda w: jnp.pad(w, ((0, LANE - w.shape[0]), (0, 0)))
        rw = _rwkv_call(ps, vecs, lora_pad(w_up[l]), lora_pad(a_up[l]), g_up[l], batch, seq)
        wq, wqs = _layout_wq(w_qb[l])
        wk, wv = _layout_wkv(w_kvb[l])
        q, k, v = _mlaprep_call(mla, pos_col, invf, row(q_norm[l]), row(kv_norm[l]), wq, wqs, wk, wv)
        o = _attn_call(q, k, v, batch, seq)

        shift2, scale2, gate2 = (mods[l, 1, :, j] for j in range(3))
        moe = l % 2 == 1
        router = (router_w[l // 2].T, router_b[l // 2].reshape(N_EXPERTS, 1)) if moe else None
        outs = _mixout_call(rw, o, gates, x2, gate, rwkv_out[l].astype(BF16), mla_out[l].astype(BF16),
                            w_o[l].astype(BF16), row(norm_ffn[l]), shift2, scale2, seq, router=router)
        final = l == DEPTH - 1
        if moe:
            x2, h2, top_i, top_p = outs
            pos, ends, tile_e, n_used = _route(top_i, MOE_TM)
            xs = _dispatch_call(pos, ends, h2, 2 * t + N_EXPERTS * MOE_TM, MOE_TM)
            y = _ffn_routed_call(tile_e, n_used, xs, moe_up[0][l // 2], moe_up[1][l // 2], moe_w2[l // 2])
            x2 = _combine_call(pos, x2, top_p.T, gate2, row(norm_final), y, seq, final)
        else:
            x2, h2 = outs
            x2 = _ffn_call(h2, x2, gate2, row(norm_final), ffn_w1[l // 2].astype(BF16),
                           ffn_w3[l // 2].astype(BF16), ffn_w2[l // 2].astype(BF16), seq, final)
    return x2.reshape(batch, seq, D_MODEL)
```

```python
import functools
import math

import jax
import jax.numpy as jnp
from jax import lax
from jax.experimental import pallas as pl
from jax.experimental.pallas import tpu as pltpu

F32 = jnp.float32
BF16 = jnp.bfloat16

D_MODEL = 1024
DEPTH = 2
CHUNK = 64
NORM_EPS = 1e-6
RWKV_HEADS = 8
RWKV_HEAD_DIM = 64
RW = RWKV_HEADS * RWKV_HEAD_DIM
DECAY_LORA = 64
ICLR_LORA = 64
GATE_LORA = 128
DECAY_SCALE = math.exp(-0.5)
LNX_EPS = 64e-5
MLA_HEADS = 8
Q_LORA = 384
KV_LORA = 256
QK_NOPE = 64
QK_ROPE = 32
V_HEAD = 64
ROPE_THETA = 10000.0
LOG2E = math.log2(math.e)
N_EXPERTS = 8

LANE = 128
HEAD_PAD = 128
PS_W = 3 * RW + 3 * LANE
MLA_W = Q_LORA + KV_LORA + 2 * LANE
GATE_W = 2 * D_MODEL
HALF = 256
VMEM_LIMIT = 56 * 1024 * 1024


def _sigmoid(x):
    return 1.0 / (1.0 + jnp.exp(-x))


def _dot(a, b):
    return jnp.dot(a, b, preferred_element_type=F32)


def _dot_nt(a, b):
    return lax.dot_general(a, b, (((1,), (1,)), ((), ())), preferred_element_type=F32)


def _split2(x):
    hi = x.astype(BF16)
    lo = (x - hi.astype(F32)).astype(BF16)
    return hi, lo


def _dot3(a, b):
    ah, al = _split2(a)
    bh, bl = _split2(b)
    return _dot(ah, bh) + _dot(al, bh) + _dot(ah, bl)


def _params(sem):
    return pltpu.CompilerParams(dimension_semantics=sem, vmem_limit_bytes=VMEM_LIMIT)


def _ada_kernel(c_ref, w_ref, b_ref, o_ref):
    c = c_ref[...]
    o_ref[...] = _dot((c * _sigmoid(c)).astype(BF16), w_ref[...].astype(BF16)) + b_ref[...]


def _ada_call(c, ada_w, ada_b):
    n = ada_w.shape[0]
    nb = c.shape[0]
    return pl.pallas_call(
        _ada_kernel,
        grid=(n, 3),
        in_specs=[
            pl.BlockSpec((nb, D_MODEL), lambda i, j: (0, 0)),
            pl.BlockSpec((None, D_MODEL, D_MODEL), lambda i, j: (i, 0, j)),
            pl.BlockSpec((None, 1, D_MODEL), lambda i, j: (i, 0, j)),
        ],
        out_specs=pl.BlockSpec((None, nb, D_MODEL), lambda i, j: (i, 0, j)),
        out_shape=jax.ShapeDtypeStruct((n, nb, 3 * D_MODEL), F32),
        compiler_params=_params(("arbitrary", "arbitrary")),
        name="ada",
    )(c, ada_w, ada_b)


def _norm_mod(x, g, scale, shift):
    var = jnp.mean(x * x, axis=-1, keepdims=True)
    return (x * lax.rsqrt(var + NORM_EPS) * g) * (1.0 + scale) + shift


def _inproj_kernel(x_ref, shift_ref, scale_ref, g_ref, w_ref, mu_ref, side_ref,
                   ps_ref, mla_ref, gates_ref, side_out_ref, carry_ref, *, tiles_per_seq, tm):
    i = pl.program_id(0)
    side_out_ref[...] = side_ref[...].astype(BF16)

    @pl.when(i % tiles_per_seq == 0)
    def _():
        carry_ref[...] = jnp.zeros_like(carry_ref)

    carry = carry_ref[0:1, :]
    row = lax.broadcasted_iota(jnp.int32, (INPROJ_SUB, PS_W), 0)
    for s in range(0, tm, INPROJ_SUB):
        r = slice(s, s + INPROJ_SUB)
        h = _norm_mod(x_ref[r, :], g_ref[...], scale_ref[...], shift_ref[...]).astype(BF16)
        p = _dot_nt(h, w_ref[0:PS_W, :])
        prev = jnp.where(row == 0, carry, pltpu.roll(p, 1, axis=0))
        carry = p[INPROJ_SUB - 1:INPROJ_SUB, :]
        ps_ref[r, :] = p + (prev - p) * mu_ref[...]
        mla_ref[r, :] = _dot_nt(h, w_ref[PS_W:PS_W + MLA_W, :])
        gates_ref[r, :] = _sigmoid(_dot_nt(h, w_ref[PS_W + MLA_W:, :])).astype(BF16)
    carry_ref[0:1, :] = carry


INPROJ_SUB = 256


def _inproj_call(x2, shift, scale, g, w_pad, mu_pad, side, seq, tm=512):
    t = x2.shape[0]
    steps = t // tm
    tiles_per_seq = seq // tm
    bmap = lambda i: (i // tiles_per_seq, 0, 0)
    wtot = PS_W + MLA_W + GATE_W
    side_spec = pl.BlockSpec((side.shape[0] // steps, side.shape[1]), lambda i: (i, 0))
    return pl.pallas_call(
        functools.partial(_inproj_kernel, tiles_per_seq=tiles_per_seq, tm=tm),
        grid=(steps,),
        in_specs=[
            pl.BlockSpec((tm, D_MODEL), lambda i: (i, 0)),
            pl.BlockSpec((None, 1, D_MODEL), bmap),
            pl.BlockSpec((None, 1, D_MODEL), bmap),
            pl.BlockSpec((1, D_MODEL), lambda i: (0, 0)),
            pl.BlockSpec((wtot, D_MODEL), lambda i: (0, 0), pipeline_mode=pl.Buffered(1)),
            pl.BlockSpec((1, PS_W), lambda i: (0, 0)),
            side_spec,
        ],
        out_specs=[
            pl.BlockSpec((tm, PS_W), lambda i: (i, 0)),
            pl.BlockSpec((tm, MLA_W), lambda i: (i, 0)),
            pl.BlockSpec((tm, GATE_W), lambda i: (i, 0)),
            side_spec,
        ],
        out_shape=[
            jax.ShapeDtypeStruct((t, PS_W), F32),
            jax.ShapeDtypeStruct((t, MLA_W), F32),
            jax.ShapeDtypeStruct((t, GATE_W), BF16),
            jax.ShapeDtypeStruct(side.shape, BF16),
        ],
        scratch_shapes=[pltpu.VMEM((8, PS_W), F32)],
        compiler_params=_params(("arbitrary",)),
        name="inproj",
    )(x2, shift, scale, g, w_pad, mu_pad, side)


def _blockdiag(xb, bdmask):
    return jnp.tile(xb, (HALF // CHUNK, 1)) * bdmask


def _bd_nn(a, b, bdmask):
    return _dot(a.astype(BF16), _blockdiag(b.astype(BF16), bdmask))


def _bd_nt(a, b, bdmask):
    return _dot_nt(a.astype(BF16), _blockdiag(b.astype(BF16), bdmask))


def _bd_tn(a, b, lane_head):
    full = _dot(a.T.astype(BF16), b.astype(BF16))
    out = full[(HALF // CHUNK - 1) * CHUNK:, :]
    for h in range(HALF // CHUNK - 2, -1, -1):
        out = jnp.where(lane_head == h, full[h * CHUNK:(h + 1) * CHUNK, :], out)
    return out


def _halves(fn, *xs):
    outs = [fn(*[x[:, h * HALF:(h + 1) * HALF] for x in xs]) for h in range(RW // HALF)]
    return jnp.concatenate(outs, axis=1)


def _rwkv_kernel(ps_ref, w0_ref, a0_ref, kk_ref, ka_ref, rk_ref, lg_ref, lb_ref,
                 wup_ref, aup_ref, gup_ref, out_ref, s_ref, *, nc):
    @pl.when(pl.program_id(1) == 0)
    def _():
        s_ref[...] = jnp.zeros_like(s_ref)

    C = CHUNK
    r_i = lax.broadcasted_iota(jnp.int32, (HALF, HALF), 0)
    c_i = lax.broadcasted_iota(jnp.int32, (HALF, HALF), 1)
    bdmask = ((r_i // C) == (c_i // C)).astype(BF16)
    t_i = lax.broadcasted_iota(jnp.int32, (C, HALF), 0)
    lane_head = lax.broadcasted_iota(jnp.int32, (C, HALF), 1) // C
    i_i = lax.broadcasted_iota(jnp.int32, (C, HALF), 1) % C
    strict = i_i < t_i
    incl = i_i <= t_i
    eye_l = (i_i == t_i).astype(F32)
    n = nc * C
    tr = lax.broadcasted_iota(jnp.int32, (n, n), 0)
    tc = lax.broadcasted_iota(jnp.int32, (n, n), 1)
    tril = ((tc <= tr) & (tc // C == tr // C)).astype(BF16)

    def headsum(x, split=False):
        def one(xh):
            if not split:
                return _dot(xh.astype(BF16), bdmask)
            hi, lo = _split2(xh)
            return _dot(hi, bdmask) + _dot(lo, bdmask)
        return _halves(one, x)

    def cumsum_rows(x):
        hi = x.astype(BF16)
        r1 = x - hi.astype(F32)
        mid = r1.astype(BF16)
        lo = (r1 - mid.astype(F32)).astype(BF16)
        return _dot(tril, hi) + _dot(tril, mid) + _dot(tril, lo)

    p_r = ps_ref[:, 0:RW]
    p_k = ps_ref[:, RW:2 * RW]
    p_v = ps_ref[:, 2 * RW:3 * RW]
    p_w = ps_ref[:, 3 * RW:3 * RW + LANE]
    p_a = ps_ref[:, 3 * RW + LANE:3 * RW + 2 * LANE]
    p_g = ps_ref[:, 3 * RW + 2 * LANE:3 * RW + 3 * LANE]
    logw = -DECAY_SCALE * _sigmoid(w0_ref[...] + _dot3(jnp.tanh(p_w), wup_ref[...]))
    iclr = _sigmoid(a0_ref[...] + _dot(p_a.astype(BF16), aup_ref[...].astype(BF16)))
    gate = _dot(_sigmoid(p_g).astype(BF16), gup_ref[...].astype(BF16))
    kk = p_k * kk_ref[...]
    k2 = p_k * (1.0 + (iclr - 1.0) * ka_ref[...])
    sums = headsum(jnp.concatenate([kk * kk, p_r * k2 * rk_ref[...]], axis=0))
    kk = kk * lax.rsqrt(sums[:n] + 1e-12)
    bonus = sums[n:] * p_v
    b_v = kk * iclr
    cum = cumsum_rows(logw)
    cum_c = jnp.concatenate(
        [jnp.broadcast_to(cum[(c + 1) * C - 1:(c + 1) * C, :], (C, RW)) for c in range(nc)], axis=0)
    e_neg = jnp.exp(-cum)
    e_rem = jnp.exp(cum_c - cum)
    a_all = -kk * jnp.exp(cum - logw)
    b_all = b_v * e_neg
    k_all = k2 * e_neg
    r_all = p_r * jnp.exp(cum)
    bg_all = b_v * e_rem
    kg_all = k2 * e_rem
    gam_all = jnp.exp(cum_c)

    nh = RW // HALF
    units = [(c, h) for c in range(nc) for h in range(nh)]

    def cut(x, u):
        c, h = u
        return x[c * C:(c + 1) * C, h * HALF:(h + 1) * HALF]

    each = lambda fn, *lists: [fn(*xs) for xs in zip(*lists)]
    stack = lambda x, y: jnp.concatenate([x, y], axis=0)
    top = lambda x: x[:C]
    bot = lambda x: x[C:]
    nn = lambda a, b: _bd_nn(a, b, bdmask)
    nt = lambda a, b: _bd_nt(a, b, bdmask)

    a_t, b_t, k_t, r_t, b_g, k_g, v_u = ([cut(x, u) for u in units]
                                         for x in (a_all, b_all, k_all, r_all, bg_all, kg_all, p_v))
    gam = [gam_all[c * C:c * C + 1, h * HALF:(h + 1) * HALF] for c, h in units]
    ar = each(stack, a_t, r_t)
    sb = each(nt, ar, b_t)
    sk = each(nt, ar, k_t)
    l_ab = [jnp.where(strict, top(x), 0.0) for x in sb]
    q_b = [jnp.where(incl, bot(x), 0.0) for x in sb]
    m_ak = [jnp.where(strict, top(x), 0.0) for x in sk]
    q_k = [jnp.where(incl, bot(x), 0.0) for x in sk]
    t_m = [eye_l + x for x in l_ab]
    pw = each(nn, l_ab, l_ab)
    for _ in range(4):
        tp = each(nn, each(stack, t_m, pw), pw)
        t_m = each(lambda t, x: t + top(x), t_m, tp)
        pw = [bot(x) for x in tp]
    t_m = each(lambda t, x: t + x, t_m, each(nn, t_m, pw))
    mq = each(nn, each(stack, m_ak, q_k), v_u)
    a_h = each(nn, t_m, a_t)
    u0 = each(nn, t_m, [top(x) for x in mq])
    r_h = each(lambda r, x: r + x, r_t, each(nn, q_b, a_h))
    y0 = each(lambda m, x: bot(m) + x, mq, each(nn, q_b, u0))
    g_l = each(lambda a, b, g: _bd_tn(a, b, lane_head) + eye_l * g, a_h, b_g, gam)
    h_l = each(lambda u, v, b, k: _bd_tn(stack(u, v), stack(b, k), lane_head), u0, v_u, b_g, k_g)

    s_all = s_ref[...]
    s_cur = [s_all[:, h * HALF:(h + 1) * HALF] for h in range(nh)]
    ys = []
    for c in range(nc):
        idx = [c * nh + h for h in range(nh)]
        yc = [nt(r_h[i], s) + y0[i] for i, s in zip(idx, s_cur)]
        s_cur = [nn(s, g_l[i]) + h_l[i] for i, s in zip(idx, s_cur)]
        ys.append(jnp.concatenate(yc, axis=1))
    s_ref[...] = jnp.concatenate(s_cur, axis=1)
    y = jnp.concatenate(ys, axis=0)
    inv_n = 1.0 / RWKV_HEAD_DIM
    mu = headsum(y, split=True) * inv_n
    d = y - mu
    var = headsum(d * d) * inv_n
    yn = d * lax.rsqrt(var + LNX_EPS) * lg_ref[...] + lb_ref[...]
    out_ref[...] = ((yn + bonus) * gate).astype(BF16)


def _rwkv_call(ps, vecs, wup, aup, gup, batch, seq, nc=4):
    t = ps.shape[0]
    rows = nc * CHUNK
    steps = seq // rows
    vec_spec = pl.BlockSpec((1, RW), lambda b, j: (0, 0))
    lora_spec = pl.BlockSpec((LANE, RW), lambda b, j: (0, 0))
    return pl.pallas_call(
        functools.partial(_rwkv_kernel, nc=nc),
        grid=(batch, steps),
        in_specs=[pl.BlockSpec((rows, PS_W), lambda b, j: (b * steps + j, 0))]
                 + [vec_spec] * 7 + [lora_spec] * 3,
        out_specs=pl.BlockSpec((rows, RW), lambda b, j: (b * steps + j, 0)),
        out_shape=jax.ShapeDtypeStruct((t, RW), BF16),
        scratch_shapes=[pltpu.VMEM((CHUNK, RW), F32)],
        compiler_params=_params(("arbitrary", "arbitrary")),
        name="rwkv",
    )(ps, *vecs, wup, aup, gup)


def _mlaprep_kernel(mla_ref, pos_ref, invf_ref, qn_ref, kvn_ref, wq_ref, wqs_ref, wk_ref, wv_ref,
                    q_ref, k_ref, v_ref):
    def rms(x, g):
        return (x * lax.rsqrt(jnp.mean(x * x, axis=-1, keepdims=True) + NORM_EPS) * g).astype(BF16)

    ang = pos_ref[...].astype(F32) * invf_ref[...]
    cos = jnp.cos(ang)
    sin = jnp.sin(ang)
    lane = lax.broadcasted_iota(jnp.int32, cos.shape, 1)
    cos_q = jnp.tile(jnp.where(lane < QK_NOPE, 1.0, cos), (1, MLA_HEADS))
    sin_q = jnp.tile(sin, (1, MLA_HEADS))
    qn = rms(mla_ref[:, 0:Q_LORA], qn_ref[...])
    q = _dot(qn, wq_ref[...]) * cos_q + _dot(qn, wqs_ref[...]) * sin_q
    q_ref[...] = (q * ((QK_NOPE + QK_ROPE) ** -0.5 * LOG2E)).astype(BF16)
    kvn = rms(mla_ref[:, Q_LORA:Q_LORA + KV_LORA], kvn_ref[...])
    o = Q_LORA + KV_LORA
    k_pe = mla_ref[:, o:o + LANE] * cos + mla_ref[:, o + LANE:o + 2 * LANE] * sin
    k_ref[...] = (_dot(kvn, wk_ref[...]) + jnp.tile(k_pe, (1, MLA_HEADS))).astype(BF16)
    v_ref[...] = _dot(kvn, wv_ref[...]).astype(BF16)


def _mlaprep_call(mla, pos, invf, qn, kvn, wq, wqs, wk, wv, tm=512):
    t = mla.shape[0]
    hw = MLA_HEADS * HEAD_PAD
    vw = MLA_HEADS * V_HEAD
    full = lambda shape: pl.BlockSpec(shape, lambda i: (0, 0))
    row = lambda w: pl.BlockSpec((tm, w), lambda i: (i, 0))
    return pl.pallas_call(
        _mlaprep_kernel,
        grid=(t // tm,),
        in_specs=[row(MLA_W), row(1), full((1, LANE)), full((1, Q_LORA)), full((1, KV_LORA)),
                  full((Q_LORA, hw)), full((Q_LORA, hw)), full((KV_LORA, hw)), full((KV_LORA, vw))],
        out_specs=[row(hw), row(hw), row(vw)],
        out_shape=[jax.ShapeDtypeStruct((t, w), BF16) for w in (hw, hw, vw)],
        compiler_params=_params(("arbitrary",)),
        name="mlaprep",
    )(mla, pos, invf, qn, kvn, wq, wqs, wk, wv)


def _attn_kernel(q_ref, k_ref, v_ref, o_ref, *, seq, tq):
    r_i = lax.broadcasted_iota(jnp.int32, (tq, tq), 0) // CHUNK
    c_i = lax.broadcasted_iota(jnp.int32, (tq, tq), 1) // CHUNK
    mask = c_i <= r_i
    def scores(item):
        qi, hh = item
        lo = qi * tq
        qk = slice(hh * HEAD_PAD, (hh + 1) * HEAD_PAD)
        q = q_ref[lo:lo + tq, qk]
        sd = jnp.where(mask, _dot_nt(q, k_ref[lo:lo + tq, qk]), -1e30)
        so = _dot_nt(q, k_ref[0:lo, qk]) if qi > 0 else None
        return sd, so

    items = [(qi, hh) for qi in range(seq // tq) for hh in range(ATTN_HEADS_PER_STEP)]
    nxt = scores(items[0])
    for n, (qi, hh) in enumerate(items):
        lo = qi * tq
        vo = slice(hh * V_HEAD, (hh + 1) * V_HEAD)
        sd, so = nxt
        if n + 1 < len(items):
            nxt = scores(items[n + 1])
        m = jnp.max(sd, axis=-1, keepdims=True)
        if qi > 0:
            m = jnp.maximum(m, jnp.max(so, axis=-1, keepdims=True))
        pd = jnp.exp2(sd - m)
        l = jnp.sum(pd, axis=-1, keepdims=True)
        acc = _dot(pd.astype(BF16), v_ref[lo:lo + tq, vo])
        if qi > 0:
            po = jnp.exp2(so - m)
            l = l + jnp.sum(po, axis=-1, keepdims=True)
            acc = acc + _dot(po.astype(BF16), v_ref[0:lo, vo])
        o_ref[lo:lo + tq, vo] = (acc / l).astype(BF16)


ATTN_HEADS_PER_STEP = LANE // V_HEAD


def _attn_call(q, k, v, batch, seq, tq=256):
    hp = ATTN_HEADS_PER_STEP
    qk_spec = pl.BlockSpec((seq, hp * HEAD_PAD), lambda b, h: (b, h))
    vo_spec = pl.BlockSpec((seq, hp * V_HEAD), lambda b, h: (b, h))
    return pl.pallas_call(
        functools.partial(_attn_kernel, seq=seq, tq=tq),
        grid=(batch, MLA_HEADS // hp),
        in_specs=[qk_spec, qk_spec, vo_spec],
        out_specs=vo_spec,
        out_shape=jax.ShapeDtypeStruct(v.shape, BF16),
        compiler_params=_params(("arbitrary", "arbitrary")),
        name="attn",
    )(q, k, v)


MIX_SUB = 256


def _mixout_kernel(*refs, router):
    if router:
        (rw_ref, o_ref, gates_ref, x_ref, gate_ref, wr_ref, wm_ref, wo_ref,
         g2_ref, shift2_ref, scale2_ref, rwt_ref, rb_ref,
         xo_ref, h2_ref, ti_ref, tp_ref) = refs
    else:
        (rw_ref, o_ref, gates_ref, x_ref, gate_ref, wr_ref, wm_ref, wo_ref,
         g2_ref, shift2_ref, scale2_ref, xo_ref, h2_ref) = refs
    n = x_ref.shape[0]
    subs = [slice(s, s + MIX_SUB) for s in range(0, n, MIX_SUB)]
    yab = [(_dot(rw_ref[r, :], wr_ref[...]), _dot(o_ref[r, :], wm_ref[...])) for r in subs]
    merged = [gates_ref[r, 0:D_MODEL].astype(F32) * ya + gates_ref[r, D_MODEL:].astype(F32) * yb
              for r, (ya, yb) in zip(subs, yab)]
    mixed = [_dot(m.astype(BF16), wo_ref[...]) for m in merged]
    h2s = []
    for r, mx in zip(subs, mixed):
        xn = x_ref[r, :] + gate_ref[...] * mx
        xo_ref[r, :] = xn
        h2s.append(_norm_mod(xn, g2_ref[...], scale2_ref[...], shift2_ref[...]))
        h2_ref[r, :] = h2s[-1].astype(h2_ref.dtype)
    if router:
        h2 = jnp.concatenate(h2s, axis=0)
        hh, hl = _split2(h2)
        wh, wl = _split2(rwt_ref[...])
        logits = _dot_nt(wh, hh) + _dot_nt(wl, hh) + _dot_nt(wh, hl) + rb_ref[...]
        e_i = lax.broadcasted_iota(jnp.int32, logits.shape, 0)
        m1 = jnp.max(logits, axis=0, keepdims=True)
        i1 = jnp.min(jnp.where(logits == m1, e_i, N_EXPERTS), axis=0, keepdims=True)
        rest = jnp.where(e_i == i1, -jnp.inf, logits)
        m2 = jnp.max(rest, axis=0, keepdims=True)
        i2 = jnp.min(jnp.where(rest == m2, e_i, N_EXPERTS), axis=0, keepdims=True)
        e2 = jnp.exp(m2 - m1)
        den = 1.0 + e2
        ti_ref[0:1, :] = i1
        ti_ref[1:2, :] = i2
        tp_ref[0:1, :] = 1.0 / den
        tp_ref[1:2, :] = e2 / den


def _mixout_call(rw, o, gates, x2, gate, wr, wm, wo, g2, shift2, scale2, seq, router=None, tm=512):
    t = x2.shape[0]
    tiles_per_seq = seq // tm
    bmap = lambda i: (i // tiles_per_seq, 0, 0)
    full = lambda shape: pl.BlockSpec(shape, lambda i: (0, 0))
    row = lambda w: pl.BlockSpec((tm, w), lambda i: (i, 0))
    mod = pl.BlockSpec((None, 1, D_MODEL), bmap)
    in_specs = [row(RW), row(MLA_HEADS * V_HEAD), row(GATE_W), row(D_MODEL), mod,
                full(wr.shape), full(wm.shape), full(wo.shape), full((1, D_MODEL)), mod, mod]
    args = [rw, o, gates, x2, gate, wr, wm, wo, g2, shift2, scale2]
    out_specs = [row(D_MODEL), row(D_MODEL)]
    out_shape = [jax.ShapeDtypeStruct((t, D_MODEL), F32),
                 jax.ShapeDtypeStruct((t, D_MODEL), BF16 if router is None else F32)]
    if router is not None:
        in_specs += [full((N_EXPERTS, D_MODEL)), full((N_EXPERTS, 1))]
        args += list(router)
        out_specs += [pl.BlockSpec((2, tm), lambda i: (0, i))] * 2
        out_shape += [jax.ShapeDtypeStruct((2, t), jnp.int32), jax.ShapeDtypeStruct((2, t), F32)]
    return pl.pallas_call(
        functools.partial(_mixout_kernel, router=router is not None),
        grid=(t // tm,),
        in_specs=in_specs,
        out_specs=out_specs,
        out_shape=out_shape,
        compiler_params=_params(("arbitrary",)),
        name="mixout",
    )(*args)


def _swiglu_step(h, w1_ref, w3_ref, w2_ref, acc_ref, finish):
    f = pl.program_id(1)

    @pl.when(f == 0)
    def _():
        acc_ref[...] = jnp.zeros_like(acc_ref)

    a1 = _dot(h, w1_ref[...])
    a3 = _dot(h, w3_ref[...])
    act = (a1 * _sigmoid(a1)) * a3
    acc_ref[...] += _dot(act.astype(BF16), w2_ref[...].astype(BF16))

    @pl.when(f == pl.num_programs(1) - 1)
    def _():
        finish(acc_ref[...])


def _rms(x, g):
    return x * lax.rsqrt(jnp.mean(x * x, axis=-1, keepdims=True) + NORM_EPS) * g


def _ffn_kernel(h_ref, x_ref, gate_ref, gf_ref, w1_ref, w3_ref, w2_ref, o_ref, *, final, tf):
    h = h_ref[...]
    ff = w1_ref.shape[1]
    ups = [(_dot(h, w1_ref[:, s:s + tf]), _dot(h, w3_ref[:, s:s + tf])) for s in range(0, ff, tf)]
    acc = None
    for s, (a1, a3) in zip(range(0, ff, tf), ups):
        act = ((a1 * _sigmoid(a1)) * a3).astype(BF16)
        part = _dot(act, w2_ref[s:s + tf, :])
        acc = part if acc is None else acc + part
    xn = x_ref[...] + gate_ref[...] * acc
    o_ref[...] = _rms(xn, gf_ref[...]) if final else xn


def _ffn_call(h, x2, gate, gf, w1, w3, w2, seq, final, tm=512, tf=1408):
    r = h.shape[0]
    tiles_per_seq = seq // tm
    row = pl.BlockSpec((tm, D_MODEL), lambda i: (i, 0))
    resident = lambda a: pl.BlockSpec(a.shape, lambda i: (0, 0), pipeline_mode=pl.Buffered(1))
    return pl.pallas_call(
        functools.partial(_ffn_kernel, final=final, tf=tf),
        grid=(r // tm,),
        in_specs=[
            row, row,
            pl.BlockSpec((None, 1, D_MODEL), lambda i: (i // tiles_per_seq, 0, 0)),
            pl.BlockSpec((1, D_MODEL), lambda i: (0, 0)),
            resident(w1), resident(w3), resident(w2),
        ],
        out_specs=row,
        out_shape=jax.ShapeDtypeStruct((r, D_MODEL), F32),
        compiler_params=_params(("arbitrary",)),
        name="ffn",
    )(h, x2, gate, gf, w1, w3, w2)


MOE_TM = 512
ROW_TILE = (D_MODEL // LANE, LANE)


def _route(top_i, tm):
    t = top_i.shape[1]
    e_flat = top_i.reshape(-1)
    oh = (e_flat[:, None] == jnp.arange(N_EXPERTS, dtype=jnp.int32)[None, :]).astype(jnp.int32)
    csum = jnp.cumsum(oh, axis=0)
    rank = jnp.sum(csum * oh, axis=1) - 1
    counts = csum[-1]
    padded = ((counts + tm - 1) // tm) * tm
    ends = jnp.cumsum(padded)
    pos = jnp.sum(oh * (ends - padded)[None, :], axis=1) + rank
    n_tiles = (2 * t) // tm + N_EXPERTS
    starts = jnp.arange(n_tiles, dtype=jnp.int32) * tm
    tile_e = jnp.sum((starts[:, None] >= ends[None, :]).astype(jnp.int32), axis=1)
    return (pos.astype(jnp.int32), ends.astype(jnp.int32), jnp.minimum(tile_e, N_EXPERTS - 1),
            (ends[-1:] // tm).astype(jnp.int32))


def _dispatch_kernel(pos_ref, ends_ref, h_ref, xs_ref, zero_ref, sem, *, t, td, tm):
    i = pl.program_id(0)
    base = i * td

    @pl.when(i == 0)
    def _():
        zero_ref[...] = jnp.zeros_like(zero_ref)
        n_tiles = xs_ref.shape[0] // tm
        fills = []
        for e in range(N_EXPERTS):
            start = ends_ref[e - 1] if e else 0
            fills.append((ends_ref[e] > start, pl.multiple_of(ends_ref[e] - tm, tm)))
            fills.append((ends_ref[N_EXPERTS - 1] // tm + e < n_tiles,
                          pl.multiple_of(ends_ref[N_EXPERTS - 1] + e * tm, tm)))
        for go, row0 in fills:
            @pl.when(go)
            def _():
                pltpu.make_async_copy(zero_ref, xs_ref.at[pl.ds(row0, tm)], sem).start()
        for go, row0 in fills:
            @pl.when(go)
            def _():
                pltpu.make_async_copy(zero_ref, xs_ref.at[pl.ds(row0, tm)], sem).wait()

    for r in range(td):
        for k in range(2):
            dst = pos_ref[k * t + base + r]
            pltpu.make_async_copy(h_ref.at[pl.ds(r, 1)], xs_ref.at[pl.ds(dst, 1)], sem).start(priority=k)
    for k in range(2):
        pltpu.make_async_copy(h_ref, xs_ref.at[pl.ds(0, td)], sem).wait()


def _dispatch_call(pos, ends, h, n_rows, tm, td=512):
    t = h.shape[0]
    return pl.pallas_call(
        functools.partial(_dispatch_kernel, t=t, td=td, tm=tm),
        grid_spec=pltpu.PrefetchScalarGridSpec(
            num_scalar_prefetch=2, grid=(t // td,),
            in_specs=[pl.BlockSpec((td, D_MODEL), lambda i, pos, ends: (i, 0))],
            out_specs=pl.BlockSpec(memory_space=pl.ANY),
            scratch_shapes=[pltpu.VMEM((tm, D_MODEL), h.dtype), pltpu.SemaphoreType.DMA(())]),
        out_shape=jax.ShapeDtypeStruct((n_rows, D_MODEL), h.dtype),
        compiler_params=_params(("arbitrary",)),
        name="dispatch",
    )(pos, ends, h)


def _ffn_routed_kernel(te_ref, nu_ref, x_ref, w1_ref, w3_ref, w2_ref, y_ref, acc_ref):
    del te_ref
    used = pl.program_id(0) < nu_ref[0]

    def finish(acc):
        y_ref[...] = acc.reshape(y_ref.shape)

    @pl.when(used)
    def _():
        _swiglu_step(x_ref[...].astype(BF16), w1_ref, w3_ref, w2_ref, acc_ref, finish)

    @pl.when(jnp.logical_not(used))
    def _():
        y_ref[...] = jnp.zeros_like(y_ref)


def _ffn_routed_call(tile_e, n_used, xs, w1, w3, w2, tm=MOE_TM, tf=1792):
    r = xs.shape[0]
    ff = w1.shape[2]
    return pl.pallas_call(
        _ffn_routed_kernel,
        grid_spec=pltpu.PrefetchScalarGridSpec(
            num_scalar_prefetch=2, grid=(r // tm, ff // tf),
            in_specs=[
                pl.BlockSpec((tm, D_MODEL), lambda i, f, te, nu: (i, 0)),
                pl.BlockSpec((None, D_MODEL, tf), lambda i, f, te, nu: (te[i], 0, f)),
                pl.BlockSpec((None, D_MODEL, tf), lambda i, f, te, nu: (te[i], 0, f)),
                pl.BlockSpec((None, tf, D_MODEL), lambda i, f, te, nu: (te[i], f, 0)),
            ],
            out_specs=pl.BlockSpec((tm,) + ROW_TILE, lambda i, f, te, nu: (i, 0, 0)),
            scratch_shapes=[pltpu.VMEM((tm, D_MODEL), F32)]),
        out_shape=jax.ShapeDtypeStruct((r,) + ROW_TILE, F32),
        compiler_params=_params(("arbitrary", "arbitrary")),
        name="ffn_routed",
    )(tile_e, n_used, xs, w1, w3, w2)


def _combine_kernel(pos_ref, x_ref, p_ref, gate_ref, gf_ref, y_ref, o_ref, buf, sem, *, t, tc, final):
    i = pl.program_id(0)

    def row_copy(step, slot, r, k):
        src = pos_ref[k * t + step * tc + r]
        return pltpu.make_async_copy(y_ref.at[pl.ds(src, 1)], buf.at[slot, k, pl.ds(r, 1)], sem.at[slot])

    def issue(step, slot):
        for r in range(tc):
            for k in range(2):
                row_copy(step, slot, r, k).start(priority=k)

    @pl.when(i == 0)
    def _():
        issue(0, 0)

    @pl.when(i + 1 < pl.num_programs(0))
    def _():
        issue(i + 1, (i + 1) % 2)

    slot = i % 2
    for k in range(2):
        pltpu.make_async_copy(y_ref.at[pl.ds(0, tc)], buf.at[slot, k], sem.at[slot]).wait()
    p = p_ref[...]
    y0 = buf[slot, 0].reshape(tc, D_MODEL)
    y1 = buf[slot, 1].reshape(tc, D_MODEL)
    xn = x_ref[...] + gate_ref[...] * (p[:, 0:1] * y0 + p[:, 1:2] * y1)
    o_ref[...] = _rms(xn, gf_ref[...]) if final else xn


def _combine_call(pos, x2, probs, gate, gf, y, seq, final, tc=256):
    t = x2.shape[0]
    tiles_per_seq = seq // tc
    row = pl.BlockSpec((tc, D_MODEL), lambda i, pos: (i, 0))
    return pl.pallas_call(
        functools.partial(_combine_kernel, t=t, tc=tc, final=final),
        grid_spec=pltpu.PrefetchScalarGridSpec(
            num_scalar_prefetch=1, grid=(t // tc,),
            in_specs=[row, pl.BlockSpec((tc, 2), lambda i, pos: (i, 0)),
                      pl.BlockSpec((None, 1, D_MODEL), lambda i, pos: (i // tiles_per_seq, 0, 0)),
                      pl.BlockSpec((1, D_MODEL), lambda i, pos: (0, 0)),
                      pl.BlockSpec(memory_space=pl.ANY)],
            out_specs=row,
            scratch_shapes=[pltpu.VMEM((2, 2, tc) + ROW_TILE, F32), pltpu.SemaphoreType.DMA((2,))]),
        out_shape=jax.ShapeDtypeStruct((t, D_MODEL), F32),
        compiler_params=_params(("arbitrary",)),
        name="combine",
    )(pos, x2, probs, gate, gf, y)


def _layout_w_in_kernel(w_ref, o_ref):
    half = QK_ROPE // 2
    o_ref[...] = jnp.zeros_like(o_ref)

    def put(dst, src, width, sign=1.0):
        o_ref[dst:dst + width, :] = (sign * w_ref[src:src + width, :]).astype(BF16)

    o = 3 * RW
    put(0, 0, o + DECAY_LORA)
    put(o + LANE, o + DECAY_LORA, ICLR_LORA)
    put(o + 2 * LANE, o + DECAY_LORA + ICLR_LORA, GATE_LORA)
    src = o + DECAY_LORA + ICLR_LORA + GATE_LORA
    put(PS_W, src, Q_LORA + KV_LORA)
    kr = src + Q_LORA + KV_LORA
    dst = PS_W + Q_LORA + KV_LORA
    put(dst + QK_NOPE, kr, QK_ROPE)
    put(dst + LANE + QK_NOPE, kr + half, half, -1.0)
    put(dst + LANE + QK_NOPE + half, kr, half)
    put(PS_W + MLA_W, kr + QK_ROPE, GATE_W)


def _layout_w_in(w_t, layer):
    _, n, k = w_t.shape
    return pl.pallas_call(
        _layout_w_in_kernel,
        grid=(k // 256,),
        in_specs=[pl.BlockSpec((None, n, 256), lambda i: (layer, 0, i))],
        out_specs=pl.BlockSpec((PS_W + MLA_W + GATE_W, 256), lambda i: (0, i)),
        out_shape=jax.ShapeDtypeStruct((PS_W + MLA_W + GATE_W, k), BF16),
        compiler_params=_params(("arbitrary",)),
        name="layout_w_in",
    )(w_t)


def _layout_mu(mu):
    z = jnp.zeros((64,), mu.dtype)
    o = 3 * RW
    return jnp.concatenate([mu[:o + 64], z, mu[o + 64:o + 128], z, mu[o + 128:]])[None, :]


def _layout_wq(w_qb):
    hd = QK_NOPE + QK_ROPE
    w = w_qb.reshape(Q_LORA, MLA_HEADS, hd)
    half = QK_ROPE // 2
    x1, x2 = w[:, :, QK_NOPE:QK_NOPE + half], w[:, :, QK_NOPE + half:]
    zn = jnp.zeros((Q_LORA, MLA_HEADS, QK_NOPE), w.dtype)
    zp = jnp.zeros((Q_LORA, MLA_HEADS, HEAD_PAD - hd), w.dtype)
    wq = jnp.concatenate([w, zp], axis=2).reshape(Q_LORA, -1)
    wqs = jnp.concatenate([zn, -x2, x1, zp], axis=2).reshape(Q_LORA, -1)
    return wq.astype(BF16), wqs.astype(BF16)


def _layout_wkv(w_kvb):
    w = w_kvb.reshape(KV_LORA, MLA_HEADS, QK_NOPE + V_HEAD)
    z = jnp.zeros((KV_LORA, MLA_HEADS, HEAD_PAD - QK_NOPE), w.dtype)
    wk = jnp.concatenate([w[:, :, :QK_NOPE], z], axis=2).reshape(KV_LORA, -1)
    wv = w[:, :, QK_NOPE:].reshape(KV_LORA, -1)
    return wk.astype(BF16), wv.astype(BF16)


def _rope_freq_row():
    inv_freq = ROPE_THETA ** (-jnp.arange(0, QK_ROPE, 2, dtype=F32) / QK_ROPE)
    z = jnp.zeros((QK_NOPE,), F32)
    zp = jnp.zeros((LANE - QK_NOPE - QK_ROPE,), F32)
    return jnp.concatenate([z, inv_freq, inv_freq, zp])[None, :]


def kernel(x, c, positions, ada_w, ada_b, norm_mix, norm_ffn, norm_final, w_in, tshift_mu, w0, w_up, a0, a_up, g_up, k_k, k_a, r_k, lnx_g, lnx_b, rwkv_out, q_norm, w_qb, kv_norm, w_kvb, mla_out, w_o, ffn_w1, ffn_w3, ffn_w2, router_w, router_b, moe_w1, moe_w3, moe_w2):
    batch, seq, _ = x.shape
    t = batch * seq
    mods = _ada_call(c, ada_w.reshape(2 * DEPTH, D_MODEL, 3 * D_MODEL),
                     ada_b.reshape(2 * DEPTH, 1, 3 * D_MODEL))
    mods = mods.reshape(DEPTH, 2, batch, 3, 1, D_MODEL)
    pos_col = positions.reshape(t, 1)
    invf = _rope_freq_row()
    x2 = x.reshape(t, D_MODEL)
    row = lambda v: v.reshape(1, -1)

    side_jobs = [moe_w1.reshape(-1, moe_w1.shape[-1]), moe_w3.reshape(-1, moe_w3.shape[-1])]
    moe_up = []
    w_in_t = jnp.swapaxes(w_in, 1, 2)

    for l in range(DEPTH):
        shift, scale, gate = (mods[l, 0, :, j] for j in range(3))
        ps, mla, gates, cast = _inproj_call(x2, shift, scale, row(norm_mix[l]), _layout_w_in(w_in_t, l),
                                            _layout_mu(tshift_mu[l]), side_jobs[l], seq)
        moe_up.append(cast.reshape(moe_w1.shape))
        vecs = [row(v[l]) for v in (w0, a0, k_k, k_a, r_k, lnx_g, lnx_b)]
        lora_pad = lambda w: jnp.pad(w, ((0, LANE - w.shape[0]), (0, 0)))
        rw = _rwkv_call(ps, vecs, lora_pad(w_up[l]), lora_pad(a_up[l]), g_up[l], batch, seq)
        wq, wqs = _layout_wq(w_qb[l])
        wk, wv = _layout_wkv(w_kvb[l])
        q, k, v = _mlaprep_call(mla, pos_col, invf, row(q_norm[l]), row(kv_norm[l]), wq, wqs, wk, wv)
        o = _attn_call(q, k, v, batch, seq)

        shift2, scale2, gate2 = (mods[l, 1, :, j] for j in range(3))
        moe = l % 2 == 1
        router = (router_w[l // 2].T, router_b[l // 2].reshape(N_EXPERTS, 1)) if moe else None
        outs = _mixout_call(rw, o, gates, x2, gate, rwkv_out[l].astype(BF16), mla_out[l].astype(BF16),
                            w_o[l].astype(BF16), row(norm_ffn[l]), shift2, scale2, seq, router=router)
        final = l == DEPTH - 1
        if moe:
            x2, h2, top_i, top_p = outs
            pos, ends, tile_e, n_used = _route(top_i, MOE_TM)
            xs = _dispatch_call(pos, ends, h2, 2 * t + N_EXPERTS * MOE_TM, MOE_TM)
            y = _ffn_routed_call(tile_e, n_used, xs, moe_up[0][l // 2], moe_up[1][l // 2], moe_w2[l // 2])
            x2 = _combine_call(pos, x2, top_p.T, gate2, row(norm_final), y, seq, final)
        else:
            x2, h2 = outs
            x2 = _ffn_call(h2, x2, gate2, row(norm_final), ffn_w1[l // 2].astype(BF16),
                           ffn_w3[l // 2].astype(BF16), ffn_w2[l // 2].astype(BF16), seq, final)
    return x2.reshape(batch, seq, D_MODEL)
```

```python
import functools
import math

import jax
import jax.numpy as jnp
from jax import lax
from jax.experimental import pallas as pl
from jax.experimental.pallas import tpu as pltpu

F32 = jnp.float32
BF16 = jnp.bfloat16

D_MODEL = 1024
DEPTH = 2
CHUNK = 64
NORM_EPS = 1e-6
RWKV_HEADS = 8
RWKV_HEAD_DIM = 64
RW = RWKV_HEADS * RWKV_HEAD_DIM
DECAY_LORA = 64
ICLR_LORA = 64
GATE_LORA = 128
DECAY_SCALE = math.exp(-0.5)
LNX_EPS = 64e-5
MLA_HEADS = 8
Q_LORA = 384
KV_LORA = 256
QK_NOPE = 64
QK_ROPE = 32
V_HEAD = 64
ROPE_THETA = 10000.0
LOG2E = math.log2(math.e)
N_EXPERTS = 8

LANE = 128
HEAD_PAD = 128
PS_W = 3 * RW + 3 * LANE
MLA_W = Q_LORA + KV_LORA + 2 * LANE
GATE_W = 2 * D_MODEL
HALF = 256
VMEM_LIMIT = 56 * 1024 * 1024


def _sigmoid(x):
    return 1.0 / (1.0 + jnp.exp(-x))


def _dot(a, b):
    return jnp.dot(a, b, preferred_element_type=F32)


def _dot_nt(a, b):
    return lax.dot_general(a, b, (((1,), (1,)), ((), ())), preferred_element_type=F32)


def _split2(x):
    hi = x.astype(BF16)
    lo = (x - hi.astype(F32)).astype(BF16)
    return hi, lo


def _dot3(a, b):
    ah, al = _split2(a)
    bh, bl = _split2(b)
    return _dot(ah, bh) + _dot(al, bh) + _dot(ah, bl)


def _params(sem):
    return pltpu.CompilerParams(dimension_semantics=sem, vmem_limit_bytes=VMEM_LIMIT)


def _ada_kernel(c_ref, w_ref, b_ref, o_ref):
    c = c_ref[...]
    o_ref[...] = _dot((c * _sigmoid(c)).astype(BF16), w_ref[...].astype(BF16)) + b_ref[...]


def _ada_call(c, ada_w, ada_b):
    n = ada_w.shape[0]
    nb = c.shape[0]
    return pl.pallas_call(
        _ada_kernel,
        grid=(n, 3),
        in_specs=[
            pl.BlockSpec((nb, D_MODEL), lambda i, j: (0, 0)),
            pl.BlockSpec((None, D_MODEL, D_MODEL), lambda i, j: (i, 0, j)),
            pl.BlockSpec((None, 1, D_MODEL), lambda i, j: (i, 0, j)),
        ],
        out_specs=pl.BlockSpec((None, nb, D_MODEL), lambda i, j: (i, 0, j)),
        out_shape=jax.ShapeDtypeStruct((n, nb, 3 * D_MODEL), F32),
        compiler_params=_params(("arbitrary", "arbitrary")),
        name="ada",
    )(c, ada_w, ada_b)


def _norm_mod(x, g, scale, shift):
    var = jnp.mean(x * x, axis=-1, keepdims=True)
    return (x * lax.rsqrt(var + NORM_EPS) * g) * (1.0 + scale) + shift


def _inproj_kernel(x_ref, shift_ref, scale_ref, g_ref, w_ref, mu_ref, side_ref,
                   ps_ref, mla_ref, gates_ref, side_out_ref, carry_ref, *, tiles_per_seq, tm):
    i = pl.program_id(0)
    side_out_ref[...] = side_ref[...].astype(BF16)

    @pl.when(i % tiles_per_seq == 0)
    def _():
        carry_ref[...] = jnp.zeros_like(carry_ref)

    carry = carry_ref[0:1, :]
    row = lax.broadcasted_iota(jnp.int32, (INPROJ_SUB, PS_W), 0)
    for s in range(0, tm, INPROJ_SUB):
        r = slice(s, s + INPROJ_SUB)
        h = _norm_mod(x_ref[r, :], g_ref[...], scale_ref[...], shift_ref[...]).astype(BF16)
        p = _dot_nt(h, w_ref[0:PS_W, :])
        prev = jnp.where(row == 0, carry, pltpu.roll(p, 1, axis=0))
        carry = p[INPROJ_SUB - 1:INPROJ_SUB, :]
        ps_ref[r, :] = p + (prev - p) * mu_ref[...]
        mla_ref[r, :] = _dot_nt(h, w_ref[PS_W:PS_W + MLA_W, :])
        gates_ref[r, :] = _sigmoid(_dot_nt(h, w_ref[PS_W + MLA_W:, :])).astype(BF16)
    carry_ref[0:1, :] = carry


INPROJ_SUB = 256


def _inproj_call(x2, shift, scale, g, w_pad, mu_pad, side, seq, tm=512):
    t = x2.shape[0]
    steps = t // tm
    tiles_per_seq = seq // tm
    bmap = lambda i: (i // tiles_per_seq, 0, 0)
    wtot = PS_W + MLA_W + GATE_W
    side_spec = pl.BlockSpec((side.shape[0] // steps, side.shape[1]), lambda i: (i, 0))
    return pl.pallas_call(
        functools.partial(_inproj_kernel, tiles_per_seq=tiles_per_seq, tm=tm),
        grid=(steps,),
        in_specs=[
            pl.BlockSpec((tm, D_MODEL), lambda i: (i, 0)),
            pl.BlockSpec((None, 1, D_MODEL), bmap),
            pl.BlockSpec((None, 1, D_MODEL), bmap),
            pl.BlockSpec((1, D_MODEL), lambda i: (0, 0)),
            pl.BlockSpec((wtot, D_MODEL), lambda i: (0, 0), pipeline_mode=pl.Buffered(1)),
            pl.BlockSpec((1, PS_W), lambda i: (0, 0)),
            side_spec,
        ],
        out_specs=[
            pl.BlockSpec((tm, PS_W), lambda i: (i, 0)),
            pl.BlockSpec((tm, MLA_W), lambda i: (i, 0)),
            pl.BlockSpec((tm, GATE_W), lambda i: (i, 0)),
            side_spec,
        ],
        out_shape=[
            jax.ShapeDtypeStruct((t, PS_W), F32),
            jax.ShapeDtypeStruct((t, MLA_W), F32),
            jax.ShapeDtypeStruct((t, GATE_W), BF16),
            jax.ShapeDtypeStruct(side.shape, BF16),
        ],
        scratch_shapes=[pltpu.VMEM((8, PS_W), F32)],
        compiler_params=_params(("arbitrary",)),
        name="inproj",
    )(x2, shift, scale, g, w_pad, mu_pad, side)


def _blockdiag(xb, bdmask):
    return jnp.tile(xb, (HALF // CHUNK, 1)) * bdmask


def _bd_nn(a, b, bdmask):
    return _dot(a.astype(BF16), _blockdiag(b.astype(BF16), bdmask))


def _bd_nt(a, b, bdmask):
    return _dot_nt(a.astype(BF16), _blockdiag(b.astype(BF16), bdmask))


def _bd_tn(a, b, lane_head):
    full = _dot(a.T.astype(BF16), b.astype(BF16))
    out = full[(HALF // CHUNK - 1) * CHUNK:, :]
    for h in range(HALF // CHUNK - 2, -1, -1):
        out = jnp.where(lane_head == h, full[h * CHUNK:(h + 1) * CHUNK, :], out)
    return out


def _halves(fn, *xs):
    outs = [fn(*[x[:, h * HALF:(h + 1) * HALF] for x in xs]) for h in range(RW // HALF)]
    return jnp.concatenate(outs, axis=1)


def _rwkv_kernel(ps_ref, w0_ref, a0_ref, kk_ref, ka_ref, rk_ref, lg_ref, lb_ref,
                 wup_ref, aup_ref, gup_ref, out_ref, s_ref, *, nseq, cps):
    @pl.when(pl.program_id(1) == 0)
    def _():
        s_ref[...] = jnp.zeros_like(s_ref)

    C = CHUNK
    nc = nseq * cps
    r_i = lax.broadcasted_iota(jnp.int32, (HALF, HALF), 0)
    c_i = lax.broadcasted_iota(jnp.int32, (HALF, HALF), 1)
    bdmask = ((r_i // C) == (c_i // C)).astype(BF16)
    t_i = lax.broadcasted_iota(jnp.int32, (C, HALF), 0)
    lane_head = lax.broadcasted_iota(jnp.int32, (C, HALF), 1) // C
    i_i = lax.broadcasted_iota(jnp.int32, (C, HALF), 1) % C
    strict = i_i < t_i
    incl = i_i <= t_i
    eye_l = (i_i == t_i).astype(F32)
    n = nc * C
    tr = lax.broadcasted_iota(jnp.int32, (n, n), 0)
    tc = lax.broadcasted_iota(jnp.int32, (n, n), 1)
    tril = ((tc <= tr) & (tc // C == tr // C)).astype(BF16)

    def headsum(x, split=False):
        def one(xh):
            if not split:
                return _dot(xh.astype(BF16), bdmask)
            hi, lo = _split2(xh)
            return _dot(hi, bdmask) + _dot(lo, bdmask)
        return _halves(one, x)

    def cumsum_rows(x):
        hi = x.astype(BF16)
        r1 = x - hi.astype(F32)
        mid = r1.astype(BF16)
        lo = (r1 - mid.astype(F32)).astype(BF16)
        return _dot(tril, hi) + _dot(tril, mid) + _dot(tril, lo)

    cols = lambda lo, hi: ps_ref[:, :, lo:hi].reshape(n, hi - lo)
    p_r = cols(0, RW)
    p_k = cols(RW, 2 * RW)
    p_v = cols(2 * RW, 3 * RW)
    p_w = cols(3 * RW, 3 * RW + LANE)
    p_a = cols(3 * RW + LANE, 3 * RW + 2 * LANE)
    p_g = cols(3 * RW + 2 * LANE, 3 * RW + 3 * LANE)
    logw = -DECAY_SCALE * _sigmoid(w0_ref[...] + _dot3(jnp.tanh(p_w), wup_ref[...]))
    iclr = _sigmoid(a0_ref[...] + _dot(p_a.astype(BF16), aup_ref[...].astype(BF16)))
    gate = _dot(_sigmoid(p_g).astype(BF16), gup_ref[...].astype(BF16))
    kk = p_k * kk_ref[...]
    k2 = p_k * (1.0 + (iclr - 1.0) * ka_ref[...])
    sums = headsum(jnp.concatenate([kk * kk, p_r * k2 * rk_ref[...]], axis=0))
    kk = kk * lax.rsqrt(sums[:n] + 1e-12)
    bonus = sums[n:] * p_v
    b_v = kk * iclr
    cum = cumsum_rows(logw)
    cum_c = jnp.concatenate(
        [jnp.broadcast_to(cum[(c + 1) * C - 1:(c + 1) * C, :], (C, RW)) for c in range(nc)], axis=0)
    e_neg = jnp.exp(-cum)
    e_rem = jnp.exp(cum_c - cum)
    a_all = -kk * jnp.exp(cum - logw)
    b_all = b_v * e_neg
    k_all = k2 * e_neg
    r_all = p_r * jnp.exp(cum)
    bg_all = b_v * e_rem
    kg_all = k2 * e_rem
    gam_all = jnp.exp(cum_c)

    nh = RW // HALF
    units = [(c, h) for c in range(nc) for h in range(nh)]

    def cut(x, u):
        c, h = u
        return x[c * C:(c + 1) * C, h * HALF:(h + 1) * HALF]

    each = lambda fn, *lists: [fn(*xs) for xs in zip(*lists)]
    stack = lambda x, y: jnp.concatenate([x, y], axis=0)
    top = lambda x: x[:C]
    bot = lambda x: x[C:]
    nn = lambda a, b: _bd_nn(a, b, bdmask)
    nt = lambda a, b: _bd_nt(a, b, bdmask)

    a_t, b_t, k_t, r_t, b_g, k_g, v_u = ([cut(x, u) for u in units]
                                         for x in (a_all, b_all, k_all, r_all, bg_all, kg_all, p_v))
    gam = [gam_all[c * C:c * C + 1, h * HALF:(h + 1) * HALF] for c, h in units]
    ar = each(stack, a_t, r_t)
    sb = each(nt, ar, b_t)
    sk = each(nt, ar, k_t)
    l_ab = [jnp.where(strict, top(x), 0.0) for x in sb]
    q_b = [jnp.where(incl, bot(x), 0.0) for x in sb]
    m_ak = [jnp.where(strict, top(x), 0.0) for x in sk]
    q_k = [jnp.where(incl, bot(x), 0.0) for x in sk]
    t_m = [eye_l + x for x in l_ab]
    pw = each(nn, l_ab, l_ab)
    for _ in range(4):
        tp = each(nn, each(stack, t_m, pw), pw)
        t_m = each(lambda t, x: t + top(x), t_m, tp)
        pw = [bot(x) for x in tp]
    t_m = each(lambda t, x: t + x, t_m, each(nn, t_m, pw))
    mq = each(nn, each(stack, m_ak, q_k), v_u)
    a_h = each(nn, t_m, a_t)
    u0 = each(nn, t_m, [top(x) for x in mq])
    r_h = each(lambda r, x: r + x, r_t, each(nn, q_b, a_h))
    y0 = each(lambda m, x: bot(m) + x, mq, each(nn, q_b, u0))
    g_l = each(lambda a, b, g: _bd_tn(a, b, lane_head) + eye_l * g, a_h, b_g, gam)
    h_l = each(lambda u, v, b, k: _bd_tn(stack(u, v), stack(b, k), lane_head), u0, v_u, b_g, k_g)

    chains = [(q, h) for q in range(nseq) for h in range(nh)]
    s_cur = [s_ref[q, :, h * HALF:(h + 1) * HALF] for q, h in chains]
    ys = [[None] * nh for _ in range(nc)]
    for j in range(cps):
        idx = [(q * cps + j) * nh + h for q, h in chains]
        yj = [nt(r_h[i], s) + y0[i] for i, s in zip(idx, s_cur)]
        s_cur = [nn(s, g_l[i]) + h_l[i] for i, s in zip(idx, s_cur)]
        for (q, h), yv in zip(chains, yj):
            ys[q * cps + j][h] = yv
    for q in range(nseq):
        s_ref[q] = jnp.concatenate(s_cur[q * nh:(q + 1) * nh], axis=1)
    y = jnp.concatenate([jnp.concatenate(yc, axis=1) for yc in ys], axis=0)
    inv_n = 1.0 / RWKV_HEAD_DIM
    mu = headsum(y, split=True) * inv_n
    d = y - mu
    var = headsum(d * d) * inv_n
    yn = d * lax.rsqrt(var + LNX_EPS) * lg_ref[...] + lb_ref[...]
    out_ref[...] = ((yn + bonus) * gate).astype(BF16).reshape(out_ref.shape)


def _rwkv_call(ps, vecs, wup, aup, gup, batch, seq, nseq=2, cps=2):
    t = ps.shape[0]
    rows = cps * CHUNK
    vec_spec = pl.BlockSpec((1, RW), lambda b, j: (0, 0))
    lora_spec = pl.BlockSpec((LANE, RW), lambda b, j: (0, 0))
    out = pl.pallas_call(
        functools.partial(_rwkv_kernel, nseq=nseq, cps=cps),
        grid=(batch // nseq, seq // rows),
        in_specs=[pl.BlockSpec((nseq, rows, PS_W), lambda b, j: (b, j, 0))]
                 + [vec_spec] * 7 + [lora_spec] * 3,
        out_specs=pl.BlockSpec((nseq, rows, RW), lambda b, j: (b, j, 0)),
        out_shape=jax.ShapeDtypeStruct((batch, seq, RW), BF16),
        scratch_shapes=[pltpu.VMEM((nseq, CHUNK, RW), F32)],
        compiler_params=_params(("arbitrary", "arbitrary")),
        name="rwkv",
    )(ps.reshape(batch, seq, PS_W), *vecs, wup, aup, gup)
    return out.reshape(t, RW)


def _mlaprep_kernel(mla_ref, pos_ref, invf_ref, qn_ref, kvn_ref, wq_ref, wqs_ref, wk_ref, wv_ref,
                    q_ref, k_ref, v_ref):
    def rms(x, g):
        return (x * lax.rsqrt(jnp.mean(x * x, axis=-1, keepdims=True) + NORM_EPS) * g).astype(BF16)

    ang = pos_ref[...].astype(F32) * invf_ref[...]
    cos = jnp.cos(ang)
    sin = jnp.sin(ang)
    lane = lax.broadcasted_iota(jnp.int32, cos.shape, 1)
    cos_q = jnp.tile(jnp.where(lane < QK_NOPE, 1.0, cos), (1, MLA_HEADS))
    sin_q = jnp.tile(sin, (1, MLA_HEADS))
    qn = rms(mla_ref[:, 0:Q_LORA], qn_ref[...])
    q = _dot(qn, wq_ref[...]) * cos_q + _dot(qn, wqs_ref[...]) * sin_q
    q_ref[...] = (q * ((QK_NOPE + QK_ROPE) ** -0.5 * LOG2E)).astype(BF16)
    kvn = rms(mla_ref[:, Q_LORA:Q_LORA + KV_LORA], kvn_ref[...])
    o = Q_LORA + KV_LORA
    k_pe = mla_ref[:, o:o + LANE] * cos + mla_ref[:, o + LANE:o + 2 * LANE] * sin
    k_ref[...] = (_dot(kvn, wk_ref[...]) + jnp.tile(k_pe, (1, MLA_HEADS))).astype(BF16)
    v_ref[...] = _dot(kvn, wv_ref[...]).astype(BF16)


def _mlaprep_call(mla, pos, invf, qn, kvn, wq, wqs, wk, wv, tm=512):
    t = mla.shape[0]
    hw = MLA_HEADS * HEAD_PAD
    vw = MLA_HEADS * V_HEAD
    full = lambda shape: pl.BlockSpec(shape, lambda i: (0, 0))
    row = lambda w: pl.BlockSpec((tm, w), lambda i: (i, 0))
    return pl.pallas_call(
        _mlaprep_kernel,
        grid=(t // tm,),
        in_specs=[row(MLA_W), row(1), full((1, LANE)), full((1, Q_LORA)), full((1, KV_LORA)),
                  full((Q_LORA, hw)), full((Q_LORA, hw)), full((KV_LORA, hw)), full((KV_LORA, vw))],
        out_specs=[row(hw), row(hw), row(vw)],
        out_shape=[jax.ShapeDtypeStruct((t, w), BF16) for w in (hw, hw, vw)],
        compiler_params=_params(("arbitrary",)),
        name="mlaprep",
    )(mla, pos, invf, qn, kvn, wq, wqs, wk, wv)


def _attn_kernel(q_ref, k_ref, v_ref, o_ref, *, seq, tq):
    r_i = lax.broadcasted_iota(jnp.int32, (tq, tq), 0) // CHUNK
    c_i = lax.broadcasted_iota(jnp.int32, (tq, tq), 1) // CHUNK
    mask = c_i <= r_i
    def scores(item):
        qi, hh = item
        lo = qi * tq
        qk = slice(hh * HEAD_PAD, (hh + 1) * HEAD_PAD)
        q = q_ref[lo:lo + tq, qk]
        sd = jnp.where(mask, _dot_nt(q, k_ref[lo:lo + tq, qk]), -1e30)
        so = _dot_nt(q, k_ref[0:lo, qk]) if qi > 0 else None
        return sd, so

    items = [(qi, hh) for qi in range(seq // tq) for hh in range(ATTN_HEADS_PER_STEP)]
    nxt = scores(items[0])
    for n, (qi, hh) in enumerate(items):
        lo = qi * tq
        vo = slice(hh * V_HEAD, (hh + 1) * V_HEAD)
        sd, so = nxt
        if n + 1 < len(items):
            nxt = scores(items[n + 1])
        m = jnp.max(sd, axis=-1, keepdims=True)
        if qi > 0:
            m = jnp.maximum(m, jnp.max(so, axis=-1, keepdims=True))
        pd = jnp.exp2(sd - m)
        l = jnp.sum(pd, axis=-1, keepdims=True)
        acc = _dot(pd.astype(BF16), v_ref[lo:lo + tq, vo])
        if qi > 0:
            po = jnp.exp2(so - m)
            l = l + jnp.sum(po, axis=-1, keepdims=True)
            acc = acc + _dot(po.astype(BF16), v_ref[0:lo, vo])
        o_ref[lo:lo + tq, vo] = (acc / l).astype(BF16)


ATTN_HEADS_PER_STEP = LANE // V_HEAD


def _attn_call(q, k, v, batch, seq, tq=256):
    hp = ATTN_HEADS_PER_STEP
    qk_spec = pl.BlockSpec((seq, hp * HEAD_PAD), lambda b, h: (b, h))
    vo_spec = pl.BlockSpec((seq, hp * V_HEAD), lambda b, h: (b, h))
    return pl.pallas_call(
        functools.partial(_attn_kernel, seq=seq, tq=tq),
        grid=(batch, MLA_HEADS // hp),
        in_specs=[qk_spec, qk_spec, vo_spec],
        out_specs=vo_spec,
        out_shape=jax.ShapeDtypeStruct(v.shape, BF16),
        compiler_params=_params(("arbitrary", "arbitrary")),
        name="attn",
    )(q, k, v)


MIX_SUB = 256


def _mixout_kernel(*refs, router):
    if router:
        (rw_ref, o_ref, gates_ref, x_ref, gate_ref, wr_ref, wm_ref, wo_ref,
         g2_ref, shift2_ref, scale2_ref, rwt_ref, rb_ref,
         xo_ref, h2_ref, ti_ref, tp_ref) = refs
    else:
        (rw_ref, o_ref, gates_ref, x_ref, gate_ref, wr_ref, wm_ref, wo_ref,
         g2_ref, shift2_ref, scale2_ref, xo_ref, h2_ref) = refs
    n = x_ref.shape[0]
    subs = [slice(s, s + MIX_SUB) for s in range(0, n, MIX_SUB)]
    yab = [(_dot(rw_ref[r, :], wr_ref[...]), _dot(o_ref[r, :], wm_ref[...])) for r in subs]
    merged = [gates_ref[r, 0:D_MODEL].astype(F32) * ya + gates_ref[r, D_MODEL:].astype(F32) * yb
              for r, (ya, yb) in zip(subs, yab)]
    mixed = [_dot(m.astype(BF16), wo_ref[...]) for m in merged]
    h2s = []
    for r, mx in zip(subs, mixed):
        xn = x_ref[r, :] + gate_ref[...] * mx
        xo_ref[r, :] = xn
        h2s.append(_norm_mod(xn, g2_ref[...], scale2_ref[...], shift2_ref[...]))
        h2_ref[r, :] = h2s[-1].astype(h2_ref.dtype)
    if router:
        h2 = jnp.concatenate(h2s, axis=0)
        hh, hl = _split2(h2)
        wh, wl = _split2(rwt_ref[...])
        logits = _dot_nt(wh, hh) + _dot_nt(wl, hh) + _dot_nt(wh, hl) + rb_ref[...]
        e_i = lax.broadcasted_iota(jnp.int32, logits.shape, 0)
        m1 = jnp.max(logits, axis=0, keepdims=True)
        i1 = jnp.min(jnp.where(logits == m1, e_i, N_EXPERTS), axis=0, keepdims=True)
        rest = jnp.where(e_i == i1, -jnp.inf, logits)
        m2 = jnp.max(rest, axis=0, keepdims=True)
        i2 = jnp.min(jnp.where(rest == m2, e_i, N_EXPERTS), axis=0, keepdims=True)
        e2 = jnp.exp(m2 - m1)
        den = 1.0 + e2
        ti_ref[0:1, :] = i1
        ti_ref[1:2, :] = i2
        tp_ref[0:1, :] = 1.0 / den
        tp_ref[1:2, :] = e2 / den


def _mixout_call(rw, o, gates, x2, gate, wr, wm, wo, g2, shift2, scale2, seq, router=None, tm=512):
    t = x2.shape[0]
    tiles_per_seq = seq // tm
    bmap = lambda i: (i // tiles_per_seq, 0, 0)
    full = lambda shape: pl.BlockSpec(shape, lambda i: (0, 0))
    row = lambda w: pl.BlockSpec((tm, w), lambda i: (i, 0))
    mod = pl.BlockSpec((None, 1, D_MODEL), bmap)
    in_specs = [row(RW), row(MLA_HEADS * V_HEAD), row(GATE_W), row(D_MODEL), mod,
                full(wr.shape), full(wm.shape), full(wo.shape), full((1, D_MODEL)), mod, mod]
    args = [rw, o, gates, x2, gate, wr, wm, wo, g2, shift2, scale2]
    out_specs = [row(D_MODEL), row(D_MODEL)]
    out_shape = [jax.ShapeDtypeStruct((t, D_MODEL), F32),
                 jax.ShapeDtypeStruct((t, D_MODEL), BF16 if router is None else F32)]
    if router is not None:
        in_specs += [full((N_EXPERTS, D_MODEL)), full((N_EXPERTS, 1))]
        args += list(router)
        out_specs += [pl.BlockSpec((2, tm), lambda i: (0, i))] * 2
        out_shape += [jax.ShapeDtypeStruct((2, t), jnp.int32), jax.ShapeDtypeStruct((2, t), F32)]
    return pl.pallas_call(
        functools.partial(_mixout_kernel, router=router is not None),
        grid=(t // tm,),
        in_specs=in_specs,
        out_specs=out_specs,
        out_shape=out_shape,
        compiler_params=_params(("arbitrary",)),
        name="mixout",
    )(*args)


def _swiglu_step(h, w1_ref, w3_ref, w2_ref, acc_ref, finish):
    f = pl.program_id(1)

    @pl.when(f == 0)
    def _():
        acc_ref[...] = jnp.zeros_like(acc_ref)

    a1 = _dot(h, w1_ref[...])
    a3 = _dot(h, w3_ref[...])
    act = (a1 * _sigmoid(a1)) * a3
    acc_ref[...] += _dot(act.astype(BF16), w2_ref[...].astype(BF16))

    @pl.when(f == pl.num_programs(1) - 1)
    def _():
        finish(acc_ref[...])


def _rms(x, g):
    return x * lax.rsqrt(jnp.mean(x * x, axis=-1, keepdims=True) + NORM_EPS) * g


def _ffn_kernel(h_ref, x_ref, gate_ref, gf_ref, w1_ref, w3_ref, w2_ref, o_ref, *, final, tf):
    h = h_ref[...]
    ff = w1_ref.shape[1]
    ups = [(_dot(h, w1_ref[:, s:s + tf]), _dot(h, w3_ref[:, s:s + tf])) for s in range(0, ff, tf)]
    acc = None
    for s, (a1, a3) in zip(range(0, ff, tf), ups):
        act = ((a1 * _sigmoid(a1)) * a3).astype(BF16)
        part = _dot(act, w2_ref[s:s + tf, :])
        acc = part if acc is None else acc + part
    xn = x_ref[...] + gate_ref[...] * acc
    o_ref[...] = _rms(xn, gf_ref[...]) if final else xn


def _ffn_call(h, x2, gate, gf, w1, w3, w2, seq, final, tm=512, tf=1408):
    r = h.shape[0]
    tiles_per_seq = seq // tm
    row = pl.BlockSpec((tm, D_MODEL), lambda i: (i, 0))
    resident = lambda a: pl.BlockSpec(a.shape, lambda i: (0, 0), pipeline_mode=pl.Buffered(1))
    return pl.pallas_call(
        functools.partial(_ffn_kernel, final=final, tf=tf),
        grid=(r // tm,),
        in_specs=[
            row, row,
            pl.BlockSpec((None, 1, D_MODEL), lambda i: (i // tiles_per_seq, 0, 0)),
            pl.BlockSpec((1, D_MODEL), lambda i: (0, 0)),
            resident(w1), resident(w3), resident(w2),
        ],
        out_specs=row,
        out_shape=jax.ShapeDtypeStruct((r, D_MODEL), F32),
        compiler_params=_params(("arbitrary",)),
        name="ffn",
    )(h, x2, gate, gf, w1, w3, w2)


MOE_TM = 512
ROW_TILE = (D_MODEL // LANE, LANE)


def _route(top_i, tm):
    t = top_i.shape[1]
    e_flat = top_i.reshape(-1)
    oh = (e_flat[:, None] == jnp.arange(N_EXPERTS, dtype=jnp.int32)[None, :]).astype(jnp.int32)
    csum = jnp.cumsum(oh, axis=0)
    rank = jnp.sum(csum * oh, axis=1) - 1
    counts = csum[-1]
    padded = ((counts + tm - 1) // tm) * tm
    ends = jnp.cumsum(padded)
    pos = jnp.sum(oh * (ends - padded)[None, :], axis=1) + rank
    n_tiles = (2 * t) // tm + N_EXPERTS
    starts = jnp.arange(n_tiles, dtype=jnp.int32) * tm
    tile_e = jnp.sum((starts[:, None] >= ends[None, :]).astype(jnp.int32), axis=1)
    return (pos.astype(jnp.int32), ends.astype(jnp.int32), jnp.minimum(tile_e, N_EXPERTS - 1),
            (ends[-1:] // tm).astype(jnp.int32))


def _dispatch_kernel(pos_ref, ends_ref, h_ref, xs_ref, zero_ref, sem, *, t, td, tm):
    i = pl.program_id(0)
    base = i * td

    @pl.when(i == 0)
    def _():
        zero_ref[...] = jnp.zeros_like(zero_ref)
        n_tiles = xs_ref.shape[0] // tm
        fills = []
        for e in range(N_EXPERTS):
            start = ends_ref[e - 1] if e else 0
            fills.append((ends_ref[e] > start, pl.multiple_of(ends_ref[e] - tm, tm)))
            fills.append((ends_ref[N_EXPERTS - 1] // tm + e < n_tiles,
                          pl.multiple_of(ends_ref[N_EXPERTS - 1] + e * tm, tm)))
        for go, row0 in fills:
            @pl.when(go)
            def _():
                pltpu.make_async_copy(zero_ref, xs_ref.at[pl.ds(row0, tm)], sem).start()
        for go, row0 in fills:
            @pl.when(go)
            def _():
                pltpu.make_async_copy(zero_ref, xs_ref.at[pl.ds(row0, tm)], sem).wait()

    for r in range(td):
        for k in range(2):
            dst = pos_ref[k * t + base + r]
            pltpu.make_async_copy(h_ref.at[pl.ds(r, 1)], xs_ref.at[pl.ds(dst, 1)], sem).start(priority=k)
    for k in range(2):
        pltpu.make_async_copy(h_ref, xs_ref.at[pl.ds(0, td)], sem).wait()


def _dispatch_call(pos, ends, h, n_rows, tm, td=512):
    t = h.shape[0]
    return pl.pallas_call(
        functools.partial(_dispatch_kernel, t=t, td=td, tm=tm),
        grid_spec=pltpu.PrefetchScalarGridSpec(
            num_scalar_prefetch=2, grid=(t // td,),
            in_specs=[pl.BlockSpec((td, D_MODEL), lambda i, pos, ends: (i, 0))],
            out_specs=pl.BlockSpec(memory_space=pl.ANY),
            scratch_shapes=[pltpu.VMEM((tm, D_MODEL), h.dtype), pltpu.SemaphoreType.DMA(())]),
        out_shape=jax.ShapeDtypeStruct((n_rows, D_MODEL), h.dtype),
        compiler_params=_params(("arbitrary",)),
        name="dispatch",
    )(pos, ends, h)


def _ffn_routed_kernel(te_ref, nu_ref, x_ref, w1_ref, w3_ref, w2_ref, y_ref, acc_ref):
    del te_ref
    used = pl.program_id(0) < nu_ref[0]

    def finish(acc):
        y_ref[...] = acc.reshape(y_ref.shape)

    @pl.when(used)
    def _():
        _swiglu_step(x_ref[...].astype(BF16), w1_ref, w3_ref, w2_ref, acc_ref, finish)

    @pl.when(jnp.logical_not(used))
    def _():
        y_ref[...] = jnp.zeros_like(y_ref)


def _ffn_routed_call(tile_e, n_used, xs, w1, w3, w2, tm=MOE_TM, tf=1792):
    r = xs.shape[0]
    ff = w1.shape[2]
    return pl.pallas_call(
        _ffn_routed_kernel,
        grid_spec=pltpu.PrefetchScalarGridSpec(
            num_scalar_prefetch=2, grid=(r // tm, ff // tf),
            in_specs=[
                pl.BlockSpec((tm, D_MODEL), lambda i, f, te, nu: (i, 0)),
                pl.BlockSpec((None, D_MODEL, tf), lambda i, f, te, nu: (te[i], 0, f)),
                pl.BlockSpec((None, D_MODEL, tf), lambda i, f, te, nu: (te[i], 0, f)),
                pl.BlockSpec((None, tf, D_MODEL), lambda i, f, te, nu: (te[i], f, 0)),
            ],
            out_specs=pl.BlockSpec((tm,) + ROW_TILE, lambda i, f, te, nu: (i, 0, 0)),
            scratch_shapes=[pltpu.VMEM((tm, D_MODEL), F32)]),
        out_shape=jax.ShapeDtypeStruct((r,) + ROW_TILE, F32),
        compiler_params=_params(("arbitrary", "arbitrary")),
        name="ffn_routed",
    )(tile_e, n_used, xs, w1, w3, w2)


def _combine_kernel(pos_ref, x_ref, p_ref, gate_ref, gf_ref, y_ref, o_ref, buf, sem, *, t, tc, final):
    i = pl.program_id(0)

    def row_copy(step, slot, r, k):
        src = pos_ref[k * t + step * tc + r]
        return pltpu.make_async_copy(y_ref.at[pl.ds(src, 1)], buf.at[slot, k, pl.ds(r, 1)], sem.at[slot])

    def issue(step, slot):
        for r in range(tc):
            for k in range(2):
                row_copy(step, slot, r, k).start(priority=k)

    @pl.when(i == 0)
    def _():
        issue(0, 0)

    @pl.when(i + 1 < pl.num_programs(0))
    def _():
        issue(i + 1, (i + 1) % 2)

    slot = i % 2
    for k in range(2):
        pltpu.make_async_copy(y_ref.at[pl.ds(0, tc)], buf.at[slot, k], sem.at[slot]).wait()
    p = p_ref[...]
    y0 = buf[slot, 0].reshape(tc, D_MODEL)
    y1 = buf[slot, 1].reshape(tc, D_MODEL)
    xn = x_ref[...] + gate_ref[...] * (p[:, 0:1] * y0 + p[:, 1:2] * y1)
    o_ref[...] = _rms(xn, gf_ref[...]) if final else xn


def _combine_call(pos, x2, probs, gate, gf, y, seq, final, tc=256):
    t = x2.shape[0]
    tiles_per_seq = seq // tc
    row = pl.BlockSpec((tc, D_MODEL), lambda i, pos: (i, 0))
    return pl.pallas_call(
        functools.partial(_combine_kernel, t=t, tc=tc, final=final),
        grid_spec=pltpu.PrefetchScalarGridSpec(
            num_scalar_prefetch=1, grid=(t // tc,),
            in_specs=[row, pl.BlockSpec((tc, 2), lambda i, pos: (i, 0)),
                      pl.BlockSpec((None, 1, D_MODEL), lambda i, pos: (i // tiles_per_seq, 0, 0)),
                      pl.BlockSpec((1, D_MODEL), lambda i, pos: (0, 0)),
                      pl.BlockSpec(memory_space=pl.ANY)],
            out_specs=row,
            scratch_shapes=[pltpu.VMEM((2, 2, tc) + ROW_TILE, F32), pltpu.SemaphoreType.DMA((2,))]),
        out_shape=jax.ShapeDtypeStruct((t, D_MODEL), F32),
        compiler_params=_params(("arbitrary",)),
        name="combine",
    )(pos, x2, probs, gate, gf, y)


def _layout_w_in_kernel(w_ref, o_ref):
    half = QK_ROPE // 2
    o_ref[...] = jnp.zeros_like(o_ref)

    def put(dst, src, width, sign=1.0):
        o_ref[dst:dst + width, :] = (sign * w_ref[src:src + width, :]).astype(BF16)

    o = 3 * RW
    put(0, 0, o + DECAY_LORA)
    put(o + LANE, o + DECAY_LORA, ICLR_LORA)
    put(o + 2 * LANE, o + DECAY_LORA + ICLR_LORA, GATE_LORA)
    src = o + DECAY_LORA + ICLR_LORA + GATE_LORA
    put(PS_W, src, Q_LORA + KV_LORA)
    kr = src + Q_LORA + KV_LORA
    dst = PS_W + Q_LORA + KV_LORA
    put(dst + QK_NOPE, kr, QK_ROPE)
    put(dst + LANE + QK_NOPE, kr + half, half, -1.0)
    put(dst + LANE + QK_NOPE + half, kr, half)
    put(PS_W + MLA_W, kr + QK_ROPE, GATE_W)


def _layout_w_in(w_t, layer):
    _, n, k = w_t.shape
    return pl.pallas_call(
        _layout_w_in_kernel,
        grid=(k // 256,),
        in_specs=[pl.BlockSpec((None, n, 256), lambda i: (layer, 0, i))],
        out_specs=pl.BlockSpec((PS_W + MLA_W + GATE_W, 256), lambda i: (0, i)),
        out_shape=jax.ShapeDtypeStruct((PS_W + MLA_W + GATE_W, k), BF16),
        compiler_params=_params(("arbitrary",)),
        name="layout_w_in",
    )(w_t)


def _layout_mu(mu):
    z = jnp.zeros((64,), mu.dtype)
    o = 3 * RW
    return jnp.concatenate([mu[:o + 64], z, mu[o + 64:o + 128], z, mu[o + 128:]])[None, :]


def _layout_wq(w_qb):
    hd = QK_NOPE + QK_ROPE
    w = w_qb.reshape(Q_LORA, MLA_HEADS, hd)
    half = QK_ROPE // 2
    x1, x2 = w[:, :, QK_NOPE:QK_NOPE + half], w[:, :, QK_NOPE + half:]
    zn = jnp.zeros((Q_LORA, MLA_HEADS, QK_NOPE), w.dtype)
    zp = jnp.zeros((Q_LORA, MLA_HEADS, HEAD_PAD - hd), w.dtype)
    wq = jnp.concatenate([w, zp], axis=2).reshape(Q_LORA, -1)
    wqs = jnp.concatenate([zn, -x2, x1, zp], axis=2).reshape(Q_LORA, -1)
    return wq.astype(BF16), wqs.astype(BF16)


def _layout_wkv(w_kvb):
    w = w_kvb.reshape(KV_LORA, MLA_HEADS, QK_NOPE + V_HEAD)
    z = jnp.zeros((KV_LORA, MLA_HEADS, HEAD_PAD - QK_NOPE), w.dtype)
    wk = jnp.concatenate([w[:, :, :QK_NOPE], z], axis=2).reshape(KV_LORA, -1)
    wv = w[:, :, QK_NOPE:].reshape(KV_LORA, -1)
    return wk.astype(BF16), wv.astype(BF16)


def _rope_freq_row():
    inv_freq = ROPE_THETA ** (-jnp.arange(0, QK_ROPE, 2, dtype=F32) / QK_ROPE)
    z = jnp.zeros((QK_NOPE,), F32)
    zp = jnp.zeros((LANE - QK_NOPE - QK_ROPE,), F32)
    return jnp.concatenate([z, inv_freq, inv_freq, zp])[None, :]


def kernel(x, c, positions, ada_w, ada_b, norm_mix, norm_ffn, norm_final, w_in, tshift_mu, w0, w_up, a0, a_up, g_up, k_k, k_a, r_k, lnx_g, lnx_b, rwkv_out, q_norm, w_qb, kv_norm, w_kvb, mla_out, w_o, ffn_w1, ffn_w3, ffn_w2, router_w, router_b, moe_w1, moe_w3, moe_w2):
    batch, seq, _ = x.shape
    t = batch * seq
    mods = _ada_call(c, ada_w.reshape(2 * DEPTH, D_MODEL, 3 * D_MODEL),
                     ada_b.reshape(2 * DEPTH, 1, 3 * D_MODEL))
    mods = mods.reshape(DEPTH, 2, batch, 3, 1, D_MODEL)
    pos_col = positions.reshape(t, 1)
    invf = _rope_freq_row()
    x2 = x.reshape(t, D_MODEL)
    row = lambda v: v.reshape(1, -1)

    side_jobs = [moe_w1.reshape(-1, moe_w1.shape[-1]), moe_w3.reshape(-1, moe_w3.shape[-1])]
    moe_up = []
    w_in_t = jnp.swapaxes(w_in, 1, 2)

    for l in range(DEPTH):
        shift, scale, gate = (mods[l, 0, :, j] for j in range(3))
        ps, mla, gates, cast = _inproj_call(x2, shift, scale, row(norm_mix[l]), _layout_w_in(w_in_t, l),
                                            _layout_mu(tshift_mu[l]), side_jobs[l], seq)
        moe_up.append(cast.reshape(moe_w1.shape))
        vecs = [row(v[l]) for v in (w0, a0, k_k, k_a, r_k, lnx_g, lnx_b)]
        lora_pad = lambda w: jnp.pad(w, ((0, LANE - w.shape[0]), (0, 0)))
        rw = _rwkv_call(ps, vecs, lora_pad(w_up[l]), lora_pad(a_up[l]), g_up[l], batch, seq)
        wq, wqs = _layout_wq(w_qb[l])
        wk, wv = _layout_wkv(w_kvb[l])
        q, k, v = _mlaprep_call(mla, pos_col, invf, row(q_norm[l]), row(kv_norm[l]), wq, wqs, wk, wv)
        o = _attn_call(q, k, v, batch, seq)

        shift2, scale2, gate2 = (mods[l, 1, :, j] for j in range(3))
        moe = l % 2 == 1
        router = (router_w[l // 2].T, router_b[l // 2].reshape(N_EXPERTS, 1)) if moe else None
        outs = _mixout_call(rw, o, gates, x2, gate, rwkv_out[l].astype(BF16), mla_out[l].astype(BF16),
                            w_o[l].astype(BF16), row(norm_ffn[l]), shift2, scale2, seq, router=router)
        final = l == DEPTH - 1
        if moe:
            x2, h2, top_i, top_p = outs
            pos, ends, tile_e, n_used = _route(top_i, MOE_TM)
            xs = _dispatch_call(pos, ends, h2, 2 * t + N_EXPERTS * MOE_TM, MOE_TM)
            y = _ffn_routed_call(tile_e, n_used, xs, moe_up[0][l // 2], moe_up[1][l // 2], moe_w2[l // 2])
            x2 = _combine_call(pos, x2, top_p.T, gate2, row(norm_final), y, seq, final)
        else:
            x2, h2 = outs
            x2 = _ffn_call(h2, x2, gate2, row(norm_final), ffn_w1[l // 2].astype(BF16),
                           ffn_w3[l // 2].astype(BF16), ffn_w2[l // 2].astype(BF16), seq, final)
    return x2.reshape(batch, seq, D_MODEL)
```

```python
import functools
import math

import jax
import jax.numpy as jnp
from jax import lax
from jax.experimental import pallas as pl
from jax.experimental.pallas import tpu as pltpu

F32 = jnp.float32
BF16 = jnp.bfloat16

D_MODEL = 1024
DEPTH = 2
CHUNK = 64
NORM_EPS = 1e-6
RWKV_HEADS = 8
RWKV_HEAD_DIM = 64
RW = RWKV_HEADS * RWKV_HEAD_DIM
DECAY_LORA = 64
ICLR_LORA = 64
GATE_LORA = 128
DECAY_SCALE = math.exp(-0.5)
LNX_EPS = 64e-5
MLA_HEADS = 8
Q_LORA = 384
KV_LORA = 256
QK_NOPE = 64
QK_ROPE = 32
V_HEAD = 64
ROPE_THETA = 10000.0
LOG2E = math.log2(math.e)
N_EXPERTS = 8

LANE = 128
HEAD_PAD = 128
PS_W = 3 * RW + 3 * LANE
MLA_W = Q_LORA + KV_LORA + 2 * LANE
GATE_W = 2 * D_MODEL
HALF = 256
VMEM_LIMIT = 56 * 1024 * 1024


def _sigmoid(x):
    return 1.0 / (1.0 + jnp.exp(-x))


def _dot(a, b):
    return jnp.dot(a, b, preferred_element_type=F32)


def _dot_nt(a, b):
    return lax.dot_general(a, b, (((1,), (1,)), ((), ())), preferred_element_type=F32)


def _split2(x):
    hi = x.astype(BF16)
    lo = (x - hi.astype(F32)).astype(BF16)
    return hi, lo


def _dot3(a, b):
    ah, al = _split2(a)
    bh, bl = _split2(b)
    return _dot(ah, bh) + _dot(al, bh) + _dot(ah, bl)


def _params(sem):
    return pltpu.CompilerParams(dimension_semantics=sem, vmem_limit_bytes=VMEM_LIMIT)


def _ada_kernel(c_ref, w_ref, b_ref, o_ref):
    c = c_ref[...]
    o_ref[...] = _dot((c * _sigmoid(c)).astype(BF16), w_ref[...].astype(BF16)) + b_ref[...]


def _ada_call(c, ada_w, ada_b):
    n = ada_w.shape[0]
    nb = c.shape[0]
    return pl.pallas_call(
        _ada_kernel,
        grid=(n, 3),
        in_specs=[
            pl.BlockSpec((nb, D_MODEL), lambda i, j: (0, 0)),
            pl.BlockSpec((None, D_MODEL, D_MODEL), lambda i, j: (i, 0, j)),
            pl.BlockSpec((None, 1, D_MODEL), lambda i, j: (i, 0, j)),
        ],
        out_specs=pl.BlockSpec((None, nb, D_MODEL), lambda i, j: (i, 0, j)),
        out_shape=jax.ShapeDtypeStruct((n, nb, 3 * D_MODEL), F32),
        compiler_params=_params(("arbitrary", "arbitrary")),
        name="ada",
    )(c, ada_w, ada_b)


def _norm_mod(x, g, scale, shift):
    var = jnp.mean(x * x, axis=-1, keepdims=True)
    return (x * lax.rsqrt(var + NORM_EPS) * g) * (1.0 + scale) + shift


def _inproj_kernel(x_ref, shift_ref, scale_ref, g_ref, w_ref, mu_ref, side_ref,
                   ps_ref, mla_ref, gates_ref, side_out_ref, carry_ref, *, tiles_per_seq, tm):
    i = pl.program_id(0)
    side_out_ref[...] = side_ref[...].astype(BF16)

    @pl.when(i % tiles_per_seq == 0)
    def _():
        carry_ref[...] = jnp.zeros_like(carry_ref)

    carry = carry_ref[0:1, :]
    row = lax.broadcasted_iota(jnp.int32, (INPROJ_SUB, PS_W), 0)
    for s in range(0, tm, INPROJ_SUB):
        r = slice(s, s + INPROJ_SUB)
        h = _norm_mod(x_ref[r, :], g_ref[...], scale_ref[...], shift_ref[...]).astype(BF16)
        p = _dot_nt(h, w_ref[0:PS_W, :])
        prev = jnp.where(row == 0, carry, pltpu.roll(p, 1, axis=0))
        carry = p[INPROJ_SUB - 1:INPROJ_SUB, :]
        ps_ref[r, :] = p + (prev - p) * mu_ref[...]
        mla_ref[r, :] = _dot_nt(h, w_ref[PS_W:PS_W + MLA_W, :])
        gates_ref[r, :] = _sigmoid(_dot_nt(h, w_ref[PS_W + MLA_W:, :])).astype(BF16)
    carry_ref[0:1, :] = carry


INPROJ_SUB = 256


def _inproj_call(x2, shift, scale, g, w_pad, mu_pad, side, seq, tm=512):
    t = x2.shape[0]
    steps = t // tm
    tiles_per_seq = seq // tm
    bmap = lambda i: (i // tiles_per_seq, 0, 0)
    wtot = PS_W + MLA_W + GATE_W
    side_spec = pl.BlockSpec((side.shape[0] // steps, side.shape[1]), lambda i: (i, 0))
    return pl.pallas_call(
        functools.partial(_inproj_kernel, tiles_per_seq=tiles_per_seq, tm=tm),
        grid=(steps,),
        in_specs=[
            pl.BlockSpec((tm, D_MODEL), lambda i: (i, 0)),
            pl.BlockSpec((None, 1, D_MODEL), bmap),
            pl.BlockSpec((None, 1, D_MODEL), bmap),
            pl.BlockSpec((1, D_MODEL), lambda i: (0, 0)),
            pl.BlockSpec((wtot, D_MODEL), lambda i: (0, 0), pipeline_mode=pl.Buffered(1)),
            pl.BlockSpec((1, PS_W), lambda i: (0, 0)),
            side_spec,
        ],
        out_specs=[
            pl.BlockSpec((tm, PS_W), lambda i: (i, 0)),
            pl.BlockSpec((tm, MLA_W), lambda i: (i, 0)),
            pl.BlockSpec((tm, GATE_W), lambda i: (i, 0)),
            side_spec,
        ],
        out_shape=[
            jax.ShapeDtypeStruct((t, PS_W), F32),
            jax.ShapeDtypeStruct((t, MLA_W), F32),
            jax.ShapeDtypeStruct((t, GATE_W), BF16),
            jax.ShapeDtypeStruct(side.shape, BF16),
        ],
        scratch_shapes=[pltpu.VMEM((8, PS_W), F32)],
        compiler_params=_params(("arbitrary",)),
        name="inproj",
    )(x2, shift, scale, g, w_pad, mu_pad, side)


def _blockdiag(xb, bdmask):
    return jnp.tile(xb, (HALF // CHUNK, 1)) * bdmask


def _bd_nn(a, b, bdmask):
    return _dot(a.astype(BF16), _blockdiag(b.astype(BF16), bdmask))


def _bd_nt(a, b, bdmask):
    return _dot_nt(a.astype(BF16), _blockdiag(b.astype(BF16), bdmask))


def _bd_tn(a, b, lane_head):
    full = _dot(a.T.astype(BF16), b.astype(BF16))
    out = full[(HALF // CHUNK - 1) * CHUNK:, :]
    for h in range(HALF // CHUNK - 2, -1, -1):
        out = jnp.where(lane_head == h, full[h * CHUNK:(h + 1) * CHUNK, :], out)
    return out


def _halves(fn, *xs):
    outs = [fn(*[x[:, h * HALF:(h + 1) * HALF] for x in xs]) for h in range(RW // HALF)]
    return jnp.concatenate(outs, axis=1)


def _rwkv_kernel(ps_ref, w0_ref, a0_ref, kk_ref, ka_ref, rk_ref, lg_ref, lb_ref,
                 wup_ref, aup_ref, gup_ref, out_ref, s_ref, *, nseq, cps):
    @pl.when(pl.program_id(1) == 0)
    def _():
        s_ref[...] = jnp.zeros_like(s_ref)

    C = CHUNK
    nc = nseq * cps
    r_i = lax.broadcasted_iota(jnp.int32, (HALF, HALF), 0)
    c_i = lax.broadcasted_iota(jnp.int32, (HALF, HALF), 1)
    bdmask = ((r_i // C) == (c_i // C)).astype(BF16)
    t_i = lax.broadcasted_iota(jnp.int32, (C, HALF), 0)
    lane_head = lax.broadcasted_iota(jnp.int32, (C, HALF), 1) // C
    i_i = lax.broadcasted_iota(jnp.int32, (C, HALF), 1) % C
    strict = i_i < t_i
    incl = i_i <= t_i
    eye_l = (i_i == t_i).astype(F32)
    n = nc * C
    tr = lax.broadcasted_iota(jnp.int32, (n, n), 0)
    tc = lax.broadcasted_iota(jnp.int32, (n, n), 1)
    tril = ((tc <= tr) & (tc // C == tr // C)).astype(BF16)

    def headsum(x, split=False):
        def one(xh):
            if not split:
                return _dot(xh.astype(BF16), bdmask)
            hi, lo = _split2(xh)
            return _dot(hi, bdmask) + _dot(lo, bdmask)
        return _halves(one, x)

    def cumsum_rows(x):
        hi = x.astype(BF16)
        r1 = x - hi.astype(F32)
        mid = r1.astype(BF16)
        lo = (r1 - mid.astype(F32)).astype(BF16)
        return _dot(tril, hi) + _dot(tril, mid) + _dot(tril, lo)

    cols = lambda lo, hi: ps_ref[:, :, lo:hi].reshape(n, hi - lo)
    p_r = cols(0, RW)
    p_k = cols(RW, 2 * RW)
    p_v = cols(2 * RW, 3 * RW)
    p_w = cols(3 * RW, 3 * RW + LANE)
    p_a = cols(3 * RW + LANE, 3 * RW + 2 * LANE)
    p_g = cols(3 * RW + 2 * LANE, 3 * RW + 3 * LANE)
    logw = -DECAY_SCALE * _sigmoid(w0_ref[...] + _dot3(jnp.tanh(p_w), wup_ref[...]))
    iclr = _sigmoid(a0_ref[...] + _dot(p_a.astype(BF16), aup_ref[...].astype(BF16)))
    gate = _dot(_sigmoid(p_g).astype(BF16), gup_ref[...].astype(BF16))
    kk = p_k * kk_ref[...]
    k2 = p_k * (1.0 + (iclr - 1.0) * ka_ref[...])
    sums = headsum(jnp.concatenate([kk * kk, p_r * k2 * rk_ref[...]], axis=0))
    kk = kk * lax.rsqrt(sums[:n] + 1e-12)
    bonus = sums[n:] * p_v
    b_v = kk * iclr
    cum = cumsum_rows(logw)
    cum_c = jnp.concatenate(
        [jnp.broadcast_to(cum[(c + 1) * C - 1:(c + 1) * C, :], (C, RW)) for c in range(nc)], axis=0)
    e_neg = jnp.exp(-cum)
    e_rem = jnp.exp(cum_c - cum)
    a_all = -kk * jnp.exp(cum - logw)
    b_all = b_v * e_neg
    k_all = k2 * e_neg
    r_all = p_r * jnp.exp(cum)
    bg_all = b_v * e_rem
    kg_all = k2 * e_rem
    gam_all = jnp.exp(cum_c)

    nh = RW // HALF
    units = [(c, h) for c in range(nc) for h in range(nh)]

    def cut(x, u):
        c, h = u
        return x[c * C:(c + 1) * C, h * HALF:(h + 1) * HALF]

    each = lambda fn, *lists: [fn(*xs) for xs in zip(*lists)]
    stack = lambda x, y: jnp.concatenate([x, y], axis=0)
    top = lambda x: x[:C]
    bot = lambda x: x[C:]
    nn = lambda a, b: _bd_nn(a, b, bdmask)
    nt = lambda a, b: _bd_nt(a, b, bdmask)

    a_t, b_t, k_t, r_t, b_g, k_g, v_u = ([cut(x, u) for u in units]
                                         for x in (a_all, b_all, k_all, r_all, bg_all, kg_all, p_v))
    gam = [gam_all[c * C:c * C + 1, h * HALF:(h + 1) * HALF] for c, h in units]
    ar = each(stack, a_t, r_t)
    sb = each(nt, ar, b_t)
    sk = each(nt, ar, k_t)
    l_ab = [jnp.where(strict, top(x), 0.0) for x in sb]
    q_b = [jnp.where(incl, bot(x), 0.0) for x in sb]
    m_ak = [jnp.where(strict, top(x), 0.0) for x in sk]
    q_k = [jnp.where(incl, bot(x), 0.0) for x in sk]
    t_m = [eye_l + x for x in l_ab]
    pw = each(nn, l_ab, l_ab)
    for _ in range(4):
        tp = each(nn, each(stack, t_m, pw), pw)
        t_m = each(lambda t, x: t + top(x), t_m, tp)
        pw = [bot(x) for x in tp]
    t_m = each(lambda t, x: t + x, t_m, each(nn, t_m, pw))
    mq = each(nn, each(stack, m_ak, q_k), v_u)
    a_h = each(nn, t_m, a_t)
    u0 = each(nn, t_m, [top(x) for x in mq])
    r_h = each(lambda r, x: r + x, r_t, each(nn, q_b, a_h))
    y0 = each(lambda m, x: bot(m) + x, mq, each(nn, q_b, u0))
    g_l = each(lambda a, b, g: _bd_tn(a, b, lane_head) + eye_l * g, a_h, b_g, gam)
    h_l = each(lambda u, v, b, k: _bd_tn(stack(u, v), stack(b, k), lane_head), u0, v_u, b_g, k_g)

    chains = [(q, h) for q in range(nseq) for h in range(nh)]
    s_cur = [s_ref[q, :, h * HALF:(h + 1) * HALF] for q, h in chains]
    ys = [[None] * nh for _ in range(nc)]
    for j in range(cps):
        idx = [(q * cps + j) * nh + h for q, h in chains]
        yj = [nt(r_h[i], s) + y0[i] for i, s in zip(idx, s_cur)]
        s_cur = [nn(s, g_l[i]) + h_l[i] for i, s in zip(idx, s_cur)]
        for (q, h), yv in zip(chains, yj):
            ys[q * cps + j][h] = yv
    for q in range(nseq):
        s_ref[q] = jnp.concatenate(s_cur[q * nh:(q + 1) * nh], axis=1)
    y = jnp.concatenate([jnp.concatenate(yc, axis=1) for yc in ys], axis=0)
    inv_n = 1.0 / RWKV_HEAD_DIM
    mu = headsum(y, split=True) * inv_n
    d = y - mu
    var = headsum(d * d) * inv_n
    yn = d * lax.rsqrt(var + LNX_EPS) * lg_ref[...] + lb_ref[...]
    out_ref[...] = ((yn + bonus) * gate).astype(BF16).reshape(out_ref.shape)


def _rwkv_call(ps, vecs, wup, aup, gup, batch, seq, nseq=2, cps=2):
    t = ps.shape[0]
    rows = cps * CHUNK
    vec_spec = pl.BlockSpec((1, RW), lambda b, j: (0, 0))
    lora_spec = pl.BlockSpec((LANE, RW), lambda b, j: (0, 0))
    out = pl.pallas_call(
        functools.partial(_rwkv_kernel, nseq=nseq, cps=cps),
        grid=(batch // nseq, seq // rows),
        in_specs=[pl.BlockSpec((nseq, rows, PS_W), lambda b, j: (b, j, 0))]
                 + [vec_spec] * 7 + [lora_spec] * 3,
        out_specs=pl.BlockSpec((nseq, rows, RW), lambda b, j: (b, j, 0)),
        out_shape=jax.ShapeDtypeStruct((batch, seq, RW), BF16),
        scratch_shapes=[pltpu.VMEM((nseq, CHUNK, RW), F32)],
        compiler_params=_params(("arbitrary", "arbitrary")),
        name="rwkv",
    )(ps.reshape(batch, seq, PS_W), *vecs, wup, aup, gup)
    return out.reshape(t, RW)


def _mlaprep_kernel(mla_ref, pos_ref, invf_ref, qn_ref, kvn_ref, wq_ref, wqs_ref, wk_ref, wv_ref,
                    q_ref, k_ref, v_ref):
    def rms(x, g):
        return (x * lax.rsqrt(jnp.mean(x * x, axis=-1, keepdims=True) + NORM_EPS) * g).astype(BF16)

    ang = pos_ref[...].astype(F32) * invf_ref[...]
    cos = jnp.cos(ang)
    sin = jnp.sin(ang)
    lane = lax.broadcasted_iota(jnp.int32, cos.shape, 1)
    cos_q = jnp.tile(jnp.where(lane < QK_NOPE, 1.0, cos), (1, MLA_HEADS))
    sin_q = jnp.tile(sin, (1, MLA_HEADS))
    qn = rms(mla_ref[:, 0:Q_LORA], qn_ref[...])
    q = _dot(qn, wq_ref[...]) * cos_q + _dot(qn, wqs_ref[...]) * sin_q
    q_ref[...] = (q * ((QK_NOPE + QK_ROPE) ** -0.5 * LOG2E)).astype(BF16)
    kvn = rms(mla_ref[:, Q_LORA:Q_LORA + KV_LORA], kvn_ref[...])
    o = Q_LORA + KV_LORA
    k_pe = mla_ref[:, o:o + LANE] * cos + mla_ref[:, o + LANE:o + 2 * LANE] * sin
    k_ref[...] = (_dot(kvn, wk_ref[...]) + jnp.tile(k_pe, (1, MLA_HEADS))).astype(BF16)
    v_ref[...] = _dot(kvn, wv_ref[...]).astype(BF16)


def _mlaprep_call(mla, pos, invf, qn, kvn, wq, wqs, wk, wv, tm=512):
    t = mla.shape[0]
    hw = MLA_HEADS * HEAD_PAD
    vw = MLA_HEADS * V_HEAD
    full = lambda shape: pl.BlockSpec(shape, lambda i: (0, 0))
    row = lambda w: pl.BlockSpec((tm, w), lambda i: (i, 0))
    return pl.pallas_call(
        _mlaprep_kernel,
        grid=(t // tm,),
        in_specs=[row(MLA_W), row(1), full((1, LANE)), full((1, Q_LORA)), full((1, KV_LORA)),
                  full((Q_LORA, hw)), full((Q_LORA, hw)), full((KV_LORA, hw)), full((KV_LORA, vw))],
        out_specs=[row(hw), row(hw), row(vw)],
        out_shape=[jax.ShapeDtypeStruct((t, w), BF16) for w in (hw, hw, vw)],
        compiler_params=_params(("arbitrary",)),
        name="mlaprep",
    )(mla, pos, invf, qn, kvn, wq, wqs, wk, wv)


def _attn_kernel(q_ref, k_ref, v_ref, o_ref, *, seq, tq):
    r_i = lax.broadcasted_iota(jnp.int32, (tq, tq), 0) // CHUNK
    c_i = lax.broadcasted_iota(jnp.int32, (tq, tq), 1) // CHUNK
    mask = c_i <= r_i
    def scores(item):
        qi, hh = item
        lo = qi * tq
        qk = slice(hh * HEAD_PAD, (hh + 1) * HEAD_PAD)
        q = q_ref[lo:lo + tq, qk]
        sd = jnp.where(mask, _dot_nt(q, k_ref[lo:lo + tq, qk]), -1e30)
        so = _dot_nt(q, k_ref[0:lo, qk]) if qi > 0 else None
        return sd, so

    items = [(qi, hh) for qi in range(seq // tq) for hh in range(ATTN_HEADS_PER_STEP)]
    ones = jnp.ones((seq, V_HEAD), BF16)
    v_aug = [jnp.concatenate([v_ref[:, hh * V_HEAD:(hh + 1) * V_HEAD], ones], axis=1)
             for hh in range(ATTN_HEADS_PER_STEP)]
    nxt = scores(items[0])
    for n, (qi, hh) in enumerate(items):
        lo = qi * tq
        vo = slice(hh * V_HEAD, (hh + 1) * V_HEAD)
        sd, so = nxt
        if n + 1 < len(items):
            nxt = scores(items[n + 1])
        m = jnp.max(sd, axis=-1, keepdims=True)
        if qi > 0:
            m = jnp.maximum(m, jnp.max(so, axis=-1, keepdims=True))
        acc = _dot(jnp.exp2(sd - m).astype(BF16), v_aug[hh][lo:lo + tq])
        if qi > 0:
            acc = acc + _dot(jnp.exp2(so - m).astype(BF16), v_aug[hh][0:lo])
        o_ref[lo:lo + tq, vo] = (acc[:, :V_HEAD] / acc[:, V_HEAD:V_HEAD + 1]).astype(BF16)


ATTN_HEADS_PER_STEP = LANE // V_HEAD


def _attn_call(q, k, v, batch, seq, tq=256):
    hp = ATTN_HEADS_PER_STEP
    qk_spec = pl.BlockSpec((seq, hp * HEAD_PAD), lambda b, h: (b, h))
    vo_spec = pl.BlockSpec((seq, hp * V_HEAD), lambda b, h: (b, h))
    return pl.pallas_call(
        functools.partial(_attn_kernel, seq=seq, tq=tq),
        grid=(batch, MLA_HEADS // hp),
        in_specs=[qk_spec, qk_spec, vo_spec],
        out_specs=vo_spec,
        out_shape=jax.ShapeDtypeStruct(v.shape, BF16),
        compiler_params=_params(("arbitrary", "arbitrary")),
        name="attn",
    )(q, k, v)


MIX_SUB = 256


def _mixout_kernel(*refs, router):
    if router:
        (rw_ref, o_ref, gates_ref, x_ref, gate_ref, wr_ref, wm_ref, wo_ref,
         g2_ref, shift2_ref, scale2_ref, rwt_ref, rb_ref,
         xo_ref, h2_ref, ti_ref, tp_ref) = refs
    else:
        (rw_ref, o_ref, gates_ref, x_ref, gate_ref, wr_ref, wm_ref, wo_ref,
         g2_ref, shift2_ref, scale2_ref, xo_ref, h2_ref) = refs
    n = x_ref.shape[0]
    subs = [slice(s, s + MIX_SUB) for s in range(0, n, MIX_SUB)]
    yab = [(_dot(rw_ref[r, :], wr_ref[...]), _dot(o_ref[r, :], wm_ref[...])) for r in subs]
    merged = [gates_ref[r, 0:D_MODEL].astype(F32) * ya + gates_ref[r, D_MODEL:].astype(F32) * yb
              for r, (ya, yb) in zip(subs, yab)]
    mixed = [_dot(m.astype(BF16), wo_ref[...]) for m in merged]
    h2s = []
    for r, mx in zip(subs, mixed):
        xn = x_ref[r, :] + gate_ref[...] * mx
        xo_ref[r, :] = xn
        h2s.append(_norm_mod(xn, g2_ref[...], scale2_ref[...], shift2_ref[...]))
        h2_ref[r, :] = h2s[-1].astype(h2_ref.dtype)
    if router:
        h2 = jnp.concatenate(h2s, axis=0)
        hh, hl = _split2(h2)
        wh, wl = _split2(rwt_ref[...])
        logits = _dot_nt(wh, hh) + _dot_nt(wl, hh) + _dot_nt(wh, hl) + rb_ref[...]
        e_i = lax.broadcasted_iota(jnp.int32, logits.shape, 0)
        m1 = jnp.max(logits, axis=0, keepdims=True)
        i1 = jnp.min(jnp.where(logits == m1, e_i, N_EXPERTS), axis=0, keepdims=True)
        rest = jnp.where(e_i == i1, -jnp.inf, logits)
        m2 = jnp.max(rest, axis=0, keepdims=True)
        i2 = jnp.min(jnp.where(rest == m2, e_i, N_EXPERTS), axis=0, keepdims=True)
        e2 = jnp.exp(m2 - m1)
        den = 1.0 + e2
        ti_ref[0:1, :] = i1
        ti_ref[1:2, :] = i2
        tp_ref[0:1, :] = 1.0 / den
        tp_ref[1:2, :] = e2 / den


def _mixout_call(rw, o, gates, x2, gate, wr, wm, wo, g2, shift2, scale2, seq, router=None, tm=512):
    t = x2.shape[0]
    tiles_per_seq = seq // tm
    bmap = lambda i: (i // tiles_per_seq, 0, 0)
    full = lambda shape: pl.BlockSpec(shape, lambda i: (0, 0))
    row = lambda w: pl.BlockSpec((tm, w), lambda i: (i, 0))
    mod = pl.BlockSpec((None, 1, D_MODEL), bmap)
    in_specs = [row(RW), row(MLA_HEADS * V_HEAD), row(GATE_W), row(D_MODEL), mod,
                full(wr.shape), full(wm.shape), full(wo.shape), full((1, D_MODEL)), mod, mod]
    args = [rw, o, gates, x2, gate, wr, wm, wo, g2, shift2, scale2]
    out_specs = [row(D_MODEL), row(D_MODEL)]
    out_shape = [jax.ShapeDtypeStruct((t, D_MODEL), F32),
                 jax.ShapeDtypeStruct((t, D_MODEL), BF16 if router is None else F32)]
    if router is not None:
        in_specs += [full((N_EXPERTS, D_MODEL)), full((N_EXPERTS, 1))]
        args += list(router)
        out_specs += [pl.BlockSpec((2, tm), lambda i: (0, i))] * 2
        out_shape += [jax.ShapeDtypeStruct((2, t), jnp.int32), jax.ShapeDtypeStruct((2, t), F32)]
    return pl.pallas_call(
        functools.partial(_mixout_kernel, router=router is not None),
        grid=(t // tm,),
        in_specs=in_specs,
        out_specs=out_specs,
        out_shape=out_shape,
        compiler_params=_params(("arbitrary",)),
        name="mixout",
    )(*args)


def _swiglu_step(h, w1_ref, w3_ref, w2_ref, acc_ref, finish):
    f = pl.program_id(1)

    @pl.when(f == 0)
    def _():
        acc_ref[...] = jnp.zeros_like(acc_ref)

    a1 = _dot(h, w1_ref[...])
    a3 = _dot(h, w3_ref[...])
    act = (a1 * _sigmoid(a1)) * a3
    acc_ref[...] += _dot(act.astype(BF16), w2_ref[...].astype(BF16))

    @pl.when(f == pl.num_programs(1) - 1)
    def _():
        finish(acc_ref[...])


def _rms(x, g):
    return x * lax.rsqrt(jnp.mean(x * x, axis=-1, keepdims=True) + NORM_EPS) * g


def _ffn_kernel(h_ref, x_ref, gate_ref, gf_ref, w1_ref, w3_ref, w2_ref, o_ref, *, final, tf):
    h = h_ref[...]
    ff = w1_ref.shape[1]
    ups = [(_dot(h, w1_ref[:, s:s + tf]), _dot(h, w3_ref[:, s:s + tf])) for s in range(0, ff, tf)]
    acc = None
    for s, (a1, a3) in zip(range(0, ff, tf), ups):
        act = ((a1 * _sigmoid(a1)) * a3).astype(BF16)
        part = _dot(act, w2_ref[s:s + tf, :])
        acc = part if acc is None else acc + part
    xn = x_ref[...] + gate_ref[...] * acc
    o_ref[...] = _rms(xn, gf_ref[...]) if final else xn


def _ffn_call(h, x2, gate, gf, w1, w3, w2, seq, final, tm=512, tf=1408):
    r = h.shape[0]
    tiles_per_seq = seq // tm
    row = pl.BlockSpec((tm, D_MODEL), lambda i: (i, 0))
    resident = lambda a: pl.BlockSpec(a.shape, lambda i: (0, 0), pipeline_mode=pl.Buffered(1))
    return pl.pallas_call(
        functools.partial(_ffn_kernel, final=final, tf=tf),
        grid=(r // tm,),
        in_specs=[
            row, row,
            pl.BlockSpec((None, 1, D_MODEL), lambda i: (i // tiles_per_seq, 0, 0)),
            pl.BlockSpec((1, D_MODEL), lambda i: (0, 0)),
            resident(w1), resident(w3), resident(w2),
        ],
        out_specs=row,
        out_shape=jax.ShapeDtypeStruct((r, D_MODEL), F32),
        compiler_params=_params(("arbitrary",)),
        name="ffn",
    )(h, x2, gate, gf, w1, w3, w2)


MOE_TM = 512
ROW_TILE = (D_MODEL // LANE, LANE)


def _route(top_i, tm):
    t = top_i.shape[1]
    e_flat = top_i.reshape(-1)
    oh = (e_flat[:, None] == jnp.arange(N_EXPERTS, dtype=jnp.int32)[None, :]).astype(jnp.int32)
    csum = jnp.cumsum(oh, axis=0)
    rank = jnp.sum(csum * oh, axis=1) - 1
    counts = csum[-1]
    padded = ((counts + tm - 1) // tm) * tm
    ends = jnp.cumsum(padded)
    pos = jnp.sum(oh * (ends - padded)[None, :], axis=1) + rank
    n_tiles = (2 * t) // tm + N_EXPERTS
    starts = jnp.arange(n_tiles, dtype=jnp.int32) * tm
    tile_e = jnp.sum((starts[:, None] >= ends[None, :]).astype(jnp.int32), axis=1)
    return (pos.astype(jnp.int32), ends.astype(jnp.int32), jnp.minimum(tile_e, N_EXPERTS - 1),
            (ends[-1:] // tm).astype(jnp.int32))


def _dispatch_kernel(pos_ref, ends_ref, h_ref, xs_ref, zero_ref, sem, *, t, td, tm):
    i = pl.program_id(0)
    base = i * td

    @pl.when(i == 0)
    def _():
        zero_ref[...] = jnp.zeros_like(zero_ref)
        n_tiles = xs_ref.shape[0] // tm
        fills = []
        for e in range(N_EXPERTS):
            start = ends_ref[e - 1] if e else 0
            fills.append((ends_ref[e] > start, pl.multiple_of(ends_ref[e] - tm, tm)))
            fills.append((ends_ref[N_EXPERTS - 1] // tm + e < n_tiles,
                          pl.multiple_of(ends_ref[N_EXPERTS - 1] + e * tm, tm)))
        for go, row0 in fills:
            @pl.when(go)
            def _():
                pltpu.make_async_copy(zero_ref, xs_ref.at[pl.ds(row0, tm)], sem).start()
        for go, row0 in fills:
            @pl.when(go)
            def _():
                pltpu.make_async_copy(zero_ref, xs_ref.at[pl.ds(row0, tm)], sem).wait()

    for r in range(td):
        for k in range(2):
            dst = pos_ref[k * t + base + r]
            pltpu.make_async_copy(h_ref.at[pl.ds(r, 1)], xs_ref.at[pl.ds(dst, 1)], sem).start(priority=k)
    for k in range(2):
        pltpu.make_async_copy(h_ref, xs_ref.at[pl.ds(0, td)], sem).wait()


def _dispatch_call(pos, ends, h, n_rows, tm, td=512):
    t = h.shape[0]
    return pl.pallas_call(
        functools.partial(_dispatch_kernel, t=t, td=td, tm=tm),
        grid_spec=pltpu.PrefetchScalarGridSpec(
            num_scalar_prefetch=2, grid=(t // td,),
            in_specs=[pl.BlockSpec((td, D_MODEL), lambda i, pos, ends: (i, 0))],
            out_specs=pl.BlockSpec(memory_space=pl.ANY),
            scratch_shapes=[pltpu.VMEM((tm, D_MODEL), h.dtype), pltpu.SemaphoreType.DMA(())]),
        out_shape=jax.ShapeDtypeStruct((n_rows, D_MODEL), h.dtype),
        compiler_params=_params(("arbitrary",)),
        name="dispatch",
    )(pos, ends, h)


def _ffn_routed_kernel(te_ref, nu_ref, x_ref, w1_ref, w3_ref, w2_ref, y_ref, acc_ref):
    del te_ref
    used = pl.program_id(0) < nu_ref[0]

    def finish(acc):
        y_ref[...] = acc.reshape(y_ref.shape)

    @pl.when(used)
    def _():
        _swiglu_step(x_ref[...].astype(BF16), w1_ref, w3_ref, w2_ref, acc_ref, finish)

    @pl.when(jnp.logical_not(used))
    def _():
        y_ref[...] = jnp.zeros_like(y_ref)


def _ffn_routed_call(tile_e, n_used, xs, w1, w3, w2, tm=MOE_TM, tf=1792):
    r = xs.shape[0]
    ff = w1.shape[2]
    return pl.pallas_call(
        _ffn_routed_kernel,
        grid_spec=pltpu.PrefetchScalarGridSpec(
            num_scalar_prefetch=2, grid=(r // tm, ff // tf),
            in_specs=[
                pl.BlockSpec((tm, D_MODEL), lambda i, f, te, nu: (i, 0)),
                pl.BlockSpec((None, D_MODEL, tf), lambda i, f, te, nu: (te[i], 0, f)),
                pl.BlockSpec((None, D_MODEL, tf), lambda i, f, te, nu: (te[i], 0, f)),
                pl.BlockSpec((None, tf, D_MODEL), lambda i, f, te, nu: (te[i], f, 0)),
            ],
            out_specs=pl.BlockSpec((tm,) + ROW_TILE, lambda i, f, te, nu: (i, 0, 0)),
            scratch_shapes=[pltpu.VMEM((tm, D_MODEL), F32)]),
        out_shape=jax.ShapeDtypeStruct((r,) + ROW_TILE, F32),
        compiler_params=_params(("arbitrary", "arbitrary")),
        name="ffn_routed",
    )(tile_e, n_used, xs, w1, w3, w2)


def _combine_kernel(pos_ref, x_ref, p_ref, gate_ref, gf_ref, y_ref, o_ref, buf, sem, *, t, tc, final):
    i = pl.program_id(0)

    def row_copy(step, slot, r, k):
        src = pos_ref[k * t + step * tc + r]
        return pltpu.make_async_copy(y_ref.at[pl.ds(src, 1)], buf.at[slot, k, pl.ds(r, 1)], sem.at[slot])

    def issue(step, slot):
        for r in range(tc):
            for k in range(2):
                row_copy(step, slot, r, k).start(priority=k)

    @pl.when(i == 0)
    def _():
        issue(0, 0)

    @pl.when(i + 1 < pl.num_programs(0))
    def _():
        issue(i + 1, (i + 1) % 2)

    slot = i % 2
    for k in range(2):
        pltpu.make_async_copy(y_ref.at[pl.ds(0, tc)], buf.at[slot, k], sem.at[slot]).wait()
    p = p_ref[...]
    y0 = buf[slot, 0].reshape(tc, D_MODEL)
    y1 = buf[slot, 1].reshape(tc, D_MODEL)
    xn = x_ref[...] + gate_ref[...] * (p[:, 0:1] * y0 + p[:, 1:2] * y1)
    o_ref[...] = _rms(xn, gf_ref[...]) if final else xn


def _combine_call(pos, x2, probs, gate, gf, y, seq, final, tc=256):
    t = x2.shape[0]
    tiles_per_seq = seq // tc
    row = pl.BlockSpec((tc, D_MODEL), lambda i, pos: (i, 0))
    return pl.pallas_call(
        functools.partial(_combine_kernel, t=t, tc=tc, final=final),
        grid_spec=pltpu.PrefetchScalarGridSpec(
            num_scalar_prefetch=1, grid=(t // tc,),
            in_specs=[row, pl.BlockSpec((tc, 2), lambda i, pos: (i, 0)),
                      pl.BlockSpec((None, 1, D_MODEL), lambda i, pos: (i // tiles_per_seq, 0, 0)),
                      pl.BlockSpec((1, D_MODEL), lambda i, pos: (0, 0)),
                      pl.BlockSpec(memory_space=pl.ANY)],
            out_specs=row,
            scratch_shapes=[pltpu.VMEM((2, 2, tc) + ROW_TILE, F32), pltpu.SemaphoreType.DMA((2,))]),
        out_shape=jax.ShapeDtypeStruct((t, D_MODEL), F32),
        compiler_params=_params(("arbitrary",)),
        name="combine",
    )(pos, x2, probs, gate, gf, y)


def _layout_w_in_kernel(w_ref, o_ref):
    half = QK_ROPE // 2
    o_ref[...] = jnp.zeros_like(o_ref)

    def put(dst, src, width, sign=1.0):
        o_ref[dst:dst + width, :] = (sign * w_ref[src:src + width, :]).astype(BF16)

    o = 3 * RW
    put(0, 0, o + DECAY_LORA)
    put(o + LANE, o + DECAY_LORA, ICLR_LORA)
    put(o + 2 * LANE, o + DECAY_LORA + ICLR_LORA, GATE_LORA)
    src = o + DECAY_LORA + ICLR_LORA + GATE_LORA
    put(PS_W, src, Q_LORA + KV_LORA)
    kr = src + Q_LORA + KV_LORA
    dst = PS_W + Q_LORA + KV_LORA
    put(dst + QK_NOPE, kr, QK_ROPE)
    put(dst + LANE + QK_NOPE, kr + half, half, -1.0)
    put(dst + LANE + QK_NOPE + half, kr, half)
    put(PS_W + MLA_W, kr + QK_ROPE, GATE_W)


def _layout_w_in(w_t, layer):
    _, n, k = w_t.shape
    return pl.pallas_call(
        _layout_w_in_kernel,
        grid=(k // 256,),
        in_specs=[pl.BlockSpec((None, n, 256), lambda i: (layer, 0, i))],
        out_specs=pl.BlockSpec((PS_W + MLA_W + GATE_W, 256), lambda i: (0, i)),
        out_shape=jax.ShapeDtypeStruct((PS_W + MLA_W + GATE_W, k), BF16),
        compiler_params=_params(("arbitrary",)),
        name="layout_w_in",
    )(w_t)


def _layout_mu(mu):
    z = jnp.zeros((64,), mu.dtype)
    o = 3 * RW
    return jnp.concatenate([mu[:o + 64], z, mu[o + 64:o + 128], z, mu[o + 128:]])[None, :]


def _layout_wq(w_qb):
    hd = QK_NOPE + QK_ROPE
    w = w_qb.reshape(Q_LORA, MLA_HEADS, hd)
    half = QK_ROPE // 2
    x1, x2 = w[:, :, QK_NOPE:QK_NOPE + half], w[:, :, QK_NOPE + half:]
    zn = jnp.zeros((Q_LORA, MLA_HEADS, QK_NOPE), w.dtype)
    zp = jnp.zeros((Q_LORA, MLA_HEADS, HEAD_PAD - hd), w.dtype)
    wq = jnp.concatenate([w, zp], axis=2).reshape(Q_LORA, -1)
    wqs = jnp.concatenate([zn, -x2, x1, zp], axis=2).reshape(Q_LORA, -1)
    return wq.astype(BF16), wqs.astype(BF16)


def _layout_wkv(w_kvb):
    w = w_kvb.reshape(KV_LORA, MLA_HEADS, QK_NOPE + V_HEAD)
    z = jnp.zeros((KV_LORA, MLA_HEADS, HEAD_PAD - QK_NOPE), w.dtype)
    wk = jnp.concatenate([w[:, :, :QK_NOPE], z], axis=2).reshape(KV_LORA, -1)
    wv = w[:, :, QK_NOPE:].reshape(KV_LORA, -1)
    return wk.astype(BF16), wv.astype(BF16)


def _rope_freq_row():
    inv_freq = ROPE_THETA ** (-jnp.arange(0, QK_ROPE, 2, dtype=F32) / QK_ROPE)
    z = jnp.zeros((QK_NOPE,), F32)
    zp = jnp.zeros((LANE - QK_NOPE - QK_ROPE,), F32)
    return jnp.concatenate([z, inv_freq, inv_freq, zp])[None, :]


def kernel(x, c, positions, ada_w, ada_b, norm_mix, norm_ffn, norm_final, w_in, tshift_mu, w0, w_up, a0, a_up, g_up, k_k, k_a, r_k, lnx_g, lnx_b, rwkv_out, q_norm, w_qb, kv_norm, w_kvb, mla_out, w_o, ffn_w1, ffn_w3, ffn_w2, router_w, router_b, moe_w1, moe_w3, moe_w2):
    batch, seq, _ = x.shape
    t = batch * seq
    mods = _ada_call(c, ada_w.reshape(2 * DEPTH, D_MODEL, 3 * D_MODEL),
                     ada_b.reshape(2 * DEPTH, 1, 3 * D_MODEL))
    mods = mods.reshape(DEPTH, 2, batch, 3, 1, D_MODEL)
    pos_col = positions.reshape(t, 1)
    invf = _rope_freq_row()
    x2 = x.reshape(t, D_MODEL)
    row = lambda v: v.reshape(1, -1)

    side_jobs = [moe_w1.reshape(-1, moe_w1.shape[-1]), moe_w3.reshape(-1, moe_w3.shape[-1])]
    moe_up = []
    w_in_t = jnp.swapaxes(w_in, 1, 2)

    for l in range(DEPTH):
        shift, scale, gate = (mods[l, 0, :, j] for j in range(3))
        ps, mla, gates, cast = _inproj_call(x2, shift, scale, row(norm_mix[l]), _layout_w_in(w_in_t, l),
                                            _layout_mu(tshift_mu[l]), side_jobs[l], seq)
        moe_up.append(cast.reshape(moe_w1.shape))
        vecs = [row(v[l]) for v in (w0, a0, k_k, k_a, r_k, lnx_g, lnx_b)]
        lora_pad = lambda w: jnp.pad(w, ((0, LANE - w.shape[0]), (0, 0)))
        rw = _rwkv_call(ps, vecs, lora_pad(w_up[l]), lora_pad(a_up[l]), g_up[l], batch, seq)
        wq, wqs = _layout_wq(w_qb[l])
        wk, wv = _layout_wkv(w_kvb[l])
        q, k, v = _mlaprep_call(mla, pos_col, invf, row(q_norm[l]), row(kv_norm[l]), wq, wqs, wk, wv)
        o = _attn_call(q, k, v, batch, seq)

        shift2, scale2, gate2 = (mods[l, 1, :, j] for j in range(3))
        moe = l % 2 == 1
        router = (router_w[l // 2].T, router_b[l // 2].reshape(N_EXPERTS, 1)) if moe else None
        outs = _mixout_call(rw, o, gates, x2, gate, rwkv_out[l].astype(BF16), mla_out[l].astype(BF16),
                            w_o[l].astype(BF16), row(norm_ffn[l]), shift2, scale2, seq, router=router)
        final = l == DEPTH - 1
        if moe:
            x2, h2, top_i, top_p = outs
            pos, ends, tile_e, n_used = _route(top_i, MOE_TM)
            xs = _dispatch_call(pos, ends, h2, 2 * t + N_EXPERTS * MOE_TM, MOE_TM)
            y = _ffn_routed_call(tile_e, n_used, xs, moe_up[0][l // 2], moe_up[1][l // 2], moe_w2[l // 2])
            x2 = _combine_call(pos, x2, top_p.T, gate2, row(norm_final), y, seq, final)
        else:
            x2, h2 = outs
            x2 = _ffn_call(h2, x2, gate2, row(norm_final), ffn_w1[l // 2].astype(BF16),
                           ffn_w3[l // 2].astype(BF16), ffn_w2[l // 2].astype(BF16), seq, final)
    return x2.reshape(batch, seq, D_MODEL)
```
